```python
import math
import jax, jax.numpy as jnp
from jax import lax
import numpy as np

D_MODEL = 1024
BATCH = 4
SEQ = 4096
DEPTH = 1

ATTN_HEADS = 8
ATTN_KV_HEADS = 2
HEAD_DIM = 64
ATTN_WIDTH = ATTN_HEADS * HEAD_DIM
WINDOW = 128
ATTN_BLOCK = 128
N_BUCKETS = 32
MAX_DISTANCE = 128
SSD_HEADS = 8
SSD_HEAD_DIM = 64
SSD_INNER = SSD_HEADS * SSD_HEAD_DIM
SSD_STATE = 128
SSD_GROUPS = 2
SSD_CONV = 4
SSD_CHUNK = 128
MIX_WIDTH = ATTN_WIDTH + SSD_INNER
PEER_HEADS = 8
N_KEYS = 128
N_EXPERTS = N_KEYS * N_KEYS
PEER_KEY_DIM = 256
PEER_HALF = PEER_KEY_DIM // 2
PEER_TOPK = 16
PEER_BLOCK = 128
EPS = 1e-6

KV_COLS = ATTN_KV_HEADS * HEAD_DIM
XBC_COLS = SSD_INNER + 2 * SSD_GROUPS * SSD_STATE
OFF_K = ATTN_WIDTH
OFF_V = OFF_K + KV_COLS
OFF_Z = OFF_V + KV_COLS
OFF_XBC = OFF_Z + SSD_INNER
OFF_DT = OFF_XBC + XBC_COLS
IN_COLS = OFF_DT + SSD_HEADS

kernel_name = 'hymba_swa_ssd_peer_block'


def rms_norm(x, w):
    xf = x.astype(jnp.float32)
    y = xf * lax.rsqrt(jnp.mean(xf * xf, axis=-1, keepdims=True) + EPS)
    return (y * w.astype(jnp.float32)).astype(x.dtype)


def t5_bucket(dist):
    n = jnp.maximum(dist, 0)
    max_exact = N_BUCKETS // 2
    nf = jnp.maximum(n, 1).astype(jnp.float32)
    large = max_exact + (jnp.log(nf / max_exact) / math.log(MAX_DISTANCE / max_exact)
                         * (N_BUCKETS - max_exact)).astype(jnp.int32)
    large = jnp.minimum(large, N_BUCKETS - 1)
    return jnp.where(n < max_exact, n, large)


def sliding_window_attention(q, k, v, sinks, rel_bias):
    b, s = q.shape[0], q.shape[1]
    nb = s // ATTN_BLOCK
    grp = ATTN_HEADS // ATTN_KV_HEADS
    qb = q.reshape(b, nb, ATTN_BLOCK, ATTN_KV_HEADS, grp, HEAD_DIM)

    def band(t):
        tp = jnp.pad(t, ((0, 0), (ATTN_BLOCK, 0), (0, 0), (0, 0)))
        tp = tp.reshape(b, nb + 1, ATTN_BLOCK, ATTN_KV_HEADS, HEAD_DIM)
        return jnp.concatenate([tp[:, :-1], tp[:, 1:]], axis=2)

    kb, vb = band(k), band(v)
    scores = jnp.einsum('bnqhgd,bnkhd->bnhgqk', qb, kb).astype(jnp.float32) * (HEAD_DIM ** -0.5)
    qi = jnp.arange(ATTN_BLOCK)[:, None]
    kj = jnp.arange(2 * ATTN_BLOCK)[None, :]
    dist = qi + ATTN_BLOCK - kj
    in_window = (dist >= 0) & (dist < WINDOW)
    key_pos = jnp.arange(nb)[:, None, None] * ATTN_BLOCK - ATTN_BLOCK + kj[None]
    valid = in_window[None] & (key_pos >= 0)
    bias = rel_bias[t5_bucket(dist)].astype(jnp.float32)
    bias = bias.transpose(2, 0, 1).reshape(ATTN_KV_HEADS, grp, ATTN_BLOCK, 2 * ATTN_BLOCK)
    scores = jnp.where(valid[None, :, None, None], scores + bias, -jnp.inf)
    sink = sinks.astype(jnp.float32).reshape(ATTN_KV_HEADS, grp)[:, :, None, None]
    m = jnp.maximum(jnp.max(scores, axis=-1, keepdims=True), sink)
    p = jnp.exp(scores - m)
    denom = jnp.sum(p, axis=-1, keepdims=True) + jnp.exp(sink - m)
    probs = (p / denom).astype(v.dtype)
    out = jnp.einsum('bnhgqk,bnkhd->bnqhgd', probs, vb)
    return out.reshape(b, s, ATTN_WIDTH)


def causal_depthwise_conv(x, w, bias):
    kw = w.shape[0]
    y = lax.conv_general_dilated(x, w[:, None, :], window_strides=(1,), padding=[(kw - 1, 0)],
                                 dimension_numbers=('NWC', 'WIO', 'NWC'),
                                 feature_group_count=x.shape[-1])
    return y + bias


def ssd_scan(x, dt, A, Bm, Cm):
    f32 = jnp.float32
    b, s, h, p = x.shape
    c = s // SSD_CHUNK
    L = SSD_CHUNK
    g = SSD_GROUPS
    r = h // g
    X = (x.astype(f32) * dt[..., None]).reshape(b, c, L, g, r, p)
    a = (dt * A).reshape(b, c, L, g, r).transpose(0, 3, 4, 1, 2)
    a_cum = jnp.cumsum(a, axis=-1)
    Bg = Bm.astype(f32).reshape(b, c, L, g, -1)
    Cg = Cm.astype(f32).reshape(b, c, L, g, -1)
    seg = a_cum[..., :, None] - a_cum[..., None, :]
    causal = jnp.tril(jnp.ones((L, L), dtype=bool))
    Lmat = jnp.where(causal, jnp.exp(jnp.where(causal, seg, 0.0)), 0.0)
    CB = jnp.einsum('bclgn,bcsgn->bgcls', Cg, Bg)
    y_diag = jnp.einsum('bgrcls,bcsgrp->bclgrp', CB[:, :, None] * Lmat, X)
    decay_to_end = jnp.exp(a_cum[..., -1:] - a_cum).transpose(0, 3, 4, 1, 2)
    states = jnp.einsum('bclgn,bclgrp->bcgrpn', Bg, X * decay_to_end[..., None])
    chunk_decay = jnp.exp(a_cum[..., -1]).transpose(3, 0, 1, 2)

    def step(h_prev, inp):
        st, dec = inp
        return h_prev * dec[..., None, None] + st, h_prev

    h0 = jnp.zeros((b, g, r, p, Bg.shape[-1]), f32)
    _, prev_states = lax.scan(step, h0, (states.transpose(1, 0, 2, 3, 4, 5), chunk_decay))
    decay_from_start = jnp.exp(a_cum).transpose(0, 3, 4, 1, 2)
    y_off = jnp.einsum('bclgn,cbgrpn->bclgrp', Cg, prev_states) * decay_from_start[..., None]
    return (y_diag + y_off).reshape(b, s, h, p)


def hybrid_mixer(xn, w_in, conv_w, conv_b, dt_bias, a_log, d_skip, ssd_norm_w, sinks, w_out, rel_bias):
    b, s, _ = xn.shape
    proj = xn @ w_in
    q = proj[..., :OFF_K].reshape(b, s, ATTN_HEADS, HEAD_DIM)
    k = proj[..., OFF_K:OFF_V].reshape(b, s, ATTN_KV_HEADS, HEAD_DIM)
    v = proj[..., OFF_V:OFF_Z].reshape(b, s, ATTN_KV_HEADS, HEAD_DIM)
    z = proj[..., OFF_Z:OFF_XBC]
    xbc = proj[..., OFF_XBC:OFF_DT]
    dt_raw = proj[..., OFF_DT:]
    attn_out = sliding_window_attention(q, k, v, sinks, rel_bias)
    xbc = jax.nn.silu(causal_depthwise_conv(xbc, conv_w, conv_b))
    xs = xbc[..., :SSD_INNER].reshape(b, s, SSD_HEADS, SSD_HEAD_DIM)
    Bm = xbc[..., SSD_INNER:SSD_INNER + SSD_GROUPS * SSD_STATE].reshape(b, s, SSD_GROUPS, SSD_STATE)
    Cm = xbc[..., SSD_INNER + SSD_GROUPS * SSD_STATE:].reshape(b, s, SSD_GROUPS, SSD_STATE)
    dt = jax.nn.softplus(dt_raw.astype(jnp.float32) + dt_bias.astype(jnp.float32))
    A = -jnp.exp(a_log.astype(jnp.float32))
    y = ssd_scan(xs, dt, A, Bm, Cm) + d_skip.astype(jnp.float32)[:, None] * xs.astype(jnp.float32)
    y = y.reshape(b, s, SSD_INNER) * jax.nn.silu(z.astype(jnp.float32))
    ssd_out = rms_norm(y, ssd_norm_w).astype(xn.dtype)
    return jnp.concatenate([attn_out, ssd_out], axis=-1) @ w_out


def peer(xn, w_query, sub_keys1, sub_keys2, expert_down, expert_up):
    f32 = jnp.float32
    b, s, d = xn.shape
    q = (xn @ w_query).reshape(b, s, PEER_HEADS, 2, PEER_HALF).astype(f32)
    s1 = jnp.einsum('bshk,nk->bshn', q[..., 0, :], sub_keys1.astype(f32))
    s2 = jnp.einsum('bshk,nk->bshn', q[..., 1, :], sub_keys2.astype(f32))
    v1, i1 = lax.top_k(s1, PEER_TOPK)
    v2, i2 = lax.top_k(s2, PEER_TOPK)
    cand = (v1[..., :, None] + v2[..., None, :]).reshape(b, s, PEER_HEADS, PEER_TOPK * PEER_TOPK)
    cand_idx = (i1[..., :, None] * N_KEYS + i2[..., None, :]).reshape(b, s, PEER_HEADS, PEER_TOPK * PEER_TOPK)
    top_s, pos = lax.top_k(cand, PEER_TOPK)
    idx = jnp.take_along_axis(cand_idx, pos, axis=-1)
    gates = jax.nn.softmax(top_s, axis=-1)
    nblk = s // PEER_BLOCK

    def to_blocks(t):
        return t.reshape((b, nblk, PEER_BLOCK) + t.shape[2:]).swapaxes(0, 1)

    def eval_block(args):
        xb, ib, gb = args
        u = expert_down[ib]
        pre = jnp.einsum('bthkd,btd->bthk', u, xb).astype(f32)
        act = (jax.nn.gelu(pre, approximate=False) * gb).astype(xb.dtype)
        return jnp.einsum('bthk,bthkd->btd', act, expert_up[ib])

    out = lax.map(eval_block, (to_blocks(xn), to_blocks(idx), to_blocks(gates)))
    return out.swapaxes(0, 1).reshape(b, s, d)


def setup_inputs(seed: int = 0) -> dict:
    key = jax.random.key(seed)
    ks = jax.random.split(key, 20)
    f32 = jnp.float32
    nrm = lambda k, shape, scale: (jax.random.normal(k, shape, f32) * scale)
    dt0 = jnp.exp(jax.random.uniform(ks[5], (DEPTH, SSD_HEADS), f32)
                  * (math.log(0.1) - math.log(0.001)) + math.log(0.001))
    return {
        'x': nrm(ks[0], (BATCH, SEQ, D_MODEL), 1.0),
        'norm_mix': 1.0 + nrm(ks[1], (DEPTH, D_MODEL), 0.05),
        'w_in': nrm(ks[2], (DEPTH, D_MODEL, IN_COLS), D_MODEL ** -0.5),
        'conv_w': nrm(ks[3], (DEPTH, SSD_CONV, XBC_COLS), SSD_CONV ** -0.5),
        'conv_b': nrm(ks[4], (DEPTH, XBC_COLS), 0.02),
        'dt_bias': dt0 + jnp.log(-jnp.expm1(-dt0)),
        'a_log': jnp.log(jax.random.uniform(ks[6], (DEPTH, SSD_HEADS), f32, 1.0, 16.0)),
        'd_skip': 1.0 + nrm(ks[7], (DEPTH, SSD_HEADS), 0.1),
        'ssd_norm_w': 1.0 + nrm(ks[8], (DEPTH, SSD_INNER), 0.05),
        'attn_sinks': nrm(ks[9], (DEPTH, ATTN_HEADS), 1.0),
        'w_out': nrm(ks[10], (DEPTH, MIX_WIDTH, D_MODEL), MIX_WIDTH ** -0.5),
        'rel_bias': nrm(ks[11], (N_BUCKETS, ATTN_HEADS), 0.5),
        'norm_ffn': 1.0 + nrm(ks[12], (DEPTH, D_MODEL), 0.05),
        'w_query': nrm(ks[13], (DEPTH, D_MODEL, PEER_HEADS * PEER_KEY_DIM), D_MODEL ** -0.5),
        'sub_keys1': nrm(ks[14], (DEPTH, N_KEYS, PEER_HALF), PEER_HALF ** -0.5),
        'sub_keys2': nrm(ks[15], (DEPTH, N_KEYS, PEER_HALF), PEER_HALF ** -0.5),
        'expert_down': nrm(ks[16], (DEPTH, N_EXPERTS, D_MODEL), D_MODEL ** -0.5),
        'expert_up': nrm(ks[17], (DEPTH, N_EXPERTS, D_MODEL), PEER_HEADS ** -0.5),
        'norm_final': 1.0 + nrm(ks[18], (D_MODEL,), 0.05),
    }


def reference(x, norm_mix, w_in, conv_w, conv_b, dt_bias, a_log, d_skip, ssd_norm_w, attn_sinks,
              w_out, rel_bias, norm_ffn, w_query, sub_keys1, sub_keys2, expert_down, expert_up,
              norm_final):
    h = x
    for l in range(DEPTH):
        h = h + hybrid_mixer(rms_norm(h, norm_mix[l]), w_in[l], conv_w[l], conv_b[l], dt_bias[l],
                             a_log[l], d_skip[l], ssd_norm_w[l], attn_sinks[l], w_out[l], rel_bias)
        h = h + peer(rms_norm(h, norm_ffn[l]), w_query[l], sub_keys1[l], sub_keys2[l],
                     expert_down[l], expert_up[l])
    return rms_norm(h, norm_final)
```

```python
import functools
import math

import jax
import jax.numpy as jnp
from jax import lax
from jax.experimental import pallas as pl
from jax.experimental.pallas import tpu as pltpu

F32 = jnp.float32
BF16 = jnp.bfloat16
HIGHEST = lax.Precision.HIGHEST

D_MODEL = 1024
ATTN_HEADS = 8
ATTN_KV_HEADS = 2
ATTN_GROUP = ATTN_HEADS // ATTN_KV_HEADS
HEAD_DIM = 64
ATTN_WIDTH = ATTN_HEADS * HEAD_DIM
ATTN_BLOCK = 128
N_BUCKETS = 32
MAX_DISTANCE = 128
SSD_HEADS = 8
SSD_HEAD_DIM = 64
SSD_INNER = SSD_HEADS * SSD_HEAD_DIM
SSD_STATE = 128
SSD_GROUPS = 2
SSD_HEADS_PER_GROUP = SSD_HEADS // SSD_GROUPS
SSD_CONV = 4
SSD_CHUNK = 128
PEER_HEADS = 8
N_KEYS = 128
N_EXPERTS = N_KEYS * N_KEYS
PEER_HALF = 128
PEER_TOPK = 16
EPS = 1e-6

KV_COLS = ATTN_KV_HEADS * HEAD_DIM
XBC_COLS = SSD_INNER + 2 * SSD_GROUPS * SSD_STATE
OFF_K = ATTN_WIDTH
OFF_V = OFF_K + KV_COLS
OFF_Z = OFF_V + KV_COLS
OFF_XBC = OFF_Z + SSD_INNER
OFF_DT = OFF_XBC + XBC_COLS
LANES = 128
DT_PAD = LANES
IN_COLS_PAD = OFF_DT + DT_PAD
CONV_HALO = 8

VMEM_LIMIT = 48 * 1024 * 1024

INPROJ_TM = 512
OUTPROJ_TM = 256
ROUTE_TM = 128
PEER_TM = 512
PEER_TE = 512
PEER_ROWS = 16


def _dot(a, b, precision=None):
    return jnp.dot(a, b, preferred_element_type=F32, precision=precision)


def _dot_nt(a, b):
    return lax.dot_general(a, b, (((1,), (1,)), ((), ())), preferred_element_type=F32)


def _dot_tn(a, b):
    return lax.dot_general(a, b, (((0,), (0,)), ((), ())), preferred_element_type=F32)


def _rms(x, w):
    return x * lax.rsqrt(jnp.mean(x * x, axis=-1, keepdims=True) + EPS) * w


def _silu(x):
    return x / (1.0 + jnp.exp(-x))


def _softplus(x):
    return jnp.maximum(x, 0.0) + jnp.log1p(jnp.exp(-jnp.abs(x)))


def _params(*sem):
    return pltpu.CompilerParams(dimension_semantics=sem, vmem_limit_bytes=VMEM_LIMIT)


def _inproj_kernel(x_ref, nw_ref, w_ref, q_ref, k_ref, v_ref, z_ref, xbc_ref, dt_ref):
    xb = _rms(x_ref[...], nw_ref[...]).astype(BF16)

    def proj(lo, hi):
        return _dot(xb, w_ref[:, lo:hi])

    q_ref[...] = (proj(0, OFF_K) * (HEAD_DIM ** -0.5)).astype(BF16)
    k_ref[...] = proj(OFF_K, OFF_V).astype(BF16)
    v_ref[...] = proj(OFF_V, OFF_Z).astype(BF16)
    z_ref[...] = proj(OFF_Z, OFF_XBC)
    xbc_ref[...] = proj(OFF_XBC, OFF_DT)
    dt_ref[...] = proj(OFF_DT, IN_COLS_PAD)


def _inproj(x2, norm_w, w_in_pad):
    t = x2.shape[0]
    tm = min(INPROJ_TM, t)
    row = lambda i: (i, 0)
    fixed = lambda i: (0, 0)
    return pl.pallas_call(
        _inproj_kernel,
        grid=(t // tm,),
        in_specs=[
            pl.BlockSpec((tm, D_MODEL), row),
            pl.BlockSpec((1, D_MODEL), fixed),
            pl.BlockSpec((D_MODEL, IN_COLS_PAD), fixed),
        ],
        out_specs=[
            pl.BlockSpec((tm, ATTN_WIDTH), row),
            pl.BlockSpec((tm, KV_COLS), row),
            pl.BlockSpec((tm, KV_COLS), row),
            pl.BlockSpec((tm, SSD_INNER), row),
            pl.BlockSpec((tm, XBC_COLS), row),
            pl.BlockSpec((tm, DT_PAD), row),
        ],
        out_shape=[
            jax.ShapeDtypeStruct((t, ATTN_WIDTH), BF16),
            jax.ShapeDtypeStruct((t, KV_COLS), BF16),
            jax.ShapeDtypeStruct((t, KV_COLS), BF16),
            jax.ShapeDtypeStruct((t, SSD_INNER), F32),
            jax.ShapeDtypeStruct((t, XBC_COLS), F32),
            jax.ShapeDtypeStruct((t, DT_PAD), F32),
        ],
        compiler_params=_params("parallel"),
        name="inproj",
    )(x2, norm_w, w_in_pad)


def _t5_bucket(dist):
    n = jnp.maximum(dist, 0)
    max_exact = N_BUCKETS // 2
    nf = jnp.maximum(n, 1).astype(F32)
    large = max_exact + (jnp.log(nf / max_exact) / math.log(MAX_DISTANCE / max_exact)
                         * (N_BUCKETS - max_exact)).astype(jnp.int32)
    large = jnp.minimum(large, N_BUCKETS - 1)
    return jnp.where(n < max_exact, n, large)


def _attn_kernel(bucket_ref, relb_ref, sink_ref, q_ref, kp_ref, kc_ref, vp_ref, vc_ref,
                 o_ref, bias_ref):
    blk = ATTN_BLOCK
    rows = ATTN_GROUP * blk
    n = pl.program_id(1)

    @pl.when((pl.program_id(0) == 0) & (n == 0))
    def _():
        bucket = bucket_ref[...]
        for h in range(ATTN_HEADS):
            acc = jnp.zeros((blk, blk), F32)
            for b in range(N_BUCKETS):
                acc = jnp.where(bucket == b, relb_ref[b, h], acc)
            hk, g = divmod(h, ATTN_GROUP)
            bias_ref[hk, g * blk:(g + 1) * blk, :] = acc

    qi = lax.broadcasted_iota(jnp.int32, (rows, blk), 0) & (blk - 1)
    kj = lax.broadcasted_iota(jnp.int32, (rows, blk), 1)
    own = kj <= qi
    has_prev = n > 0
    for hk in range(ATTN_KV_HEADS):
        ksl = slice(hk * HEAD_DIM, (hk + 1) * HEAD_DIM)
        kband = jnp.concatenate([kp_ref[:, ksl], kc_ref[:, ksl]], axis=0)
        vband = jnp.concatenate([vp_ref[:, ksl], vc_ref[:, ksl]], axis=0)
        qg = jnp.concatenate(
            [q_ref[:, (hk * ATTN_GROUP + g) * HEAD_DIM:(hk * ATTN_GROUP + g + 1) * HEAD_DIM]
             for g in range(ATTN_GROUP)], axis=0)
        s2 = _dot_nt(qg, kband)
        s = jnp.where(own, s2[:, blk:], s2[:, :blk]) + bias_ref[hk]
        s = jnp.where(own | has_prev, s, -jnp.inf)
        sink = jnp.concatenate(
            [jnp.full((blk, 1), sink_ref[hk * ATTN_GROUP + g], F32) for g in range(ATTN_GROUP)],
            axis=0)
        m = jnp.maximum(jnp.max(s, axis=-1, keepdims=True), sink)
        p = jnp.exp(s - m)
        denom = jnp.sum(p, axis=-1, keepdims=True) + jnp.exp(sink - m)
        pb = p.astype(BF16)
        zero = jnp.zeros_like(pb)
        pfull = jnp.concatenate([jnp.where(own, zero, pb), jnp.where(own, pb, zero)], axis=1)
        o = _dot(pfull, vband) / denom
        for g in range(ATTN_GROUP):
            h = hk * ATTN_GROUP + g
            o_ref[:, h * HEAD_DIM:(h + 1) * HEAD_DIM] = o[g * blk:(g + 1) * blk].astype(BF16)


def _attention(q, k, v, bucket, rel_bias, sinks):
    b, s, _ = q.shape
    nb = s // ATTN_BLOCK
    cur = lambda i, j: (i, j, 0)
    prev = lambda i, j: (i, jnp.maximum(j - 1, 0), 0)
    smem = pl.BlockSpec(memory_space=pltpu.SMEM)
    return pl.pallas_call(
        _attn_kernel,
        grid=(b, nb),
        in_specs=[
            pl.BlockSpec((ATTN_BLOCK, ATTN_BLOCK), lambda i, j: (0, 0)),
            smem, smem,
            pl.BlockSpec((None, ATTN_BLOCK, ATTN_WIDTH), cur),
            pl.BlockSpec((None, ATTN_BLOCK, KV_COLS), prev),
            pl.BlockSpec((None, ATTN_BLOCK, KV_COLS), cur),
            pl.BlockSpec((None, ATTN_BLOCK, KV_COLS), prev),
            pl.BlockSpec((None, ATTN_BLOCK, KV_COLS), cur),
        ],
        out_specs=pl.BlockSpec((None, ATTN_BLOCK, ATTN_WIDTH), cur),
        out_shape=jax.ShapeDtypeStruct((b, s, ATTN_WIDTH), BF16),
        scratch_shapes=[pltpu.VMEM((ATTN_KV_HEADS, ATTN_GROUP * ATTN_BLOCK, ATTN_BLOCK), F32)],
        compiler_params=_params("arbitrary", "arbitrary"),
        name="swa",
    )(bucket, rel_bias, sinks, q, k, k, v, v)


def _ssd_kernel(xbc_ref, z_ref, dt_ref, convw_ref, convb_ref, dtb_ref, alog_ref, dskip_ref,
                nw_ref, expand_ref, tri_ref, o_ref, xpad_ref, state_ref):
    L = SSD_CHUNK
    P = SSD_HEAD_DIM
    c = pl.program_id(1)

    @pl.when(c == 0)
    def _():
        xpad_ref[0:CONV_HALO, :] = jnp.zeros((CONV_HALO, XBC_COLS), F32)
        state_ref[...] = jnp.zeros_like(state_ref)

    @pl.when(c > 0)
    def _():
        xpad_ref[0:CONV_HALO, :] = xpad_ref[L:L + CONV_HALO, :]

    xpad_ref[CONV_HALO:CONV_HALO + L, :] = xbc_ref[...]
    conv = convb_ref[...]
    for j in range(SSD_CONV):
        off = CONV_HALO - (SSD_CONV - 1) + j
        conv = conv + convw_ref[j:j + 1, :] * xpad_ref[off:off + L, :]
    xbc = _silu(conv)
    xs = xbc[:, :SSD_INNER]
    bm = xbc[:, SSD_INNER:SSD_INNER + SSD_GROUPS * SSD_STATE].astype(BF16)
    cm = xbc[:, SSD_INNER + SSD_GROUPS * SSD_STATE:].astype(BF16)

    expand = expand_ref[...]
    dt = _softplus(dt_ref[...] + dtb_ref[...])
    a = dt * (-jnp.exp(alog_ref[...]))
    a_cum = _dot(tri_ref[...], a, HIGHEST)
    a_last = a_cum[L - 1:L, :]
    x_dt = xs * _dot(dt, expand, HIGHEST)
    x_end = (x_dt * _dot(jnp.exp(a_last - a_cum), expand, HIGHEST)).astype(BF16)
    dec_start = _dot(jnp.exp(a_cum), expand, HIGHEST)
    x_dt = x_dt.astype(BF16)
    a_cum_t = a_cum.T

    li = lax.broadcasted_iota(jnp.int32, (L, L), 0)
    si = lax.broadcasted_iota(jnp.int32, (L, L), 1)
    causal = si <= li
    ys = []
    for g in range(SSD_GROUPS):
        bg = bm[:, g * SSD_STATE:(g + 1) * SSD_STATE]
        cg = cm[:, g * SSD_STATE:(g + 1) * SSD_STATE]
        cb = _dot_nt(cg, bg)
        for r in range(SSD_HEADS_PER_GROUP):
            h = g * SSD_HEADS_PER_GROUP + r
            sl = slice(h * P, (h + 1) * P)
            seg = a_cum[:, h:h + 1] - a_cum_t[h:h + 1, :]
            lmat = jnp.where(causal, jnp.exp(jnp.where(causal, seg, 0.0)), 0.0)
            h_prev = state_ref[sl, :]
            y_diag = _dot((cb * lmat).astype(BF16), x_dt[:, sl])
            y_off = _dot_nt(cg, h_prev.astype(BF16)) * dec_start[:, sl]
            ys.append(y_diag + y_off)
            chunk_decay = jnp.exp(a_cum_t[h:h + 1, L - 1:L])
            state_ref[sl, :] = h_prev * chunk_decay + _dot_tn(x_end[:, sl], bg)
    y = jnp.concatenate(ys, axis=1) + dskip_ref[...] * xs
    y = y * _silu(z_ref[...])
    o_ref[...] = _rms(y, nw_ref[...]).astype(BF16)


def _ssd(xbc, z, dt, conv_w, conv_b, dt_bias_pad, a_log_pad, d_skip_wide, norm_w, expand, tri):
    b, s, _ = xbc.shape
    nc = s // SSD_CHUNK
    cur = lambda i, j: (i, j, 0)
    fixed = lambda i, j: (0, 0)
    full = lambda shape: pl.BlockSpec(shape, fixed)
    return pl.pallas_call(
        _ssd_kernel,
        grid=(b, nc),
        in_specs=[
            pl.BlockSpec((None, SSD_CHUNK, XBC_COLS), cur),
            pl.BlockSpec((None, SSD_CHUNK, SSD_INNER), cur),
            pl.BlockSpec((None, SSD_CHUNK, DT_PAD), cur),
            full((SSD_CONV, XBC_COLS)), full((1, XBC_COLS)), full((1, DT_PAD)), full((1, DT_PAD)),
            full((1, SSD_INNER)), full((1, SSD_INNER)), full((DT_PAD, SSD_INNER)),
            full((SSD_CHUNK, SSD_CHUNK)),
        ],
        out_specs=pl.BlockSpec((None, SSD_CHUNK, SSD_INNER), cur),
        out_shape=jax.ShapeDtypeStruct((b, s, SSD_INNER), BF16),
        scratch_shapes=[
            pltpu.VMEM((CONV_HALO + SSD_CHUNK, XBC_COLS), F32),
            pltpu.VMEM((SSD_INNER, SSD_STATE), F32),
        ],
        compiler_params=_params("arbitrary", "arbitrary"),
        name="ssd",
    )(xbc, z, dt, conv_w, conv_b, dt_bias_pad, a_log_pad, d_skip_wide, norm_w, expand, tri)


def _outproj_kernel(x_ref, attn_ref, ssd_ref, wo_ref, nw_ref, wq_ref, h1_ref, xnt_ref, q_ref):
    mix = (_dot(attn_ref[...], wo_ref[0:ATTN_WIDTH, :])
           + _dot(ssd_ref[...], wo_ref[ATTN_WIDTH:ATTN_WIDTH + SSD_INNER, :]))
    h1 = x_ref[...] + mix
    h1_ref[...] = h1
    xn = _rms(h1, nw_ref[...])
    xnt_ref[...] = xn.T.astype(BF16)
    qf = _dot(xn.astype(BF16), wq_ref[...])
    for j in range(2 * PEER_HEADS):
        q_ref[j] = qf[:, j * PEER_HALF:(j + 1) * PEER_HALF].astype(BF16)


def _outproj(x2, attn, ssd, w_out, norm_w, w_query):
    t = x2.shape[0]
    tm = min(OUTPROJ_TM, t)
    row = lambda i: (i, 0)
    fixed = lambda i: (0, 0)
    nq = 2 * PEER_HEADS
    return pl.pallas_call(
        _outproj_kernel,
        grid=(t // tm,),
        in_specs=[
            pl.BlockSpec((tm, D_MODEL), row),
            pl.BlockSpec((tm, ATTN_WIDTH), row),
            pl.BlockSpec((tm, SSD_INNER), row),
            pl.BlockSpec((ATTN_WIDTH + SSD_INNER, D_MODEL), fixed),
            pl.BlockSpec((1, D_MODEL), fixed),
            pl.BlockSpec((D_MODEL, nq * PEER_HALF), fixed),
        ],
        out_specs=[
            pl.BlockSpec((tm, D_MODEL), row),
            pl.BlockSpec((D_MODEL, tm), lambda i: (0, i)),
            pl.BlockSpec((nq, tm, PEER_HALF), lambda i: (0, i, 0)),
        ],
        out_shape=[
            jax.ShapeDtypeStruct((t, D_MODEL), F32),
            jax.ShapeDtypeStruct((D_MODEL, t), BF16),
            jax.ShapeDtypeStruct((nq, t, PEER_HALF), BF16),
        ],
        compiler_params=_params("parallel"),
        name="outproj",
    )(x2, attn, ssd, w_out, norm_w, w_query)


def _top16(s):
    cur = s
    rank = jnp.full(s.shape, 255.0, F32)
    vals = []
    for k in range(PEER_TOPK):
        m = jnp.max(cur, axis=0, keepdims=True)
        hit = cur == m
        rank = jnp.where(hit, float(k), rank)
        cur = jnp.where(hit, -jnp.inf, cur)
        vals.append(m)
    return vals, rank


def _route_kernel(q_ref, k1_ref, k2_ref, r2_ref, c1_ref, e1_ref, e2_ref):
    def head(h, carry):
        s1 = _dot_nt(k1_ref[...], q_ref[2 * h])
        s2 = _dot_nt(k2_ref[...], q_ref[2 * h + 1])
        v1, r1 = _top16(s1)
        v2, r2 = _top16(s2)
        v2s = jnp.concatenate(v2, axis=0)
        cands = [v2s + v1[a] for a in range(PEER_TOPK)]
        cur = cands
        tau = None
        for k in range(PEER_TOPK):
            m = cur[0]
            for a in range(1, PEER_TOPK):
                m = jnp.maximum(m, cur[a])
            tau = jnp.max(m, axis=0, keepdims=True)
            if k + 1 < PEER_TOPK:
                cur = [jnp.where(x == tau, -jnp.inf, x) for x in cur]
        e2_top = jnp.exp(v2s - v2[0])
        zsum = jnp.zeros_like(tau)
        c1 = jnp.zeros(s1.shape, F32)
        for a in range(PEER_TOPK):
            sel = cands[a] >= tau
            cnt = jnp.sum(jnp.where(sel, 1.0, 0.0), axis=0, keepdims=True)
            w = jnp.exp(v1[a] - v1[0]) * e2_top
            zsum = zsum + jnp.sum(jnp.where(sel, w, 0.0), axis=0, keepdims=True)
            c1 = jnp.where(r1 == float(a), cnt, c1)
        r2_ref[h] = r2
        c1_ref[h] = c1
        e1_ref[h] = jnp.exp(s1 - v1[0]) / zsum
        e2_ref[h] = jnp.exp(s2 - v2[0])
        return carry

    lax.fori_loop(0, PEER_HEADS, head, 0)


def _route(q, keys1, keys2):
    nq, t, _ = q.shape
    tm = min(ROUTE_TM, t)
    fixed = lambda i: (0, 0)
    out_spec = pl.BlockSpec((PEER_HEADS, N_KEYS, tm), lambda i: (0, 0, i))
    out_shape = jax.ShapeDtypeStruct((PEER_HEADS, N_KEYS, t), F32)
    return pl.pallas_call(
        _route_kernel,
        grid=(t // tm,),
        in_specs=[
            pl.BlockSpec((nq, tm, PEER_HALF), lambda i: (0, i, 0)),
            pl.BlockSpec((N_KEYS, PEER_HALF), fixed),
            pl.BlockSpec((N_KEYS, PEER_HALF), fixed),
        ],
        out_specs=[out_spec] * 4,
        out_shape=[out_shape] * 4,
        compiler_params=_params("parallel"),
        name="route",
    )(q, keys1, keys2)


def _peer_kernel(xnt_ref, d_ref, ut_ref, r2_ref, c1_ref, e1_ref, e2_ref, h1_ref, nw_ref,
                 o_ref, acc_ref, pre_ref, act_ref):
    j = pl.program_id(1)
    te = d_ref.shape[0]
    blocks_per_key = N_KEYS // PEER_ROWS

    @pl.when(j == 0)
    def _():
        acc_ref[...] = jnp.zeros_like(acc_ref)

    pre_ref[...] = _dot(d_ref[...], xnt_ref[...])

    def chunk(k, carry):
        i1 = j * (te // N_KEYS) + k // blocks_per_key
        i2 = pl.multiple_of((k % blocks_per_key) * PEER_ROWS, PEER_ROWS)
        row = pl.multiple_of(k * PEER_ROWS, PEER_ROWS)
        gate = jnp.zeros((PEER_ROWS, xnt_ref.shape[1]), F32)
        for h in range(PEER_HEADS):
            cnt = c1_ref[h, pl.ds(i1, 1), :]
            e1 = e1_ref[h, pl.ds(i1, 1), :]
            rank = r2_ref[h, pl.ds(i2, PEER_ROWS), :]
            e2 = e2_ref[h, pl.ds(i2, PEER_ROWS), :]
            gate = gate + jnp.where(rank < cnt, e2, 0.0) * e1
        p = pre_ref[pl.ds(row, PEER_ROWS), :]
        act = 0.5 * p * (1.0 + lax.erf(p * (2.0 ** -0.5))) * gate
        act_ref[pl.ds(row, PEER_ROWS), :] = act.astype(BF16)
        return carry

    lax.fori_loop(0, te // PEER_ROWS, chunk, 0)
    acc_ref[...] += _dot(ut_ref[...], act_ref[...])

    @pl.when(j == pl.num_programs(1) - 1)
    def _():
        h2 = h1_ref[...] + acc_ref[...].T
        o_ref[...] = _rms(h2, nw_ref[...])


def _peer(xnt, down, up_t, r2, c1, e1, e2, h1, norm_w):
    t = h1.shape[0]
    tm = min(PEER_TM, t)
    te = PEER_TE
    route_spec = pl.BlockSpec((PEER_HEADS, N_KEYS, tm), lambda i, j: (0, 0, i))
    return pl.pallas_call(
        _peer_kernel,
        grid=(t // tm, N_EXPERTS // te),
        in_specs=[
            pl.BlockSpec((D_MODEL, tm), lambda i, j: (0, i)),
            pl.BlockSpec((te, D_MODEL), lambda i, j: (j, 0)),
            pl.BlockSpec((D_MODEL, te), lambda i, j: (0, j)),
            route_spec, route_spec, route_spec, route_spec,
            pl.BlockSpec((tm, D_MODEL), lambda i, j: (i, 0)),
            pl.BlockSpec((1, D_MODEL), lambda i, j: (0, 0)),
        ],
        out_specs=pl.BlockSpec((tm, D_MODEL), lambda i, j: (i, 0)),
        out_shape=jax.ShapeDtypeStruct((t, D_MODEL), F32),
        scratch_shapes=[
            pltpu.VMEM((D_MODEL, tm), F32),
            pltpu.VMEM((te, tm), F32),
            pltpu.VMEM((te, tm), BF16),
        ],
        compiler_params=_params("parallel", "arbitrary"),
        name="peer",
    )(xnt, down, up_t, r2, c1, e1, e2, h1, norm_w)


def _pad_lanes(v, width):
    return jnp.pad(v.astype(F32), (0, width - v.shape[0])).reshape(1, width)


def _layer(x, norm_mix, w_in, conv_w, conv_b, dt_bias, a_log, d_skip, ssd_norm_w, attn_sinks,
           w_out, rel_bias, norm_ffn, w_query, sub_keys1, sub_keys2, expert_down, expert_up,
           out_norm):
    b, s, d = x.shape
    t = b * s
    x2 = x.reshape(t, d)

    w_in_pad = jnp.pad(w_in, ((0, 0), (0, IN_COLS_PAD - w_in.shape[1]))).astype(BF16)
    q, k, v, z, xbc, dt = _inproj(x2, norm_mix.reshape(1, d), w_in_pad)

    qi = jnp.arange(ATTN_BLOCK)[:, None]
    kj = jnp.arange(ATTN_BLOCK)[None, :]
    bucket = _t5_bucket(jnp.where(kj <= qi, qi - kj, qi + ATTN_BLOCK - kj)).astype(jnp.int32)
    attn = _attention(q.reshape(b, s, -1), k.reshape(b, s, -1), v.reshape(b, s, -1),
                      bucket, rel_bias.astype(F32), attn_sinks.astype(F32))

    lane = jnp.arange(SSD_INNER)[None, :] // SSD_HEAD_DIM
    expand = (jnp.arange(DT_PAD)[:, None] == lane).astype(F32)
    tri = (jnp.arange(SSD_CHUNK)[None, :] <= jnp.arange(SSD_CHUNK)[:, None]).astype(F32)
    ssd = _ssd(xbc.reshape(b, s, -1), z.reshape(b, s, -1), dt.reshape(b, s, -1),
               conv_w, conv_b.reshape(1, -1), _pad_lanes(dt_bias, DT_PAD),
               _pad_lanes(a_log, DT_PAD), jnp.repeat(d_skip.astype(F32), SSD_HEAD_DIM).reshape(1, -1),
               ssd_norm_w.reshape(1, -1), expand, tri)

    h1, xnt, pq = _outproj(x2, attn.reshape(t, -1), ssd.reshape(t, -1), w_out.astype(BF16),
                           norm_ffn.reshape(1, d), w_query.astype(BF16))
    r2, c1, e1, e2 = _route(pq, sub_keys1.astype(BF16), sub_keys2.astype(BF16))
    out = _peer(xnt, expert_down.astype(BF16), expert_up.T.astype(BF16), r2, c1, e1, e2, h1,
                out_norm.reshape(1, d))
    return out.reshape(b, s, d)


@jax.jit
def kernel(x, norm_mix, w_in, conv_w, conv_b, dt_bias, a_log, d_skip, ssd_norm_w, attn_sinks,
           w_out, rel_bias, norm_ffn, w_query, sub_keys1, sub_keys2, expert_down, expert_up,
           norm_final):
    assert norm_mix.shape[0] == 1, "single-layer block"
    return _layer(x, norm_mix[0], w_in[0], conv_w[0], conv_b[0], dt_bias[0], a_log[0], d_skip[0],
                  ssd_norm_w[0], attn_sinks[0], w_out[0], rel_bias, norm_ffn[0], w_query[0],
                  sub_keys1[0], sub_keys2[0], expert_down[0], expert_up[0], norm_final)
```

```python
import functools
import math

import jax
import jax.numpy as jnp
from jax import lax
from jax.experimental import pallas as pl
from jax.experimental.pallas import tpu as pltpu

F32 = jnp.float32
BF16 = jnp.bfloat16
HIGHEST = lax.Precision.HIGHEST

D_MODEL = 1024
ATTN_HEADS = 8
ATTN_KV_HEADS = 2
ATTN_GROUP = ATTN_HEADS // ATTN_KV_HEADS
HEAD_DIM = 64
ATTN_WIDTH = ATTN_HEADS * HEAD_DIM
ATTN_BLOCK = 128
N_BUCKETS = 32
MAX_DISTANCE = 128
SSD_HEADS = 8
SSD_HEAD_DIM = 64
SSD_INNER = SSD_HEADS * SSD_HEAD_DIM
SSD_STATE = 128
SSD_GROUPS = 2
SSD_HEADS_PER_GROUP = SSD_HEADS // SSD_GROUPS
SSD_CONV = 4
SSD_CHUNK = 128
PEER_HEADS = 8
N_KEYS = 128
N_EXPERTS = N_KEYS * N_KEYS
PEER_HALF = 128
PEER_TOPK = 16
EPS = 1e-6

KV_COLS = ATTN_KV_HEADS * HEAD_DIM
XBC_COLS = SSD_INNER + 2 * SSD_GROUPS * SSD_STATE
OFF_K = ATTN_WIDTH
OFF_V = OFF_K + KV_COLS
OFF_Z = OFF_V + KV_COLS
OFF_XBC = OFF_Z + SSD_INNER
OFF_DT = OFF_XBC + XBC_COLS
LANES = 128
DT_PAD = LANES
IN_COLS_PAD = OFF_DT + DT_PAD
CONV_HALO = 8

VMEM_LIMIT = 56 * 1024 * 1024

INPROJ_TM = 512
OUTPROJ_TM = 256
ROUTE_TM = 128
PEER_TM = 1024
PEER_TE = 1024
PEER_SUB = 256
PEER_LANE_CHUNK = 512
PEER_ROWS = 16


def _dot(a, b, precision=None):
    return jnp.dot(a, b, preferred_element_type=F32, precision=precision)


def _dot_nt(a, b):
    return lax.dot_general(a, b, (((1,), (1,)), ((), ())), preferred_element_type=F32)


def _dot_tn(a, b):
    return lax.dot_general(a, b, (((0,), (0,)), ((), ())), preferred_element_type=F32)


def _rms(x, w):
    return x * lax.rsqrt(jnp.mean(x * x, axis=-1, keepdims=True) + EPS) * w


def _silu(x):
    return x / (1.0 + jnp.exp(-x))


def _softplus(x):
    return jnp.maximum(x, 0.0) + jnp.log1p(jnp.exp(-jnp.abs(x)))


def _params(*sem):
    return pltpu.CompilerParams(dimension_semantics=sem, vmem_limit_bytes=VMEM_LIMIT)


def _inproj_kernel(x_ref, nw_ref, w_ref, q_ref, k_ref, v_ref, z_ref, xbc_ref, dt_ref):
    xb = _rms(x_ref[...], nw_ref[...]).astype(BF16)

    def proj(lo, hi):
        return _dot(xb, w_ref[:, lo:hi])

    q_ref[...] = (proj(0, OFF_K) * (HEAD_DIM ** -0.5)).astype(BF16)
    k_ref[...] = proj(OFF_K, OFF_V).astype(BF16)
    v_ref[...] = proj(OFF_V, OFF_Z).astype(BF16)
    z_ref[...] = proj(OFF_Z, OFF_XBC)
    xbc_ref[...] = proj(OFF_XBC, OFF_DT)
    dt_ref[...] = proj(OFF_DT, IN_COLS_PAD)


def _inproj(x2, norm_w, w_in_pad):
    t = x2.shape[0]
    tm = min(INPROJ_TM, t)
    row = lambda i: (i, 0)
    fixed = lambda i: (0, 0)
    return pl.pallas_call(
        _inproj_kernel,
        grid=(t // tm,),
        in_specs=[
            pl.BlockSpec((tm, D_MODEL), row),
            pl.BlockSpec((1, D_MODEL), fixed),
            pl.BlockSpec((D_MODEL, IN_COLS_PAD), fixed),
        ],
        out_specs=[
            pl.BlockSpec((tm, ATTN_WIDTH), row),
            pl.BlockSpec((tm, KV_COLS), row),
            pl.BlockSpec((tm, KV_COLS), row),
            pl.BlockSpec((tm, SSD_INNER), row),
            pl.BlockSpec((tm, XBC_COLS), row),
            pl.BlockSpec((tm, DT_PAD), row),
        ],
        out_shape=[
            jax.ShapeDtypeStruct((t, ATTN_WIDTH), BF16),
            jax.ShapeDtypeStruct((t, KV_COLS), BF16),
            jax.ShapeDtypeStruct((t, KV_COLS), BF16),
            jax.ShapeDtypeStruct((t, SSD_INNER), F32),
            jax.ShapeDtypeStruct((t, XBC_COLS), F32),
            jax.ShapeDtypeStruct((t, DT_PAD), F32),
        ],
        compiler_params=_params("parallel"),
        name="inproj",
    )(x2, norm_w, w_in_pad)


def _t5_bucket(dist):
    n = jnp.maximum(dist, 0)
    max_exact = N_BUCKETS // 2
    nf = jnp.maximum(n, 1).astype(F32)
    large = max_exact + (jnp.log(nf / max_exact) / math.log(MAX_DISTANCE / max_exact)
                         * (N_BUCKETS - max_exact)).astype(jnp.int32)
    large = jnp.minimum(large, N_BUCKETS - 1)
    return jnp.where(n < max_exact, n, large)


def _attn_kernel(bucket_ref, relb_ref, sink_ref, q_ref, kp_ref, kc_ref, vp_ref, vc_ref,
                 o_ref, bias_ref):
    blk = ATTN_BLOCK
    rows = ATTN_GROUP * blk
    n = pl.program_id(1)

    @pl.when((pl.program_id(0) == 0) & (n == 0))
    def _():
        bucket = bucket_ref[...]
        for h in range(ATTN_HEADS):
            acc = jnp.zeros((blk, blk), F32)
            for b in range(N_BUCKETS):
                acc = jnp.where(bucket == b, relb_ref[b, h], acc)
            hk, g = divmod(h, ATTN_GROUP)
            bias_ref[hk, g * blk:(g + 1) * blk, :] = acc

    qi = lax.broadcasted_iota(jnp.int32, (rows, blk), 0) & (blk - 1)
    kj = lax.broadcasted_iota(jnp.int32, (rows, blk), 1)
    own = kj <= qi
    has_prev = n > 0
    for hk in range(ATTN_KV_HEADS):
        ksl = slice(hk * HEAD_DIM, (hk + 1) * HEAD_DIM)
        kband = jnp.concatenate([kp_ref[:, ksl], kc_ref[:, ksl]], axis=0)
        vband = jnp.concatenate([vp_ref[:, ksl], vc_ref[:, ksl]], axis=0)
        qg = jnp.concatenate(
            [q_ref[:, (hk * ATTN_GROUP + g) * HEAD_DIM:(hk * ATTN_GROUP + g + 1) * HEAD_DIM]
             for g in range(ATTN_GROUP)], axis=0)
        s2 = _dot_nt(qg, kband)
        s = jnp.where(own, s2[:, blk:], s2[:, :blk]) + bias_ref[hk]
        s = jnp.where(own | has_prev, s, -jnp.inf)
        sink = jnp.concatenate(
            [jnp.full((blk, 1), sink_ref[hk * ATTN_GROUP + g], F32) for g in range(ATTN_GROUP)],
            axis=0)
        m = jnp.maximum(jnp.max(s, axis=-1, keepdims=True), sink)
        p = jnp.exp(s - m)
        denom = jnp.sum(p, axis=-1, keepdims=True) + jnp.exp(sink - m)
        pb = p.astype(BF16)
        zero = jnp.zeros_like(pb)
        pfull = jnp.concatenate([jnp.where(own, zero, pb), jnp.where(own, pb, zero)], axis=1)
        o = _dot(pfull, vband) / denom
        for g in range(ATTN_GROUP):
            h = hk * ATTN_GROUP + g
            o_ref[:, h * HEAD_DIM:(h + 1) * HEAD_DIM] = o[g * blk:(g + 1) * blk].astype(BF16)


def _attention(q, k, v, bucket, rel_bias, sinks):
    b, s, _ = q.shape
    nb = s // ATTN_BLOCK
    cur = lambda i, j: (i, j, 0)
    prev = lambda i, j: (i, jnp.maximum(j - 1, 0), 0)
    smem = pl.BlockSpec(memory_space=pltpu.SMEM)
    return pl.pallas_call(
        _attn_kernel,
        grid=(b, nb),
        in_specs=[
            pl.BlockSpec((ATTN_BLOCK, ATTN_BLOCK), lambda i, j: (0, 0)),
            smem, smem,
            pl.BlockSpec((None, ATTN_BLOCK, ATTN_WIDTH), cur),
            pl.BlockSpec((None, ATTN_BLOCK, KV_COLS), prev),
            pl.BlockSpec((None, ATTN_BLOCK, KV_COLS), cur),
            pl.BlockSpec((None, ATTN_BLOCK, KV_COLS), prev),
            pl.BlockSpec((None, ATTN_BLOCK, KV_COLS), cur),
        ],
        out_specs=pl.BlockSpec((None, ATTN_BLOCK, ATTN_WIDTH), cur),
        out_shape=jax.ShapeDtypeStruct((b, s, ATTN_WIDTH), BF16),
        scratch_shapes=[pltpu.VMEM((ATTN_KV_HEADS, ATTN_GROUP * ATTN_BLOCK, ATTN_BLOCK), F32)],
        compiler_params=_params("arbitrary", "arbitrary"),
        name="swa",
    )(bucket, rel_bias, sinks, q, k, k, v, v)


def _ssd_kernel(xbc_ref, z_ref, dt_ref, convw_ref, convb_ref, dtb_ref, alog_ref, dskip_ref,
                nw_ref, expand_ref, tri_ref, o_ref, xpad_ref, state_ref):
    L = SSD_CHUNK
    P = SSD_HEAD_DIM
    c = pl.program_id(1)

    @pl.when(c == 0)
    def _():
        xpad_ref[0:CONV_HALO, :] = jnp.zeros((CONV_HALO, XBC_COLS), F32)
        state_ref[...] = jnp.zeros_like(state_ref)

    @pl.when(c > 0)
    def _():
        xpad_ref[0:CONV_HALO, :] = xpad_ref[L:L + CONV_HALO, :]

    xpad_ref[CONV_HALO:CONV_HALO + L, :] = xbc_ref[...]
    conv = convb_ref[...]
    for j in range(SSD_CONV):
        off = CONV_HALO - (SSD_CONV - 1) + j
        conv = conv + convw_ref[j:j + 1, :] * xpad_ref[off:off + L, :]
    xbc = _silu(conv)
    xs = xbc[:, :SSD_INNER]
    bm = xbc[:, SSD_INNER:SSD_INNER + SSD_GROUPS * SSD_STATE].astype(BF16)
    cm = xbc[:, SSD_INNER + SSD_GROUPS * SSD_STATE:].astype(BF16)

    expand = expand_ref[...]
    dt = _softplus(dt_ref[...] + dtb_ref[...])
    a = dt * (-jnp.exp(alog_ref[...]))
    a_cum = _dot(tri_ref[...], a, HIGHEST)
    a_last = a_cum[L - 1:L, :]
    x_dt = xs * _dot(dt, expand, HIGHEST)
    x_end = (x_dt * _dot(jnp.exp(a_last - a_cum), expand, HIGHEST)).astype(BF16)
    dec_start = _dot(jnp.exp(a_cum), expand, HIGHEST)
    x_dt = x_dt.astype(BF16)
    a_cum_t = a_cum.T

    li = lax.broadcasted_iota(jnp.int32, (L, L), 0)
    si = lax.broadcasted_iota(jnp.int32, (L, L), 1)
    causal = si <= li
    ys = []
    for g in range(SSD_GROUPS):
        bg = bm[:, g * SSD_STATE:(g + 1) * SSD_STATE]
        cg = cm[:, g * SSD_STATE:(g + 1) * SSD_STATE]
        cb = _dot_nt(cg, bg)
        for r in range(SSD_HEADS_PER_GROUP):
            h = g * SSD_HEADS_PER_GROUP + r
            sl = slice(h * P, (h + 1) * P)
            seg = a_cum[:, h:h + 1] - a_cum_t[h:h + 1, :]
            lmat = jnp.where(causal, jnp.exp(jnp.where(causal, seg, 0.0)), 0.0)
            h_prev = state_ref[sl, :]
            y_diag = _dot((cb * lmat).astype(BF16), x_dt[:, sl])
            y_off = _dot_nt(cg, h_prev.astype(BF16)) * dec_start[:, sl]
            ys.append(y_diag + y_off)
            chunk_decay = jnp.exp(a_cum_t[h:h + 1, L - 1:L])
            state_ref[sl, :] = h_prev * chunk_decay + _dot_tn(x_end[:, sl], bg)
    y = jnp.concatenate(ys, axis=1) + dskip_ref[...] * xs
    y = y * _silu(z_ref[...])
    o_ref[...] = _rms(y, nw_ref[...]).astype(BF16)


def _ssd(xbc, z, dt, conv_w, conv_b, dt_bias_pad, a_log_pad, d_skip_wide, norm_w, expand, tri):
    b, s, _ = xbc.shape
    nc = s // SSD_CHUNK
    cur = lambda i, j: (i, j, 0)
    fixed = lambda i, j: (0, 0)
    full = lambda shape: pl.BlockSpec(shape, fixed)
    return pl.pallas_call(
        _ssd_kernel,
        grid=(b, nc),
        in_specs=[
            pl.BlockSpec((None, SSD_CHUNK, XBC_COLS), cur),
            pl.BlockSpec((None, SSD_CHUNK, SSD_INNER), cur),
            pl.BlockSpec((None, SSD_CHUNK, DT_PAD), cur),
            full((SSD_CONV, XBC_COLS)), full((1, XBC_COLS)), full((1, DT_PAD)), full((1, DT_PAD)),
            full((1, SSD_INNER)), full((1, SSD_INNER)), full((DT_PAD, SSD_INNER)),
            full((SSD_CHUNK, SSD_CHUNK)),
        ],
        out_specs=pl.BlockSpec((None, SSD_CHUNK, SSD_INNER), cur),
        out_shape=jax.ShapeDtypeStruct((b, s, SSD_INNER), BF16),
        scratch_shapes=[
            pltpu.VMEM((CONV_HALO + SSD_CHUNK, XBC_COLS), F32),
            pltpu.VMEM((SSD_INNER, SSD_STATE), F32),
        ],
        compiler_params=_params("arbitrary", "arbitrary"),
        name="ssd",
    )(xbc, z, dt, conv_w, conv_b, dt_bias_pad, a_log_pad, d_skip_wide, norm_w, expand, tri)


def _outproj_kernel(x_ref, attn_ref, ssd_ref, wo_ref, nw_ref, wq_ref, h1_ref, xnt_ref, q_ref):
    mix = (_dot(attn_ref[...], wo_ref[0:ATTN_WIDTH, :])
           + _dot(ssd_ref[...], wo_ref[ATTN_WIDTH:ATTN_WIDTH + SSD_INNER, :]))
    h1 = x_ref[...] + mix
    h1_ref[...] = h1
    xn = _rms(h1, nw_ref[...])
    xnt_ref[...] = xn.T.astype(BF16)
    qf = _dot(xn.astype(BF16), wq_ref[...])
    for j in range(2 * PEER_HEADS):
        q_ref[j] = qf[:, j * PEER_HALF:(j + 1) * PEER_HALF].astype(BF16)


def _outproj(x2, attn, ssd, w_out, norm_w, w_query):
    t = x2.shape[0]
    tm = min(OUTPROJ_TM, t)
    row = lambda i: (i, 0)
    fixed = lambda i: (0, 0)
    nq = 2 * PEER_HEADS
    return pl.pallas_call(
        _outproj_kernel,
        grid=(t // tm,),
        in_specs=[
            pl.BlockSpec((tm, D_MODEL), row),
            pl.BlockSpec((tm, ATTN_WIDTH), row),
            pl.BlockSpec((tm, SSD_INNER), row),
            pl.BlockSpec((ATTN_WIDTH + SSD_INNER, D_MODEL), fixed),
            pl.BlockSpec((1, D_MODEL), fixed),
            pl.BlockSpec((D_MODEL, nq * PEER_HALF), fixed),
        ],
        out_specs=[
            pl.BlockSpec((tm, D_MODEL), row),
            pl.BlockSpec((D_MODEL, tm), lambda i: (0, i)),
            pl.BlockSpec((nq, tm, PEER_HALF), lambda i: (0, i, 0)),
        ],
        out_shape=[
            jax.ShapeDtypeStruct((t, D_MODEL), F32),
            jax.ShapeDtypeStruct((D_MODEL, t), BF16),
            jax.ShapeDtypeStruct((nq, t, PEER_HALF), BF16),
        ],
        compiler_params=_params("parallel"),
        name="outproj",
    )(x2, attn, ssd, w_out, norm_w, w_query)


def _top16(s):
    cur = s
    rank = jnp.full(s.shape, 255.0, F32)
    vals = []
    for k in range(PEER_TOPK):
        m = jnp.max(cur, axis=0, keepdims=True)
        hit = cur == m
        rank = jnp.where(hit, float(k), rank)
        cur = jnp.where(hit, -jnp.inf, cur)
        vals.append(m)
    return vals, rank


def _route_kernel(q_ref, k1_ref, k2_ref, r2_ref, e2_ref, c1_ref, e1_ref):
    def head(h, carry):
        s1 = _dot_nt(k1_ref[...], q_ref[2 * h])
        s2 = _dot_nt(k2_ref[...], q_ref[2 * h + 1])
        v1, r1 = _top16(s1)
        v2, r2 = _top16(s2)
        v2s = jnp.concatenate(v2, axis=0)
        cands = [v2s + v1[a] for a in range(PEER_TOPK)]
        cur = cands
        tau = None
        for k in range(PEER_TOPK):
            m = cur[0]
            for a in range(1, PEER_TOPK):
                m = jnp.maximum(m, cur[a])
            tau = jnp.max(m, axis=0, keepdims=True)
            if k + 1 < PEER_TOPK:
                cur = [jnp.where(x == tau, -jnp.inf, x) for x in cur]
        e2_top = jnp.exp(v2s - v2[0])
        zsum = jnp.zeros_like(tau)
        c1 = jnp.zeros(s1.shape, F32)
        for a in range(PEER_TOPK):
            sel = cands[a] >= tau
            cnt = jnp.sum(jnp.where(sel, 1.0, 0.0), axis=0, keepdims=True)
            w = jnp.exp(v1[a] - v1[0]) * e2_top
            zsum = zsum + jnp.sum(jnp.where(sel, w, 0.0), axis=0, keepdims=True)
            c1 = jnp.where(r1 == float(a), cnt, c1)
        r2_ref[h] = r2.astype(BF16)
        e2_ref[h] = jnp.exp(s2 - v2[0]).astype(BF16)
        c1_ref[h] = c1
        e1_ref[h] = jnp.exp(s1 - v1[0]) / zsum
        return carry

    lax.fori_loop(0, PEER_HEADS, head, 0)


def _route(q, keys1, keys2):
    nq, t, _ = q.shape
    tm = min(ROUTE_TM, t)
    fixed = lambda i: (0, 0)
    out_spec = pl.BlockSpec((PEER_HEADS, N_KEYS, tm), lambda i: (0, 0, i))
    shape = (PEER_HEADS, N_KEYS, t)
    return pl.pallas_call(
        _route_kernel,
        grid=(t // tm,),
        in_specs=[
            pl.BlockSpec((nq, tm, PEER_HALF), lambda i: (0, i, 0)),
            pl.BlockSpec((N_KEYS, PEER_HALF), fixed),
            pl.BlockSpec((N_KEYS, PEER_HALF), fixed),
        ],
        out_specs=[out_spec] * 4,
        out_shape=[jax.ShapeDtypeStruct(shape, BF16)] * 2 + [jax.ShapeDtypeStruct(shape, F32)] * 2,
        compiler_params=_params("parallel"),
        name="route",
    )(q, keys1, keys2)


def _peer_activations(pre_ref, act_ref, r2_ref, e2_ref, c1_ref, e1_ref, sub):
    tm = pre_ref.shape[1]
    chunks = N_KEYS // PEER_ROWS
    shape = (PEER_ROWS, PEER_LANE_CHUNK)
    zero = jnp.zeros(shape, BF16)
    for a in range(PEER_SUB // N_KEYS):
        key1 = sub * (PEER_SUB // N_KEYS) + a
        for lc in range(tm // PEER_LANE_CHUNK):
            lanes = slice(lc * PEER_LANE_CHUNK, (lc + 1) * PEER_LANE_CHUNK)
            gates = [zero] * chunks
            for h in range(PEER_HEADS):
                cnt = jnp.broadcast_to(c1_ref[h, key1:key1 + 1, lanes], shape).astype(BF16)
                e1 = jnp.broadcast_to(e1_ref[h, key1:key1 + 1, lanes], shape).astype(BF16)
                for c in range(chunks):
                    rows = slice(c * PEER_ROWS, (c + 1) * PEER_ROWS)
                    sel = r2_ref[h, rows, lanes] < cnt
                    gates[c] = gates[c] + jnp.where(sel, e2_ref[h, rows, lanes], zero) * e1
            for c in range(chunks):
                rows = slice(a * N_KEYS + c * PEER_ROWS, a * N_KEYS + (c + 1) * PEER_ROWS)
                half = 0.5 * pre_ref[rows, lanes]
                gelu = half + half * lax.erf(half * (2.0 ** 0.5))
                act_ref[rows, lanes] = gelu.astype(BF16) * gates[c]


def _peer_kernel(xnt_ref, d_ref, ut_ref, r2_ref, e2_ref, c1_ref, e1_ref, h1_ref, nw_ref,
                 o_ref, acc_ref, pre_ref, act_ref):
    j = pl.program_id(1)
    nsub = d_ref.shape[0] // PEER_SUB

    @pl.when(j == 0)
    def _():
        acc_ref[...] = jnp.zeros_like(acc_ref)

    def down(sub):
        pre_ref[sub % 2] = _dot(d_ref[sub * PEER_SUB:(sub + 1) * PEER_SUB, :], xnt_ref[...])

    down(0)
    for sub in range(nsub):
        if sub + 1 < nsub:
            down(sub + 1)
        _peer_activations(pre_ref.at[sub % 2], act_ref.at[sub % 2], r2_ref, e2_ref, c1_ref,
                          e1_ref, sub)
        acc_ref[...] += _dot(ut_ref[:, sub * PEER_SUB:(sub + 1) * PEER_SUB], act_ref[sub % 2])

    @pl.when(j == pl.num_programs(1) - 1)
    def _():
        h2 = h1_ref[...] + acc_ref[...].T
        o_ref[...] = _rms(h2, nw_ref[...])


def _peer(xnt, down, up_t, r2, e2, c1, e1, h1, norm_w):
    t = h1.shape[0]
    tm = min(PEER_TM, t)
    te = PEER_TE
    key2_spec = pl.BlockSpec((PEER_HEADS, N_KEYS, tm), lambda i, j: (0, 0, i))
    key1_spec = pl.BlockSpec((PEER_HEADS, te // N_KEYS, tm), lambda i, j: (0, j, i))
    return pl.pallas_call(
        _peer_kernel,
        grid=(t // tm, N_EXPERTS // te),
        in_specs=[
            pl.BlockSpec((D_MODEL, tm), lambda i, j: (0, i)),
            pl.BlockSpec((te, D_MODEL), lambda i, j: (j, 0)),
            pl.BlockSpec((D_MODEL, te), lambda i, j: (0, j)),
            key2_spec, key2_spec, key1_spec, key1_spec,
            pl.BlockSpec((tm, D_MODEL), lambda i, j: (i, 0)),
            pl.BlockSpec((1, D_MODEL), lambda i, j: (0, 0)),
        ],
        out_specs=pl.BlockSpec((tm, D_MODEL), lambda i, j: (i, 0)),
        out_shape=jax.ShapeDtypeStruct((t, D_MODEL), F32),
        scratch_shapes=[
            pltpu.VMEM((D_MODEL, tm), F32),
            pltpu.VMEM((2, PEER_SUB, tm), F32),
            pltpu.VMEM((2, PEER_SUB, tm), BF16),
        ],
        compiler_params=_params("parallel", "arbitrary"),
        name="peer",
    )(xnt, down, up_t, r2, e2, c1, e1, h1, norm_w)


def _pad_lanes(v, width):
    return jnp.pad(v.astype(F32), (0, width - v.shape[0])).reshape(1, width)


def _layer(x, norm_mix, w_in, conv_w, conv_b, dt_bias, a_log, d_skip, ssd_norm_w, attn_sinks,
           w_out, rel_bias, norm_ffn, w_query, sub_keys1, sub_keys2, expert_down, expert_up,
           out_norm):
    b, s, d = x.shape
    t = b * s
    x2 = x.reshape(t, d)

    w_in_pad = jnp.pad(w_in, ((0, 0), (0, IN_COLS_PAD - w_in.shape[1]))).astype(BF16)
    q, k, v, z, xbc, dt = _inproj(x2, norm_mix.reshape(1, d), w_in_pad)

    qi = jnp.arange(ATTN_BLOCK)[:, None]
    kj = jnp.arange(ATTN_BLOCK)[None, :]
    bucket = _t5_bucket(jnp.where(kj <= qi, qi - kj, qi + ATTN_BLOCK - kj)).astype(jnp.int32)
    attn = _attention(q.reshape(b, s, -1), k.reshape(b, s, -1), v.reshape(b, s, -1),
                      bucket, rel_bias.astype(F32), attn_sinks.astype(F32))

    lane = jnp.arange(SSD_INNER)[None, :] // SSD_HEAD_DIM
    expand = (jnp.arange(DT_PAD)[:, None] == lane).astype(F32)
    tri = (jnp.arange(SSD_CHUNK)[None, :] <= jnp.arange(SSD_CHUNK)[:, None]).astype(F32)
    ssd = _ssd(xbc.reshape(b, s, -1), z.reshape(b, s, -1), dt.reshape(b, s, -1),
               conv_w, conv_b.reshape(1, -1), _pad_lanes(dt_bias, DT_PAD),
               _pad_lanes(a_log, DT_PAD), jnp.repeat(d_skip.astype(F32), SSD_HEAD_DIM).reshape(1, -1),
               ssd_norm_w.reshape(1, -1), expand, tri)

    h1, xnt, pq = _outproj(x2, attn.reshape(t, -1), ssd.reshape(t, -1), w_out.astype(BF16),
                           norm_ffn.reshape(1, d), w_query.astype(BF16))
    r2, e2, c1, e1 = _route(pq, sub_keys1.astype(BF16), sub_keys2.astype(BF16))
    out = _peer(xnt, expert_down.astype(BF16), expert_up.T.astype(BF16), r2, e2, c1, e1, h1,
                out_norm.reshape(1, d))
    return out.reshape(b, s, d)


@jax.jit
def kernel(x, norm_mix, w_in, conv_w, conv_b, dt_bias, a_log, d_skip, ssd_norm_w, attn_sinks,
           w_out, rel_bias, norm_ffn, w_query, sub_keys1, sub_keys2, expert_down, expert_up,
           norm_final):
    assert norm_mix.shape[0] == 1, "single-layer block"
    return _layer(x, norm_mix[0], w_in[0], conv_w[0], conv_b[0], dt_bias[0], a_log[0], d_skip[0],
                  ssd_norm_w[0], attn_sinks[0], w_out[0], rel_bias, norm_ffn[0], w_query[0],
                  sub_keys1[0], sub_keys2[0], expert_down[0], expert_up[0], norm_final)
```

```python
import functools
import math

import jax
import jax.numpy as jnp
from jax import lax
from jax.experimental import pallas as pl
from jax.experimental.pallas import tpu as pltpu

F32 = jnp.float32
BF16 = jnp.bfloat16
HIGHEST = lax.Precision.HIGHEST

D_MODEL = 1024
ATTN_HEADS = 8
ATTN_KV_HEADS = 2
ATTN_GROUP = ATTN_HEADS // ATTN_KV_HEADS
HEAD_DIM = 64
ATTN_WIDTH = ATTN_HEADS * HEAD_DIM
ATTN_BLOCK = 128
N_BUCKETS = 32
MAX_DISTANCE = 128
SSD_HEADS = 8
SSD_HEAD_DIM = 64
SSD_INNER = SSD_HEADS * SSD_HEAD_DIM
SSD_STATE = 128
SSD_GROUPS = 2
SSD_HEADS_PER_GROUP = SSD_HEADS // SSD_GROUPS
SSD_CONV = 4
SSD_CHUNK = 128
PEER_HEADS = 8
N_KEYS = 128
N_EXPERTS = N_KEYS * N_KEYS
PEER_HALF = 128
PEER_TOPK = 16
EPS = 1e-6

KV_COLS = ATTN_KV_HEADS * HEAD_DIM
XBC_COLS = SSD_INNER + 2 * SSD_GROUPS * SSD_STATE
OFF_K = ATTN_WIDTH
OFF_V = OFF_K + KV_COLS
OFF_Z = OFF_V + KV_COLS
OFF_XBC = OFF_Z + SSD_INNER
OFF_DT = OFF_XBC + XBC_COLS
LANES = 128
DT_PAD = LANES
IN_COLS_PAD = OFF_DT + DT_PAD
CONV_HALO = 8

VMEM_LIMIT = 56 * 1024 * 1024

INPROJ_TM = 512
OUTPROJ_TM = 256
ROUTE_TM = 128
PEER_TM = 1024
PEER_TE = 1024
PEER_TOKEN_CHUNK = 512
FINAL_TM = 512
PEER_LANE_CHUNK = 256
PEER_ROWS = 16


def _dot(a, b, precision=None):
    return jnp.dot(a, b, preferred_element_type=F32, precision=precision)


def _dot_nt(a, b):
    return lax.dot_general(a, b, (((1,), (1,)), ((), ())), preferred_element_type=F32)


def _dot_tn(a, b):
    return lax.dot_general(a, b, (((0,), (0,)), ((), ())), preferred_element_type=F32)


def _rms(x, w):
    return x * lax.rsqrt(jnp.mean(x * x, axis=-1, keepdims=True) + EPS) * w


def _silu(x):
    return x / (1.0 + jnp.exp(-x))


def _softplus(x):
    return jnp.maximum(x, 0.0) + jnp.log1p(jnp.exp(-jnp.abs(x)))


def _params(*sem):
    return pltpu.CompilerParams(dimension_semantics=sem, vmem_limit_bytes=VMEM_LIMIT)


def _inproj_kernel(x_ref, nw_ref, w_ref, q_ref, k_ref, v_ref, z_ref, xbc_ref, dt_ref):
    xb = _rms(x_ref[...], nw_ref[...]).astype(BF16)

    def proj(lo, hi):
        return _dot(xb, w_ref[:, lo:hi])

    q_ref[...] = (proj(0, OFF_K) * (HEAD_DIM ** -0.5)).astype(BF16)
    k_ref[...] = proj(OFF_K, OFF_V).astype(BF16)
    v_ref[...] = proj(OFF_V, OFF_Z).astype(BF16)
    z_ref[...] = proj(OFF_Z, OFF_XBC)
    xbc_ref[...] = proj(OFF_XBC, OFF_DT)
    dt_ref[...] = proj(OFF_DT, IN_COLS_PAD)


def _inproj(x2, norm_w, w_in_pad):
    t = x2.shape[0]
    tm = min(INPROJ_TM, t)
    row = lambda i: (i, 0)
    fixed = lambda i: (0, 0)
    return pl.pallas_call(
        _inproj_kernel,
        grid=(t // tm,),
        in_specs=[
            pl.BlockSpec((tm, D_MODEL), row),
            pl.BlockSpec((1, D_MODEL), fixed),
            pl.BlockSpec((D_MODEL, IN_COLS_PAD), fixed),
        ],
        out_specs=[
            pl.BlockSpec((tm, ATTN_WIDTH), row),
            pl.BlockSpec((tm, KV_COLS), row),
            pl.BlockSpec((tm, KV_COLS), row),
            pl.BlockSpec((tm, SSD_INNER), row),
            pl.BlockSpec((tm, XBC_COLS), row),
            pl.BlockSpec((tm, DT_PAD), row),
        ],
        out_shape=[
            jax.ShapeDtypeStruct((t, ATTN_WIDTH), BF16),
            jax.ShapeDtypeStruct((t, KV_COLS), BF16),
            jax.ShapeDtypeStruct((t, KV_COLS), BF16),
            jax.ShapeDtypeStruct((t, SSD_INNER), F32),
            jax.ShapeDtypeStruct((t, XBC_COLS), F32),
            jax.ShapeDtypeStruct((t, DT_PAD), F32),
        ],
        compiler_params=_params("parallel"),
        name="inproj",
    )(x2, norm_w, w_in_pad)


def _t5_bucket(dist):
    n = jnp.maximum(dist, 0)
    max_exact = N_BUCKETS // 2
    nf = jnp.maximum(n, 1).astype(F32)
    large = max_exact + (jnp.log(nf / max_exact) / math.log(MAX_DISTANCE / max_exact)
                         * (N_BUCKETS - max_exact)).astype(jnp.int32)
    large = jnp.minimum(large, N_BUCKETS - 1)
    return jnp.where(n < max_exact, n, large)


def _attn_kernel(bucket_ref, relb_ref, sink_ref, q_ref, kp_ref, kc_ref, vp_ref, vc_ref,
                 o_ref, bias_ref):
    blk = ATTN_BLOCK
    rows = ATTN_GROUP * blk
    n = pl.program_id(1)

    @pl.when((pl.program_id(0) == 0) & (n == 0))
    def _():
        bucket = bucket_ref[...]
        for h in range(ATTN_HEADS):
            acc = jnp.zeros((blk, blk), F32)
            for b in range(N_BUCKETS):
                acc = jnp.where(bucket == b, relb_ref[b, h], acc)
            hk, g = divmod(h, ATTN_GROUP)
            bias_ref[hk, g * blk:(g + 1) * blk, :] = acc

    qi = lax.broadcasted_iota(jnp.int32, (rows, blk), 0) & (blk - 1)
    kj = lax.broadcasted_iota(jnp.int32, (rows, blk), 1)
    own = kj <= qi
    has_prev = n > 0
    for hk in range(ATTN_KV_HEADS):
        ksl = slice(hk * HEAD_DIM, (hk + 1) * HEAD_DIM)
        kband = jnp.concatenate([kp_ref[:, ksl], kc_ref[:, ksl]], axis=0)
        vband = jnp.concatenate([vp_ref[:, ksl], vc_ref[:, ksl]], axis=0)
        qg = jnp.concatenate(
            [q_ref[:, (hk * ATTN_GROUP + g) * HEAD_DIM:(hk * ATTN_GROUP + g + 1) * HEAD_DIM]
             for g in range(ATTN_GROUP)], axis=0)
        s2 = _dot_nt(qg, kband)
        s = jnp.where(own, s2[:, blk:], s2[:, :blk]) + bias_ref[hk]
        s = jnp.where(own | has_prev, s, -jnp.inf)
        sink = jnp.concatenate(
            [jnp.full((blk, 1), sink_ref[hk * ATTN_GROUP + g], F32) for g in range(ATTN_GROUP)],
            axis=0)
        m = jnp.maximum(jnp.max(s, axis=-1, keepdims=True), sink)
        p = jnp.exp(s - m)
        denom = jnp.sum(p, axis=-1, keepdims=True) + jnp.exp(sink - m)
        pb = p.astype(BF16)
        zero = jnp.zeros_like(pb)
        pfull = jnp.concatenate([jnp.where(own, zero, pb), jnp.where(own, pb, zero)], axis=1)
        o = _dot(pfull, vband) / denom
        for g in range(ATTN_GROUP):
            h = hk * ATTN_GROUP + g
            o_ref[:, h * HEAD_DIM:(h + 1) * HEAD_DIM] = o[g * blk:(g + 1) * blk].astype(BF16)


def _attention(q, k, v, bucket, rel_bias, sinks):
    b, s, _ = q.shape
    nb = s // ATTN_BLOCK
    cur = lambda i, j: (i, j, 0)
    prev = lambda i, j: (i, jnp.maximum(j - 1, 0), 0)
    smem = pl.BlockSpec(memory_space=pltpu.SMEM)
    return pl.pallas_call(
        _attn_kernel,
        grid=(b, nb),
        in_specs=[
            pl.BlockSpec((ATTN_BLOCK, ATTN_BLOCK), lambda i, j: (0, 0)),
            smem, smem,
            pl.BlockSpec((None, ATTN_BLOCK, ATTN_WIDTH), cur),
            pl.BlockSpec((None, ATTN_BLOCK, KV_COLS), prev),
            pl.BlockSpec((None, ATTN_BLOCK, KV_COLS), cur),
            pl.BlockSpec((None, ATTN_BLOCK, KV_COLS), prev),
            pl.BlockSpec((None, ATTN_BLOCK, KV_COLS), cur),
        ],
        out_specs=pl.BlockSpec((None, ATTN_BLOCK, ATTN_WIDTH), cur),
        out_shape=jax.ShapeDtypeStruct((b, s, ATTN_WIDTH), BF16),
        scratch_shapes=[pltpu.VMEM((ATTN_KV_HEADS, ATTN_GROUP * ATTN_BLOCK, ATTN_BLOCK), F32)],
        compiler_params=_params("arbitrary", "arbitrary"),
        name="swa",
    )(bucket, rel_bias, sinks, q, k, k, v, v)


def _ssd_kernel(xbc_ref, z_ref, dt_ref, convw_ref, convb_ref, dtb_ref, alog_ref, dskip_ref,
                nw_ref, expand_ref, tri_ref, o_ref, xpad_ref, state_ref):
    L = SSD_CHUNK
    P = SSD_HEAD_DIM
    c = pl.program_id(1)

    @pl.when(c == 0)
    def _():
        xpad_ref[0:CONV_HALO, :] = jnp.zeros((CONV_HALO, XBC_COLS), F32)
        state_ref[...] = jnp.zeros_like(state_ref)

    @pl.when(c > 0)
    def _():
        xpad_ref[0:CONV_HALO, :] = xpad_ref[L:L + CONV_HALO, :]

    xpad_ref[CONV_HALO:CONV_HALO + L, :] = xbc_ref[...]
    conv = convb_ref[...]
    for j in range(SSD_CONV):
        off = CONV_HALO - (SSD_CONV - 1) + j
        conv = conv + convw_ref[j:j + 1, :] * xpad_ref[off:off + L, :]
    xbc = _silu(conv)
    xs = xbc[:, :SSD_INNER]
    bm = xbc[:, SSD_INNER:SSD_INNER + SSD_GROUPS * SSD_STATE].astype(BF16)
    cm = xbc[:, SSD_INNER + SSD_GROUPS * SSD_STATE:].astype(BF16)

    expand = expand_ref[...]
    dt = _softplus(dt_ref[...] + dtb_ref[...])
    a = dt * (-jnp.exp(alog_ref[...]))
    a_cum = _dot(tri_ref[...], a, HIGHEST)
    a_last = a_cum[L - 1:L, :]
    x_dt = xs * _dot(dt, expand, HIGHEST)
    x_end = (x_dt * _dot(jnp.exp(a_last - a_cum), expand, HIGHEST)).astype(BF16)
    dec_start = _dot(jnp.exp(a_cum), expand, HIGHEST)
    x_dt = x_dt.astype(BF16)
    a_cum_t = a_cum.T

    li = lax.broadcasted_iota(jnp.int32, (L, L), 0)
    si = lax.broadcasted_iota(jnp.int32, (L, L), 1)
    causal = si <= li
    ys = []
    for g in range(SSD_GROUPS):
        bg = bm[:, g * SSD_STATE:(g + 1) * SSD_STATE]
        cg = cm[:, g * SSD_STATE:(g + 1) * SSD_STATE]
        cb = _dot_nt(cg, bg)
        for r in range(SSD_HEADS_PER_GROUP):
            h = g * SSD_HEADS_PER_GROUP + r
            sl = slice(h * P, (h + 1) * P)
            seg = a_cum[:, h:h + 1] - a_cum_t[h:h + 1, :]
            lmat = jnp.where(causal, jnp.exp(jnp.where(causal, seg, 0.0)), 0.0)
            h_prev = state_ref[sl, :]
            y_diag = _dot((cb * lmat).astype(BF16), x_dt[:, sl])
            y_off = _dot_nt(cg, h_prev.astype(BF16)) * dec_start[:, sl]
            ys.append(y_diag + y_off)
            chunk_decay = jnp.exp(a_cum_t[h:h + 1, L - 1:L])
            state_ref[sl, :] = h_prev * chunk_decay + _dot_tn(x_end[:, sl], bg)
    y = jnp.concatenate(ys, axis=1) + dskip_ref[...] * xs
    y = y * _silu(z_ref[...])
    o_ref[...] = _rms(y, nw_ref[...]).astype(BF16)


def _ssd(xbc, z, dt, conv_w, conv_b, dt_bias_pad, a_log_pad, d_skip_wide, norm_w, expand, tri):
    b, s, _ = xbc.shape
    nc = s // SSD_CHUNK
    cur = lambda i, j: (i, j, 0)
    fixed = lambda i, j: (0, 0)
    full = lambda shape: pl.BlockSpec(shape, fixed)
    return pl.pallas_call(
        _ssd_kernel,
        grid=(b, nc),
        in_specs=[
            pl.BlockSpec((None, SSD_CHUNK, XBC_COLS), cur),
            pl.BlockSpec((None, SSD_CHUNK, SSD_INNER), cur),
            pl.BlockSpec((None, SSD_CHUNK, DT_PAD), cur),
            full((SSD_CONV, XBC_COLS)), full((1, XBC_COLS)), full((1, DT_PAD)), full((1, DT_PAD)),
            full((1, SSD_INNER)), full((1, SSD_INNER)), full((DT_PAD, SSD_INNER)),
            full((SSD_CHUNK, SSD_CHUNK)),
        ],
        out_specs=pl.BlockSpec((None, SSD_CHUNK, SSD_INNER), cur),
        out_shape=jax.ShapeDtypeStruct((b, s, SSD_INNER), BF16),
        scratch_shapes=[
            pltpu.VMEM((CONV_HALO + SSD_CHUNK, XBC_COLS), F32),
            pltpu.VMEM((SSD_INNER, SSD_STATE), F32),
        ],
        compiler_params=_params("arbitrary", "arbitrary"),
        name="ssd",
    )(xbc, z, dt, conv_w, conv_b, dt_bias_pad, a_log_pad, d_skip_wide, norm_w, expand, tri)


def _outproj_kernel(x_ref, attn_ref, ssd_ref, wo_ref, nw_ref, wq_ref, h1_ref, xnt_ref, q_ref):
    mix = (_dot(attn_ref[...], wo_ref[0:ATTN_WIDTH, :])
           + _dot(ssd_ref[...], wo_ref[ATTN_WIDTH:ATTN_WIDTH + SSD_INNER, :]))
    h1 = x_ref[...] + mix
    h1_ref[...] = h1
    xn = _rms(h1, nw_ref[...])
    xnt_ref[...] = pltpu.bitcast(xn.T.astype(BF16), jnp.uint32)
    qf = _dot(xn.astype(BF16), wq_ref[...])
    for j in range(2 * PEER_HEADS):
        q_ref[j] = qf[:, j * PEER_HALF:(j + 1) * PEER_HALF].astype(BF16)


def _outproj(x2, attn, ssd, w_out, norm_w, w_query):
    t = x2.shape[0]
    tm = min(OUTPROJ_TM, t)
    row = lambda i: (i, 0)
    fixed = lambda i: (0, 0)
    nq = 2 * PEER_HEADS
    return pl.pallas_call(
        _outproj_kernel,
        grid=(t // tm,),
        in_specs=[
            pl.BlockSpec((tm, D_MODEL), row),
            pl.BlockSpec((tm, ATTN_WIDTH), row),
            pl.BlockSpec((tm, SSD_INNER), row),
            pl.BlockSpec((ATTN_WIDTH + SSD_INNER, D_MODEL), fixed),
            pl.BlockSpec((1, D_MODEL), fixed),
            pl.BlockSpec((D_MODEL, nq * PEER_HALF), fixed),
        ],
        out_specs=[
            pl.BlockSpec((tm, D_MODEL), row),
            pl.BlockSpec((D_MODEL // 2, tm), lambda i: (0, i)),
            pl.BlockSpec((nq, tm, PEER_HALF), lambda i: (0, i, 0)),
        ],
        out_shape=[
            jax.ShapeDtypeStruct((t, D_MODEL), F32),
            jax.ShapeDtypeStruct((D_MODEL // 2, t), jnp.uint32),
            jax.ShapeDtypeStruct((nq, t, PEER_HALF), BF16),
        ],
        compiler_params=_params("parallel"),
        name="outproj",
    )(x2, attn, ssd, w_out, norm_w, w_query)


def _top16(s):
    cur = s
    rank = jnp.full(s.shape, 255.0, F32)
    vals = []
    for k in range(PEER_TOPK):
        m = jnp.max(cur, axis=0, keepdims=True)
        hit = cur == m
        rank = jnp.where(hit, float(k), rank)
        cur = jnp.where(hit, -jnp.inf, cur)
        vals.append(m)
    return vals, rank


def _route_kernel(q_ref, k1_ref, k2_ref, r2_ref, e2_ref, c1_ref, e1_ref):
    def head(h, carry):
        s1 = _dot_nt(k1_ref[...], q_ref[2 * h])
        s2 = _dot_nt(k2_ref[...], q_ref[2 * h + 1])
        v1, r1 = _top16(s1)
        v2, r2 = _top16(s2)
        v2s = jnp.concatenate(v2, axis=0)
        cands = [v2s + v1[a] for a in range(PEER_TOPK)]
        cur = cands
        tau = None
        for k in range(PEER_TOPK):
            m = cur[0]
            for a in range(1, PEER_TOPK):
                m = jnp.maximum(m, cur[a])
            tau = jnp.max(m, axis=0, keepdims=True)
            if k + 1 < PEER_TOPK:
                cur = [jnp.where(x == tau, -jnp.inf, x) for x in cur]
        e2_top = jnp.exp(v2s - v2[0])
        zsum = jnp.zeros_like(tau)
        c1 = jnp.zeros(s1.shape, F32)
        for a in range(PEER_TOPK):
            sel = cands[a] >= tau
            cnt = jnp.sum(jnp.where(sel, 1.0, 0.0), axis=0, keepdims=True)
            w = jnp.exp(v1[a] - v1[0]) * e2_top
            zsum = zsum + jnp.sum(jnp.where(sel, w, 0.0), axis=0, keepdims=True)
            c1 = jnp.where(r1 == float(a), cnt, c1)
        r2_ref[h] = r2.astype(BF16)
        e2_ref[h] = jnp.exp(s2 - v2[0]).astype(BF16)
        c1_ref[h] = c1
        e1_ref[h] = (0.5 * jnp.exp(s1 - v1[0])) / zsum
        return carry

    lax.fori_loop(0, PEER_HEADS, head, 0)


def _route(q, keys1, keys2):
    nq, t, _ = q.shape
    tm = min(ROUTE_TM, t)
    fixed = lambda i: (0, 0)
    out_spec = pl.BlockSpec((PEER_HEADS, N_KEYS, tm), lambda i: (0, 0, i))
    shape = (PEER_HEADS, N_KEYS, t)
    return pl.pallas_call(
        _route_kernel,
        grid=(t // tm,),
        in_specs=[
            pl.BlockSpec((nq, tm, PEER_HALF), lambda i: (0, i, 0)),
            pl.BlockSpec((N_KEYS, PEER_HALF), fixed),
            pl.BlockSpec((N_KEYS, PEER_HALF), fixed),
        ],
        out_specs=[out_spec] * 4,
        out_shape=[jax.ShapeDtypeStruct(shape, BF16)] * 2 + [jax.ShapeDtypeStruct(shape, F32)] * 2,
        compiler_params=_params("parallel"),
        name="route",
    )(q, keys1, keys2)


def _peer_activations(pre_ref, act_ref, r2_ref, e2_ref, c1_ref, e1_ref, lane0):
    chunks = N_KEYS // PEER_ROWS
    shape = (PEER_ROWS, PEER_LANE_CHUNK)
    zero = jnp.zeros(shape, BF16)
    for lc in range(pre_ref.shape[1] // PEER_LANE_CHUNK):
        own = slice(lc * PEER_LANE_CHUNK, (lc + 1) * PEER_LANE_CHUNK)
        lanes = slice(lane0 + own.start, lane0 + own.stop)
        for key1 in range(pre_ref.shape[0] // N_KEYS):
            gates = [zero] * chunks
            for h in range(PEER_HEADS):
                cnt = jnp.broadcast_to(c1_ref[h, key1:key1 + 1, lanes], shape).astype(BF16)
                e1 = jnp.broadcast_to(e1_ref[h, key1:key1 + 1, lanes], shape).astype(BF16)
                for c in range(chunks):
                    rows = slice(c * PEER_ROWS, (c + 1) * PEER_ROWS)
                    sel = r2_ref[h, rows, lanes] < cnt
                    gates[c] = gates[c] + jnp.where(sel, e2_ref[h, rows, lanes], zero) * e1
            for c in range(chunks):
                rows = slice(key1 * N_KEYS + c * PEER_ROWS, key1 * N_KEYS + (c + 1) * PEER_ROWS)
                p = pre_ref[rows, own]
                gelu2 = p + p * lax.erf(p * (2.0 ** -0.5))
                act_ref[rows, own] = gelu2.astype(BF16) * gates[c]


def _peer_kernel(xnt_ref, d_ref, ut_ref, r2_ref, e2_ref, c1_ref, e1_ref, o_ref, *scratch,
                 tiles_per_token_block):
    s = pl.program_id(0)
    nchunks = len(scratch) // 4
    pre = (scratch[:nchunks], scratch[2 * nchunks:3 * nchunks])
    act = (scratch[nchunks:2 * nchunks], scratch[3 * nchunks:])

    @pl.when(s == 0)
    def _():
        for ref in pre[1] + act[0]:
            ref[...] = jnp.zeros_like(ref)

    @pl.when((s == 0) | ((s >= 2) & ((s - 2) % tiles_per_token_block == 0)))
    def _():
        o_ref[...] = jnp.zeros_like(o_ref)

    def body(parity):
        for n in range(nchunks):
            cols = slice(n * PEER_TOKEN_CHUNK, (n + 1) * PEER_TOKEN_CHUNK)
            pre[parity][n][...] = _dot(pltpu.bitcast(d_ref[...], BF16),
                                       pltpu.bitcast(xnt_ref[:, cols], BF16))
            _peer_activations(pre[1 - parity][n], act[1 - parity][n], r2_ref, e2_ref, c1_ref,
                              e1_ref, n * PEER_TOKEN_CHUNK)
            o_ref[:, cols] += _dot(pltpu.bitcast(ut_ref[...], BF16), act[parity][n][...])

    for parity in range(2):
        pl.when(s % 2 == parity)(functools.partial(body, parity))


def _pack_rows(x):
    *lead, rows, cols = x.shape
    pairs = x.reshape(*lead, rows // 2, 2, cols)
    return lax.bitcast_convert_type(jnp.swapaxes(pairs, -1, -2), jnp.uint32)


def _peer(xnt, down, up_t, r2, e2, c1, e1):
    t = xnt.shape[1]
    tm = min(PEER_TM, t)
    te = PEER_TE
    nj = N_EXPERTS // te
    steps = (t // tm) * nj
    down_tile = lambda s: jnp.minimum(s, steps - 1)
    act_tile = lambda s: jnp.clip(s - 1, 0, steps - 1)
    up_tile = lambda s: jnp.clip(s - 2, 0, steps - 1)
    key2_spec = pl.BlockSpec((PEER_HEADS, N_KEYS, tm), lambda s: (0, 0, act_tile(s) // nj))
    key1_spec = pl.BlockSpec((PEER_HEADS, te // N_KEYS, tm),
                             lambda s: (0, act_tile(s) % nj, act_tile(s) // nj))
    return pl.pallas_call(
        functools.partial(_peer_kernel, tiles_per_token_block=nj),
        grid=(steps + 2,),
        in_specs=[
            pl.BlockSpec((D_MODEL // 2, tm), lambda s: (0, down_tile(s) // nj)),
            pl.BlockSpec((te // 2, D_MODEL), lambda s: (down_tile(s) % nj, 0)),
            pl.BlockSpec((None, D_MODEL // 2, te), lambda s: (up_tile(s) % nj, 0, 0)),
            key2_spec, key2_spec, key1_spec, key1_spec,
        ],
        out_specs=pl.BlockSpec((D_MODEL, tm), lambda s: (0, up_tile(s) // nj)),
        out_shape=jax.ShapeDtypeStruct((D_MODEL, t), F32),
        scratch_shapes=2 * (
            [pltpu.VMEM((te, PEER_TOKEN_CHUNK), F32)] * (tm // PEER_TOKEN_CHUNK)
            + [pltpu.VMEM((te, PEER_TOKEN_CHUNK), BF16)] * (tm // PEER_TOKEN_CHUNK)),
        compiler_params=_params("arbitrary"),
        name="peer",
    )(xnt, down, up_t, r2, e2, c1, e1)


def _final_kernel(h1_ref, peer_t_ref, nw_ref, o_ref):
    o_ref[...] = _rms(h1_ref[...] + peer_t_ref[...].T, nw_ref[...])


def _final(h1, peer_t, norm_w):
    t = h1.shape[0]
    tm = min(FINAL_TM, t)
    return pl.pallas_call(
        _final_kernel,
        grid=(t // tm,),
        in_specs=[
            pl.BlockSpec((tm, D_MODEL), lambda i: (i, 0)),
            pl.BlockSpec((D_MODEL, tm), lambda i: (0, i)),
            pl.BlockSpec((1, D_MODEL), lambda i: (0, 0)),
        ],
        out_specs=pl.BlockSpec((tm, D_MODEL), lambda i: (i, 0)),
        out_shape=jax.ShapeDtypeStruct((t, D_MODEL), F32),
        compiler_params=_params("parallel"),
        name="final",
    )(h1, peer_t, norm_w)


def _pad_lanes(v, width):
    return jnp.pad(v.astype(F32), (0, width - v.shape[0])).reshape(1, width)


def _layer(x, norm_mix, w_in, conv_w, conv_b, dt_bias, a_log, d_skip, ssd_norm_w, attn_sinks,
           w_out, rel_bias, norm_ffn, w_query, sub_keys1, sub_keys2, expert_down, expert_up,
           out_norm):
    b, s, d = x.shape
    t = b * s
    x2 = x.reshape(t, d)

    w_in_pad = jnp.pad(w_in, ((0, 0), (0, IN_COLS_PAD - w_in.shape[1]))).astype(BF16)
    q, k, v, z, xbc, dt = _inproj(x2, norm_mix.reshape(1, d), w_in_pad)

    qi = jnp.arange(ATTN_BLOCK)[:, None]
    kj = jnp.arange(ATTN_BLOCK)[None, :]
    bucket = _t5_bucket(jnp.where(kj <= qi, qi - kj, qi + ATTN_BLOCK - kj)).astype(jnp.int32)
    attn = _attention(q.reshape(b, s, -1), k.reshape(b, s, -1), v.reshape(b, s, -1),
                      bucket, rel_bias.astype(F32), attn_sinks.astype(F32))

    lane = jnp.arange(SSD_INNER)[None, :] // SSD_HEAD_DIM
    expand = (jnp.arange(DT_PAD)[:, None] == lane).astype(F32)
    tri = (jnp.arange(SSD_CHUNK)[None, :] <= jnp.arange(SSD_CHUNK)[:, None]).astype(F32)
    ssd = _ssd(xbc.reshape(b, s, -1), z.reshape(b, s, -1), dt.reshape(b, s, -1),
               conv_w, conv_b.reshape(1, -1), _pad_lanes(dt_bias, DT_PAD),
               _pad_lanes(a_log, DT_PAD), jnp.repeat(d_skip.astype(F32), SSD_HEAD_DIM).reshape(1, -1),
               ssd_norm_w.reshape(1, -1), expand, tri)

    h1, xnt, pq = _outproj(x2, attn.reshape(t, -1), ssd.reshape(t, -1), w_out.astype(BF16),
                           norm_ffn.reshape(1, d), w_query.astype(BF16))
    r2, e2, c1, e1 = _route(pq, sub_keys1.astype(BF16), sub_keys2.astype(BF16))
    up_t = expert_up.astype(BF16).reshape(N_EXPERTS // PEER_TE, PEER_TE, d).transpose(0, 2, 1)
    peer_t = _peer(xnt, _pack_rows(expert_down.astype(BF16)), _pack_rows(up_t), r2, e2, c1, e1)
    return _final(h1, peer_t, out_norm.reshape(1, d)).reshape(b, s, d)


@jax.jit
def kernel(x, norm_mix, w_in, conv_w, conv_b, dt_bias, a_log, d_skip, ssd_norm_w, attn_sinks,
           w_out, rel_bias, norm_ffn, w_query, sub_keys1, sub_keys2, expert_down, expert_up,
           norm_final):
    assert norm_mix.shape[0] == 1, "single-layer block"
    return _layer(x, norm_mix[0], w_in[0], conv_w[0], conv_b[0], dt_bias[0], a_log[0], d_skip[0],
                  ssd_norm_w[0], attn_sinks[0], w_out[0], rel_bias, norm_ffn[0], w_query[0],
                  sub_keys1[0], sub_keys2[0], expert_down[0], expert_up[0], norm_final)
```

```python
import functools
import math

import jax
import jax.numpy as jnp
from jax import lax
from jax.experimental import pallas as pl
from jax.experimental.pallas import tpu as pltpu

F32 = jnp.float32
BF16 = jnp.bfloat16
HIGHEST = lax.Precision.HIGHEST

D_MODEL = 1024
ATTN_HEADS = 8
ATTN_KV_HEADS = 2
ATTN_GROUP = ATTN_HEADS // ATTN_KV_HEADS
HEAD_DIM = 64
ATTN_WIDTH = ATTN_HEADS * HEAD_DIM
ATTN_BLOCK = 128
N_BUCKETS = 32
MAX_DISTANCE = 128
SSD_HEADS = 8
SSD_HEAD_DIM = 64
SSD_INNER = SSD_HEADS * SSD_HEAD_DIM
SSD_STATE = 128
SSD_GROUPS = 2
SSD_HEADS_PER_GROUP = SSD_HEADS // SSD_GROUPS
SSD_CONV = 4
SSD_CHUNK = 128
PEER_HEADS = 8
N_KEYS = 128
N_EXPERTS = N_KEYS * N_KEYS
PEER_HALF = 128
PEER_TOPK = 16
EPS = 1e-6

KV_COLS = ATTN_KV_HEADS * HEAD_DIM
XBC_COLS = SSD_INNER + 2 * SSD_GROUPS * SSD_STATE
OFF_K = ATTN_WIDTH
OFF_V = OFF_K + KV_COLS
OFF_Z = OFF_V + KV_COLS
OFF_XBC = OFF_Z + SSD_INNER
OFF_DT = OFF_XBC + XBC_COLS
LANES = 128
SUBLANES = 8
DT_PAD = LANES
IN_COLS_PAD = OFF_DT + DT_PAD
CONV_HALO = SUBLANES

VMEM_LIMIT = 56 * 1024 * 1024

INPROJ_TM = 512
OUTPROJ_TM = 256
ROUTE_TM = SUBLANES * LANES
PEER_TM = 1024
PEER_TE = 1024
PEER_SUB = 256
PEER_LANE_CHUNK = 512
PEER_ROWS = 16


def _dot(a, b, precision=None):
    return jnp.dot(a, b, preferred_element_type=F32, precision=precision)


def _dot_nt(a, b):
    return lax.dot_general(a, b, (((1,), (1,)), ((), ())), preferred_element_type=F32)


def _dot_tn(a, b):
    return lax.dot_general(a, b, (((0,), (0,)), ((), ())), preferred_element_type=F32)


def _rms(x, w):
    return x * lax.rsqrt(jnp.mean(x * x, axis=-1, keepdims=True) + EPS) * w


def _silu(x):
    return x / (1.0 + jnp.exp(-x))


def _softplus(x):
    return jnp.maximum(x, 0.0) + jnp.log1p(jnp.exp(-jnp.abs(x)))


def _params(*sem):
    return pltpu.CompilerParams(dimension_semantics=sem, vmem_limit_bytes=VMEM_LIMIT)


def _inproj_kernel(x_ref, nw_ref, w_ref, q_ref, k_ref, v_ref, z_ref, xbc_ref, dt_ref):
    xb = _rms(x_ref[...], nw_ref[...]).astype(BF16)

    def proj(lo, hi):
        return _dot(xb, w_ref[:, lo:hi])

    q_ref[...] = (proj(0, OFF_K) * (HEAD_DIM ** -0.5)).astype(BF16)
    k_ref[...] = proj(OFF_K, OFF_V).astype(BF16)
    v_ref[...] = proj(OFF_V, OFF_Z).astype(BF16)
    z_ref[...] = proj(OFF_Z, OFF_XBC)
    xbc_ref[...] = proj(OFF_XBC, OFF_DT)
    dt_ref[...] = proj(OFF_DT, IN_COLS_PAD)


def _inproj(x2, norm_w, w_in_pad):
    t = x2.shape[0]
    tm = min(INPROJ_TM, t)
    row = lambda i: (i, 0)
    fixed = lambda i: (0, 0)
    return pl.pallas_call(
        _inproj_kernel,
        grid=(t // tm,),
        in_specs=[
            pl.BlockSpec((tm, D_MODEL), row),
            pl.BlockSpec((1, D_MODEL), fixed),
            pl.BlockSpec((D_MODEL, IN_COLS_PAD), fixed),
        ],
        out_specs=[
            pl.BlockSpec((tm, ATTN_WIDTH), row),
            pl.BlockSpec((tm, KV_COLS), row),
            pl.BlockSpec((tm, KV_COLS), row),
            pl.BlockSpec((tm, SSD_INNER), row),
            pl.BlockSpec((tm, XBC_COLS), row),
            pl.BlockSpec((tm, DT_PAD), row),
        ],
        out_shape=[
            jax.ShapeDtypeStruct((t, ATTN_WIDTH), BF16),
            jax.ShapeDtypeStruct((t, KV_COLS), BF16),
            jax.ShapeDtypeStruct((t, KV_COLS), BF16),
            jax.ShapeDtypeStruct((t, SSD_INNER), F32),
            jax.ShapeDtypeStruct((t, XBC_COLS), F32),
            jax.ShapeDtypeStruct((t, DT_PAD), F32),
        ],
        compiler_params=_params("parallel"),
        name="inproj",
    )(x2, norm_w, w_in_pad)


def _t5_bucket(dist):
    n = jnp.maximum(dist, 0)
    max_exact = N_BUCKETS // 2
    nf = jnp.maximum(n, 1).astype(F32)
    large = max_exact + (jnp.log(nf / max_exact) / math.log(MAX_DISTANCE / max_exact)
                         * (N_BUCKETS - max_exact)).astype(jnp.int32)
    large = jnp.minimum(large, N_BUCKETS - 1)
    return jnp.where(n < max_exact, n, large)


def _attn_kernel(bucket_ref, relb_ref, sink_ref, q_ref, kp_ref, kc_ref, vp_ref, vc_ref,
                 o_ref, bias_ref):
    blk = ATTN_BLOCK
    rows = ATTN_GROUP * blk
    n = pl.program_id(1)

    @pl.when((pl.program_id(0) == 0) & (n == 0))
    def _():
        bucket = bucket_ref[...]
        for h in range(ATTN_HEADS):
            acc = jnp.zeros((blk, blk), F32)
            for b in range(N_BUCKETS):
                acc = jnp.where(bucket == b, relb_ref[b, h], acc)
            hk, g = divmod(h, ATTN_GROUP)
            bias_ref[hk, g * blk:(g + 1) * blk, :] = acc

    qi = lax.broadcasted_iota(jnp.int32, (rows, blk), 0) & (blk - 1)
    kj = lax.broadcasted_iota(jnp.int32, (rows, blk), 1)
    own = kj <= qi
    has_prev = n > 0
    for hk in range(ATTN_KV_HEADS):
        ksl = slice(hk * HEAD_DIM, (hk + 1) * HEAD_DIM)
        kband = jnp.concatenate([kp_ref[:, ksl], kc_ref[:, ksl]], axis=0)
        vband = jnp.concatenate([vp_ref[:, ksl], vc_ref[:, ksl]], axis=0)
        qg = jnp.concatenate(
            [q_ref[:, (hk * ATTN_GROUP + g) * HEAD_DIM:(hk * ATTN_GROUP + g + 1) * HEAD_DIM]
             for g in range(ATTN_GROUP)], axis=0)
        s2 = _dot_nt(qg, kband)
        s = jnp.where(own, s2[:, blk:], s2[:, :blk]) + bias_ref[hk]
        s = jnp.where(own | has_prev, s, -jnp.inf)
        sink = jnp.concatenate(
            [jnp.full((blk, 1), sink_ref[hk * ATTN_GROUP + g], F32) for g in range(ATTN_GROUP)],
            axis=0)
        m = jnp.maximum(jnp.max(s, axis=-1, keepdims=True), sink)
        p = jnp.exp(s - m)
        denom = jnp.sum(p, axis=-1, keepdims=True) + jnp.exp(sink - m)
        pb = p.astype(BF16)
        zero = jnp.zeros_like(pb)
        pfull = jnp.concatenate([jnp.where(own, zero, pb), jnp.where(own, pb, zero)], axis=1)
        o = _dot(pfull, vband) / denom
        for g in range(ATTN_GROUP):
            h = hk * ATTN_GROUP + g
            o_ref[:, h * HEAD_DIM:(h + 1) * HEAD_DIM] = o[g * blk:(g + 1) * blk].astype(BF16)


def _attention(q, k, v, bucket, rel_bias, sinks):
    b, s, _ = q.shape
    nb = s // ATTN_BLOCK
    cur = lambda i, j: (i, j, 0)
    prev = lambda i, j: (i, jnp.maximum(j - 1, 0), 0)
    smem = pl.BlockSpec(memory_space=pltpu.SMEM)
    return pl.pallas_call(
        _attn_kernel,
        grid=(b, nb),
        in_specs=[
            pl.BlockSpec((ATTN_BLOCK, ATTN_BLOCK), lambda i, j: (0, 0)),
            smem, smem,
            pl.BlockSpec((None, ATTN_BLOCK, ATTN_WIDTH), cur),
            pl.BlockSpec((None, ATTN_BLOCK, KV_COLS), prev),
            pl.BlockSpec((None, ATTN_BLOCK, KV_COLS), cur),
            pl.BlockSpec((None, ATTN_BLOCK, KV_COLS), prev),
            pl.BlockSpec((None, ATTN_BLOCK, KV_COLS), cur),
        ],
        out_specs=pl.BlockSpec((None, ATTN_BLOCK, ATTN_WIDTH), cur),
        out_shape=jax.ShapeDtypeStruct((b, s, ATTN_WIDTH), BF16),
        scratch_shapes=[pltpu.VMEM((ATTN_KV_HEADS, ATTN_GROUP * ATTN_BLOCK, ATTN_BLOCK), F32)],
        compiler_params=_params("arbitrary", "arbitrary"),
        name="swa",
    )(bucket, rel_bias, sinks, q, k, k, v, v)


def _ssd_kernel(xbc_ref, z_ref, dt_ref, convw_ref, convb_ref, dtb_ref, alog_ref, dskip_ref,
                nw_ref, expand_ref, tri_ref, o_ref, xpad_ref, state_ref):
    L = SSD_CHUNK
    P = SSD_HEAD_DIM
    c = pl.program_id(1)

    @pl.when(c == 0)
    def _():
        xpad_ref[0:CONV_HALO, :] = jnp.zeros((CONV_HALO, XBC_COLS), F32)
        state_ref[...] = jnp.zeros_like(state_ref)

    @pl.when(c > 0)
    def _():
        xpad_ref[0:CONV_HALO, :] = xpad_ref[L:L + CONV_HALO, :]

    xpad_ref[CONV_HALO:CONV_HALO + L, :] = xbc_ref[...]
    conv = convb_ref[...]
    for j in range(SSD_CONV):
        off = CONV_HALO - (SSD_CONV - 1) + j
        conv = conv + convw_ref[j:j + 1, :] * xpad_ref[off:off + L, :]
    xbc = _silu(conv)
    xs = xbc[:, :SSD_INNER]
    bm = xbc[:, SSD_INNER:SSD_INNER + SSD_GROUPS * SSD_STATE].astype(BF16)
    cm = xbc[:, SSD_INNER + SSD_GROUPS * SSD_STATE:].astype(BF16)

    expand = expand_ref[...]
    dt = _softplus(dt_ref[...] + dtb_ref[...])
    a = dt * (-jnp.exp(alog_ref[...]))
    a_cum = _dot(tri_ref[...], a, HIGHEST)
    a_last = a_cum[L - 1:L, :]
    x_dt = xs * _dot(dt, expand, HIGHEST)
    x_end = (x_dt * _dot(jnp.exp(a_last - a_cum), expand, HIGHEST)).astype(BF16)
    dec_start = _dot(jnp.exp(a_cum), expand, HIGHEST)
    x_dt = x_dt.astype(BF16)
    a_cum_t = a_cum.T

    li = lax.broadcasted_iota(jnp.int32, (L, L), 0)
    si = lax.broadcasted_iota(jnp.int32, (L, L), 1)
    causal = si <= li
    ys = []
    for g in range(SSD_GROUPS):
        bg = bm[:, g * SSD_STATE:(g + 1) * SSD_STATE]
        cg = cm[:, g * SSD_STATE:(g + 1) * SSD_STATE]
        cb = _dot_nt(cg, bg)
        for r in range(SSD_HEADS_PER_GROUP):
            h = g * SSD_HEADS_PER_GROUP + r
            sl = slice(h * P, (h + 1) * P)
            seg = a_cum[:, h:h + 1] - a_cum_t[h:h + 1, :]
            lmat = jnp.where(causal, jnp.exp(jnp.where(causal, seg, 0.0)), 0.0)
            h_prev = state_ref[sl, :]
            y_diag = _dot((cb * lmat).astype(BF16), x_dt[:, sl])
            y_off = _dot_nt(cg, h_prev.astype(BF16)) * dec_start[:, sl]
            ys.append(y_diag + y_off)
            chunk_decay = jnp.exp(a_cum_t[h:h + 1, L - 1:L])
            state_ref[sl, :] = h_prev * chunk_decay + _dot_tn(x_end[:, sl], bg)
    y = jnp.concatenate(ys, axis=1) + dskip_ref[...] * xs
    y = y * _silu(z_ref[...])
    o_ref[...] = _rms(y, nw_ref[...]).astype(BF16)


def _ssd(xbc, z, dt, conv_w, conv_b, dt_bias_pad, a_log_pad, d_skip_wide, norm_w, expand, tri):
    b, s, _ = xbc.shape
    nc = s // SSD_CHUNK
    cur = lambda i, j: (i, j, 0)
    fixed = lambda i, j: (0, 0)
    full = lambda shape: pl.BlockSpec(shape, fixed)
    return pl.pallas_call(
        _ssd_kernel,
        grid=(b, nc),
        in_specs=[
            pl.BlockSpec((None, SSD_CHUNK, XBC_COLS), cur),
            pl.BlockSpec((None, SSD_CHUNK, SSD_INNER), cur),
            pl.BlockSpec((None, SSD_CHUNK, DT_PAD), cur),
            full((SSD_CONV, XBC_COLS)), full((1, XBC_COLS)), full((1, DT_PAD)), full((1, DT_PAD)),
            full((1, SSD_INNER)), full((1, SSD_INNER)), full((DT_PAD, SSD_INNER)),
            full((SSD_CHUNK, SSD_CHUNK)),
        ],
        out_specs=pl.BlockSpec((None, SSD_CHUNK, SSD_INNER), cur),
        out_shape=jax.ShapeDtypeStruct((b, s, SSD_INNER), BF16),
        scratch_shapes=[
            pltpu.VMEM((CONV_HALO + SSD_CHUNK, XBC_COLS), F32),
            pltpu.VMEM((SSD_INNER, SSD_STATE), F32),
        ],
        compiler_params=_params("arbitrary", "arbitrary"),
        name="ssd",
    )(xbc, z, dt, conv_w, conv_b, dt_bias_pad, a_log_pad, d_skip_wide, norm_w, expand, tri)


def _outproj_kernel(x_ref, attn_ref, ssd_ref, wo_ref, nw_ref, wq_ref, h1_ref, xnt_ref, q_ref):
    mix = (_dot(attn_ref[...], wo_ref[0:ATTN_WIDTH, :])
           + _dot(ssd_ref[...], wo_ref[ATTN_WIDTH:ATTN_WIDTH + SSD_INNER, :]))
    h1 = x_ref[...] + mix
    h1_ref[...] = h1
    xn = _rms(h1, nw_ref[...])
    xnt_ref[...] = xn.T.astype(BF16)
    qf = _dot(xn.astype(BF16), wq_ref[...])
    for j in range(2 * PEER_HEADS):
        q_ref[j] = qf[:, j * PEER_HALF:(j + 1) * PEER_HALF].astype(BF16)


def _outproj(x2, attn, ssd, w_out, norm_w, w_query):
    t = x2.shape[0]
    tm = min(OUTPROJ_TM, t)
    row = lambda i: (i, 0)
    fixed = lambda i: (0, 0)
    nq = 2 * PEER_HEADS
    return pl.pallas_call(
        _outproj_kernel,
        grid=(t // tm,),
        in_specs=[
            pl.BlockSpec((tm, D_MODEL), row),
            pl.BlockSpec((tm, ATTN_WIDTH), row),
            pl.BlockSpec((tm, SSD_INNER), row),
            pl.BlockSpec((ATTN_WIDTH + SSD_INNER, D_MODEL), fixed),
            pl.BlockSpec((1, D_MODEL), fixed),
            pl.BlockSpec((D_MODEL, nq * PEER_HALF), fixed),
        ],
        out_specs=[
            pl.BlockSpec((tm, D_MODEL), row),
            pl.BlockSpec((D_MODEL, tm), lambda i: (0, i)),
            pl.BlockSpec((nq, tm, PEER_HALF), lambda i: (0, i, 0)),
        ],
        out_shape=[
            jax.ShapeDtypeStruct((t, D_MODEL), F32),
            jax.ShapeDtypeStruct((D_MODEL, t), BF16),
            jax.ShapeDtypeStruct((nq, t, PEER_HALF), BF16),
        ],
        compiler_params=_params("parallel"),
        name="outproj",
    )(x2, attn, ssd, w_out, norm_w, w_query)


def _ce(x, hi, lo):
    a, b = x[hi], x[lo]
    if b is None:
        return
    if a is None:
        x[hi], x[lo] = b, None
        return
    x[hi], x[lo] = jnp.maximum(a, b), jnp.minimum(a, b)


def _bitonic_merge_desc(x):
    n = len(x)
    j = n // 2
    while j >= 1:
        for i in range(n):
            if i & j == 0:
                _ce(x, i, i | j)
        j //= 2


def _sort_desc(x):
    n = len(x)
    k = 2
    while k <= n:
        j = k // 2
        while j >= 1:
            for i in range(n):
                l = i ^ j
                if l > i:
                    if i & k == 0:
                        _ce(x, i, l)
                    else:
                        _ce(x, l, i)
            j //= 2
        k *= 2


def _merge_top(a, b, sort=True):
    n = len(a)
    out = []
    for i in range(n):
        u, v = a[i], b[n - 1 - i]
        out.append(v if u is None else u if v is None else jnp.maximum(u, v))
    if sort:
        _bitonic_merge_desc(out)
    return out


def _key_rows(k):
    return slice(k * SUBLANES, (k + 1) * SUBLANES)


def _top16_sorted(s_ref):
    groups = []
    for g in range(N_KEYS // PEER_TOPK):
        x = [s_ref[_key_rows(g * PEER_TOPK + i), :] for i in range(PEER_TOPK)]
        _sort_desc(x)
        groups.append(x)
    while len(groups) > 1:
        groups = [_merge_top(groups[i], groups[i + 1]) for i in range(0, len(groups), 2)]
    return groups[0]


def _route_kernel(q_ref, k1_ref, k2_ref, r2_ref, e2_ref, c1_ref, e1_ref,
                  s1_ref, s2_ref, r2s_ref, e2s_ref, c1s_ref, e1s_ref):
    def head(h, carry):
        for half, (k_ref, s_ref) in enumerate(((k1_ref, s1_ref), (k2_ref, s2_ref))):
            qh = q_ref[2 * h + half]
            qcat = jnp.concatenate([qh[j * LANES:(j + 1) * LANES, :] for j in range(SUBLANES)],
                                   axis=1)
            s_ref[...] = _dot_nt(k_ref[...], qcat)
        v1 = _top16_sorted(s1_ref)
        v2 = _top16_sorted(s2_ref)

        lists = []
        for a in range(PEER_TOPK):
            n_b = PEER_TOPK // (a + 1)
            lists.append([v1[a] + v2[b] if b < n_b else None for b in range(PEER_TOPK)])
        while len(lists) > 2:
            lists = [_merge_top(lists[i], lists[i + 1]) for i in range(0, len(lists), 2)]
        top = _merge_top(lists[0], lists[1], sort=False)
        tau = top[0]
        for x in top[1:]:
            tau = jnp.minimum(tau, x)

        e1_top = [jnp.exp(v - v1[0]) for v in v1]
        e2_top = [jnp.exp(v - v2[0]) for v in v2]
        zsum = jnp.zeros_like(tau)
        cnt = []
        for a in range(PEER_TOPK):
            n_a = jnp.zeros_like(tau)
            w_a = jnp.zeros_like(tau)
            for b in range(PEER_TOPK // (a + 1)):
                sel = (v1[a] + v2[b]) >= tau
                n_a = n_a + jnp.where(sel, 1.0, 0.0)
                w_a = w_a + jnp.where(sel, e2_top[b], 0.0)
            cnt.append(n_a)
            zsum = zsum + e1_top[a] * w_a
        inv_z = 1.0 / zsum

        for k in range(N_KEYS):
            rows = _key_rows(k)
            s1 = s1_ref[rows, :]
            s2 = s2_ref[rows, :]
            c1 = jnp.zeros_like(tau)
            for a in reversed(range(PEER_TOPK)):
                c1 = jnp.where(s1 >= v1[a], cnt[a], c1)
            r2 = jnp.zeros_like(tau)
            for b in range(PEER_TOPK):
                r2 = jnp.where(v2[b] > s2, float(b + 1), r2)
            c1s_ref[rows, :] = c1
            r2s_ref[rows, :] = r2
            e1s_ref[rows, :] = jnp.exp(s1 - v1[0]) * inv_z
            e2s_ref[rows, :] = jnp.exp(s2 - v2[0])

        for j in range(SUBLANES):
            cols = slice(j * LANES, (j + 1) * LANES)
            take = pl.ds(j, N_KEYS, stride=SUBLANES)
            r2_ref[h, :, cols] = r2s_ref[take, :].astype(BF16)
            e2_ref[h, :, cols] = e2s_ref[take, :].astype(BF16)
            c1_ref[h, :, cols] = c1s_ref[take, :]
            e1_ref[h, :, cols] = e1s_ref[take, :]
        return carry

    lax.fori_loop(0, PEER_HEADS, head, 0)


def _expand_keys(keys):
    eye = jnp.eye(SUBLANES, dtype=keys.dtype)
    return jnp.einsum("kd,jc->kjcd", keys, eye).reshape(N_KEYS * SUBLANES, SUBLANES * PEER_HALF)


def _route(q, keys1, keys2):
    nq, t, _ = q.shape
    tm = ROUTE_TM
    assert t % tm == 0
    fixed = lambda i: (0, 0)
    out_spec = pl.BlockSpec((PEER_HEADS, N_KEYS, tm), lambda i: (0, 0, i))
    shape = (PEER_HEADS, N_KEYS, t)
    key_spec = pl.BlockSpec((N_KEYS * SUBLANES, SUBLANES * PEER_HALF), fixed)
    stage = pltpu.VMEM((N_KEYS * SUBLANES, LANES), F32)
    return pl.pallas_call(
        _route_kernel,
        grid=(t // tm,),
        in_specs=[pl.BlockSpec((nq, tm, PEER_HALF), lambda i: (0, i, 0)), key_spec, key_spec],
        out_specs=[out_spec] * 4,
        out_shape=[jax.ShapeDtypeStruct(shape, BF16)] * 2 + [jax.ShapeDtypeStruct(shape, F32)] * 2,
        scratch_shapes=[stage] * 6,
        compiler_params=_params("parallel"),
        name="route",
    )(q, _expand_keys(keys1), _expand_keys(keys2))


def _peer_activations(pre_ref, act_ref, r2_ref, e2_ref, c1_ref, e1_ref, sub):
    tm = pre_ref.shape[1]
    chunks = N_KEYS // PEER_ROWS
    shape = (PEER_ROWS, PEER_LANE_CHUNK)
    zero = jnp.zeros(shape, BF16)
    for a in range(PEER_SUB // N_KEYS):
        key1 = sub * (PEER_SUB // N_KEYS) + a
        for lc in range(tm // PEER_LANE_CHUNK):
            lanes = slice(lc * PEER_LANE_CHUNK, (lc + 1) * PEER_LANE_CHUNK)
            gates = [zero] * chunks
            for h in range(PEER_HEADS):
                cnt = jnp.broadcast_to(c1_ref[h, key1:key1 + 1, lanes], shape).astype(BF16)
                e1 = jnp.broadcast_to(e1_ref[h, key1:key1 + 1, lanes], shape).astype(BF16)
                for c in range(chunks):
                    rows = slice(c * PEER_ROWS, (c + 1) * PEER_ROWS)
                    sel = r2_ref[h, rows, lanes] < cnt
                    gates[c] = gates[c] + jnp.where(sel, e2_ref[h, rows, lanes], zero) * e1
            for c in range(chunks):
                rows = slice(a * N_KEYS + c * PEER_ROWS, a * N_KEYS + (c + 1) * PEER_ROWS)
                half = 0.5 * pre_ref[rows, lanes]
                gelu = half + half * lax.erf(half * (2.0 ** 0.5))
                act_ref[rows, lanes] = gelu.astype(BF16) * gates[c]


def _peer_kernel(xnt_ref, d_ref, ut_ref, r2_ref, e2_ref, c1_ref, e1_ref, h1_ref, nw_ref,
                 o_ref, acc_ref, pre_ref, act_ref):
    j = pl.program_id(1)
    nsub = d_ref.shape[0] // PEER_SUB

    @pl.when(j == 0)
    def _():
        acc_ref[...] = jnp.zeros_like(acc_ref)

    def down(sub):
        pre_ref[sub % 2] = _dot(d_ref[sub * PEER_SUB:(sub + 1) * PEER_SUB, :], xnt_ref[...])

    down(0)
    for sub in range(nsub):
        if sub + 1 < nsub:
            down(sub + 1)
        _peer_activations(pre_ref.at[sub % 2], act_ref.at[sub % 2], r2_ref, e2_ref, c1_ref,
                          e1_ref, sub)
        acc_ref[...] += _dot(ut_ref[:, sub * PEER_SUB:(sub + 1) * PEER_SUB], act_ref[sub % 2])

    @pl.when(j == pl.num_programs(1) - 1)
    def _():
        h2 = h1_ref[...] + acc_ref[...].T
        o_ref[...] = _rms(h2, nw_ref[...])


def _peer(xnt, down, up_t, r2, e2, c1, e1, h1, norm_w):
    t = h1.shape[0]
    tm = min(PEER_TM, t)
    te = PEER_TE
    key2_spec = pl.BlockSpec((PEER_HEADS, N_KEYS, tm), lambda i, j: (0, 0, i))
    key1_spec = pl.BlockSpec((PEER_HEADS, te // N_KEYS, tm), lambda i, j: (0, j, i))
    return pl.pallas_call(
        _peer_kernel,
        grid=(t // tm, N_EXPERTS // te),
        in_specs=[
            pl.BlockSpec((D_MODEL, tm), lambda i, j: (0, i)),
            pl.BlockSpec((te, D_MODEL), lambda i, j: (j, 0)),
            pl.BlockSpec((D_MODEL, te), lambda i, j: (0, j)),
            key2_spec, key2_spec, key1_spec, key1_spec,
            pl.BlockSpec((tm, D_MODEL), lambda i, j: (i, 0)),
            pl.BlockSpec((1, D_MODEL), lambda i, j: (0, 0)),
        ],
        out_specs=pl.BlockSpec((tm, D_MODEL), lambda i, j: (i, 0)),
        out_shape=jax.ShapeDtypeStruct((t, D_MODEL), F32),
        scratch_shapes=[
            pltpu.VMEM((D_MODEL, tm), F32),
            pltpu.VMEM((2, PEER_SUB, tm), F32),
            pltpu.VMEM((2, PEER_SUB, tm), BF16),
        ],
        compiler_params=_params("parallel", "arbitrary"),
        name="peer",
    )(xnt, down, up_t, r2, e2, c1, e1, h1, norm_w)


def _pad_lanes(v, width):
    return jnp.pad(v.astype(F32), (0, width - v.shape[0])).reshape(1, width)


def _layer(x, norm_mix, w_in, conv_w, conv_b, dt_bias, a_log, d_skip, ssd_norm_w, attn_sinks,
           w_out, rel_bias, norm_ffn, w_query, sub_keys1, sub_keys2, expert_down, expert_up,
           out_norm):
    b, s, d = x.shape
    t = b * s
    x2 = x.reshape(t, d)

    w_in_pad = jnp.pad(w_in, ((0, 0), (0, IN_COLS_PAD - w_in.shape[1]))).astype(BF16)
    q, k, v, z, xbc, dt = _inproj(x2, norm_mix.reshape(1, d), w_in_pad)

    qi = jnp.arange(ATTN_BLOCK)[:, None]
    kj = jnp.arange(ATTN_BLOCK)[None, :]
    bucket = _t5_bucket(jnp.where(kj <= qi, qi - kj, qi + ATTN_BLOCK - kj)).astype(jnp.int32)
    attn = _attention(q.reshape(b, s, -1), k.reshape(b, s, -1), v.reshape(b, s, -1),
                      bucket, rel_bias.astype(F32), attn_sinks.astype(F32))

    lane = jnp.arange(SSD_INNER)[None, :] // SSD_HEAD_DIM
    expand = (jnp.arange(DT_PAD)[:, None] == lane).astype(F32)
    tri = (jnp.arange(SSD_CHUNK)[None, :] <= jnp.arange(SSD_CHUNK)[:, None]).astype(F32)
    ssd = _ssd(xbc.reshape(b, s, -1), z.reshape(b, s, -1), dt.reshape(b, s, -1),
               conv_w, conv_b.reshape(1, -1), _pad_lanes(dt_bias, DT_PAD),
               _pad_lanes(a_log, DT_PAD), jnp.repeat(d_skip.astype(F32), SSD_HEAD_DIM).reshape(1, -1),
               ssd_norm_w.reshape(1, -1), expand, tri)

    h1, xnt, pq = _outproj(x2, attn.reshape(t, -1), ssd.reshape(t, -1), w_out.astype(BF16),
                           norm_ffn.reshape(1, d), w_query.astype(BF16))
    r2, e2, c1, e1 = _route(pq, sub_keys1.astype(BF16), sub_keys2.astype(BF16))
    out = _peer(xnt, expert_down.astype(BF16), expert_up.T.astype(BF16), r2, e2, c1, e1, h1,
                out_norm.reshape(1, d))
    return out.reshape(b, s, d)


@jax.jit
def kernel(x, norm_mix, w_in, conv_w, conv_b, dt_bias, a_log, d_skip, ssd_norm_w, attn_sinks,
           w_out, rel_bias, norm_ffn, w_query, sub_keys1, sub_keys2, expert_down, expert_up,
           norm_final):
    assert norm_mix.shape[0] == 1, "single-layer block"
    return _layer(x, norm_mix[0], w_in[0], conv_w[0], conv_b[0], dt_bias[0], a_log[0], d_skip[0],
                  ssd_norm_w[0], attn_sinks[0], w_out[0], rel_bias, norm_ffn[0], w_query[0],
                  sub_keys1[0], sub_keys2[0], expert_down[0], expert_up[0], norm_final)
```

```python
import functools
import math

import jax
import jax.numpy as jnp
from jax import lax
from jax.experimental import pallas as pl
from jax.experimental.pallas import tpu as pltpu

F32 = jnp.float32
BF16 = jnp.bfloat16
HIGHEST = lax.Precision.HIGHEST

D_MODEL = 1024
ATTN_HEADS = 8
ATTN_KV_HEADS = 2
ATTN_GROUP = ATTN_HEADS // ATTN_KV_HEADS
HEAD_DIM = 64
ATTN_WIDTH = ATTN_HEADS * HEAD_DIM
ATTN_BLOCK = 128
N_BUCKETS = 32
MAX_DISTANCE = 128
SSD_HEADS = 8
SSD_HEAD_DIM = 64
SSD_INNER = SSD_HEADS * SSD_HEAD_DIM
SSD_STATE = 128
SSD_GROUPS = 2
SSD_HEADS_PER_GROUP = SSD_HEADS // SSD_GROUPS
SSD_CONV = 4
SSD_CHUNK = 128
PEER_HEADS = 8
N_KEYS = 128
N_EXPERTS = N_KEYS * N_KEYS
PEER_HALF = 128
PEER_TOPK = 16
EPS = 1e-6

KV_COLS = ATTN_KV_HEADS * HEAD_DIM
XBC_COLS = SSD_INNER + 2 * SSD_GROUPS * SSD_STATE
OFF_K = ATTN_WIDTH
OFF_V = OFF_K + KV_COLS
OFF_Z = OFF_V + KV_COLS
OFF_XBC = OFF_Z + SSD_INNER
OFF_DT = OFF_XBC + XBC_COLS
LANES = 128
SUBLANES = 8
DT_PAD = LANES
IN_COLS_PAD = OFF_DT + DT_PAD
CONV_HALO = SUBLANES

VMEM_LIMIT = 56 * 1024 * 1024

ATTN_BLOCKS_PER_STEP = 4
INPROJ_TM = 1024
OUTPROJ_TM = 512
ROUTE_TM = SUBLANES * LANES
PEER_TM = 1024
PEER_TE = 1024
PEER_SUB = 512
PEER_LANE_CHUNK = 512
PEER_ROWS = 16


def _dot(a, b, precision=None):
    return jnp.dot(a, b, preferred_element_type=F32, precision=precision)


def _dot_nt(a, b):
    return lax.dot_general(a, b, (((1,), (1,)), ((), ())), preferred_element_type=F32)


def _dot_tn(a, b):
    return lax.dot_general(a, b, (((0,), (0,)), ((), ())), preferred_element_type=F32)


def _rms(x, w):
    return x * lax.rsqrt(jnp.mean(x * x, axis=-1, keepdims=True) + EPS) * w


def _silu(x):
    return x / (1.0 + jnp.exp(-x))


def _softplus(x):
    return jnp.maximum(x, 0.0) + jnp.log1p(jnp.exp(-jnp.abs(x)))


def _params(*sem):
    return pltpu.CompilerParams(dimension_semantics=sem, vmem_limit_bytes=VMEM_LIMIT)


def _inproj_kernel(x_ref, nw_ref, w_ref, q_ref, k_ref, v_ref, z_ref, xbc_ref, dt_ref):
    xb = _rms(x_ref[...], nw_ref[...]).astype(BF16)

    def proj(lo, hi):
        return _dot(xb, w_ref[:, lo:hi])

    q_ref[...] = (proj(0, OFF_K) * (HEAD_DIM ** -0.5)).astype(BF16)
    k_ref[...] = proj(OFF_K, OFF_V).astype(BF16)
    v_ref[...] = proj(OFF_V, OFF_Z).astype(BF16)
    z_ref[...] = proj(OFF_Z, OFF_XBC)
    xbc_ref[...] = proj(OFF_XBC, OFF_DT)
    dt_ref[...] = proj(OFF_DT, IN_COLS_PAD)


def _inproj(x2, norm_w, w_in_pad):
    t = x2.shape[0]
    tm = min(INPROJ_TM, t)
    row = lambda i: (i, 0)
    fixed = lambda i: (0, 0)
    return pl.pallas_call(
        _inproj_kernel,
        grid=(t // tm,),
        in_specs=[
            pl.BlockSpec((tm, D_MODEL), row),
            pl.BlockSpec((1, D_MODEL), fixed),
            pl.BlockSpec((D_MODEL, IN_COLS_PAD), fixed),
        ],
        out_specs=[
            pl.BlockSpec((tm, ATTN_WIDTH), row),
            pl.BlockSpec((tm, KV_COLS), row),
            pl.BlockSpec((tm, KV_COLS), row),
            pl.BlockSpec((tm, SSD_INNER), row),
            pl.BlockSpec((tm, XBC_COLS), row),
            pl.BlockSpec((tm, DT_PAD), row),
        ],
        out_shape=[
            jax.ShapeDtypeStruct((t, ATTN_WIDTH), BF16),
            jax.ShapeDtypeStruct((t, KV_COLS), BF16),
            jax.ShapeDtypeStruct((t, KV_COLS), BF16),
            jax.ShapeDtypeStruct((t, SSD_INNER), F32),
            jax.ShapeDtypeStruct((t, XBC_COLS), F32),
            jax.ShapeDtypeStruct((t, DT_PAD), F32),
        ],
        compiler_params=_params("parallel"),
        name="inproj",
    )(x2, norm_w, w_in_pad)


def _t5_bucket(dist):
    n = jnp.maximum(dist, 0)
    max_exact = N_BUCKETS // 2
    nf = jnp.maximum(n, 1).astype(F32)
    large = max_exact + (jnp.log(nf / max_exact) / math.log(MAX_DISTANCE / max_exact)
                         * (N_BUCKETS - max_exact)).astype(jnp.int32)
    large = jnp.minimum(large, N_BUCKETS - 1)
    return jnp.where(n < max_exact, n, large)


def _attn_kernel(bucket_ref, relb_ref, sink_ref, q_ref, kp_ref, kc_ref, vp_ref, vc_ref,
                 o_ref, bias_ref):
    blk = ATTN_BLOCK
    rows = ATTN_GROUP * blk
    n = pl.program_id(1)

    @pl.when((pl.program_id(0) == 0) & (n == 0))
    def _():
        bucket = bucket_ref[...]
        for h in range(ATTN_HEADS):
            acc = jnp.zeros((blk, blk), F32)
            for b in range(N_BUCKETS):
                acc = jnp.where(bucket == b, relb_ref[b, h], acc)
            hk, g = divmod(h, ATTN_GROUP)
            bias_ref[hk, g * blk:(g + 1) * blk, :] = acc

    qi = lax.broadcasted_iota(jnp.int32, (rows, blk), 0) & (blk - 1)
    kj = lax.broadcasted_iota(jnp.int32, (rows, blk), 1)
    own = kj <= qi
    for i in range(ATTN_BLOCKS_PER_STEP):
        cur = slice(i * blk, (i + 1) * blk)
        kprev, vprev = (kp_ref, vp_ref) if i == 0 else (kc_ref.at[(i - 1) * blk:i * blk],
                                                        vc_ref.at[(i - 1) * blk:i * blk])
        has_prev = (n > 0) if i == 0 else True
        for hk in range(ATTN_KV_HEADS):
            ksl = slice(hk * HEAD_DIM, (hk + 1) * HEAD_DIM)
            kband = jnp.concatenate([kprev[:, ksl], kc_ref[cur, ksl]], axis=0)
            vband = jnp.concatenate([vprev[:, ksl], vc_ref[cur, ksl]], axis=0)
            qg = jnp.concatenate(
                [q_ref[cur, (hk * ATTN_GROUP + g) * HEAD_DIM:(hk * ATTN_GROUP + g + 1) * HEAD_DIM]
                 for g in range(ATTN_GROUP)], axis=0)
            s2 = _dot_nt(qg, kband)
            s = jnp.where(own, s2[:, blk:], s2[:, :blk]) + bias_ref[hk]
            if i == 0:
                s = jnp.where(own | has_prev, s, -jnp.inf)
            sink = jnp.concatenate(
                [jnp.full((blk, 1), sink_ref[hk * ATTN_GROUP + g], F32)
                 for g in range(ATTN_GROUP)], axis=0)
            m = jnp.maximum(jnp.max(s, axis=-1, keepdims=True), sink)
            p = jnp.exp(s - m)
            denom = jnp.sum(p, axis=-1, keepdims=True) + jnp.exp(sink - m)
            pb = p.astype(BF16)
            zero = jnp.zeros_like(pb)
            pfull = jnp.concatenate([jnp.where(own, zero, pb), jnp.where(own, pb, zero)], axis=1)
            o = _dot(pfull, vband) / denom
            for g in range(ATTN_GROUP):
                h = hk * ATTN_GROUP + g
                o_ref[cur, h * HEAD_DIM:(h + 1) * HEAD_DIM] = o[g * blk:(g + 1) * blk].astype(BF16)


def _attention(q, k, v, bucket, rel_bias, sinks):
    b, s, _ = q.shape
    rows = ATTN_BLOCKS_PER_STEP * ATTN_BLOCK
    assert s % rows == 0
    cur = lambda i, j: (i, j, 0)
    prev = lambda i, j: (i, jnp.maximum(j * ATTN_BLOCKS_PER_STEP - 1, 0), 0)
    smem = pl.BlockSpec(memory_space=pltpu.SMEM)
    return pl.pallas_call(
        _attn_kernel,
        grid=(b, s // rows),
        in_specs=[
            pl.BlockSpec((ATTN_BLOCK, ATTN_BLOCK), lambda i, j: (0, 0)),
            smem, smem,
            pl.BlockSpec((None, rows, ATTN_WIDTH), cur),
            pl.BlockSpec((None, ATTN_BLOCK, KV_COLS), prev),
            pl.BlockSpec((None, rows, KV_COLS), cur),
            pl.BlockSpec((None, ATTN_BLOCK, KV_COLS), prev),
            pl.BlockSpec((None, rows, KV_COLS), cur),
        ],
        out_specs=pl.BlockSpec((None, rows, ATTN_WIDTH), cur),
        out_shape=jax.ShapeDtypeStruct((b, s, ATTN_WIDTH), BF16),
        scratch_shapes=[pltpu.VMEM((ATTN_KV_HEADS, ATTN_GROUP * ATTN_BLOCK, ATTN_BLOCK), F32)],
        compiler_params=_params("arbitrary", "arbitrary"),
        name="swa",
    )(bucket, rel_bias, sinks, q, k, k, v, v)


def _ssd_kernel(xbc_ref, z_ref, dt_ref, convw_ref, convb_ref, dtb_ref, alog_ref, dskip_ref,
                nw_ref, expand_ref, tri_ref, o_ref, xpad_ref, state_ref):
    L = SSD_CHUNK
    P = SSD_HEAD_DIM
    c = pl.program_id(1)

    @pl.when(c == 0)
    def _():
        xpad_ref[0:CONV_HALO, :] = jnp.zeros((CONV_HALO, XBC_COLS), F32)
        state_ref[...] = jnp.zeros_like(state_ref)

    @pl.when(c > 0)
    def _():
        xpad_ref[0:CONV_HALO, :] = xpad_ref[L:L + CONV_HALO, :]

    xpad_ref[CONV_HALO:CONV_HALO + L, :] = xbc_ref[...]
    conv = convb_ref[...]
    for j in range(SSD_CONV):
        off = CONV_HALO - (SSD_CONV - 1) + j
        conv = conv + convw_ref[j:j + 1, :] * xpad_ref[off:off + L, :]
    xbc = _silu(conv)
    xs = xbc[:, :SSD_INNER]
    bm = xbc[:, SSD_INNER:SSD_INNER + SSD_GROUPS * SSD_STATE].astype(BF16)
    cm = xbc[:, SSD_INNER + SSD_GROUPS * SSD_STATE:].astype(BF16)

    expand = expand_ref[...]
    dt = _softplus(dt_ref[...] + dtb_ref[...])
    a = dt * (-jnp.exp(alog_ref[...]))
    a_cum = _dot(tri_ref[...], a, HIGHEST)
    a_last = a_cum[L - 1:L, :]
    x_dt = xs * _dot(dt, expand, HIGHEST)
    x_end = (x_dt * _dot(jnp.exp(a_last - a_cum), expand, HIGHEST)).astype(BF16)
    dec_start = _dot(jnp.exp(a_cum), expand, HIGHEST)
    x_dt = x_dt.astype(BF16)
    a_cum_t = a_cum.T

    li = lax.broadcasted_iota(jnp.int32, (L, L), 0)
    si = lax.broadcasted_iota(jnp.int32, (L, L), 1)
    causal = si <= li
    ys = []
    for g in range(SSD_GROUPS):
        bg = bm[:, g * SSD_STATE:(g + 1) * SSD_STATE]
        cg = cm[:, g * SSD_STATE:(g + 1) * SSD_STATE]
        cb = _dot_nt(cg, bg)
        for r in range(SSD_HEADS_PER_GROUP):
            h = g * SSD_HEADS_PER_GROUP + r
            sl = slice(h * P, (h + 1) * P)
            seg = a_cum[:, h:h + 1] - a_cum_t[h:h + 1, :]
            lmat = jnp.where(causal, jnp.exp(jnp.where(causal, seg, 0.0)), 0.0)
            h_prev = state_ref[sl, :]
            y_diag = _dot((cb * lmat).astype(BF16), x_dt[:, sl])
            y_off = _dot_nt(cg, h_prev.astype(BF16)) * dec_start[:, sl]
            ys.append(y_diag + y_off)
            chunk_decay = jnp.exp(a_cum_t[h:h + 1, L - 1:L])
            state_ref[sl, :] = h_prev * chunk_decay + _dot_tn(x_end[:, sl], bg)
    y = jnp.concatenate(ys, axis=1) + dskip_ref[...] * xs
    y = y * _silu(z_ref[...])
    o_ref[...] = _rms(y, nw_ref[...]).astype(BF16)


def _ssd(xbc, z, dt, conv_w, conv_b, dt_bias_pad, a_log_pad, d_skip_wide, norm_w, expand, tri):
    b, s, _ = xbc.shape
    nc = s // SSD_CHUNK
    cur = lambda i, j: (i, j, 0)
    fixed = lambda i, j: (0, 0)
    full = lambda shape: pl.BlockSpec(shape, fixed)
    return pl.pallas_call(
        _ssd_kernel,
        grid=(b, nc),
        in_specs=[
            pl.BlockSpec((None, SSD_CHUNK, XBC_COLS), cur),
            pl.BlockSpec((None, SSD_CHUNK, SSD_INNER), cur),
            pl.BlockSpec((None, SSD_CHUNK, DT_PAD), cur),
            full((SSD_CONV, XBC_COLS)), full((1, XBC_COLS)), full((1, DT_PAD)), full((1, DT_PAD)),
            full((1, SSD_INNER)), full((1, SSD_INNER)), full((DT_PAD, SSD_INNER)),
            full((SSD_CHUNK, SSD_CHUNK)),
        ],
        out_specs=pl.BlockSpec((None, SSD_CHUNK, SSD_INNER), cur),
        out_shape=jax.ShapeDtypeStruct((b, s, SSD_INNER), BF16),
        scratch_shapes=[
            pltpu.VMEM((CONV_HALO + SSD_CHUNK, XBC_COLS), F32),
            pltpu.VMEM((SSD_INNER, SSD_STATE), F32),
        ],
        compiler_params=_params("arbitrary", "arbitrary"),
        name="ssd",
    )(xbc, z, dt, conv_w, conv_b, dt_bias_pad, a_log_pad, d_skip_wide, norm_w, expand, tri)


def _outproj_kernel(x_ref, attn_ref, ssd_ref, wo_ref, nw_ref, wq_ref, h1_ref, xnt_ref, q_ref):
    mix = (_dot(attn_ref[...], wo_ref[0:ATTN_WIDTH, :])
           + _dot(ssd_ref[...], wo_ref[ATTN_WIDTH:ATTN_WIDTH + SSD_INNER, :]))
    h1 = x_ref[...] + mix
    h1_ref[...] = h1
    xn = _rms(h1, nw_ref[...])
    xnt_ref[...] = xn.T.astype(BF16)
    qf = _dot(xn.astype(BF16), wq_ref[...])
    for j in range(2 * PEER_HEADS):
        q_ref[j] = qf[:, j * PEER_HALF:(j + 1) * PEER_HALF].astype(BF16)


def _outproj(x2, attn, ssd, w_out, norm_w, w_query):
    t = x2.shape[0]
    tm = min(OUTPROJ_TM, t)
    row = lambda i: (i, 0)
    fixed = lambda i: (0, 0)
    nq = 2 * PEER_HEADS
    return pl.pallas_call(
        _outproj_kernel,
        grid=(t // tm,),
        in_specs=[
            pl.BlockSpec((tm, D_MODEL), row),
            pl.BlockSpec((tm, ATTN_WIDTH), row),
            pl.BlockSpec((tm, SSD_INNER), row),
            pl.BlockSpec((ATTN_WIDTH + SSD_INNER, D_MODEL), fixed),
            pl.BlockSpec((1, D_MODEL), fixed),
            pl.BlockSpec((D_MODEL, nq * PEER_HALF), fixed),
        ],
        out_specs=[
            pl.BlockSpec((tm, D_MODEL), row),
            pl.BlockSpec((D_MODEL, tm), lambda i: (0, i)),
            pl.BlockSpec((nq, tm, PEER_HALF), lambda i: (0, i, 0)),
        ],
        out_shape=[
            jax.ShapeDtypeStruct((t, D_MODEL), F32),
            jax.ShapeDtypeStruct((D_MODEL, t), BF16),
            jax.ShapeDtypeStruct((nq, t, PEER_HALF), BF16),
        ],
        compiler_params=_params("parallel"),
        name="outproj",
    )(x2, attn, ssd, w_out, norm_w, w_query)


def _ce(x, hi, lo):
    a, b = x[hi], x[lo]
    if b is None:
        return
    if a is None:
        x[hi], x[lo] = b, None
        return
    x[hi], x[lo] = jnp.maximum(a, b), jnp.minimum(a, b)


def _bitonic_merge_desc(x):
    n = len(x)
    j = n // 2
    while j >= 1:
        for i in range(n):
            if i & j == 0:
                _ce(x, i, i | j)
        j //= 2


def _sort_desc(x):
    n = len(x)
    k = 2
    while k <= n:
        j = k // 2
        while j >= 1:
            for i in range(n):
                l = i ^ j
                if l > i:
                    if i & k == 0:
                        _ce(x, i, l)
                    else:
                        _ce(x, l, i)
            j //= 2
        k *= 2


def _merge_top(a, b, sort=True):
    n = len(a)
    out = []
    for i in range(n):
        u, v = a[i], b[n - 1 - i]
        out.append(v if u is None else u if v is None else jnp.maximum(u, v))
    if sort:
        _bitonic_merge_desc(out)
    return out


def _key_rows(k):
    return slice(k * SUBLANES, (k + 1) * SUBLANES)


def _top16_sorted(s_ref):
    groups = []
    for g in range(N_KEYS // PEER_TOPK):
        x = [s_ref[_key_rows(g * PEER_TOPK + i), :] for i in range(PEER_TOPK)]
        _sort_desc(x)
        groups.append(x)
    while len(groups) > 1:
        groups = [_merge_top(groups[i], groups[i + 1]) for i in range(0, len(groups), 2)]
    return groups[0]


def _route_kernel(q_ref, k1_ref, k2_ref, r2_ref, e2_ref, c1_ref, e1_ref,
                  s1_ref, s2_ref, r2s_ref, e2s_ref, c1s_ref, e1s_ref):
    def head(h, carry):
        for half, (k_ref, s_ref) in enumerate(((k1_ref, s1_ref), (k2_ref, s2_ref))):
            qh = q_ref[2 * h + half]
            qcat = jnp.concatenate([qh[j * LANES:(j + 1) * LANES, :] for j in range(SUBLANES)],
                                   axis=1)
            s_ref[...] = _dot_nt(k_ref[...], qcat)
        v1 = _top16_sorted(s1_ref)
        v2 = _top16_sorted(s2_ref)

        lists = []
        for a in range(PEER_TOPK):
            n_b = PEER_TOPK // (a + 1)
            lists.append([v1[a] + v2[b] if b < n_b else None for b in range(PEER_TOPK)])
        while len(lists) > 2:
            lists = [_merge_top(lists[i], lists[i + 1]) for i in range(0, len(lists), 2)]
        top = _merge_top(lists[0], lists[1], sort=False)
        tau = top[0]
        for x in top[1:]:
            tau = jnp.minimum(tau, x)

        e1_top = [jnp.exp(v - v1[0]) for v in v1]
        e2_top = [jnp.exp(v - v2[0]) for v in v2]
        zsum = jnp.zeros_like(tau)
        cnt = []
        for a in range(PEER_TOPK):
            n_a = jnp.zeros_like(tau)
            w_a = jnp.zeros_like(tau)
            for b in range(PEER_TOPK // (a + 1)):
                sel = (v1[a] + v2[b]) >= tau
                n_a = n_a + jnp.where(sel, 1.0, 0.0)
                w_a = w_a + jnp.where(sel, e2_top[b], 0.0)
            cnt.append(n_a)
            zsum = zsum + e1_top[a] * w_a
        inv_z = 1.0 / zsum

        for k in range(N_KEYS):
            rows = _key_rows(k)
            s1 = s1_ref[rows, :]
            s2 = s2_ref[rows, :]
            c1 = jnp.zeros_like(tau)
            for a in reversed(range(PEER_TOPK)):
                c1 = jnp.where(s1 >= v1[a], cnt[a], c1)
            r2 = jnp.zeros_like(tau)
            for b in range(PEER_TOPK):
                r2 = jnp.where(v2[b] > s2, float(b + 1), r2)
            c1s_ref[rows, :] = c1
            r2s_ref[rows, :] = r2
            e1s_ref[rows, :] = jnp.exp(s1 - v1[0]) * inv_z
            e2s_ref[rows, :] = jnp.exp(s2 - v2[0])

        for j in range(SUBLANES):
            cols = slice(j * LANES, (j + 1) * LANES)
            take = pl.ds(j, N_KEYS, stride=SUBLANES)
            r2_ref[h, :, cols] = r2s_ref[take, :].astype(BF16)
            e2_ref[h, :, cols] = e2s_ref[take, :].astype(BF16)
            c1_ref[h, :, cols] = c1s_ref[take, :]
            e1_ref[h, :, cols] = e1s_ref[take, :]
        return carry

    lax.fori_loop(0, PEER_HEADS, head, 0)


def _expand_keys(keys):
    eye = jnp.eye(SUBLANES, dtype=keys.dtype)
    return jnp.einsum("kd,jc->kjcd", keys, eye).reshape(N_KEYS * SUBLANES, SUBLANES * PEER_HALF)


def _route(q, keys1, keys2):
    nq, t, _ = q.shape
    tm = ROUTE_TM
    assert t % tm == 0
    fixed = lambda i: (0, 0)
    out_spec = pl.BlockSpec((PEER_HEADS, N_KEYS, tm), lambda i: (0, 0, i))
    shape = (PEER_HEADS, N_KEYS, t)
    key_spec = pl.BlockSpec((N_KEYS * SUBLANES, SUBLANES * PEER_HALF), fixed)
    stage = pltpu.VMEM((N_KEYS * SUBLANES, LANES), F32)
    return pl.pallas_call(
        _route_kernel,
        grid=(t // tm,),
        in_specs=[pl.BlockSpec((nq, tm, PEER_HALF), lambda i: (0, i, 0)), key_spec, key_spec],
        out_specs=[out_spec] * 4,
        out_shape=[jax.ShapeDtypeStruct(shape, BF16)] * 2 + [jax.ShapeDtypeStruct(shape, F32)] * 2,
        scratch_shapes=[stage] * 6,
        compiler_params=_params("parallel"),
        name="route",
    )(q, _expand_keys(keys1), _expand_keys(keys2))


def _peer_activations(pre_ref, act_ref, r2_ref, e2_ref, c1_ref, e1_ref, sub):
    tm = pre_ref.shape[1]
    chunks = N_KEYS // PEER_ROWS
    shape = (PEER_ROWS, PEER_LANE_CHUNK)
    zero = jnp.zeros(shape, BF16)
    for a in range(PEER_SUB // N_KEYS):
        key1 = sub * (PEER_SUB // N_KEYS) + a
        for lc in range(tm // PEER_LANE_CHUNK):
            lanes = slice(lc * PEER_LANE_CHUNK, (lc + 1) * PEER_LANE_CHUNK)
            gates = [zero] * chunks
            for h in range(PEER_HEADS):
                cnt = jnp.broadcast_to(c1_ref[h, key1:key1 + 1, lanes], shape).astype(BF16)
                e1 = jnp.broadcast_to(e1_ref[h, key1:key1 + 1, lanes], shape).astype(BF16)
                for c in range(chunks):
                    rows = slice(c * PEER_ROWS, (c + 1) * PEER_ROWS)
                    sel = r2_ref[h, rows, lanes] < cnt
                    gates[c] = gates[c] + jnp.where(sel, e2_ref[h, rows, lanes], zero) * e1
            for c in range(chunks):
                rows = slice(a * N_KEYS + c * PEER_ROWS, a * N_KEYS + (c + 1) * PEER_ROWS)
                half = 0.5 * pre_ref[rows, lanes]
                gelu = half + half * lax.erf(half * (2.0 ** 0.5))
                act_ref[rows, lanes] = gelu.astype(BF16) * gates[c]


def _peer_kernel(xnt_ref, d_ref, ut_ref, r2_ref, e2_ref, c1_ref, e1_ref, h1_ref, nw_ref,
                 o_ref, acc_ref, pre_ref, act_ref):
    j = pl.program_id(1)
    nsub = d_ref.shape[0] // PEER_SUB

    @pl.when(j == 0)
    def _():
        acc_ref[...] = jnp.zeros_like(acc_ref)

    def down(sub):
        pre_ref[sub % 2] = _dot(d_ref[sub * PEER_SUB:(sub + 1) * PEER_SUB, :], xnt_ref[...])

    down(0)
    for sub in range(nsub):
        if sub + 1 < nsub:
            down(sub + 1)
        _peer_activations(pre_ref.at[sub % 2], act_ref.at[sub % 2], r2_ref, e2_ref, c1_ref,
                          e1_ref, sub)
        acc_ref[...] += _dot(ut_ref[:, sub * PEER_SUB:(sub + 1) * PEER_SUB], act_ref[sub % 2])

    @pl.when(j == pl.num_programs(1) - 1)
    def _():
        h2 = h1_ref[...] + acc_ref[...].T
        o_ref[...] = _rms(h2, nw_ref[...])


def _peer(xnt, down, up_t, r2, e2, c1, e1, h1, norm_w):
    t = h1.shape[0]
    tm = min(PEER_TM, t)
    te = PEER_TE
    key2_spec = pl.BlockSpec((PEER_HEADS, N_KEYS, tm), lambda i, j: (0, 0, i))
    key1_spec = pl.BlockSpec((PEER_HEADS, te // N_KEYS, tm), lambda i, j: (0, j, i))
    return pl.pallas_call(
        _peer_kernel,
        grid=(t // tm, N_EXPERTS // te),
        in_specs=[
            pl.BlockSpec((D_MODEL, tm), lambda i, j: (0, i)),
            pl.BlockSpec((te, D_MODEL), lambda i, j: (j, 0)),
            pl.BlockSpec((D_MODEL, te), lambda i, j: (0, j)),
            key2_spec, key2_spec, key1_spec, key1_spec,
            pl.BlockSpec((tm, D_MODEL), lambda i, j: (i, 0)),
            pl.BlockSpec((1, D_MODEL), lambda i, j: (0, 0)),
        ],
        out_specs=pl.BlockSpec((tm, D_MODEL), lambda i, j: (i, 0)),
        out_shape=jax.ShapeDtypeStruct((t, D_MODEL), F32),
        scratch_shapes=[
            pltpu.VMEM((D_MODEL, tm), F32),
            pltpu.VMEM((2, PEER_SUB, tm), F32),
            pltpu.VMEM((2, PEER_SUB, tm), BF16),
        ],
        compiler_params=_params("parallel", "arbitrary"),
        name="peer",
    )(xnt, down, up_t, r2, e2, c1, e1, h1, norm_w)


def _pad_lanes(v, width):
    return jnp.pad(v.astype(F32), (0, width - v.shape[0])).reshape(1, width)


def _layer(x, norm_mix, w_in, conv_w, conv_b, dt_bias, a_log, d_skip, ssd_norm_w, attn_sinks,
           w_out, rel_bias, norm_ffn, w_query, sub_keys1, sub_keys2, expert_down, expert_up,
           out_norm):
    b, s, d = x.shape
    t = b * s
    x2 = x.reshape(t, d)

    w_in_pad = jnp.pad(w_in, ((0, 0), (0, IN_COLS_PAD - w_in.shape[1]))).astype(BF16)
    q, k, v, z, xbc, dt = _inproj(x2, norm_mix.reshape(1, d), w_in_pad)

    qi = jnp.arange(ATTN_BLOCK)[:, None]
    kj = jnp.arange(ATTN_BLOCK)[None, :]
    bucket = _t5_bucket(jnp.where(kj <= qi, qi - kj, qi + ATTN_BLOCK - kj)).astype(jnp.int32)
    attn = _attention(q.reshape(b, s, -1), k.reshape(b, s, -1), v.reshape(b, s, -1),
                      bucket, rel_bias.astype(F32), attn_sinks.astype(F32))

    lane = jnp.arange(SSD_INNER)[None, :] // SSD_HEAD_DIM
    expand = (jnp.arange(DT_PAD)[:, None] == lane).astype(F32)
    tri = (jnp.arange(SSD_CHUNK)[None, :] <= jnp.arange(SSD_CHUNK)[:, None]).astype(F32)
    ssd = _ssd(xbc.reshape(b, s, -1), z.reshape(b, s, -1), dt.reshape(b, s, -1),
               conv_w, conv_b.reshape(1, -1), _pad_lanes(dt_bias, DT_PAD),
               _pad_lanes(a_log, DT_PAD), jnp.repeat(d_skip.astype(F32), SSD_HEAD_DIM).reshape(1, -1),
               ssd_norm_w.reshape(1, -1), expand, tri)

    h1, xnt, pq = _outproj(x2, attn.reshape(t, -1), ssd.reshape(t, -1), w_out.astype(BF16),
                           norm_ffn.reshape(1, d), w_query.astype(BF16))
    r2, e2, c1, e1 = _route(pq, sub_keys1.astype(BF16), sub_keys2.astype(BF16))
    out = _peer(xnt, expert_down.astype(BF16), expert_up.T.astype(BF16), r2, e2, c1, e1, h1,
                out_norm.reshape(1, d))
    return out.reshape(b, s, d)


@jax.jit
def kernel(x, norm_mix, w_in, conv_w, conv_b, dt_bias, a_log, d_skip, ssd_norm_w, attn_sinks,
           w_out, rel_bias, norm_ffn, w_query, sub_keys1, sub_keys2, expert_down, expert_up,
           norm_final):
    assert norm_mix.shape[0] == 1, "single-layer block"
    return _layer(x, norm_mix[0], w_in[0], conv_w[0], conv_b[0], dt_bias[0], a_log[0], d_skip[0],
                  ssd_norm_w[0], attn_sinks[0], w_out[0], rel_bias, norm_ffn[0], w_query[0],
                  sub_keys1[0], sub_keys2[0], expert_down[0], expert_up[0], norm_final)
```

```python
import functools
import math

import jax
import jax.numpy as jnp
from jax import lax
from jax.experimental import pallas as pl
from jax.experimental.pallas import tpu as pltpu

F32 = jnp.float32
BF16 = jnp.bfloat16
HIGHEST = lax.Precision.HIGHEST

D_MODEL = 1024
ATTN_HEADS = 8
ATTN_KV_HEADS = 2
ATTN_GROUP = ATTN_HEADS // ATTN_KV_HEADS
HEAD_DIM = 64
ATTN_WIDTH = ATTN_HEADS * HEAD_DIM
ATTN_BLOCK = 128
N_BUCKETS = 32
MAX_DISTANCE = 128
SSD_HEADS = 8
SSD_HEAD_DIM = 64
SSD_INNER = SSD_HEADS * SSD_HEAD_DIM
SSD_STATE = 128
SSD_GROUPS = 2
SSD_HEADS_PER_GROUP = SSD_HEADS // SSD_GROUPS
SSD_CONV = 4
SSD_CHUNK = 128
PEER_HEADS = 8
N_KEYS = 128
N_EXPERTS = N_KEYS * N_KEYS
PEER_HALF = 128
PEER_TOPK = 16
EPS = 1e-6

KV_COLS = ATTN_KV_HEADS * HEAD_DIM
XBC_COLS = SSD_INNER + 2 * SSD_GROUPS * SSD_STATE
OFF_K = ATTN_WIDTH
OFF_V = OFF_K + KV_COLS
OFF_Z = OFF_V + KV_COLS
OFF_XBC = OFF_Z + SSD_INNER
OFF_DT = OFF_XBC + XBC_COLS
LANES = 128
SUBLANES = 8
DT_PAD = LANES
IN_COLS_PAD = OFF_DT + DT_PAD
CONV_HALO = SUBLANES

VMEM_LIMIT = 56 * 1024 * 1024

ATTN_BLOCKS_PER_STEP = 4
INPROJ_TM = 1024
OUTPROJ_TM = 512
ROUTE_TM = SUBLANES * LANES
PEER_TM = 1024
PEER_TE = 1024
PEER_SUB = 512
PEER_LANE_CHUNK = 512
PEER_ROWS = 16


def _dot(a, b, precision=None):
    return jnp.dot(a, b, preferred_element_type=F32, precision=precision)


def _dot_nt(a, b):
    return lax.dot_general(a, b, (((1,), (1,)), ((), ())), preferred_element_type=F32)


def _dot_tn(a, b):
    return lax.dot_general(a, b, (((0,), (0,)), ((), ())), preferred_element_type=F32)


def _rms(x, w):
    return x * lax.rsqrt(jnp.mean(x * x, axis=-1, keepdims=True) + EPS) * w


def _silu(x):
    return x / (1.0 + jnp.exp(-x))


def _softplus(x):
    return jnp.maximum(x, 0.0) + jnp.log1p(jnp.exp(-jnp.abs(x)))


def _params(*sem):
    return pltpu.CompilerParams(dimension_semantics=sem, vmem_limit_bytes=VMEM_LIMIT)


def _inproj_kernel(x_ref, nw_ref, w_ref, q_ref, k_ref, v_ref, z_ref, xbc_ref, dt_ref):
    xb = _rms(x_ref[...], nw_ref[...]).astype(BF16)

    def proj(lo, hi):
        return _dot(xb, w_ref[:, lo:hi])

    q_ref[...] = (proj(0, OFF_K) * (HEAD_DIM ** -0.5)).astype(BF16)
    k_ref[...] = proj(OFF_K, OFF_V).astype(BF16)
    v_ref[...] = proj(OFF_V, OFF_Z).astype(BF16)
    z_ref[...] = proj(OFF_Z, OFF_XBC)
    xbc_ref[...] = proj(OFF_XBC, OFF_DT)
    dt_ref[...] = proj(OFF_DT, IN_COLS_PAD)


def _inproj(x2, norm_w, w_in_pad):
    t = x2.shape[0]
    tm = min(INPROJ_TM, t)
    row = lambda i: (i, 0)
    fixed = lambda i: (0, 0)
    return pl.pallas_call(
        _inproj_kernel,
        grid=(t // tm,),
        in_specs=[
            pl.BlockSpec((tm, D_MODEL), row),
            pl.BlockSpec((1, D_MODEL), fixed),
            pl.BlockSpec((D_MODEL, IN_COLS_PAD), fixed),
        ],
        out_specs=[
            pl.BlockSpec((tm, ATTN_WIDTH), row),
            pl.BlockSpec((tm, KV_COLS), row),
            pl.BlockSpec((tm, KV_COLS), row),
            pl.BlockSpec((tm, SSD_INNER), row),
            pl.BlockSpec((tm, XBC_COLS), row),
            pl.BlockSpec((tm, DT_PAD), row),
        ],
        out_shape=[
            jax.ShapeDtypeStruct((t, ATTN_WIDTH), BF16),
            jax.ShapeDtypeStruct((t, KV_COLS), BF16),
            jax.ShapeDtypeStruct((t, KV_COLS), BF16),
            jax.ShapeDtypeStruct((t, SSD_INNER), F32),
            jax.ShapeDtypeStruct((t, XBC_COLS), F32),
            jax.ShapeDtypeStruct((t, DT_PAD), F32),
        ],
        compiler_params=_params("parallel"),
        name="inproj",
    )(x2, norm_w, w_in_pad)


def _t5_bucket(dist):
    n = jnp.maximum(dist, 0)
    max_exact = N_BUCKETS // 2
    nf = jnp.maximum(n, 1).astype(F32)
    large = max_exact + (jnp.log(nf / max_exact) / math.log(MAX_DISTANCE / max_exact)
                         * (N_BUCKETS - max_exact)).astype(jnp.int32)
    large = jnp.minimum(large, N_BUCKETS - 1)
    return jnp.where(n < max_exact, n, large)


def _attn_kernel(bucket_ref, relb_ref, sink_ref, q_ref, kp_ref, kc_ref, vp_ref, vc_ref,
                 o_ref, bias_ref):
    blk = ATTN_BLOCK
    rows = ATTN_GROUP * blk
    n = pl.program_id(1)

    @pl.when((pl.program_id(0) == 0) & (n == 0))
    def _():
        bucket = bucket_ref[...]
        for h in range(ATTN_HEADS):
            acc = jnp.zeros((blk, blk), F32)
            for b in range(N_BUCKETS):
                acc = jnp.where(bucket == b, relb_ref[b, h], acc)
            hk, g = divmod(h, ATTN_GROUP)
            bias_ref[hk, g * blk:(g + 1) * blk, :] = acc

    qi = lax.broadcasted_iota(jnp.int32, (rows, blk), 0) & (blk - 1)
    kj = lax.broadcasted_iota(jnp.int32, (rows, blk), 1)
    own = kj <= qi
    for i in range(ATTN_BLOCKS_PER_STEP):
        cur = slice(i * blk, (i + 1) * blk)
        kprev, vprev = (kp_ref, vp_ref) if i == 0 else (kc_ref.at[(i - 1) * blk:i * blk],
                                                        vc_ref.at[(i - 1) * blk:i * blk])
        has_prev = (n > 0) if i == 0 else True
        for hk in range(ATTN_KV_HEADS):
            ksl = slice(hk * HEAD_DIM, (hk + 1) * HEAD_DIM)
            kband = jnp.concatenate([kprev[:, ksl], kc_ref[cur, ksl]], axis=0)
            vband = jnp.concatenate([vprev[:, ksl], vc_ref[cur, ksl]], axis=0)
            qg = jnp.concatenate(
                [q_ref[cur, (hk * ATTN_GROUP + g) * HEAD_DIM:(hk * ATTN_GROUP + g + 1) * HEAD_DIM]
                 for g in range(ATTN_GROUP)], axis=0)
            s2 = _dot_nt(qg, kband)
            s = jnp.where(own, s2[:, blk:], s2[:, :blk]) + bias_ref[hk]
            if i == 0:
                s = jnp.where(own | has_prev, s, -jnp.inf)
            sink = jnp.concatenate(
                [jnp.full((blk, 1), sink_ref[hk * ATTN_GROUP + g], F32)
                 for g in range(ATTN_GROUP)], axis=0)
            m = jnp.maximum(jnp.max(s, axis=-1, keepdims=True), sink)
            p = jnp.exp(s - m)
            denom = jnp.sum(p, axis=-1, keepdims=True) + jnp.exp(sink - m)
            pb = p.astype(BF16)
            zero = jnp.zeros_like(pb)
            pfull = jnp.concatenate([jnp.where(own, zero, pb), jnp.where(own, pb, zero)], axis=1)
            o = _dot(pfull, vband) / denom
            for g in range(ATTN_GROUP):
                h = hk * ATTN_GROUP + g
                o_ref[cur, h * HEAD_DIM:(h + 1) * HEAD_DIM] = o[g * blk:(g + 1) * blk].astype(BF16)


def _attention(q, k, v, bucket, rel_bias, sinks):
    b, s, _ = q.shape
    rows = ATTN_BLOCKS_PER_STEP * ATTN_BLOCK
    assert s % rows == 0
    cur = lambda i, j: (i, j, 0)
    prev = lambda i, j: (i, jnp.maximum(j * ATTN_BLOCKS_PER_STEP - 1, 0), 0)
    smem = pl.BlockSpec(memory_space=pltpu.SMEM)
    return pl.pallas_call(
        _attn_kernel,
        grid=(b, s // rows),
        in_specs=[
            pl.BlockSpec((ATTN_BLOCK, ATTN_BLOCK), lambda i, j: (0, 0)),
            smem, smem,
            pl.BlockSpec((None, rows, ATTN_WIDTH), cur),
            pl.BlockSpec((None, ATTN_BLOCK, KV_COLS), prev),
            pl.BlockSpec((None, rows, KV_COLS), cur),
            pl.BlockSpec((None, ATTN_BLOCK, KV_COLS), prev),
            pl.BlockSpec((None, rows, KV_COLS), cur),
        ],
        out_specs=pl.BlockSpec((None, rows, ATTN_WIDTH), cur),
        out_shape=jax.ShapeDtypeStruct((b, s, ATTN_WIDTH), BF16),
        scratch_shapes=[pltpu.VMEM((ATTN_KV_HEADS, ATTN_GROUP * ATTN_BLOCK, ATTN_BLOCK), F32)],
        compiler_params=_params("arbitrary", "arbitrary"),
        name="swa",
    )(bucket, rel_bias, sinks, q, k, k, v, v)


def _ssd_kernel(xbc_ref, z_ref, dt_ref, convw_ref, convb_ref, dtb_ref, alog_ref, dskip_ref,
                nw_ref, expand_ref, tri_ref, o_ref, xpad_ref, state_ref):
    L = SSD_CHUNK
    P = SSD_HEAD_DIM
    c = pl.program_id(1)

    @pl.when(c == 0)
    def _():
        xpad_ref[0:CONV_HALO, :] = jnp.zeros((CONV_HALO, XBC_COLS), F32)
        state_ref[...] = jnp.zeros_like(state_ref)

    @pl.when(c > 0)
    def _():
        xpad_ref[0:CONV_HALO, :] = xpad_ref[L:L + CONV_HALO, :]

    xpad_ref[CONV_HALO:CONV_HALO + L, :] = xbc_ref[...]
    conv = convb_ref[...]
    for j in range(SSD_CONV):
        off = CONV_HALO - (SSD_CONV - 1) + j
        conv = conv + convw_ref[j:j + 1, :] * xpad_ref[off:off + L, :]
    xbc = _silu(conv)
    xs = xbc[:, :SSD_INNER]
    bm = xbc[:, SSD_INNER:SSD_INNER + SSD_GROUPS * SSD_STATE].astype(BF16)
    cm = xbc[:, SSD_INNER + SSD_GROUPS * SSD_STATE:].astype(BF16)

    expand = expand_ref[...]
    dt = _softplus(dt_ref[...] + dtb_ref[...])
    a = dt * (-jnp.exp(alog_ref[...]))
    a_cum = _dot(tri_ref[...], a, HIGHEST)
    a_last = a_cum[L - 1:L, :]
    x_dt = xs * _dot(dt, expand, HIGHEST)
    x_end = (x_dt * _dot(jnp.exp(a_last - a_cum), expand, HIGHEST)).astype(BF16)
    dec_start = _dot(jnp.exp(a_cum), expand, HIGHEST)
    x_dt = x_dt.astype(BF16)
    a_cum_t = a_cum.T

    li = lax.broadcasted_iota(jnp.int32, (L, L), 0)
    si = lax.broadcasted_iota(jnp.int32, (L, L), 1)
    causal = si <= li
    ys = []
    for g in range(SSD_GROUPS):
        bg = bm[:, g * SSD_STATE:(g + 1) * SSD_STATE]
        cg = cm[:, g * SSD_STATE:(g + 1) * SSD_STATE]
        cb = _dot_nt(cg, bg)
        for r in range(SSD_HEADS_PER_GROUP):
            h = g * SSD_HEADS_PER_GROUP + r
            sl = slice(h * P, (h + 1) * P)
            seg = a_cum[:, h:h + 1] - a_cum_t[h:h + 1, :]
            lmat = jnp.where(causal, jnp.exp(jnp.where(causal, seg, 0.0)), 0.0)
            h_prev = state_ref[sl, :]
            y_diag = _dot((cb * lmat).astype(BF16), x_dt[:, sl])
            y_off = _dot_nt(cg, h_prev.astype(BF16)) * dec_start[:, sl]
            ys.append(y_diag + y_off)
            chunk_decay = jnp.exp(a_cum_t[h:h + 1, L - 1:L])
            state_ref[sl, :] = h_prev * chunk_decay + _dot_tn(x_end[:, sl], bg)
    y = jnp.concatenate(ys, axis=1) + dskip_ref[...] * xs
    y = y * _silu(z_ref[...])
    o_ref[...] = _rms(y, nw_ref[...]).astype(BF16)


def _ssd(xbc, z, dt, conv_w, conv_b, dt_bias_pad, a_log_pad, d_skip_wide, norm_w, expand, tri):
    b, s, _ = xbc.shape
    nc = s // SSD_CHUNK
    cur = lambda i, j: (i, j, 0)
    fixed = lambda i, j: (0, 0)
    full = lambda shape: pl.BlockSpec(shape, fixed)
    return pl.pallas_call(
        _ssd_kernel,
        grid=(b, nc),
        in_specs=[
            pl.BlockSpec((None, SSD_CHUNK, XBC_COLS), cur),
            pl.BlockSpec((None, SSD_CHUNK, SSD_INNER), cur),
            pl.BlockSpec((None, SSD_CHUNK, DT_PAD), cur),
            full((SSD_CONV, XBC_COLS)), full((1, XBC_COLS)), full((1, DT_PAD)), full((1, DT_PAD)),
            full((1, SSD_INNER)), full((1, SSD_INNER)), full((DT_PAD, SSD_INNER)),
            full((SSD_CHUNK, SSD_CHUNK)),
        ],
        out_specs=pl.BlockSpec((None, SSD_CHUNK, SSD_INNER), cur),
        out_shape=jax.ShapeDtypeStruct((b, s, SSD_INNER), BF16),
        scratch_shapes=[
            pltpu.VMEM((CONV_HALO + SSD_CHUNK, XBC_COLS), F32),
            pltpu.VMEM((SSD_INNER, SSD_STATE), F32),
        ],
        compiler_params=_params("arbitrary", "arbitrary"),
        name="ssd",
    )(xbc, z, dt, conv_w, conv_b, dt_bias_pad, a_log_pad, d_skip_wide, norm_w, expand, tri)


def _outproj_kernel(x_ref, attn_ref, ssd_ref, wo_ref, nw_ref, wq_ref, h1_ref, xnt_ref, q_ref):
    mix = (_dot(attn_ref[...], wo_ref[0:ATTN_WIDTH, :])
           + _dot(ssd_ref[...], wo_ref[ATTN_WIDTH:ATTN_WIDTH + SSD_INNER, :]))
    h1 = x_ref[...] + mix
    h1_ref[...] = h1
    xn = _rms(h1, nw_ref[...])
    xnt_ref[...] = pltpu.bitcast(xn.T.astype(BF16), jnp.uint32)
    qf = _dot(xn.astype(BF16), wq_ref[...])
    for j in range(2 * PEER_HEADS):
        q_ref[j] = qf[:, j * PEER_HALF:(j + 1) * PEER_HALF].astype(BF16)


def _outproj(x2, attn, ssd, w_out, norm_w, w_query):
    t = x2.shape[0]
    tm = min(OUTPROJ_TM, t)
    row = lambda i: (i, 0)
    fixed = lambda i: (0, 0)
    nq = 2 * PEER_HEADS
    return pl.pallas_call(
        _outproj_kernel,
        grid=(t // tm,),
        in_specs=[
            pl.BlockSpec((tm, D_MODEL), row),
            pl.BlockSpec((tm, ATTN_WIDTH), row),
            pl.BlockSpec((tm, SSD_INNER), row),
            pl.BlockSpec((ATTN_WIDTH + SSD_INNER, D_MODEL), fixed),
            pl.BlockSpec((1, D_MODEL), fixed),
            pl.BlockSpec((D_MODEL, nq * PEER_HALF), fixed),
        ],
        out_specs=[
            pl.BlockSpec((tm, D_MODEL), row),
            pl.BlockSpec((D_MODEL // 2, tm), lambda i: (0, i)),
            pl.BlockSpec((nq, tm, PEER_HALF), lambda i: (0, i, 0)),
        ],
        out_shape=[
            jax.ShapeDtypeStruct((t, D_MODEL), F32),
            jax.ShapeDtypeStruct((D_MODEL // 2, t), jnp.uint32),
            jax.ShapeDtypeStruct((nq, t, PEER_HALF), BF16),
        ],
        compiler_params=_params("parallel"),
        name="outproj",
    )(x2, attn, ssd, w_out, norm_w, w_query)


def _ce(x, hi, lo):
    a, b = x[hi], x[lo]
    if b is None:
        return
    if a is None:
        x[hi], x[lo] = b, None
        return
    x[hi], x[lo] = jnp.maximum(a, b), jnp.minimum(a, b)


def _bitonic_merge_desc(x):
    n = len(x)
    j = n // 2
    while j >= 1:
        for i in range(n):
            if i & j == 0:
                _ce(x, i, i | j)
        j //= 2


def _sort_desc(x):
    n = len(x)
    k = 2
    while k <= n:
        j = k // 2
        while j >= 1:
            for i in range(n):
                l = i ^ j
                if l > i:
                    if i & k == 0:
                        _ce(x, i, l)
                    else:
                        _ce(x, l, i)
            j //= 2
        k *= 2


def _merge_top(a, b, sort=True):
    n = len(a)
    out = []
    for i in range(n):
        u, v = a[i], b[n - 1 - i]
        out.append(v if u is None else u if v is None else jnp.maximum(u, v))
    if sort:
        _bitonic_merge_desc(out)
    return out


def _key_rows(k):
    return slice(k * SUBLANES, (k + 1) * SUBLANES)


def _top16_sorted(s_ref):
    groups = []
    for g in range(N_KEYS // PEER_TOPK):
        x = [s_ref[_key_rows(g * PEER_TOPK + i), :] for i in range(PEER_TOPK)]
        _sort_desc(x)
        groups.append(x)
    while len(groups) > 1:
        groups = [_merge_top(groups[i], groups[i + 1]) for i in range(0, len(groups), 2)]
    return groups[0]


def _route_kernel(q_ref, k1_ref, k2_ref, r2_ref, e2_ref, c1_ref, e1_ref,
                  s1_ref, s2_ref, r2s_ref, e2s_ref, c1s_ref, e1s_ref):
    def head(h, carry):
        for half, (k_ref, s_ref) in enumerate(((k1_ref, s1_ref), (k2_ref, s2_ref))):
            qh = q_ref[2 * h + half]
            qcat = jnp.concatenate([qh[j * LANES:(j + 1) * LANES, :] for j in range(SUBLANES)],
                                   axis=1)
            s_ref[...] = _dot_nt(k_ref[...], qcat)
        v1 = _top16_sorted(s1_ref)
        v2 = _top16_sorted(s2_ref)

        lists = []
        for a in range(PEER_TOPK):
            n_b = PEER_TOPK // (a + 1)
            lists.append([v1[a] + v2[b] if b < n_b else None for b in range(PEER_TOPK)])
        while len(lists) > 2:
            lists = [_merge_top(lists[i], lists[i + 1]) for i in range(0, len(lists), 2)]
        top = _merge_top(lists[0], lists[1], sort=False)
        tau = top[0]
        for x in top[1:]:
            tau = jnp.minimum(tau, x)

        e1_top = [jnp.exp(v - v1[0]) for v in v1]
        e2_top = [jnp.exp(v - v2[0]) for v in v2]
        zsum = jnp.zeros_like(tau)
        cnt = []
        for a in range(PEER_TOPK):
            n_a = jnp.zeros_like(tau)
            w_a = jnp.zeros_like(tau)
            for b in range(PEER_TOPK // (a + 1)):
                sel = (v1[a] + v2[b]) >= tau
                n_a = n_a + jnp.where(sel, 1.0, 0.0)
                w_a = w_a + jnp.where(sel, e2_top[b], 0.0)
            cnt.append(n_a)
            zsum = zsum + e1_top[a] * w_a
        half_inv_z = 0.5 / zsum

        for k in range(N_KEYS):
            rows = _key_rows(k)
            s1 = s1_ref[rows, :]
            s2 = s2_ref[rows, :]
            c1 = jnp.zeros_like(tau)
            for a in reversed(range(PEER_TOPK)):
                c1 = jnp.where(s1 >= v1[a], cnt[a], c1)
            r2 = jnp.zeros_like(tau)
            for b in range(PEER_TOPK):
                r2 = jnp.where(v2[b] > s2, float(b + 1), r2)
            c1s_ref[rows, :] = c1
            r2s_ref[rows, :] = r2
            e1s_ref[rows, :] = jnp.exp(s1 - v1[0]) * half_inv_z
            e2s_ref[rows, :] = jnp.exp(s2 - v2[0])

        for j in range(SUBLANES):
            cols = slice(j * LANES, (j + 1) * LANES)
            take = pl.ds(j, N_KEYS, stride=SUBLANES)
            r2_ref[h, :, cols] = r2s_ref[take, :].astype(BF16)
            e2_ref[h, :, cols] = e2s_ref[take, :].astype(BF16)
            c1_ref[h, :, cols] = c1s_ref[take, :]
            e1_ref[h, :, cols] = e1s_ref[take, :]
        return carry

    lax.fori_loop(0, PEER_HEADS, head, 0)


def _expand_keys(keys):
    eye = jnp.eye(SUBLANES, dtype=keys.dtype)
    return jnp.einsum("kd,jc->kjcd", keys, eye).reshape(N_KEYS * SUBLANES, SUBLANES * PEER_HALF)


def _route(q, keys1, keys2):
    nq, t, _ = q.shape
    tm = ROUTE_TM
    assert t % tm == 0
    fixed = lambda i: (0, 0)
    out_spec = pl.BlockSpec((PEER_HEADS, N_KEYS, tm), lambda i: (0, 0, i))
    shape = (PEER_HEADS, N_KEYS, t)
    key_spec = pl.BlockSpec((N_KEYS * SUBLANES, SUBLANES * PEER_HALF), fixed)
    stage = pltpu.VMEM((N_KEYS * SUBLANES, LANES), F32)
    return pl.pallas_call(
        _route_kernel,
        grid=(t // tm,),
        in_specs=[pl.BlockSpec((nq, tm, PEER_HALF), lambda i: (0, i, 0)), key_spec, key_spec],
        out_specs=[out_spec] * 4,
        out_shape=[jax.ShapeDtypeStruct(shape, BF16)] * 2 + [jax.ShapeDtypeStruct(shape, F32)] * 2,
        scratch_shapes=[stage] * 6,
        compiler_params=_params("parallel"),
        name="route",
    )(q, _expand_keys(keys1), _expand_keys(keys2))


def _peer_activations(pre_ref, act_ref, r2_ref, e2_ref, c1_ref, e1_ref, sub):
    tm = pre_ref.shape[1]
    chunks = N_KEYS // PEER_ROWS
    shape = (PEER_ROWS, PEER_LANE_CHUNK)
    zero = jnp.zeros(shape, BF16)
    for a in range(PEER_SUB // N_KEYS):
        key1 = sub * (PEER_SUB // N_KEYS) + a
        for lc in range(tm // PEER_LANE_CHUNK):
            lanes = slice(lc * PEER_LANE_CHUNK, (lc + 1) * PEER_LANE_CHUNK)
            gates = [zero] * chunks
            for h in range(PEER_HEADS):
                cnt = jnp.broadcast_to(c1_ref[h, key1:key1 + 1, lanes], shape).astype(BF16)
                e1 = jnp.broadcast_to(e1_ref[h, key1:key1 + 1, lanes], shape).astype(BF16)
                for c in range(chunks):
                    rows = slice(c * PEER_ROWS, (c + 1) * PEER_ROWS)
                    sel = r2_ref[h, rows, lanes] < cnt
                    gates[c] = gates[c] + jnp.where(sel, e2_ref[h, rows, lanes], zero) * e1
            for c in range(chunks):
                rows = slice(a * N_KEYS + c * PEER_ROWS, a * N_KEYS + (c + 1) * PEER_ROWS)
                p = pre_ref[rows, lanes]
                gelu2 = p + p * lax.erf(p * (2.0 ** -0.5))
                act_ref[rows, lanes] = gelu2.astype(BF16) * gates[c]


def _peer_kernel(xnt_ref, d_ref, ut_ref, r2_ref, e2_ref, c1_ref, e1_ref, h1_ref, nw_ref,
                 o_ref, acc_ref, pre_ref, act_ref):
    j = pl.program_id(1)
    nsub = ut_ref.shape[1] // PEER_SUB

    @pl.when(j == 0)
    def _():
        acc_ref[...] = jnp.zeros_like(acc_ref)

    def down(sub):
        d_sub = d_ref[sub * (PEER_SUB // 2):(sub + 1) * (PEER_SUB // 2), :]
        pre_ref[sub % 2] = _dot(pltpu.bitcast(d_sub, BF16), pltpu.bitcast(xnt_ref[...], BF16))

    down(0)
    for sub in range(nsub):
        if sub + 1 < nsub:
            down(sub + 1)
        _peer_activations(pre_ref.at[sub % 2], act_ref.at[sub % 2], r2_ref, e2_ref, c1_ref,
                          e1_ref, sub)
        ut_sub = ut_ref[:, sub * PEER_SUB:(sub + 1) * PEER_SUB]
        acc_ref[...] += _dot(pltpu.bitcast(ut_sub, BF16), act_ref[sub % 2])

    @pl.when(j == pl.num_programs(1) - 1)
    def _():
        h2 = h1_ref[...] + acc_ref[...].T
        o_ref[...] = _rms(h2, nw_ref[...])


def _pack_kernel(w_ref, o_ref, *, transpose):
    w = w_ref[...].T if transpose else w_ref[...]
    o_ref[...] = pltpu.bitcast(w.astype(BF16), jnp.uint32)


def _pack_experts(w, transpose):
    tiles = N_EXPERTS // PEER_TE
    if transpose:
        out_spec = pl.BlockSpec((None, D_MODEL // 2, PEER_TE), lambda i: (i, 0, 0))
        out_shape = jax.ShapeDtypeStruct((tiles, D_MODEL // 2, PEER_TE), jnp.uint32)
    else:
        out_spec = pl.BlockSpec((PEER_TE // 2, D_MODEL), lambda i: (i, 0))
        out_shape = jax.ShapeDtypeStruct((N_EXPERTS // 2, D_MODEL), jnp.uint32)
    return pl.pallas_call(
        functools.partial(_pack_kernel, transpose=transpose),
        grid=(tiles,),
        in_specs=[pl.BlockSpec((PEER_TE, D_MODEL), lambda i: (i, 0))],
        out_specs=out_spec,
        out_shape=out_shape,
        compiler_params=_params("parallel"),
        name="pack_up" if transpose else "pack_down",
    )(w)


def _peer(xnt, down, up_t, r2, e2, c1, e1, h1, norm_w):
    t = h1.shape[0]
    tm = min(PEER_TM, t)
    te = PEER_TE
    key2_spec = pl.BlockSpec((PEER_HEADS, N_KEYS, tm), lambda i, j: (0, 0, i))
    key1_spec = pl.BlockSpec((PEER_HEADS, te // N_KEYS, tm), lambda i, j: (0, j, i))
    return pl.pallas_call(
        _peer_kernel,
        grid=(t // tm, N_EXPERTS // te),
        in_specs=[
            pl.BlockSpec((D_MODEL // 2, tm), lambda i, j: (0, i)),
            pl.BlockSpec((te // 2, D_MODEL), lambda i, j: (j, 0)),
            pl.BlockSpec((None, D_MODEL // 2, te), lambda i, j: (j, 0, 0)),
            key2_spec, key2_spec, key1_spec, key1_spec,
            pl.BlockSpec((tm, D_MODEL), lambda i, j: (i, 0)),
            pl.BlockSpec((1, D_MODEL), lambda i, j: (0, 0)),
        ],
        out_specs=pl.BlockSpec((tm, D_MODEL), lambda i, j: (i, 0)),
        out_shape=jax.ShapeDtypeStruct((t, D_MODEL), F32),
        scratch_shapes=[
            pltpu.VMEM((D_MODEL, tm), F32),
            pltpu.VMEM((2, PEER_SUB, tm), F32),
            pltpu.VMEM((2, PEER_SUB, tm), BF16),
        ],
        compiler_params=_params("parallel", "arbitrary"),
        name="peer",
    )(xnt, down, up_t, r2, e2, c1, e1, h1, norm_w)


def _pad_lanes(v, width):
    return jnp.pad(v.astype(F32), (0, width - v.shape[0])).reshape(1, width)


def _layer(x, norm_mix, w_in, conv_w, conv_b, dt_bias, a_log, d_skip, ssd_norm_w, attn_sinks,
           w_out, rel_bias, norm_ffn, w_query, sub_keys1, sub_keys2, expert_down, expert_up,
           out_norm):
    b, s, d = x.shape
    t = b * s
    x2 = x.reshape(t, d)

    w_in_pad = jnp.pad(w_in, ((0, 0), (0, IN_COLS_PAD - w_in.shape[1]))).astype(BF16)
    q, k, v, z, xbc, dt = _inproj(x2, norm_mix.reshape(1, d), w_in_pad)

    qi = jnp.arange(ATTN_BLOCK)[:, None]
    kj = jnp.arange(ATTN_BLOCK)[None, :]
    bucket = _t5_bucket(jnp.where(kj <= qi, qi - kj, qi + ATTN_BLOCK - kj)).astype(jnp.int32)
    attn = _attention(q.reshape(b, s, -1), k.reshape(b, s, -1), v.reshape(b, s, -1),
                      bucket, rel_bias.astype(F32), attn_sinks.astype(F32))

    lane = jnp.arange(SSD_INNER)[None, :] // SSD_HEAD_DIM
    expand = (jnp.arange(DT_PAD)[:, None] == lane).astype(F32)
    tri = (jnp.arange(SSD_CHUNK)[None, :] <= jnp.arange(SSD_CHUNK)[:, None]).astype(F32)
    ssd = _ssd(xbc.reshape(b, s, -1), z.reshape(b, s, -1), dt.reshape(b, s, -1),
               conv_w, conv_b.reshape(1, -1), _pad_lanes(dt_bias, DT_PAD),
               _pad_lanes(a_log, DT_PAD), jnp.repeat(d_skip.astype(F32), SSD_HEAD_DIM).reshape(1, -1),
               ssd_norm_w.reshape(1, -1), expand, tri)

    h1, xnt, pq = _outproj(x2, attn.reshape(t, -1), ssd.reshape(t, -1), w_out.astype(BF16),
                           norm_ffn.reshape(1, d), w_query.astype(BF16))
    r2, e2, c1, e1 = _route(pq, sub_keys1.astype(BF16), sub_keys2.astype(BF16))
    out = _peer(xnt, _pack_experts(expert_down, False), _pack_experts(expert_up, True),
                r2, e2, c1, e1, h1, out_norm.reshape(1, d))
    return out.reshape(b, s, d)


@jax.jit
def kernel(x, norm_mix, w_in, conv_w, conv_b, dt_bias, a_log, d_skip, ssd_norm_w, attn_sinks,
           w_out, rel_bias, norm_ffn, w_query, sub_keys1, sub_keys2, expert_down, expert_up,
           norm_final):
    assert norm_mix.shape[0] == 1, "single-layer block"
    return _layer(x, norm_mix[0], w_in[0], conv_w[0], conv_b[0], dt_bias[0], a_log[0], d_skip[0],
                  ssd_norm_w[0], attn_sinks[0], w_out[0], rel_bias, norm_ffn[0], w_query[0],
                  sub_keys1[0], sub_keys2[0], expert_down[0], expert_up[0], norm_final)
```

```python
import functools
import math

import jax
import jax.numpy as jnp
from jax import lax
from jax.experimental import pallas as pl
from jax.experimental.pallas import tpu as pltpu

F32 = jnp.float32
BF16 = jnp.bfloat16

D_MODEL = 1024
ATTN_HEADS = 8
ATTN_KV_HEADS = 2
ATTN_GROUP = ATTN_HEADS // ATTN_KV_HEADS
HEAD_DIM = 64
ATTN_WIDTH = ATTN_HEADS * HEAD_DIM
ATTN_BLOCK = 128
N_BUCKETS = 32
MAX_DISTANCE = 128
SSD_HEADS = 8
SSD_HEAD_DIM = 64
SSD_INNER = SSD_HEADS * SSD_HEAD_DIM
SSD_STATE = 128
SSD_GROUPS = 2
SSD_HEADS_PER_GROUP = SSD_HEADS // SSD_GROUPS
SSD_CONV = 4
SSD_CHUNK = 128
PEER_HEADS = 8
N_KEYS = 128
N_EXPERTS = N_KEYS * N_KEYS
PEER_HALF = 128
PEER_TOPK = 16
EPS = 1e-6

KV_COLS = ATTN_KV_HEADS * HEAD_DIM
XBC_COLS = SSD_INNER + 2 * SSD_GROUPS * SSD_STATE
OFF_K = ATTN_WIDTH
OFF_V = OFF_K + KV_COLS
OFF_Z = OFF_V + KV_COLS
OFF_XBC = OFF_Z + SSD_INNER
OFF_DT = OFF_XBC + XBC_COLS
LANES = 128
SUBLANES = 8
DT_PAD = LANES
IN_COLS_PAD = OFF_DT + DT_PAD
CONV_HALO = SUBLANES

VMEM_LIMIT = 56 * 1024 * 1024

ATTN_BLOCKS_PER_STEP = 4
INPROJ_TM = 1024
OUTPROJ_TM = 512
ROUTE_TM = SUBLANES * LANES
PEER_TM = 1024
PEER_TE = 1024
PEER_SUB = 512
PEER_LANE_CHUNK = 512
PEER_ROWS = 16


def _dot(a, b):
    return jnp.dot(a, b, preferred_element_type=F32)


def _dot_nt(a, b):
    return lax.dot_general(a, b, (((1,), (1,)), ((), ())), preferred_element_type=F32)


def _dot_tn(a, b):
    return lax.dot_general(a, b, (((0,), (0,)), ((), ())), preferred_element_type=F32)


def _rms(x, w):
    return x * lax.rsqrt(jnp.mean(x * x, axis=-1, keepdims=True) + EPS) * w


def _silu(x):
    return x / (1.0 + jnp.exp(-x))


def _softplus(x):
    return jnp.maximum(x, 0.0) + jnp.log1p(jnp.exp(-jnp.abs(x)))


def _split3(x):
    hi = x.astype(BF16)
    rest = x - hi.astype(F32)
    mid = rest.astype(BF16)
    return hi, mid, (rest - mid.astype(F32)).astype(BF16)


def _params(*sem):
    return pltpu.CompilerParams(dimension_semantics=sem, vmem_limit_bytes=VMEM_LIMIT)


def _inproj_kernel(x_ref, nw_ref, w_ref, q_ref, k_ref, v_ref, z_ref, xbc_ref, dt_ref):
    xb = _rms(x_ref[...], nw_ref[...]).astype(BF16)

    def proj(lo, hi):
        return _dot(xb, w_ref[:, lo:hi])

    q_ref[...] = (proj(0, OFF_K) * (HEAD_DIM ** -0.5)).astype(BF16)
    k_ref[...] = proj(OFF_K, OFF_V).astype(BF16)
    v_ref[...] = proj(OFF_V, OFF_Z).astype(BF16)
    z_ref[...] = proj(OFF_Z, OFF_XBC)
    xbc_ref[...] = proj(OFF_XBC, OFF_DT)
    dt_ref[...] = proj(OFF_DT, IN_COLS_PAD)


def _inproj(x2, norm_w, w_in_pad):
    t = x2.shape[0]
    tm = min(INPROJ_TM, t)
    row = lambda i: (i, 0)
    fixed = lambda i: (0, 0)
    return pl.pallas_call(
        _inproj_kernel,
        grid=(t // tm,),
        in_specs=[
            pl.BlockSpec((tm, D_MODEL), row),
            pl.BlockSpec((1, D_MODEL), fixed),
            pl.BlockSpec((D_MODEL, IN_COLS_PAD), fixed),
        ],
        out_specs=[
            pl.BlockSpec((tm, ATTN_WIDTH), row),
            pl.BlockSpec((tm, KV_COLS), row),
            pl.BlockSpec((tm, KV_COLS), row),
            pl.BlockSpec((tm, SSD_INNER), row),
            pl.BlockSpec((tm, XBC_COLS), row),
            pl.BlockSpec((tm, DT_PAD), row),
        ],
        out_shape=[
            jax.ShapeDtypeStruct((t, ATTN_WIDTH), BF16),
            jax.ShapeDtypeStruct((t, KV_COLS), BF16),
            jax.ShapeDtypeStruct((t, KV_COLS), BF16),
            jax.ShapeDtypeStruct((t, SSD_INNER), F32),
            jax.ShapeDtypeStruct((t, XBC_COLS), F32),
            jax.ShapeDtypeStruct((t, DT_PAD), F32),
        ],
        compiler_params=_params("parallel"),
        name="inproj",
    )(x2, norm_w, w_in_pad)


def _t5_bucket(dist):
    n = jnp.maximum(dist, 0)
    max_exact = N_BUCKETS // 2
    nf = jnp.maximum(n, 1).astype(F32)
    large = max_exact + (jnp.log(nf / max_exact) / math.log(MAX_DISTANCE / max_exact)
                         * (N_BUCKETS - max_exact)).astype(jnp.int32)
    large = jnp.minimum(large, N_BUCKETS - 1)
    return jnp.where(n < max_exact, n, large)


def _attn_kernel(bucket_ref, relb_ref, sink_ref, q_ref, kp_ref, kc_ref, vp_ref, vc_ref,
                 o_ref, bias_ref):
    blk = ATTN_BLOCK
    rows = ATTN_GROUP * blk
    n = pl.program_id(1)

    @pl.when((pl.program_id(0) == 0) & (n == 0))
    def _():
        bucket = bucket_ref[...]
        for h in range(ATTN_HEADS):
            acc = jnp.zeros((blk, blk), F32)
            for b in range(N_BUCKETS):
                acc = jnp.where(bucket == b, relb_ref[b, h], acc)
            hk, g = divmod(h, ATTN_GROUP)
            bias_ref[hk, g * blk:(g + 1) * blk, :] = acc

    qi = lax.broadcasted_iota(jnp.int32, (rows, blk), 0) & (blk - 1)
    kj = lax.broadcasted_iota(jnp.int32, (rows, blk), 1)
    own = kj <= qi
    for i in range(ATTN_BLOCKS_PER_STEP):
        cur = slice(i * blk, (i + 1) * blk)
        kprev, vprev = (kp_ref, vp_ref) if i == 0 else (kc_ref.at[(i - 1) * blk:i * blk],
                                                        vc_ref.at[(i - 1) * blk:i * blk])
        has_prev = (n > 0) if i == 0 else True
        for hk in range(ATTN_KV_HEADS):
            ksl = slice(hk * HEAD_DIM, (hk + 1) * HEAD_DIM)
            kband = jnp.concatenate([kprev[:, ksl], kc_ref[cur, ksl]], axis=0)
            vband = jnp.concatenate([vprev[:, ksl], vc_ref[cur, ksl]], axis=0)
            qg = jnp.concatenate(
                [q_ref[cur, (hk * ATTN_GROUP + g) * HEAD_DIM:(hk * ATTN_GROUP + g + 1) * HEAD_DIM]
                 for g in range(ATTN_GROUP)], axis=0)
            s2 = _dot_nt(qg, kband)
            s = jnp.where(own, s2[:, blk:], s2[:, :blk]) + bias_ref[hk]
            if i == 0:
                s = jnp.where(own | has_prev, s, -jnp.inf)
            sink = jnp.concatenate(
                [jnp.full((blk, blk), sink_ref[hk * ATTN_GROUP + g], F32)
                 for g in range(ATTN_GROUP)], axis=0)
            m = jnp.maximum(jnp.broadcast_to(jnp.max(s, axis=-1, keepdims=True), s.shape), sink)
            pb = jnp.exp(s - m).astype(BF16)
            zero = jnp.zeros_like(pb)
            pfull = jnp.concatenate([jnp.where(own, zero, pb), jnp.where(own, pb, zero)], axis=1)
            denom = _dot(pfull, jnp.ones((2 * blk, blk), BF16)) + jnp.exp(sink - m)
            o = _dot(pfull, vband) / denom[:, :HEAD_DIM]
            for g in range(ATTN_GROUP):
                h = hk * ATTN_GROUP + g
                o_ref[cur, h * HEAD_DIM:(h + 1) * HEAD_DIM] = o[g * blk:(g + 1) * blk].astype(BF16)


def _attention(q, k, v, bucket, rel_bias, sinks):
    b, s, _ = q.shape
    rows = ATTN_BLOCKS_PER_STEP * ATTN_BLOCK
    assert s % rows == 0
    cur = lambda i, j: (i, j, 0)
    prev = lambda i, j: (i, jnp.maximum(j * ATTN_BLOCKS_PER_STEP - 1, 0), 0)
    smem = pl.BlockSpec(memory_space=pltpu.SMEM)
    return pl.pallas_call(
        _attn_kernel,
        grid=(b, s // rows),
        in_specs=[
            pl.BlockSpec((ATTN_BLOCK, ATTN_BLOCK), lambda i, j: (0, 0)),
            smem, smem,
            pl.BlockSpec((None, rows, ATTN_WIDTH), cur),
            pl.BlockSpec((None, ATTN_BLOCK, KV_COLS), prev),
            pl.BlockSpec((None, rows, KV_COLS), cur),
            pl.BlockSpec((None, ATTN_BLOCK, KV_COLS), prev),
            pl.BlockSpec((None, rows, KV_COLS), cur),
        ],
        out_specs=pl.BlockSpec((None, rows, ATTN_WIDTH), cur),
        out_shape=jax.ShapeDtypeStruct((b, s, ATTN_WIDTH), BF16),
        scratch_shapes=[pltpu.VMEM((ATTN_KV_HEADS, ATTN_GROUP * ATTN_BLOCK, ATTN_BLOCK), F32)],
        compiler_params=_params("arbitrary", "arbitrary"),
        name="swa",
    )(bucket, rel_bias, sinks, q, k, k, v, v)


def _ssd_kernel(xbc_ref, z_ref, dt_ref, convw_ref, convb_ref, dtb_ref, alog_ref, dskip_ref,
                nw_ref, expand_ref, tri_ref, o_ref, xpad_ref, state_ref):
    L = SSD_CHUNK
    P = SSD_HEAD_DIM
    c = pl.program_id(1)

    @pl.when(c == 0)
    def _():
        xpad_ref[0:CONV_HALO, :] = jnp.zeros((CONV_HALO, XBC_COLS), F32)
        state_ref[...] = jnp.zeros_like(state_ref)

    @pl.when(c > 0)
    def _():
        xpad_ref[0:CONV_HALO, :] = xpad_ref[L:L + CONV_HALO, :]

    xpad_ref[CONV_HALO:CONV_HALO + L, :] = xbc_ref[...]
    conv = convb_ref[...]
    for j in range(SSD_CONV):
        off = CONV_HALO - (SSD_CONV - 1) + j
        conv = conv + convw_ref[j:j + 1, :] * xpad_ref[off:off + L, :]
    xbc = _silu(conv)
    xs = xbc[:, :SSD_INNER]
    bm = xbc[:, SSD_INNER:SSD_INNER + SSD_GROUPS * SSD_STATE].astype(BF16)
    cm = xbc[:, SSD_INNER + SSD_GROUPS * SSD_STATE:].astype(BF16)

    expand = expand_ref[...]
    dt = _softplus(dt_ref[...] + dtb_ref[...])
    a = dt * (-jnp.exp(alog_ref[...]))
    a_cum = sum(_dot(tri_ref[...], term) for term in _split3(a))
    a_last = a_cum[L - 1:L, :]
    stacked = jnp.concatenate([dt, jnp.exp(a_last - a_cum), jnp.exp(a_cum)], axis=0)
    wide = sum(_dot(term, expand) for term in _split3(stacked))
    x_dt = xs * wide[0:L]
    x_end = (x_dt * wide[L:2 * L]).astype(BF16)
    dec_start = wide[2 * L:3 * L]
    x_dt = x_dt.astype(BF16)
    a_cum_t = a_cum.T

    li = lax.broadcasted_iota(jnp.int32, (L, L), 0)
    si = lax.broadcasted_iota(jnp.int32, (L, L), 1)
    causal = si <= li
    ys = []
    for g in range(SSD_GROUPS):
        bg = bm[:, g * SSD_STATE:(g + 1) * SSD_STATE]
        cg = cm[:, g * SSD_STATE:(g + 1) * SSD_STATE]
        cb = _dot_nt(cg, bg)
        for r in range(SSD_HEADS_PER_GROUP):
            h = g * SSD_HEADS_PER_GROUP + r
            sl = slice(h * P, (h + 1) * P)
            seg = a_cum[:, h:h + 1] - a_cum_t[h:h + 1, :]
            lmat = jnp.where(causal, jnp.exp(jnp.where(causal, seg, 0.0)), 0.0)
            h_prev = state_ref[sl, :]
            y_diag = _dot((cb * lmat).astype(BF16), x_dt[:, sl])
            y_off = _dot_nt(cg, h_prev.astype(BF16)) * dec_start[:, sl]
            ys.append(y_diag + y_off)
            chunk_decay = jnp.exp(a_cum_t[h:h + 1, L - 1:L])
            state_ref[sl, :] = h_prev * chunk_decay + _dot_tn(x_end[:, sl], bg)
    y = jnp.concatenate(ys, axis=1) + dskip_ref[...] * xs
    y = y * _silu(z_ref[...])
    o_ref[...] = _rms(y, nw_ref[...]).astype(BF16)


def _ssd(xbc, z, dt, conv_w, conv_b, dt_bias_pad, a_log_pad, d_skip_wide, norm_w, expand, tri):
    b, s, _ = xbc.shape
    nc = s // SSD_CHUNK
    cur = lambda i, j: (i, j, 0)
    fixed = lambda i, j: (0, 0)
    full = lambda shape: pl.BlockSpec(shape, fixed)
    return pl.pallas_call(
        _ssd_kernel,
        grid=(b, nc),
        in_specs=[
            pl.BlockSpec((None, SSD_CHUNK, XBC_COLS), cur),
            pl.BlockSpec((None, SSD_CHUNK, SSD_INNER), cur),
            pl.BlockSpec((None, SSD_CHUNK, DT_PAD), cur),
            full((SSD_CONV, XBC_COLS)), full((1, XBC_COLS)), full((1, DT_PAD)), full((1, DT_PAD)),
            full((1, SSD_INNER)), full((1, SSD_INNER)), full((DT_PAD, SSD_INNER)),
            full((SSD_CHUNK, SSD_CHUNK)),
        ],
        out_specs=pl.BlockSpec((None, SSD_CHUNK, SSD_INNER), cur),
        out_shape=jax.ShapeDtypeStruct((b, s, SSD_INNER), BF16),
        scratch_shapes=[
            pltpu.VMEM((CONV_HALO + SSD_CHUNK, XBC_COLS), F32),
            pltpu.VMEM((SSD_INNER, SSD_STATE), F32),
        ],
        compiler_params=_params("arbitrary", "arbitrary"),
        name="ssd",
    )(xbc, z, dt, conv_w, conv_b, dt_bias_pad, a_log_pad, d_skip_wide, norm_w, expand, tri)


def _outproj_kernel(x_ref, attn_ref, ssd_ref, wo_ref, nw_ref, wq_ref, h1_ref, xnt_ref, q_ref):
    mix = (_dot(attn_ref[...], wo_ref[0:ATTN_WIDTH, :])
           + _dot(ssd_ref[...], wo_ref[ATTN_WIDTH:ATTN_WIDTH + SSD_INNER, :]))
    h1 = x_ref[...] + mix
    h1_ref[...] = h1
    xn = _rms(h1, nw_ref[...])
    xnt_ref[...] = pltpu.bitcast(xn.T.astype(BF16), jnp.uint32)
    qf = _dot(xn.astype(BF16), wq_ref[...])
    for j in range(2 * PEER_HEADS):
        q_ref[j] = qf[:, j * PEER_HALF:(j + 1) * PEER_HALF].astype(BF16)


def _outproj(x2, attn, ssd, w_out, norm_w, w_query):
    t = x2.shape[0]
    tm = min(OUTPROJ_TM, t)
    row = lambda i: (i, 0)
    fixed = lambda i: (0, 0)
    nq = 2 * PEER_HEADS
    return pl.pallas_call(
        _outproj_kernel,
        grid=(t // tm,),
        in_specs=[
            pl.BlockSpec((tm, D_MODEL), row),
            pl.BlockSpec((tm, ATTN_WIDTH), row),
            pl.BlockSpec((tm, SSD_INNER), row),
            pl.BlockSpec((ATTN_WIDTH + SSD_INNER, D_MODEL), fixed),
            pl.BlockSpec((1, D_MODEL), fixed),
            pl.BlockSpec((D_MODEL, nq * PEER_HALF), fixed),
        ],
        out_specs=[
            pl.BlockSpec((tm, D_MODEL), row),
            pl.BlockSpec((D_MODEL // 2, tm), lambda i: (0, i)),
            pl.BlockSpec((nq, tm, PEER_HALF), lambda i: (0, i, 0)),
        ],
        out_shape=[
            jax.ShapeDtypeStruct((t, D_MODEL), F32),
            jax.ShapeDtypeStruct((D_MODEL // 2, t), jnp.uint32),
            jax.ShapeDtypeStruct((nq, t, PEER_HALF), BF16),
        ],
        compiler_params=_params("parallel"),
        name="outproj",
    )(x2, attn, ssd, w_out, norm_w, w_query)


def _ce(x, hi, lo):
    a, b = x[hi], x[lo]
    if b is None:
        return
    if a is None:
        x[hi], x[lo] = b, None
        return
    x[hi], x[lo] = jnp.maximum(a, b), jnp.minimum(a, b)


def _bitonic_merge_desc(x):
    n = len(x)
    j = n // 2
    while j >= 1:
        for i in range(n):
            if i & j == 0:
                _ce(x, i, i | j)
        j //= 2


def _sort_desc(x):
    n = len(x)
    k = 2
    while k <= n:
        j = k // 2
        while j >= 1:
            for i in range(n):
                l = i ^ j
                if l > i:
                    if i & k == 0:
                        _ce(x, i, l)
                    else:
                        _ce(x, l, i)
            j //= 2
        k *= 2


def _merge_top(a, b, sort=True):
    n = len(a)
    out = []
    for i in range(n):
        u, v = a[i], b[n - 1 - i]
        out.append(v if u is None else u if v is None else jnp.maximum(u, v))
    if sort:
        _bitonic_merge_desc(out)
    return out


def _key_rows(k):
    return slice(k * SUBLANES, (k + 1) * SUBLANES)


def _top16_sorted(s_ref):
    groups = []
    for g in range(N_KEYS // PEER_TOPK):
        x = [s_ref[_key_rows(g * PEER_TOPK + i), :] for i in range(PEER_TOPK)]
        _sort_desc(x)
        groups.append(x)
    while len(groups) > 1:
        groups = [_merge_top(groups[i], groups[i + 1]) for i in range(0, len(groups), 2)]
    return groups[0]


def _route_kernel(q_ref, k1_ref, k2_ref, r2_ref, e2_ref, c1_ref, e1_ref,
                  s1_ref, s2_ref, r2s_ref, e2s_ref, c1s_ref, e1s_ref):
    def head(h, carry):
        for half, (k_ref, s_ref) in enumerate(((k1_ref, s1_ref), (k2_ref, s2_ref))):
            qh = q_ref[2 * h + half]
            qcat = jnp.concatenate([qh[j * LANES:(j + 1) * LANES, :] for j in range(SUBLANES)],
                                   axis=1)
            s_ref[...] = _dot_nt(k_ref[...], qcat)
        v1 = _top16_sorted(s1_ref)
        v2 = _top16_sorted(s2_ref)

        lists = []
        for a in range(PEER_TOPK):
            n_b = PEER_TOPK // (a + 1)
            lists.append([v1[a] + v2[b] if b < n_b else None for b in range(PEER_TOPK)])
        while len(lists) > 2:
            lists = [_merge_top(lists[i], lists[i + 1]) for i in range(0, len(lists), 2)]
        top = _merge_top(lists[0], lists[1], sort=False)
        tau = top[0]
        for x in top[1:]:
            tau = jnp.minimum(tau, x)

        e1_top = [jnp.exp(v - v1[0]) for v in v1]
        e2_top = [jnp.exp(v - v2[0]) for v in v2]
        zsum = jnp.zeros_like(tau)
        cnt = []
        for a in range(PEER_TOPK):
            n_a = jnp.zeros_like(tau)
            w_a = jnp.zeros_like(tau)
            for b in range(PEER_TOPK // (a + 1)):
                sel = (v1[a] + v2[b]) >= tau
                n_a = n_a + jnp.where(sel, 1.0, 0.0)
                w_a = w_a + jnp.where(sel, e2_top[b], 0.0)
            cnt.append(n_a)
            zsum = zsum + e1_top[a] * w_a
        half_inv_z = 0.5 / zsum

        for k in range(N_KEYS):
            rows = _key_rows(k)
            s1 = s1_ref[rows, :]
            s2 = s2_ref[rows, :]
            c1 = jnp.zeros_like(tau)
            for a in reversed(range(PEER_TOPK)):
                c1 = jnp.where(s1 >= v1[a], cnt[a], c1)
            r2 = jnp.zeros_like(tau)
            for b in range(PEER_TOPK):
                r2 = jnp.where(v2[b] > s2, float(b + 1), r2)
            c1s_ref[rows, :] = c1
            r2s_ref[rows, :] = r2
            e1s_ref[rows, :] = jnp.exp(s1 - v1[0]) * half_inv_z
            e2s_ref[rows, :] = jnp.exp(s2 - v2[0])

        for j in range(SUBLANES):
            cols = slice(j * LANES, (j + 1) * LANES)
            take = pl.ds(j, N_KEYS, stride=SUBLANES)
            r2_ref[h, :, cols] = r2s_ref[take, :].astype(BF16)
            e2_ref[h, :, cols] = e2s_ref[take, :].astype(BF16)
            c1_ref[h, :, cols] = c1s_ref[take, :]
            e1_ref[h, :, cols] = e1s_ref[take, :]
        return carry

    lax.fori_loop(0, PEER_HEADS, head, 0)


def _expand_keys(keys):
    eye = jnp.eye(SUBLANES, dtype=keys.dtype)
    return jnp.einsum("kd,jc->kjcd", keys, eye).reshape(N_KEYS * SUBLANES, SUBLANES * PEER_HALF)


def _route(q, keys1, keys2):
    nq, t, _ = q.shape
    tm = ROUTE_TM
    assert t % tm == 0
    fixed = lambda i: (0, 0)
    out_spec = pl.BlockSpec((PEER_HEADS, N_KEYS, tm), lambda i: (0, 0, i))
    shape = (PEER_HEADS, N_KEYS, t)
    key_spec = pl.BlockSpec((N_KEYS * SUBLANES, SUBLANES * PEER_HALF), fixed)
    stage = pltpu.VMEM((N_KEYS * SUBLANES, LANES), F32)
    return pl.pallas_call(
        _route_kernel,
        grid=(t // tm,),
        in_specs=[pl.BlockSpec((nq, tm, PEER_HALF), lambda i: (0, i, 0)), key_spec, key_spec],
        out_specs=[out_spec] * 4,
        out_shape=[jax.ShapeDtypeStruct(shape, BF16)] * 2 + [jax.ShapeDtypeStruct(shape, F32)] * 2,
        scratch_shapes=[stage] * 6,
        compiler_params=_params("parallel"),
        name="route",
    )(q, _expand_keys(keys1), _expand_keys(keys2))


def _peer_activations(pre_ref, act_ref, r2_ref, e2_ref, c1_ref, e1_ref, sub):
    tm = pre_ref.shape[1]
    chunks = N_KEYS // PEER_ROWS
    shape = (PEER_ROWS, PEER_LANE_CHUNK)
    zero = jnp.zeros(shape, BF16)
    for a in range(PEER_SUB // N_KEYS):
        key1 = sub * (PEER_SUB // N_KEYS) + a
        for lc in range(tm // PEER_LANE_CHUNK):
            lanes = slice(lc * PEER_LANE_CHUNK, (lc + 1) * PEER_LANE_CHUNK)
            gates = [zero] * chunks
            for h in range(PEER_HEADS):
                cnt = jnp.broadcast_to(c1_ref[h, key1:key1 + 1, lanes], shape).astype(BF16)
                e1 = jnp.broadcast_to(e1_ref[h, key1:key1 + 1, lanes], shape).astype(BF16)
                for c in range(chunks):
                    rows = slice(c * PEER_ROWS, (c + 1) * PEER_ROWS)
                    sel = r2_ref[h, rows, lanes] < cnt
                    gates[c] = gates[c] + jnp.where(sel, e2_ref[h, rows, lanes], zero) * e1
            for c in range(chunks):
                rows = slice(a * N_KEYS + c * PEER_ROWS, a * N_KEYS + (c + 1) * PEER_ROWS)
                p = pre_ref[rows, lanes]
                gelu2 = p + p * lax.erf(p * (2.0 ** -0.5))
                act_ref[rows, lanes] = gelu2.astype(BF16) * gates[c]


def _peer_kernel(xnt_ref, d_ref, ut_ref, r2_ref, e2_ref, c1_ref, e1_ref, h1_ref, nw_ref,
                 o_ref, acc_ref, pre_ref, act_ref):
    j = pl.program_id(1)
    nsub = ut_ref.shape[1] // PEER_SUB

    @pl.when(j == 0)
    def _():
        acc_ref[...] = jnp.zeros_like(acc_ref)

    def down(sub):
        d_sub = d_ref[sub * (PEER_SUB // 2):(sub + 1) * (PEER_SUB // 2), :]
        pre_ref[sub % 2] = _dot(pltpu.bitcast(d_sub, BF16), pltpu.bitcast(xnt_ref[...], BF16))

    down(0)
    for sub in range(nsub):
        if sub + 1 < nsub:
            down(sub + 1)
        _peer_activations(pre_ref.at[sub % 2], act_ref.at[sub % 2], r2_ref, e2_ref, c1_ref,
                          e1_ref, sub)
        ut_sub = ut_ref[:, sub * PEER_SUB:(sub + 1) * PEER_SUB]
        acc_ref[...] += _dot(pltpu.bitcast(ut_sub, BF16), act_ref[sub % 2])

    @pl.when(j == pl.num_programs(1) - 1)
    def _():
        h2 = h1_ref[...] + acc_ref[...].T
        o_ref[...] = _rms(h2, nw_ref[...])


def _pack_kernel(w_ref, o_ref, *, transpose):
    w = w_ref[...].T if transpose else w_ref[...]
    o_ref[...] = pltpu.bitcast(w.astype(BF16), jnp.uint32)


def _pack_experts(w, transpose):
    tiles = N_EXPERTS // PEER_TE
    if transpose:
        out_spec = pl.BlockSpec((None, D_MODEL // 2, PEER_TE), lambda i: (i, 0, 0))
        out_shape = jax.ShapeDtypeStruct((tiles, D_MODEL // 2, PEER_TE), jnp.uint32)
    else:
        out_spec = pl.BlockSpec((PEER_TE // 2, D_MODEL), lambda i: (i, 0))
        out_shape = jax.ShapeDtypeStruct((N_EXPERTS // 2, D_MODEL), jnp.uint32)
    return pl.pallas_call(
        functools.partial(_pack_kernel, transpose=transpose),
        grid=(tiles,),
        in_specs=[pl.BlockSpec((PEER_TE, D_MODEL), lambda i: (i, 0))],
        out_specs=out_spec,
        out_shape=out_shape,
        compiler_params=_params("parallel"),
        name="pack_up" if transpose else "pack_down",
    )(w)


def _peer(xnt, down, up_t, r2, e2, c1, e1, h1, norm_w):
    t = h1.shape[0]
    tm = min(PEER_TM, t)
    te = PEER_TE
    key2_spec = pl.BlockSpec((PEER_HEADS, N_KEYS, tm), lambda i, j: (0, 0, i))
    key1_spec = pl.BlockSpec((PEER_HEADS, te // N_KEYS, tm), lambda i, j: (0, j, i))
    return pl.pallas_call(
        _peer_kernel,
        grid=(t // tm, N_EXPERTS // te),
        in_specs=[
            pl.BlockSpec((D_MODEL // 2, tm), lambda i, j: (0, i)),
            pl.BlockSpec((te // 2, D_MODEL), lambda i, j: (j, 0)),
            pl.BlockSpec((None, D_MODEL // 2, te), lambda i, j: (j, 0, 0)),
            key2_spec, key2_spec, key1_spec, key1_spec,
            pl.BlockSpec((tm, D_MODEL), lambda i, j: (i, 0)),
            pl.BlockSpec((1, D_MODEL), lambda i, j: (0, 0)),
        ],
        out_specs=pl.BlockSpec((tm, D_MODEL), lambda i, j: (i, 0)),
        out_shape=jax.ShapeDtypeStruct((t, D_MODEL), F32),
        scratch_shapes=[
            pltpu.VMEM((D_MODEL, tm), F32),
            pltpu.VMEM((2, PEER_SUB, tm), F32),
            pltpu.VMEM((2, PEER_SUB, tm), BF16),
        ],
        compiler_params=_params("parallel", "arbitrary"),
        name="peer",
    )(xnt, down, up_t, r2, e2, c1, e1, h1, norm_w)


def _pad_lanes(v, width):
    return jnp.pad(v.astype(F32), (0, width - v.shape[0])).reshape(1, width)


def _layer(x, norm_mix, w_in, conv_w, conv_b, dt_bias, a_log, d_skip, ssd_norm_w, attn_sinks,
           w_out, rel_bias, norm_ffn, w_query, sub_keys1, sub_keys2, expert_down, expert_up,
           out_norm):
    b, s, d = x.shape
    t = b * s
    x2 = x.reshape(t, d)

    w_in_pad = jnp.pad(w_in, ((0, 0), (0, IN_COLS_PAD - w_in.shape[1]))).astype(BF16)
    q, k, v, z, xbc, dt = _inproj(x2, norm_mix.reshape(1, d), w_in_pad)

    qi = jnp.arange(ATTN_BLOCK)[:, None]
    kj = jnp.arange(ATTN_BLOCK)[None, :]
    bucket = _t5_bucket(jnp.where(kj <= qi, qi - kj, qi + ATTN_BLOCK - kj)).astype(jnp.int32)
    attn = _attention(q.reshape(b, s, -1), k.reshape(b, s, -1), v.reshape(b, s, -1),
                      bucket, rel_bias.astype(F32), attn_sinks.astype(F32))

    lane = jnp.arange(SSD_INNER)[None, :] // SSD_HEAD_DIM
    expand = (jnp.arange(DT_PAD)[:, None] == lane).astype(BF16)
    tri = (jnp.arange(SSD_CHUNK)[None, :] <= jnp.arange(SSD_CHUNK)[:, None]).astype(BF16)
    ssd = _ssd(xbc.reshape(b, s, -1), z.reshape(b, s, -1), dt.reshape(b, s, -1),
               conv_w, conv_b.reshape(1, -1), _pad_lanes(dt_bias, DT_PAD),
               _pad_lanes(a_log, DT_PAD), jnp.repeat(d_skip.astype(F32), SSD_HEAD_DIM).reshape(1, -1),
               ssd_norm_w.reshape(1, -1), expand, tri)

    h1, xnt, pq = _outproj(x2, attn.reshape(t, -1), ssd.reshape(t, -1), w_out.astype(BF16),
                           norm_ffn.reshape(1, d), w_query.astype(BF16))
    r2, e2, c1, e1 = _route(pq, sub_keys1.astype(BF16), sub_keys2.astype(BF16))
    out = _peer(xnt, _pack_experts(expert_down, False), _pack_experts(expert_up, True),
                r2, e2, c1, e1, h1, out_norm.reshape(1, d))
    return out.reshape(b, s, d)


@jax.jit
def kernel(x, norm_mix, w_in, conv_w, conv_b, dt_bias, a_log, d_skip, ssd_norm_w, attn_sinks,
           w_out, rel_bias, norm_ffn, w_query, sub_keys1, sub_keys2, expert_down, expert_up,
           norm_final):
    assert norm_mix.shape[0] == 1, "single-layer block"
    return _layer(x, norm_mix[0], w_in[0], conv_w[0], conv_b[0], dt_bias[0], a_log[0], d_skip[0],
                  ssd_norm_w[0], attn_sinks[0], w_out[0], rel_bias, norm_ffn[0], w_query[0],
                  sub_keys1[0], sub_keys2[0], expert_down[0], expert_up[0], norm_final)
```

```python
import functools
import math

import jax
import jax.numpy as jnp
from jax import lax
from jax.experimental import pallas as pl
from jax.experimental.pallas import tpu as pltpu

F32 = jnp.float32
BF16 = jnp.bfloat16

D_MODEL = 1024
ATTN_HEADS = 8
ATTN_KV_HEADS = 2
ATTN_GROUP = ATTN_HEADS // ATTN_KV_HEADS
HEAD_DIM = 64
ATTN_WIDTH = ATTN_HEADS * HEAD_DIM
ATTN_BLOCK = 128
N_BUCKETS = 32
MAX_DISTANCE = 128
SSD_HEADS = 8
SSD_HEAD_DIM = 64
SSD_INNER = SSD_HEADS * SSD_HEAD_DIM
SSD_STATE = 128
SSD_GROUPS = 2
SSD_HEADS_PER_GROUP = SSD_HEADS // SSD_GROUPS
SSD_CONV = 4
SSD_CHUNK = 128
PEER_HEADS = 8
N_KEYS = 128
N_EXPERTS = N_KEYS * N_KEYS
PEER_HALF = 128
PEER_TOPK = 16
EPS = 1e-6

KV_COLS = ATTN_KV_HEADS * HEAD_DIM
XBC_COLS = SSD_INNER + 2 * SSD_GROUPS * SSD_STATE
OFF_K = ATTN_WIDTH
OFF_V = OFF_K + KV_COLS
OFF_Z = OFF_V + KV_COLS
OFF_XBC = OFF_Z + SSD_INNER
OFF_DT = OFF_XBC + XBC_COLS
LANES = 128
SUBLANES = 8
DT_PAD = LANES
IN_COLS_PAD = OFF_DT + DT_PAD
CONV_HALO = SUBLANES

VMEM_LIMIT = 56 * 1024 * 1024

ATTN_BLOCKS_PER_STEP = 4
INPROJ_TM = 1024
OUTPROJ_TM = 512
ROUTE_TM = SUBLANES * LANES
PEER_TM = 512
PEER_TE = 2048
PEER_SUB = 512
PEER_LANE_CHUNK = 512
PEER_ROWS = 16


def _dot(a, b):
    return jnp.dot(a, b, preferred_element_type=F32)


def _dot_nt(a, b):
    return lax.dot_general(a, b, (((1,), (1,)), ((), ())), preferred_element_type=F32)


def _dot_tn(a, b):
    return lax.dot_general(a, b, (((0,), (0,)), ((), ())), preferred_element_type=F32)


def _rms(x, w):
    return x * lax.rsqrt(jnp.mean(x * x, axis=-1, keepdims=True) + EPS) * w


def _silu(x):
    return x / (1.0 + jnp.exp(-x))


def _softplus(x):
    return jnp.maximum(x, 0.0) + jnp.log1p(jnp.exp(-jnp.abs(x)))


def _split3(x):
    hi = x.astype(BF16)
    rest = x - hi.astype(F32)
    mid = rest.astype(BF16)
    return hi, mid, (rest - mid.astype(F32)).astype(BF16)


def _params(*sem):
    return pltpu.CompilerParams(dimension_semantics=sem, vmem_limit_bytes=VMEM_LIMIT)


def _inproj_kernel(x_ref, nw_ref, w_ref, q_ref, k_ref, v_ref, z_ref, xbc_ref, dt_ref):
    xb = _rms(x_ref[...], nw_ref[...]).astype(BF16)

    def proj(lo, hi):
        return _dot(xb, w_ref[:, lo:hi])

    q_ref[...] = (proj(0, OFF_K) * (HEAD_DIM ** -0.5)).astype(BF16)
    k_ref[...] = proj(OFF_K, OFF_V).astype(BF16)
    v_ref[...] = proj(OFF_V, OFF_Z).astype(BF16)
    z_ref[...] = proj(OFF_Z, OFF_XBC)
    xbc_ref[...] = proj(OFF_XBC, OFF_DT)
    dt_ref[...] = proj(OFF_DT, IN_COLS_PAD)


def _inproj(x2, norm_w, w_in_pad):
    t = x2.shape[0]
    tm = min(INPROJ_TM, t)
    row = lambda i: (i, 0)
    fixed = lambda i: (0, 0)
    return pl.pallas_call(
        _inproj_kernel,
        grid=(t // tm,),
        in_specs=[
            pl.BlockSpec((tm, D_MODEL), row),
            pl.BlockSpec((1, D_MODEL), fixed),
            pl.BlockSpec((D_MODEL, IN_COLS_PAD), fixed),
        ],
        out_specs=[
            pl.BlockSpec((tm, ATTN_WIDTH), row),
            pl.BlockSpec((tm, KV_COLS), row),
            pl.BlockSpec((tm, KV_COLS), row),
            pl.BlockSpec((tm, SSD_INNER), row),
            pl.BlockSpec((tm, XBC_COLS), row),
            pl.BlockSpec((tm, DT_PAD), row),
        ],
        out_shape=[
            jax.ShapeDtypeStruct((t, ATTN_WIDTH), BF16),
            jax.ShapeDtypeStruct((t, KV_COLS), BF16),
            jax.ShapeDtypeStruct((t, KV_COLS), BF16),
            jax.ShapeDtypeStruct((t, SSD_INNER), F32),
            jax.ShapeDtypeStruct((t, XBC_COLS), F32),
            jax.ShapeDtypeStruct((t, DT_PAD), F32),
        ],
        compiler_params=_params("parallel"),
        name="inproj",
    )(x2, norm_w, w_in_pad)


def _t5_bucket(dist):
    n = jnp.maximum(dist, 0)
    max_exact = N_BUCKETS // 2
    nf = jnp.maximum(n, 1).astype(F32)
    large = max_exact + (jnp.log(nf / max_exact) / math.log(MAX_DISTANCE / max_exact)
                         * (N_BUCKETS - max_exact)).astype(jnp.int32)
    large = jnp.minimum(large, N_BUCKETS - 1)
    return jnp.where(n < max_exact, n, large)


def _attn_kernel(bucket_ref, relb_ref, sink_ref, q_ref, kp_ref, kc_ref, vp_ref, vc_ref,
                 o_ref, bias_ref):
    blk = ATTN_BLOCK
    rows = ATTN_GROUP * blk
    n = pl.program_id(1)

    @pl.when((pl.program_id(0) == 0) & (n == 0))
    def _():
        bucket = bucket_ref[...]
        for h in range(ATTN_HEADS):
            acc = jnp.zeros((blk, blk), F32)
            for b in range(N_BUCKETS):
                acc = jnp.where(bucket == b, relb_ref[b, h], acc)
            hk, g = divmod(h, ATTN_GROUP)
            bias_ref[hk, g * blk:(g + 1) * blk, :] = acc

    qi = lax.broadcasted_iota(jnp.int32, (rows, blk), 0) & (blk - 1)
    kj = lax.broadcasted_iota(jnp.int32, (rows, blk), 1)
    own = kj <= qi
    for i in range(ATTN_BLOCKS_PER_STEP):
        cur = slice(i * blk, (i + 1) * blk)
        kprev, vprev = (kp_ref, vp_ref) if i == 0 else (kc_ref.at[(i - 1) * blk:i * blk],
                                                        vc_ref.at[(i - 1) * blk:i * blk])
        has_prev = (n > 0) if i == 0 else True
        for hk in range(ATTN_KV_HEADS):
            ksl = slice(hk * HEAD_DIM, (hk + 1) * HEAD_DIM)
            kband = jnp.concatenate([kprev[:, ksl], kc_ref[cur, ksl]], axis=0)
            vband = jnp.concatenate([vprev[:, ksl], vc_ref[cur, ksl]], axis=0)
            qg = jnp.concatenate(
                [q_ref[cur, (hk * ATTN_GROUP + g) * HEAD_DIM:(hk * ATTN_GROUP + g + 1) * HEAD_DIM]
                 for g in range(ATTN_GROUP)], axis=0)
            s2 = _dot_nt(qg, kband)
            s = jnp.where(own, s2[:, blk:], s2[:, :blk]) + bias_ref[hk]
            if i == 0:
                s = jnp.where(own | has_prev, s, -jnp.inf)
            sink = jnp.concatenate(
                [jnp.full((blk, blk), sink_ref[hk * ATTN_GROUP + g], F32)
                 for g in range(ATTN_GROUP)], axis=0)
            m = jnp.maximum(jnp.broadcast_to(jnp.max(s, axis=-1, keepdims=True), s.shape), sink)
            pb = jnp.exp(s - m).astype(BF16)
            zero = jnp.zeros_like(pb)
            pfull = jnp.concatenate([jnp.where(own, zero, pb), jnp.where(own, pb, zero)], axis=1)
            denom = _dot(pfull, jnp.ones((2 * blk, blk), BF16)) + jnp.exp(sink - m)
            o = _dot(pfull, vband) / denom[:, :HEAD_DIM]
            for g in range(ATTN_GROUP):
                h = hk * ATTN_GROUP + g
                o_ref[cur, h * HEAD_DIM:(h + 1) * HEAD_DIM] = o[g * blk:(g + 1) * blk].astype(BF16)


def _attention(q, k, v, bucket, rel_bias, sinks):
    b, s, _ = q.shape
    rows = ATTN_BLOCKS_PER_STEP * ATTN_BLOCK
    assert s % rows == 0
    cur = lambda i, j: (i, j, 0)
    prev = lambda i, j: (i, jnp.maximum(j * ATTN_BLOCKS_PER_STEP - 1, 0), 0)
    smem = pl.BlockSpec(memory_space=pltpu.SMEM)
    return pl.pallas_call(
        _attn_kernel,
        grid=(b, s // rows),
        in_specs=[
            pl.BlockSpec((ATTN_BLOCK, ATTN_BLOCK), lambda i, j: (0, 0)),
            smem, smem,
            pl.BlockSpec((None, rows, ATTN_WIDTH), cur),
            pl.BlockSpec((None, ATTN_BLOCK, KV_COLS), prev),
            pl.BlockSpec((None, rows, KV_COLS), cur),
            pl.BlockSpec((None, ATTN_BLOCK, KV_COLS), prev),
            pl.BlockSpec((None, rows, KV_COLS), cur),
        ],
        out_specs=pl.BlockSpec((None, rows, ATTN_WIDTH), cur),
        out_shape=jax.ShapeDtypeStruct((b, s, ATTN_WIDTH), BF16),
        scratch_shapes=[pltpu.VMEM((ATTN_KV_HEADS, ATTN_GROUP * ATTN_BLOCK, ATTN_BLOCK), F32)],
        compiler_params=_params("arbitrary", "arbitrary"),
        name="swa",
    )(bucket, rel_bias, sinks, q, k, k, v, v)


def _ssd_kernel(xbc_ref, z_ref, dt_ref, convw_ref, convb_ref, dtb_ref, alog_ref, dskip_ref,
                nw_ref, expand_ref, tri_ref, o_ref, xpad_ref, state_ref):
    L = SSD_CHUNK
    P = SSD_HEAD_DIM
    c = pl.program_id(1)

    @pl.when(c == 0)
    def _():
        xpad_ref[0:CONV_HALO, :] = jnp.zeros((CONV_HALO, XBC_COLS), F32)
        state_ref[...] = jnp.zeros_like(state_ref)

    @pl.when(c > 0)
    def _():
        xpad_ref[0:CONV_HALO, :] = xpad_ref[L:L + CONV_HALO, :]

    xpad_ref[CONV_HALO:CONV_HALO + L, :] = xbc_ref[...]
    conv = convb_ref[...]
    for j in range(SSD_CONV):
        off = CONV_HALO - (SSD_CONV - 1) + j
        conv = conv + convw_ref[j:j + 1, :] * xpad_ref[off:off + L, :]
    xbc = _silu(conv)
    xs = xbc[:, :SSD_INNER]
    bm = xbc[:, SSD_INNER:SSD_INNER + SSD_GROUPS * SSD_STATE].astype(BF16)
    cm = xbc[:, SSD_INNER + SSD_GROUPS * SSD_STATE:].astype(BF16)

    expand = expand_ref[...]
    dt = _softplus(dt_ref[...] + dtb_ref[...])
    a = dt * (-jnp.exp(alog_ref[...]))
    a_cum = sum(_dot(tri_ref[...], term) for term in _split3(a))
    a_last = a_cum[L - 1:L, :]
    stacked = jnp.concatenate([dt, jnp.exp(a_last - a_cum), jnp.exp(a_cum)], axis=0)
    wide = sum(_dot(term, expand) for term in _split3(stacked))
    x_dt = xs * wide[0:L]
    x_end = (x_dt * wide[L:2 * L]).astype(BF16)
    dec_start = wide[2 * L:3 * L]
    x_dt = x_dt.astype(BF16)
    a_cum_t = a_cum.T

    li = lax.broadcasted_iota(jnp.int32, (L, L), 0)
    si = lax.broadcasted_iota(jnp.int32, (L, L), 1)
    causal = si <= li
    ys = []
    for g in range(SSD_GROUPS):
        bg = bm[:, g * SSD_STATE:(g + 1) * SSD_STATE]
        cg = cm[:, g * SSD_STATE:(g + 1) * SSD_STATE]
        cb = _dot_nt(cg, bg)
        for r in range(SSD_HEADS_PER_GROUP):
            h = g * SSD_HEADS_PER_GROUP + r
            sl = slice(h * P, (h + 1) * P)
            seg = a_cum[:, h:h + 1] - a_cum_t[h:h + 1, :]
            lmat = jnp.where(causal, jnp.exp(jnp.where(causal, seg, 0.0)), 0.0)
            h_prev = state_ref[sl, :]
            y_diag = _dot((cb * lmat).astype(BF16), x_dt[:, sl])
            y_off = _dot_nt(cg, h_prev.astype(BF16)) * dec_start[:, sl]
            ys.append(y_diag + y_off)
            chunk_decay = jnp.exp(a_cum_t[h:h + 1, L - 1:L])
            state_ref[sl, :] = h_prev * chunk_decay + _dot_tn(x_end[:, sl], bg)
    y = jnp.concatenate(ys, axis=1) + dskip_ref[...] * xs
    y = y * _silu(z_ref[...])
    o_ref[...] = _rms(y, nw_ref[...]).astype(BF16)


def _ssd(xbc, z, dt, conv_w, conv_b, dt_bias_pad, a_log_pad, d_skip_wide, norm_w, expand, tri):
    b, s, _ = xbc.shape
    nc = s // SSD_CHUNK
    cur = lambda i, j: (i, j, 0)
    fixed = lambda i, j: (0, 0)
    full = lambda shape: pl.BlockSpec(shape, fixed)
    return pl.pallas_call(
        _ssd_kernel,
        grid=(b, nc),
        in_specs=[
            pl.BlockSpec((None, SSD_CHUNK, XBC_COLS), cur),
            pl.BlockSpec((None, SSD_CHUNK, SSD_INNER), cur),
            pl.BlockSpec((None, SSD_CHUNK, DT_PAD), cur),
            full((SSD_CONV, XBC_COLS)), full((1, XBC_COLS)), full((1, DT_PAD)), full((1, DT_PAD)),
            full((1, SSD_INNER)), full((1, SSD_INNER)), full((DT_PAD, SSD_INNER)),
            full((SSD_CHUNK, SSD_CHUNK)),
        ],
        out_specs=pl.BlockSpec((None, SSD_CHUNK, SSD_INNER), cur),
        out_shape=jax.ShapeDtypeStruct((b, s, SSD_INNER), BF16),
        scratch_shapes=[
            pltpu.VMEM((CONV_HALO + SSD_CHUNK, XBC_COLS), F32),
            pltpu.VMEM((SSD_INNER, SSD_STATE), F32),
        ],
        compiler_params=_params("arbitrary", "arbitrary"),
        name="ssd",
    )(xbc, z, dt, conv_w, conv_b, dt_bias_pad, a_log_pad, d_skip_wide, norm_w, expand, tri)


def _outproj_kernel(x_ref, attn_ref, ssd_ref, wo_ref, nw_ref, wq_ref, h1_ref, xnt_ref, q_ref):
    mix = (_dot(attn_ref[...], wo_ref[0:ATTN_WIDTH, :])
           + _dot(ssd_ref[...], wo_ref[ATTN_WIDTH:ATTN_WIDTH + SSD_INNER, :]))
    h1 = x_ref[...] + mix
    h1_ref[...] = h1
    xn = _rms(h1, nw_ref[...])
    xnt_ref[...] = pltpu.bitcast(xn.T.astype(BF16), jnp.uint32)
    qf = _dot(xn.astype(BF16), wq_ref[...])
    for j in range(2 * PEER_HEADS):
        q_ref[j] = qf[:, j * PEER_HALF:(j + 1) * PEER_HALF].astype(BF16)


def _outproj(x2, attn, ssd, w_out, norm_w, w_query):
    t = x2.shape[0]
    tm = min(OUTPROJ_TM, t)
    row = lambda i: (i, 0)
    fixed = lambda i: (0, 0)
    nq = 2 * PEER_HEADS
    return pl.pallas_call(
        _outproj_kernel,
        grid=(t // tm,),
        in_specs=[
            pl.BlockSpec((tm, D_MODEL), row),
            pl.BlockSpec((tm, ATTN_WIDTH), row),
            pl.BlockSpec((tm, SSD_INNER), row),
            pl.BlockSpec((ATTN_WIDTH + SSD_INNER, D_MODEL), fixed),
            pl.BlockSpec((1, D_MODEL), fixed),
            pl.BlockSpec((D_MODEL, nq * PEER_HALF), fixed),
        ],
        out_specs=[
            pl.BlockSpec((tm, D_MODEL), row),
            pl.BlockSpec((D_MODEL // 2, tm), lambda i: (0, i)),
            pl.BlockSpec((nq, tm, PEER_HALF), lambda i: (0, i, 0)),
        ],
        out_shape=[
            jax.ShapeDtypeStruct((t, D_MODEL), F32),
            jax.ShapeDtypeStruct((D_MODEL // 2, t), jnp.uint32),
            jax.ShapeDtypeStruct((nq, t, PEER_HALF), BF16),
        ],
        compiler_params=_params("parallel"),
        name="outproj",
    )(x2, attn, ssd, w_out, norm_w, w_query)


def _ce(x, hi, lo):
    a, b = x[hi], x[lo]
    if b is None:
        return
    if a is None:
        x[hi], x[lo] = b, None
        return
    x[hi], x[lo] = jnp.maximum(a, b), jnp.minimum(a, b)


def _bitonic_merge_desc(x):
    n = len(x)
    j = n // 2
    while j >= 1:
        for i in range(n):
            if i & j == 0:
                _ce(x, i, i | j)
        j //= 2


def _sort_desc(x):
    n = len(x)
    k = 2
    while k <= n:
        j = k // 2
        while j >= 1:
            for i in range(n):
                l = i ^ j
                if l > i:
                    if i & k == 0:
                        _ce(x, i, l)
                    else:
                        _ce(x, l, i)
            j //= 2
        k *= 2


def _merge_top(a, b, sort=True):
    n = len(a)
    out = []
    for i in range(n):
        u, v = a[i], b[n - 1 - i]
        out.append(v if u is None else u if v is None else jnp.maximum(u, v))
    if sort:
        _bitonic_merge_desc(out)
    return out


def _key_rows(k):
    return slice(k * SUBLANES, (k + 1) * SUBLANES)


def _top16_sorted(s_ref):
    groups = []
    for g in range(N_KEYS // PEER_TOPK):
        x = [s_ref[_key_rows(g * PEER_TOPK + i), :] for i in range(PEER_TOPK)]
        _sort_desc(x)
        groups.append(x)
    while len(groups) > 1:
        groups = [_merge_top(groups[i], groups[i + 1]) for i in range(0, len(groups), 2)]
    return groups[0]


def _route_kernel(q_ref, k1_ref, k2_ref, r2_ref, e2_ref, c1_ref, e1_ref,
                  s1_ref, s2_ref, r2s_ref, e2s_ref, c1s_ref, e1s_ref):
    def head(h, carry):
        for half, (k_ref, s_ref) in enumerate(((k1_ref, s1_ref), (k2_ref, s2_ref))):
            qh = q_ref[2 * h + half]
            qcat = jnp.concatenate([qh[j * LANES:(j + 1) * LANES, :] for j in range(SUBLANES)],
                                   axis=1)
            s_ref[...] = _dot_nt(k_ref[...], qcat)
        v1 = _top16_sorted(s1_ref)
        v2 = _top16_sorted(s2_ref)

        lists = []
        for a in range(PEER_TOPK):
            n_b = PEER_TOPK // (a + 1)
            lists.append([v1[a] + v2[b] if b < n_b else None for b in range(PEER_TOPK)])
        while len(lists) > 2:
            lists = [_merge_top(lists[i], lists[i + 1]) for i in range(0, len(lists), 2)]
        top = _merge_top(lists[0], lists[1], sort=False)
        tau = top[0]
        for x in top[1:]:
            tau = jnp.minimum(tau, x)

        e1_top = [jnp.exp(v - v1[0]) for v in v1]
        e2_top = [jnp.exp(v - v2[0]) for v in v2]
        zsum = jnp.zeros_like(tau)
        cnt = []
        for a in range(PEER_TOPK):
            n_a = jnp.zeros_like(tau)
            w_a = jnp.zeros_like(tau)
            for b in range(PEER_TOPK // (a + 1)):
                sel = (v1[a] + v2[b]) >= tau
                n_a = n_a + jnp.where(sel, 1.0, 0.0)
                w_a = w_a + jnp.where(sel, e2_top[b], 0.0)
            cnt.append(n_a)
            zsum = zsum + e1_top[a] * w_a
        half_inv_z = 0.5 / zsum

        for k in range(N_KEYS):
            rows = _key_rows(k)
            s1 = s1_ref[rows, :]
            s2 = s2_ref[rows, :]
            c1 = jnp.zeros_like(tau)
            for a in reversed(range(PEER_TOPK)):
                c1 = jnp.where(s1 >= v1[a], cnt[a], c1)
            r2 = jnp.zeros_like(tau)
            for b in range(PEER_TOPK):
                r2 = jnp.where(v2[b] > s2, float(b + 1), r2)
            c1s_ref[rows, :] = c1
            r2s_ref[rows, :] = r2
            e1s_ref[rows, :] = jnp.exp(s1 - v1[0]) * half_inv_z
            e2s_ref[rows, :] = jnp.exp(s2 - v2[0])

        for j in range(SUBLANES):
            cols = slice(j * LANES, (j + 1) * LANES)
            take = pl.ds(j, N_KEYS, stride=SUBLANES)
            r2_ref[h, :, cols] = r2s_ref[take, :].astype(BF16)
            e2_ref[h, :, cols] = e2s_ref[take, :].astype(BF16)
            c1_ref[h, :, cols] = c1s_ref[take, :]
            e1_ref[h, :, cols] = e1s_ref[take, :]
        return carry

    lax.fori_loop(0, PEER_HEADS, head, 0)


def _expand_keys(keys):
    eye = jnp.eye(SUBLANES, dtype=keys.dtype)
    return jnp.einsum("kd,jc->kjcd", keys, eye).reshape(N_KEYS * SUBLANES, SUBLANES * PEER_HALF)


def _route(q, keys1, keys2):
    nq, t, _ = q.shape
    tm = ROUTE_TM
    assert t % tm == 0
    fixed = lambda i: (0, 0)
    out_spec = pl.BlockSpec((PEER_HEADS, N_KEYS, tm), lambda i: (0, 0, i))
    shape = (PEER_HEADS, N_KEYS, t)
    key_spec = pl.BlockSpec((N_KEYS * SUBLANES, SUBLANES * PEER_HALF), fixed)
    stage = pltpu.VMEM((N_KEYS * SUBLANES, LANES), F32)
    return pl.pallas_call(
        _route_kernel,
        grid=(t // tm,),
        in_specs=[pl.BlockSpec((nq, tm, PEER_HALF), lambda i: (0, i, 0)), key_spec, key_spec],
        out_specs=[out_spec] * 4,
        out_shape=[jax.ShapeDtypeStruct(shape, BF16)] * 2 + [jax.ShapeDtypeStruct(shape, F32)] * 2,
        scratch_shapes=[stage] * 6,
        compiler_params=_params("parallel"),
        name="route",
    )(q, _expand_keys(keys1), _expand_keys(keys2))


def _peer_activations(pre_ref, act_ref, r2_ref, e2_ref, c1_ref, e1_ref, sub):
    tm = pre_ref.shape[1]
    chunks = N_KEYS // PEER_ROWS
    shape = (PEER_ROWS, PEER_LANE_CHUNK)
    zero = jnp.zeros(shape, BF16)
    for a in range(PEER_SUB // N_KEYS):
        key1 = sub * (PEER_SUB // N_KEYS) + a
        for lc in range(tm // PEER_LANE_CHUNK):
            lanes = slice(lc * PEER_LANE_CHUNK, (lc + 1) * PEER_LANE_CHUNK)
            gates = [zero] * chunks
            for h in range(PEER_HEADS):
                cnt = jnp.broadcast_to(c1_ref[h, key1:key1 + 1, lanes], shape).astype(BF16)
                e1 = jnp.broadcast_to(e1_ref[h, key1:key1 + 1, lanes], shape).astype(BF16)
                for c in range(chunks):
                    rows = slice(c * PEER_ROWS, (c + 1) * PEER_ROWS)
                    sel = r2_ref[h, rows, lanes] < cnt
                    gates[c] = gates[c] + jnp.where(sel, e2_ref[h, rows, lanes], zero) * e1
            for c in range(chunks):
                rows = slice(a * N_KEYS + c * PEER_ROWS, a * N_KEYS + (c + 1) * PEER_ROWS)
                p = pre_ref[rows, lanes]
                gelu2 = p + p * lax.erf(p * (2.0 ** -0.5))
                act_ref[rows, lanes] = gelu2.astype(BF16) * gates[c]


def _peer_kernel(xnt_ref, d_ref, ut_ref, r2_ref, e2_ref, c1_ref, e1_ref, h1_ref, nw_ref,
                 o_ref, acc_ref, pre_ref, act_ref):
    j = pl.program_id(1)
    nsub = ut_ref.shape[1] // PEER_SUB

    @pl.when(j == 0)
    def _():
        acc_ref[...] = jnp.zeros_like(acc_ref)

    def down(sub):
        d_sub = d_ref[sub * (PEER_SUB // 2):(sub + 1) * (PEER_SUB // 2), :]
        pre_ref[sub % 2] = _dot(pltpu.bitcast(d_sub, BF16), pltpu.bitcast(xnt_ref[...], BF16))

    down(0)
    for sub in range(nsub):
        if sub + 1 < nsub:
            down(sub + 1)
        _peer_activations(pre_ref.at[sub % 2], act_ref.at[sub % 2], r2_ref, e2_ref, c1_ref,
                          e1_ref, sub)
        ut_sub = ut_ref[:, sub * PEER_SUB:(sub + 1) * PEER_SUB]
        acc_ref[...] += _dot(pltpu.bitcast(ut_sub, BF16), act_ref[sub % 2])

    @pl.when(j == pl.num_programs(1) - 1)
    def _():
        h2 = h1_ref[...] + acc_ref[...].T
        o_ref[...] = _rms(h2, nw_ref[...])


def _pack_kernel(w_ref, o_ref, *, transpose):
    w = w_ref[...].T if transpose else w_ref[...]
    o_ref[...] = pltpu.bitcast(w.astype(BF16), jnp.uint32)


def _pack_experts(w, transpose):
    tiles = N_EXPERTS // PEER_TE
    if transpose:
        out_spec = pl.BlockSpec((None, D_MODEL // 2, PEER_TE), lambda i: (i, 0, 0))
        out_shape = jax.ShapeDtypeStruct((tiles, D_MODEL // 2, PEER_TE), jnp.uint32)
    else:
        out_spec = pl.BlockSpec((PEER_TE // 2, D_MODEL), lambda i: (i, 0))
        out_shape = jax.ShapeDtypeStruct((N_EXPERTS // 2, D_MODEL), jnp.uint32)
    return pl.pallas_call(
        functools.partial(_pack_kernel, transpose=transpose),
        grid=(tiles,),
        in_specs=[pl.BlockSpec((PEER_TE, D_MODEL), lambda i: (i, 0))],
        out_specs=out_spec,
        out_shape=out_shape,
        compiler_params=_params("parallel"),
        name="pack_up" if transpose else "pack_down",
    )(w)


def _peer(xnt, down, up_t, r2, e2, c1, e1, h1, norm_w):
    t = h1.shape[0]
    tm = min(PEER_TM, t)
    te = PEER_TE
    key2_spec = pl.BlockSpec((PEER_HEADS, N_KEYS, tm), lambda i, j: (0, 0, i))
    key1_spec = pl.BlockSpec((PEER_HEADS, te // N_KEYS, tm), lambda i, j: (0, j, i))
    return pl.pallas_call(
        _peer_kernel,
        grid=(t // tm, N_EXPERTS // te),
        in_specs=[
            pl.BlockSpec((D_MODEL // 2, tm), lambda i, j: (0, i)),
            pl.BlockSpec((te // 2, D_MODEL), lambda i, j: (j, 0)),
            pl.BlockSpec((None, D_MODEL // 2, te), lambda i, j: (j, 0, 0)),
            key2_spec, key2_spec, key1_spec, key1_spec,
            pl.BlockSpec((tm, D_MODEL), lambda i, j: (i, 0)),
            pl.BlockSpec((1, D_MODEL), lambda i, j: (0, 0)),
        ],
        out_specs=pl.BlockSpec((tm, D_MODEL), lambda i, j: (i, 0)),
        out_shape=jax.ShapeDtypeStruct((t, D_MODEL), F32),
        scratch_shapes=[
            pltpu.VMEM((D_MODEL, tm), F32),
            pltpu.VMEM((2, PEER_SUB, tm), F32),
            pltpu.VMEM((2, PEER_SUB, tm), BF16),
        ],
        compiler_params=_params("parallel", "arbitrary"),
        name="peer",
    )(xnt, down, up_t, r2, e2, c1, e1, h1, norm_w)


def _pad_lanes(v, width):
    return jnp.pad(v.astype(F32), (0, width - v.shape[0])).reshape(1, width)


def _layer(x, norm_mix, w_in, conv_w, conv_b, dt_bias, a_log, d_skip, ssd_norm_w, attn_sinks,
           w_out, rel_bias, norm_ffn, w_query, sub_keys1, sub_keys2, expert_down, expert_up,
           out_norm):
    b, s, d = x.shape
    t = b * s
    x2 = x.reshape(t, d)

    w_in_pad = jnp.pad(w_in, ((0, 0), (0, IN_COLS_PAD - w_in.shape[1]))).astype(BF16)
    q, k, v, z, xbc, dt = _inproj(x2, norm_mix.reshape(1, d), w_in_pad)

    qi = jnp.arange(ATTN_BLOCK)[:, None]
    kj = jnp.arange(ATTN_BLOCK)[None, :]
    bucket = _t5_bucket(jnp.where(kj <= qi, qi - kj, qi + ATTN_BLOCK - kj)).astype(jnp.int32)
    attn = _attention(q.reshape(b, s, -1), k.reshape(b, s, -1), v.reshape(b, s, -1),
                      bucket, rel_bias.astype(F32), attn_sinks.astype(F32))

    lane = jnp.arange(SSD_INNER)[None, :] // SSD_HEAD_DIM
    expand = (jnp.arange(DT_PAD)[:, None] == lane).astype(BF16)
    tri = (jnp.arange(SSD_CHUNK)[None, :] <= jnp.arange(SSD_CHUNK)[:, None]).astype(BF16)
    ssd = _ssd(xbc.reshape(b, s, -1), z.reshape(b, s, -1), dt.reshape(b, s, -1),
               conv_w, conv_b.reshape(1, -1), _pad_lanes(dt_bias, DT_PAD),
               _pad_lanes(a_log, DT_PAD), jnp.repeat(d_skip.astype(F32), SSD_HEAD_DIM).reshape(1, -1),
               ssd_norm_w.reshape(1, -1), expand, tri)

    h1, xnt, pq = _outproj(x2, attn.reshape(t, -1), ssd.reshape(t, -1), w_out.astype(BF16),
                           norm_ffn.reshape(1, d), w_query.astype(BF16))
    r2, e2, c1, e1 = _route(pq, sub_keys1.astype(BF16), sub_keys2.astype(BF16))
    out = _peer(xnt, _pack_experts(expert_down, False), _pack_experts(expert_up, True),
                r2, e2, c1, e1, h1, out_norm.reshape(1, d))
    return out.reshape(b, s, d)


@jax.jit
def kernel(x, norm_mix, w_in, conv_w, conv_b, dt_bias, a_log, d_skip, ssd_norm_w, attn_sinks,
           w_out, rel_bias, norm_ffn, w_query, sub_keys1, sub_keys2, expert_down, expert_up,
           norm_final):
    assert norm_mix.shape[0] == 1, "single-layer block"
    return _layer(x, norm_mix[0], w_in[0], conv_w[0], conv_b[0], dt_bias[0], a_log[0], d_skip[0],
                  ssd_norm_w[0], attn_sinks[0], w_out[0], rel_bias, norm_ffn[0], w_query[0],
                  sub_keys1[0], sub_keys2[0], expert_down[0], expert_up[0], norm_final)
```

```python
import functools
import math

import jax
import jax.numpy as jnp
from jax import lax
from jax.experimental import pallas as pl
from jax.experimental.pallas import tpu as pltpu

F32 = jnp.float32
BF16 = jnp.bfloat16

D_MODEL = 1024
ATTN_HEADS = 8
ATTN_KV_HEADS = 2
ATTN_GROUP = ATTN_HEADS // ATTN_KV_HEADS
HEAD_DIM = 64
ATTN_WIDTH = ATTN_HEADS * HEAD_DIM
ATTN_BLOCK = 128
N_BUCKETS = 32
MAX_DISTANCE = 128
SSD_HEADS = 8
SSD_HEAD_DIM = 64
SSD_INNER = SSD_HEADS * SSD_HEAD_DIM
SSD_STATE = 128
SSD_GROUPS = 2
SSD_HEADS_PER_GROUP = SSD_HEADS // SSD_GROUPS
SSD_CONV = 4
SSD_CHUNK = 128
PEER_HEADS = 8
N_KEYS = 128
N_EXPERTS = N_KEYS * N_KEYS
PEER_HALF = 128
PEER_TOPK = 16
EPS = 1e-6

KV_COLS = ATTN_KV_HEADS * HEAD_DIM
XBC_COLS = SSD_INNER + 2 * SSD_GROUPS * SSD_STATE
OFF_K = ATTN_WIDTH
OFF_V = OFF_K + KV_COLS
OFF_Z = OFF_V + KV_COLS
OFF_XBC = OFF_Z + SSD_INNER
OFF_DT = OFF_XBC + XBC_COLS
LANES = 128
SUBLANES = 8
DT_PAD = LANES
IN_COLS_PAD = OFF_DT + DT_PAD
CONV_HALO = SUBLANES

VMEM_LIMIT = 56 * 1024 * 1024

ATTN_BLOCKS_PER_STEP = 4
INPROJ_TM = 1024
OUTPROJ_TM = 512
ROUTE_TM = SUBLANES * LANES
PEER_TM = 512
PEER_TE = 2048
PEER_SUB = 512
PEER_LANE_CHUNK = 512
PEER_ROWS = 16


def _dot(a, b):
    return jnp.dot(a, b, preferred_element_type=F32)


def _dot_nt(a, b):
    return lax.dot_general(a, b, (((1,), (1,)), ((), ())), preferred_element_type=F32)


def _dot_tn(a, b):
    return lax.dot_general(a, b, (((0,), (0,)), ((), ())), preferred_element_type=F32)


def _rms(x, w):
    return x * lax.rsqrt(jnp.mean(x * x, axis=-1, keepdims=True) + EPS) * w


def _silu(x):
    return x / (1.0 + jnp.exp(-x))


def _softplus(x):
    return jnp.maximum(x, 0.0) + jnp.log1p(jnp.exp(-jnp.abs(x)))


def _split3(x):
    hi = x.astype(BF16)
    rest = x - hi.astype(F32)
    mid = rest.astype(BF16)
    return hi, mid, (rest - mid.astype(F32)).astype(BF16)


def _params(*sem):
    return pltpu.CompilerParams(dimension_semantics=sem, vmem_limit_bytes=VMEM_LIMIT)


def _inproj_kernel(x_ref, nw_ref, w_ref, q_ref, k_ref, v_ref, z_ref, xbc_ref, dt_ref):
    xb = _rms(x_ref[...], nw_ref[...]).astype(BF16)

    def proj(lo, hi):
        return _dot(xb, w_ref[:, lo:hi])

    q_ref[...] = (proj(0, OFF_K) * (HEAD_DIM ** -0.5)).astype(BF16)
    k_ref[...] = proj(OFF_K, OFF_V).astype(BF16)
    v_ref[...] = proj(OFF_V, OFF_Z).astype(BF16)
    z_ref[...] = proj(OFF_Z, OFF_XBC)
    xbc_ref[...] = proj(OFF_XBC, OFF_DT)
    dt_ref[...] = proj(OFF_DT, IN_COLS_PAD)


def _inproj(x2, norm_w, w_in_pad):
    t = x2.shape[0]
    tm = min(INPROJ_TM, t)
    row = lambda i: (i, 0)
    fixed = lambda i: (0, 0)
    return pl.pallas_call(
        _inproj_kernel,
        grid=(t // tm,),
        in_specs=[
            pl.BlockSpec((tm, D_MODEL), row),
            pl.BlockSpec((1, D_MODEL), fixed),
            pl.BlockSpec((D_MODEL, IN_COLS_PAD), fixed),
        ],
        out_specs=[
            pl.BlockSpec((tm, ATTN_WIDTH), row),
            pl.BlockSpec((tm, KV_COLS), row),
            pl.BlockSpec((tm, KV_COLS), row),
            pl.BlockSpec((tm, SSD_INNER), row),
            pl.BlockSpec((tm, XBC_COLS), row),
            pl.BlockSpec((tm, DT_PAD), row),
        ],
        out_shape=[
            jax.ShapeDtypeStruct((t, ATTN_WIDTH), BF16),
            jax.ShapeDtypeStruct((t, KV_COLS), BF16),
            jax.ShapeDtypeStruct((t, KV_COLS), BF16),
            jax.ShapeDtypeStruct((t, SSD_INNER), F32),
            jax.ShapeDtypeStruct((t, XBC_COLS), F32),
            jax.ShapeDtypeStruct((t, DT_PAD), F32),
        ],
        compiler_params=_params("parallel"),
        name="inproj",
    )(x2, norm_w, w_in_pad)


def _t5_bucket(dist):
    n = jnp.maximum(dist, 0)
    max_exact = N_BUCKETS // 2
    nf = jnp.maximum(n, 1).astype(F32)
    large = max_exact + (jnp.log(nf / max_exact) / math.log(MAX_DISTANCE / max_exact)
                         * (N_BUCKETS - max_exact)).astype(jnp.int32)
    large = jnp.minimum(large, N_BUCKETS - 1)
    return jnp.where(n < max_exact, n, large)


def _attn_kernel(bucket_ref, relb_ref, sink_ref, q_ref, kp_ref, kc_ref, vp_ref, vc_ref,
                 o_ref, bias_ref):
    blk = ATTN_BLOCK
    rows = ATTN_GROUP * blk
    n = pl.program_id(1)

    @pl.when((pl.program_id(0) == 0) & (n == 0))
    def _():
        bucket = bucket_ref[...]
        for h in range(ATTN_HEADS):
            acc = jnp.zeros((blk, blk), F32)
            for b in range(N_BUCKETS):
                acc = jnp.where(bucket == b, relb_ref[b, h], acc)
            hk, g = divmod(h, ATTN_GROUP)
            bias_ref[hk, g * blk:(g + 1) * blk, :] = acc

    qi = lax.broadcasted_iota(jnp.int32, (rows, blk), 0) & (blk - 1)
    kj = lax.broadcasted_iota(jnp.int32, (rows, blk), 1)
    own = kj <= qi
    for i in range(ATTN_BLOCKS_PER_STEP):
        cur = slice(i * blk, (i + 1) * blk)
        kprev, vprev = (kp_ref, vp_ref) if i == 0 else (kc_ref.at[(i - 1) * blk:i * blk],
                                                        vc_ref.at[(i - 1) * blk:i * blk])
        has_prev = (n > 0) if i == 0 else True
        for hk in range(ATTN_KV_HEADS):
            ksl = slice(hk * HEAD_DIM, (hk + 1) * HEAD_DIM)
            kband = jnp.concatenate([kprev[:, ksl], kc_ref[cur, ksl]], axis=0)
            vband = jnp.concatenate([vprev[:, ksl], vc_ref[cur, ksl]], axis=0)
            qg = jnp.concatenate(
                [q_ref[cur, (hk * ATTN_GROUP + g) * HEAD_DIM:(hk * ATTN_GROUP + g + 1) * HEAD_DIM]
                 for g in range(ATTN_GROUP)], axis=0)
            s2 = _dot_nt(qg, kband)
            s = jnp.where(own, s2[:, blk:], s2[:, :blk]) + bias_ref[hk]
            if i == 0:
                s = jnp.where(own | has_prev, s, -jnp.inf)
            sink = jnp.concatenate(
                [jnp.full((blk, blk), sink_ref[hk * ATTN_GROUP + g], F32)
                 for g in range(ATTN_GROUP)], axis=0)
            m = jnp.maximum(jnp.broadcast_to(jnp.max(s, axis=-1, keepdims=True), s.shape), sink)
            pb = jnp.exp(s - m).astype(BF16)
            zero = jnp.zeros_like(pb)
            pfull = jnp.concatenate([jnp.where(own, zero, pb), jnp.where(own, pb, zero)], axis=1)
            denom = _dot(pfull, jnp.ones((2 * blk, blk), BF16)) + jnp.exp(sink - m)
            o = _dot(pfull, vband) / denom[:, :HEAD_DIM]
            for g in range(ATTN_GROUP):
                h = hk * ATTN_GROUP + g
                o_ref[cur, h * HEAD_DIM:(h + 1) * HEAD_DIM] = o[g * blk:(g + 1) * blk].astype(BF16)


def _attention(q, k, v, bucket, rel_bias, sinks):
    b, s, _ = q.shape
    rows = ATTN_BLOCKS_PER_STEP * ATTN_BLOCK
    assert s % rows == 0
    cur = lambda i, j: (i, j, 0)
    prev = lambda i, j: (i, jnp.maximum(j * ATTN_BLOCKS_PER_STEP - 1, 0), 0)
    smem = pl.BlockSpec(memory_space=pltpu.SMEM)
    return pl.pallas_call(
        _attn_kernel,
        grid=(b, s // rows),
        in_specs=[
            pl.BlockSpec((ATTN_BLOCK, ATTN_BLOCK), lambda i, j: (0, 0)),
            smem, smem,
            pl.BlockSpec((None, rows, ATTN_WIDTH), cur),
            pl.BlockSpec((None, ATTN_BLOCK, KV_COLS), prev),
            pl.BlockSpec((None, rows, KV_COLS), cur),
            pl.BlockSpec((None, ATTN_BLOCK, KV_COLS), prev),
            pl.BlockSpec((None, rows, KV_COLS), cur),
        ],
        out_specs=pl.BlockSpec((None, rows, ATTN_WIDTH), cur),
        out_shape=jax.ShapeDtypeStruct((b, s, ATTN_WIDTH), BF16),
        scratch_shapes=[pltpu.VMEM((ATTN_KV_HEADS, ATTN_GROUP * ATTN_BLOCK, ATTN_BLOCK), F32)],
        compiler_params=_params("arbitrary", "arbitrary"),
        name="swa",
    )(bucket, rel_bias, sinks, q, k, k, v, v)


def _ssd_kernel(xbc_ref, z_ref, dt_ref, convw_ref, convb_ref, dtb_ref, alog_ref, dskip_ref,
                nw_ref, expand_ref, tri_ref, o_ref, xpad_ref, state_ref):
    L = SSD_CHUNK
    P = SSD_HEAD_DIM
    c = pl.program_id(1)

    @pl.when(c == 0)
    def _():
        xpad_ref[0:CONV_HALO, :] = jnp.zeros((CONV_HALO, XBC_COLS), F32)
        state_ref[...] = jnp.zeros_like(state_ref)

    @pl.when(c > 0)
    def _():
        xpad_ref[0:CONV_HALO, :] = xpad_ref[L:L + CONV_HALO, :]

    xpad_ref[CONV_HALO:CONV_HALO + L, :] = xbc_ref[...]
    conv = convb_ref[...]
    for j in range(SSD_CONV):
        off = CONV_HALO - (SSD_CONV - 1) + j
        conv = conv + convw_ref[j:j + 1, :] * xpad_ref[off:off + L, :]
    xbc = _silu(conv)
    xs = xbc[:, :SSD_INNER]
    bm = xbc[:, SSD_INNER:SSD_INNER + SSD_GROUPS * SSD_STATE].astype(BF16)
    cm = xbc[:, SSD_INNER + SSD_GROUPS * SSD_STATE:].astype(BF16)

    expand = expand_ref[...]
    dt = _softplus(dt_ref[...] + dtb_ref[...])
    a = dt * (-jnp.exp(alog_ref[...]))
    a_cum = sum(_dot(tri_ref[...], term) for term in _split3(a))
    a_last = a_cum[L - 1:L, :]
    stacked = jnp.concatenate([dt, jnp.exp(a_last - a_cum), jnp.exp(a_cum)], axis=0)
    wide = sum(_dot(term, expand) for term in _split3(stacked))
    x_dt = xs * wide[0:L]
    x_end = (x_dt * wide[L:2 * L]).astype(BF16)
    dec_start = wide[2 * L:3 * L]
    x_dt = x_dt.astype(BF16)
    a_cum_t = a_cum.T

    li = lax.broadcasted_iota(jnp.int32, (L, L), 0)
    si = lax.broadcasted_iota(jnp.int32, (L, L), 1)
    causal = si <= li
    ys = []
    for g in range(SSD_GROUPS):
        bg = bm[:, g * SSD_STATE:(g + 1) * SSD_STATE]
        cg = cm[:, g * SSD_STATE:(g + 1) * SSD_STATE]
        cb = _dot_nt(cg, bg)
        for r in range(SSD_HEADS_PER_GROUP):
            h = g * SSD_HEADS_PER_GROUP + r
            sl = slice(h * P, (h + 1) * P)
            seg = a_cum[:, h:h + 1] - a_cum_t[h:h + 1, :]
            lmat = jnp.where(causal, jnp.exp(jnp.where(causal, seg, 0.0)), 0.0)
            h_prev = state_ref[sl, :]
            y_diag = _dot((cb * lmat).astype(BF16), x_dt[:, sl])
            y_off = _dot_nt(cg, h_prev.astype(BF16)) * dec_start[:, sl]
            ys.append(y_diag + y_off)
            chunk_decay = jnp.exp(a_cum_t[h:h + 1, L - 1:L])
            state_ref[sl, :] = h_prev * chunk_decay + _dot_tn(x_end[:, sl], bg)
    y = jnp.concatenate(ys, axis=1) + dskip_ref[...] * xs
    y = y * _silu(z_ref[...])
    o_ref[...] = _rms(y, nw_ref[...]).astype(BF16)


def _ssd(xbc, z, dt, conv_w, conv_b, dt_bias_pad, a_log_pad, d_skip_wide, norm_w, expand, tri):
    b, s, _ = xbc.shape
    nc = s // SSD_CHUNK
    cur = lambda i, j: (i, j, 0)
    fixed = lambda i, j: (0, 0)
    full = lambda shape: pl.BlockSpec(shape, fixed)
    return pl.pallas_call(
        _ssd_kernel,
        grid=(b, nc),
        in_specs=[
            pl.BlockSpec((None, SSD_CHUNK, XBC_COLS), cur),
            pl.BlockSpec((None, SSD_CHUNK, SSD_INNER), cur),
            pl.BlockSpec((None, SSD_CHUNK, DT_PAD), cur),
            full((SSD_CONV, XBC_COLS)), full((1, XBC_COLS)), full((1, DT_PAD)), full((1, DT_PAD)),
            full((1, SSD_INNER)), full((1, SSD_INNER)), full((DT_PAD, SSD_INNER)),
            full((SSD_CHUNK, SSD_CHUNK)),
        ],
        out_specs=pl.BlockSpec((None, SSD_CHUNK, SSD_INNER), cur),
        out_shape=jax.ShapeDtypeStruct((b, s, SSD_INNER), BF16),
        scratch_shapes=[
            pltpu.VMEM((CONV_HALO + SSD_CHUNK, XBC_COLS), F32),
            pltpu.VMEM((SSD_INNER, SSD_STATE), F32),
        ],
        compiler_params=_params("arbitrary", "arbitrary"),
        name="ssd",
    )(xbc, z, dt, conv_w, conv_b, dt_bias_pad, a_log_pad, d_skip_wide, norm_w, expand, tri)


def _outproj_kernel(x_ref, attn_ref, ssd_ref, wo_ref, nw_ref, wq_ref, h1_ref, xnt_ref, q_ref):
    mix = (_dot(attn_ref[...], wo_ref[0:ATTN_WIDTH, :])
           + _dot(ssd_ref[...], wo_ref[ATTN_WIDTH:ATTN_WIDTH + SSD_INNER, :]))
    h1 = x_ref[...] + mix
    h1_ref[...] = h1
    xn = _rms(h1, nw_ref[...])
    xnt_ref[...] = pltpu.bitcast(xn.T.astype(BF16), jnp.uint32)
    qf = _dot(xn.astype(BF16), wq_ref[...])
    for j in range(2 * PEER_HEADS):
        q_ref[j] = qf[:, j * PEER_HALF:(j + 1) * PEER_HALF].astype(BF16)


def _outproj(x2, attn, ssd, w_out, norm_w, w_query):
    t = x2.shape[0]
    tm = min(OUTPROJ_TM, t)
    row = lambda i: (i, 0)
    fixed = lambda i: (0, 0)
    nq = 2 * PEER_HEADS
    return pl.pallas_call(
        _outproj_kernel,
        grid=(t // tm,),
        in_specs=[
            pl.BlockSpec((tm, D_MODEL), row),
            pl.BlockSpec((tm, ATTN_WIDTH), row),
            pl.BlockSpec((tm, SSD_INNER), row),
            pl.BlockSpec((ATTN_WIDTH + SSD_INNER, D_MODEL), fixed),
            pl.BlockSpec((1, D_MODEL), fixed),
            pl.BlockSpec((D_MODEL, nq * PEER_HALF), fixed),
        ],
        out_specs=[
            pl.BlockSpec((tm, D_MODEL), row),
            pl.BlockSpec((D_MODEL // 2, tm), lambda i: (0, i)),
            pl.BlockSpec((nq, tm, PEER_HALF), lambda i: (0, i, 0)),
        ],
        out_shape=[
            jax.ShapeDtypeStruct((t, D_MODEL), F32),
            jax.ShapeDtypeStruct((D_MODEL // 2, t), jnp.uint32),
            jax.ShapeDtypeStruct((nq, t, PEER_HALF), BF16),
        ],
        compiler_params=_params("parallel"),
        name="outproj",
    )(x2, attn, ssd, w_out, norm_w, w_query)


def _ce(x, hi, lo):
    a, b = x[hi], x[lo]
    if b is None:
        return
    if a is None:
        x[hi], x[lo] = b, None
        return
    x[hi], x[lo] = jnp.maximum(a, b), jnp.minimum(a, b)


def _bitonic_merge_desc(x):
    n = len(x)
    j = n // 2
    while j >= 1:
        for i in range(n):
            if i & j == 0:
                _ce(x, i, i | j)
        j //= 2


def _sort_desc(x):
    n = len(x)
    k = 2
    while k <= n:
        j = k // 2
        while j >= 1:
            for i in range(n):
                l = i ^ j
                if l > i:
                    if i & k == 0:
                        _ce(x, i, l)
                    else:
                        _ce(x, l, i)
            j //= 2
        k *= 2


def _merge_top(a, b, sort=True):
    n = len(a)
    out = []
    for i in range(n):
        u, v = a[i], b[n - 1 - i]
        out.append(v if u is None else u if v is None else jnp.maximum(u, v))
    if sort:
        _bitonic_merge_desc(out)
    return out


def _key_rows(k):
    return slice(k * SUBLANES, (k + 1) * SUBLANES)


def _top16_sorted(s_ref):
    groups = []
    for g in range(N_KEYS // PEER_TOPK):
        x = [s_ref[_key_rows(g * PEER_TOPK + i), :] for i in range(PEER_TOPK)]
        _sort_desc(x)
        groups.append(x)
    while len(groups) > 1:
        groups = [_merge_top(groups[i], groups[i + 1]) for i in range(0, len(groups), 2)]
    return groups[0]


def _route_scores(q_ref, k1_ref, k2_ref, h, s1_ref, s2_ref):
    for half, (k_ref, s_ref) in enumerate(((k1_ref, s1_ref), (k2_ref, s2_ref))):
        qh = q_ref[2 * h + half]
        qcat = jnp.concatenate([qh[j * LANES:(j + 1) * LANES, :] for j in range(SUBLANES)],
                               axis=1)
        s_ref[...] = _dot_nt(k_ref[...], qcat)


def _route_select(h, s1_ref, s2_ref, r2_ref, e2_ref, c1_ref, e1_ref,
                  r2s_ref, e2s_ref, c1s_ref, e1s_ref):
    v1 = _top16_sorted(s1_ref)
    v2 = _top16_sorted(s2_ref)

    lists = []
    for a in range(PEER_TOPK):
        n_b = PEER_TOPK // (a + 1)
        lists.append([v1[a] + v2[b] if b < n_b else None for b in range(PEER_TOPK)])
    while len(lists) > 2:
        lists = [_merge_top(lists[i], lists[i + 1]) for i in range(0, len(lists), 2)]
    top = _merge_top(lists[0], lists[1], sort=False)
    tau = top[0]
    for x in top[1:]:
        tau = jnp.minimum(tau, x)

    e1_top = [jnp.exp(v - v1[0]) for v in v1]
    e2_top = [jnp.exp(v - v2[0]) for v in v2]
    zsum = jnp.zeros_like(tau)
    cnt = []
    for a in range(PEER_TOPK):
        n_a = jnp.zeros_like(tau)
        w_a = jnp.zeros_like(tau)
        for b in range(PEER_TOPK // (a + 1)):
            sel = (v1[a] + v2[b]) >= tau
            n_a = n_a + jnp.where(sel, 1.0, 0.0)
            w_a = w_a + jnp.where(sel, e2_top[b], 0.0)
        cnt.append(n_a)
        zsum = zsum + e1_top[a] * w_a
    half_inv_z = 0.5 / zsum

    for k in range(N_KEYS):
        rows = _key_rows(k)
        s1 = s1_ref[rows, :]
        s2 = s2_ref[rows, :]
        c1 = jnp.zeros_like(tau)
        for a in reversed(range(PEER_TOPK)):
            c1 = jnp.where(s1 >= v1[a], cnt[a], c1)
        r2 = jnp.zeros_like(tau)
        for b in range(PEER_TOPK):
            r2 = jnp.where(v2[b] > s2, float(b + 1), r2)
        c1s_ref[rows, :] = c1
        r2s_ref[rows, :] = r2
        e1s_ref[rows, :] = jnp.exp(s1 - v1[0]) * half_inv_z
        e2s_ref[rows, :] = jnp.exp(s2 - v2[0])

    for j in range(SUBLANES):
        cols = slice(j * LANES, (j + 1) * LANES)
        take = pl.ds(j, N_KEYS, stride=SUBLANES)
        r2_ref[h, :, cols] = r2s_ref[take, :].astype(BF16)
        e2_ref[h, :, cols] = e2s_ref[take, :].astype(BF16)
        c1_ref[h, :, cols] = c1s_ref[take, :]
        e1_ref[h, :, cols] = e1s_ref[take, :]


def _route_kernel(q_ref, k1_ref, k2_ref, r2_ref, e2_ref, c1_ref, e1_ref,
                  s1a_ref, s2a_ref, s1b_ref, s2b_ref, r2s_ref, e2s_ref, c1s_ref, e1s_ref):
    outs = (r2_ref, e2_ref, c1_ref, e1_ref, r2s_ref, e2s_ref, c1s_ref, e1s_ref)
    _route_scores(q_ref, k1_ref, k2_ref, 0, s1a_ref, s2a_ref)

    def pair(i, carry):
        h = 2 * i
        _route_scores(q_ref, k1_ref, k2_ref, h + 1, s1b_ref, s2b_ref)
        _route_select(h, s1a_ref, s2a_ref, *outs)
        _route_scores(q_ref, k1_ref, k2_ref, jnp.minimum(h + 2, PEER_HEADS - 1), s1a_ref, s2a_ref)
        _route_select(h + 1, s1b_ref, s2b_ref, *outs)
        return carry

    lax.fori_loop(0, PEER_HEADS // 2, pair, 0)


def _expand_keys(keys):
    eye = jnp.eye(SUBLANES, dtype=keys.dtype)
    return jnp.einsum("kd,jc->kjcd", keys, eye).reshape(N_KEYS * SUBLANES, SUBLANES * PEER_HALF)


def _route(q, keys1, keys2):
    nq, t, _ = q.shape
    tm = ROUTE_TM
    assert t % tm == 0
    fixed = lambda i: (0, 0)
    out_spec = pl.BlockSpec((PEER_HEADS, N_KEYS, tm), lambda i: (0, 0, i))
    shape = (PEER_HEADS, N_KEYS, t)
    key_spec = pl.BlockSpec((N_KEYS * SUBLANES, SUBLANES * PEER_HALF), fixed)
    stage = pltpu.VMEM((N_KEYS * SUBLANES, LANES), F32)
    return pl.pallas_call(
        _route_kernel,
        grid=(t // tm,),
        in_specs=[pl.BlockSpec((nq, tm, PEER_HALF), lambda i: (0, i, 0)), key_spec, key_spec],
        out_specs=[out_spec] * 4,
        out_shape=[jax.ShapeDtypeStruct(shape, BF16)] * 2 + [jax.ShapeDtypeStruct(shape, F32)] * 2,
        scratch_shapes=[stage] * 8,
        compiler_params=_params("parallel"),
        name="route",
    )(q, _expand_keys(keys1), _expand_keys(keys2))


def _peer_activations(pre_ref, act_ref, r2_ref, e2_ref, c1_ref, e1_ref, sub):
    tm = pre_ref.shape[1]
    chunks = N_KEYS // PEER_ROWS
    shape = (PEER_ROWS, PEER_LANE_CHUNK)
    zero = jnp.zeros(shape, BF16)
    for a in range(PEER_SUB // N_KEYS):
        key1 = sub * (PEER_SUB // N_KEYS) + a
        for lc in range(tm // PEER_LANE_CHUNK):
            lanes = slice(lc * PEER_LANE_CHUNK, (lc + 1) * PEER_LANE_CHUNK)
            gates = [zero] * chunks
            for h in range(PEER_HEADS):
                cnt = jnp.broadcast_to(c1_ref[h, key1:key1 + 1, lanes], shape).astype(BF16)
                e1 = jnp.broadcast_to(e1_ref[h, key1:key1 + 1, lanes], shape).astype(BF16)
                for c in range(chunks):
                    rows = slice(c * PEER_ROWS, (c + 1) * PEER_ROWS)
                    sel = r2_ref[h, rows, lanes] < cnt
                    gates[c] = gates[c] + jnp.where(sel, e2_ref[h, rows, lanes], zero) * e1
            for c in range(chunks):
                rows = slice(a * N_KEYS + c * PEER_ROWS, a * N_KEYS + (c + 1) * PEER_ROWS)
                p = pre_ref[rows, lanes]
                gelu2 = p + p * lax.erf(p * (2.0 ** -0.5))
                act_ref[rows, lanes] = gelu2.astype(BF16) * gates[c]


def _peer_kernel(xnt_ref, d_ref, ut_ref, r2_ref, e2_ref, c1_ref, e1_ref, h1_ref, nw_ref,
                 o_ref, acc_ref, pre_ref, act_ref):
    j = pl.program_id(1)
    nsub = ut_ref.shape[1] // PEER_SUB

    @pl.when(j == 0)
    def _():
        acc_ref[...] = jnp.zeros_like(acc_ref)

    def down(sub):
        d_sub = d_ref[sub * (PEER_SUB // 2):(sub + 1) * (PEER_SUB // 2), :]
        pre_ref[sub % 2] = _dot(pltpu.bitcast(d_sub, BF16), pltpu.bitcast(xnt_ref[...], BF16))

    down(0)
    for sub in range(nsub):
        if sub + 1 < nsub:
            down(sub + 1)
        _peer_activations(pre_ref.at[sub % 2], act_ref.at[sub % 2], r2_ref, e2_ref, c1_ref,
                          e1_ref, sub)
        ut_sub = ut_ref[:, sub * PEER_SUB:(sub + 1) * PEER_SUB]
        acc_ref[...] += _dot(pltpu.bitcast(ut_sub, BF16), act_ref[sub % 2])

    @pl.when(j == pl.num_programs(1) - 1)
    def _():
        h2 = h1_ref[...] + acc_ref[...].T
        o_ref[...] = _rms(h2, nw_ref[...])


def _pack_kernel(w_ref, o_ref, *, transpose):
    w = w_ref[...].T if transpose else w_ref[...]
    o_ref[...] = pltpu.bitcast(w.astype(BF16), jnp.uint32)


def _pack_experts(w, transpose):
    tiles = N_EXPERTS // PEER_TE
    if transpose:
        out_spec = pl.BlockSpec((None, D_MODEL // 2, PEER_TE), lambda i: (i, 0, 0))
        out_shape = jax.ShapeDtypeStruct((tiles, D_MODEL // 2, PEER_TE), jnp.uint32)
    else:
        out_spec = pl.BlockSpec((PEER_TE // 2, D_MODEL), lambda i: (i, 0))
        out_shape = jax.ShapeDtypeStruct((N_EXPERTS // 2, D_MODEL), jnp.uint32)
    return pl.pallas_call(
        functools.partial(_pack_kernel, transpose=transpose),
        grid=(tiles,),
        in_specs=[pl.BlockSpec((PEER_TE, D_MODEL), lambda i: (i, 0))],
        out_specs=out_spec,
        out_shape=out_shape,
        compiler_params=_params("parallel"),
        name="pack_up" if transpose else "pack_down",
    )(w)


def _peer(xnt, down, up_t, r2, e2, c1, e1, h1, norm_w):
    t = h1.shape[0]
    tm = min(PEER_TM, t)
    te = PEER_TE
    key2_spec = pl.BlockSpec((PEER_HEADS, N_KEYS, tm), lambda i, j: (0, 0, i))
    key1_spec = pl.BlockSpec((PEER_HEADS, te // N_KEYS, tm), lambda i, j: (0, j, i))
    return pl.pallas_call(
        _peer_kernel,
        grid=(t // tm, N_EXPERTS // te),
        in_specs=[
            pl.BlockSpec((D_MODEL // 2, tm), lambda i, j: (0, i)),
            pl.BlockSpec((te // 2, D_MODEL), lambda i, j: (j, 0)),
            pl.BlockSpec((None, D_MODEL // 2, te), lambda i, j: (j, 0, 0)),
            key2_spec, key2_spec, key1_spec, key1_spec,
            pl.BlockSpec((tm, D_MODEL), lambda i, j: (i, 0)),
            pl.BlockSpec((1, D_MODEL), lambda i, j: (0, 0)),
        ],
        out_specs=pl.BlockSpec((tm, D_MODEL), lambda i, j: (i, 0)),
        out_shape=jax.ShapeDtypeStruct((t, D_MODEL), F32),
        scratch_shapes=[
            pltpu.VMEM((D_MODEL, tm), F32),
            pltpu.VMEM((2, PEER_SUB, tm), F32),
            pltpu.VMEM((2, PEER_SUB, tm), BF16),
        ],
        compiler_params=_params("parallel", "arbitrary"),
        name="peer",
    )(xnt, down, up_t, r2, e2, c1, e1, h1, norm_w)


def _pad_lanes(v, width):
    return jnp.pad(v.astype(F32), (0, width - v.shape[0])).reshape(1, width)


def _layer(x, norm_mix, w_in, conv_w, conv_b, dt_bias, a_log, d_skip, ssd_norm_w, attn_sinks,
           w_out, rel_bias, norm_ffn, w_query, sub_keys1, sub_keys2, expert_down, expert_up,
           out_norm):
    b, s, d = x.shape
    t = b * s
    x2 = x.reshape(t, d)

    w_in_pad = jnp.pad(w_in, ((0, 0), (0, IN_COLS_PAD - w_in.shape[1]))).astype(BF16)
    q, k, v, z, xbc, dt = _inproj(x2, norm_mix.reshape(1, d), w_in_pad)

    qi = jnp.arange(ATTN_BLOCK)[:, None]
    kj = jnp.arange(ATTN_BLOCK)[None, :]
    bucket = _t5_bucket(jnp.where(kj <= qi, qi - kj, qi + ATTN_BLOCK - kj)).astype(jnp.int32)
    attn = _attention(q.reshape(b, s, -1), k.reshape(b, s, -1), v.reshape(b, s, -1),
                      bucket, rel_bias.astype(F32), attn_sinks.astype(F32))

    lane = jnp.arange(SSD_INNER)[None, :] // SSD_HEAD_DIM
    expand = (jnp.arange(DT_PAD)[:, None] == lane).astype(BF16)
    tri = (jnp.arange(SSD_CHUNK)[None, :] <= jnp.arange(SSD_CHUNK)[:, None]).astype(BF16)
    ssd = _ssd(xbc.reshape(b, s, -1), z.reshape(b, s, -1), dt.reshape(b, s, -1),
               conv_w, conv_b.reshape(1, -1), _pad_lanes(dt_bias, DT_PAD),
               _pad_lanes(a_log, DT_PAD), jnp.repeat(d_skip.astype(F32), SSD_HEAD_DIM).reshape(1, -1),
               ssd_norm_w.reshape(1, -1), expand, tri)

    h1, xnt, pq = _outproj(x2, attn.reshape(t, -1), ssd.reshape(t, -1), w_out.astype(BF16),
                           norm_ffn.reshape(1, d), w_query.astype(BF16))
    r2, e2, c1, e1 = _route(pq, sub_keys1.astype(BF16), sub_keys2.astype(BF16))
    out = _peer(xnt, _pack_experts(expert_down, False), _pack_experts(expert_up, True),
                r2, e2, c1, e1, h1, out_norm.reshape(1, d))
    return out.reshape(b, s, d)


@jax.jit
def kernel(x, norm_mix, w_in, conv_w, conv_b, dt_bias, a_log, d_skip, ssd_norm_w, attn_sinks,
           w_out, rel_bias, norm_ffn, w_query, sub_keys1, sub_keys2, expert_down, expert_up,
           norm_final):
    assert norm_mix.shape[0] == 1, "single-layer block"
    return _layer(x, norm_mix[0], w_in[0], conv_w[0], conv_b[0], dt_bias[0], a_log[0], d_skip[0],
                  ssd_norm_w[0], attn_sinks[0], w_out[0], rel_bias, norm_ffn[0], w_query[0],
                  sub_keys1[0], sub_keys2[0], expert_down[0], expert_up[0], norm_final)
```

```python
import functools
import math

import jax
import jax.numpy as jnp
from jax import lax
from jax.experimental import pallas as pl
from jax.experimental.pallas import tpu as pltpu

F32 = jnp.float32
BF16 = jnp.bfloat16

D_MODEL = 1024
ATTN_HEADS = 8
ATTN_KV_HEADS = 2
ATTN_GROUP = ATTN_HEADS // ATTN_KV_HEADS
HEAD_DIM = 64
ATTN_WIDTH = ATTN_HEADS * HEAD_DIM
ATTN_BLOCK = 128
N_BUCKETS = 32
MAX_DISTANCE = 128
SSD_HEADS = 8
SSD_HEAD_DIM = 64
SSD_INNER = SSD_HEADS * SSD_HEAD_DIM
SSD_STATE = 128
SSD_GROUPS = 2
SSD_HEADS_PER_GROUP = SSD_HEADS // SSD_GROUPS
SSD_CONV = 4
SSD_CHUNK = 128
PEER_HEADS = 8
N_KEYS = 128
N_EXPERTS = N_KEYS * N_KEYS
PEER_HALF = 128
PEER_TOPK = 16
EPS = 1e-6

KV_COLS = ATTN_KV_HEADS * HEAD_DIM
XBC_COLS = SSD_INNER + 2 * SSD_GROUPS * SSD_STATE
OFF_K = ATTN_WIDTH
OFF_V = OFF_K + KV_COLS
OFF_Z = OFF_V + KV_COLS
OFF_XBC = OFF_Z + SSD_INNER
OFF_DT = OFF_XBC + XBC_COLS
LANES = 128
SUBLANES = 8
DT_PAD = LANES
IN_COLS_PAD = OFF_DT + DT_PAD
CONV_HALO = SUBLANES

VMEM_LIMIT = 56 * 1024 * 1024

ATTN_BLOCKS_PER_STEP = 4
INPROJ_TM = 1024
OUTPROJ_TM = 512
ROUTE_TM = SUBLANES * LANES
PEER_TM = 512
PEER_TE = 2048
PEER_SUB = 512
PEER_LANE_CHUNK = 512
PEER_ROWS = 16


def _dot(a, b):
    return jnp.dot(a, b, preferred_element_type=F32)


def _dot_nt(a, b):
    return lax.dot_general(a, b, (((1,), (1,)), ((), ())), preferred_element_type=F32)


def _dot_tn(a, b):
    return lax.dot_general(a, b, (((0,), (0,)), ((), ())), preferred_element_type=F32)


def _rms(x, w):
    return x * lax.rsqrt(jnp.mean(x * x, axis=-1, keepdims=True) + EPS) * w


def _silu(x):
    return x / (1.0 + jnp.exp(-x))


def _softplus(x):
    return jnp.maximum(x, 0.0) + jnp.log1p(jnp.exp(-jnp.abs(x)))


def _split3(x):
    hi = x.astype(BF16)
    rest = x - hi.astype(F32)
    mid = rest.astype(BF16)
    return hi, mid, (rest - mid.astype(F32)).astype(BF16)


def _params(*sem):
    return pltpu.CompilerParams(dimension_semantics=sem, vmem_limit_bytes=VMEM_LIMIT)


def _inproj_kernel(x_ref, nw_ref, w_ref, q_ref, k_ref, v_ref, z_ref, xbc_ref, dt_ref):
    xb = _rms(x_ref[...], nw_ref[...]).astype(BF16)

    def proj(lo, hi):
        return _dot(xb, w_ref[:, lo:hi])

    q_ref[...] = (proj(0, OFF_K) * (HEAD_DIM ** -0.5)).astype(BF16)
    k_ref[...] = proj(OFF_K, OFF_V).astype(BF16)
    v_ref[...] = proj(OFF_V, OFF_Z).astype(BF16)
    z_ref[...] = proj(OFF_Z, OFF_XBC)
    xbc_ref[...] = proj(OFF_XBC, OFF_DT)
    dt_ref[...] = proj(OFF_DT, IN_COLS_PAD)


def _inproj(x2, norm_w, w_in_pad):
    t = x2.shape[0]
    tm = min(INPROJ_TM, t)
    row = lambda i: (i, 0)
    fixed = lambda i: (0, 0)
    return pl.pallas_call(
        _inproj_kernel,
        grid=(t // tm,),
        in_specs=[
            pl.BlockSpec((tm, D_MODEL), row),
            pl.BlockSpec((1, D_MODEL), fixed),
            pl.BlockSpec((D_MODEL, IN_COLS_PAD), fixed),
        ],
        out_specs=[
            pl.BlockSpec((tm, ATTN_WIDTH), row),
            pl.BlockSpec((tm, KV_COLS), row),
            pl.BlockSpec((tm, KV_COLS), row),
            pl.BlockSpec((tm, SSD_INNER), row),
            pl.BlockSpec((tm, XBC_COLS), row),
            pl.BlockSpec((tm, DT_PAD), row),
        ],
        out_shape=[
            jax.ShapeDtypeStruct((t, ATTN_WIDTH), BF16),
            jax.ShapeDtypeStruct((t, KV_COLS), BF16),
            jax.ShapeDtypeStruct((t, KV_COLS), BF16),
            jax.ShapeDtypeStruct((t, SSD_INNER), F32),
            jax.ShapeDtypeStruct((t, XBC_COLS), F32),
            jax.ShapeDtypeStruct((t, DT_PAD), F32),
        ],
        compiler_params=_params("parallel"),
        name="inproj",
    )(x2, norm_w, w_in_pad)


def _t5_bucket(dist):
    n = jnp.maximum(dist, 0)
    max_exact = N_BUCKETS // 2
    nf = jnp.maximum(n, 1).astype(F32)
    large = max_exact + (jnp.log(nf / max_exact) / math.log(MAX_DISTANCE / max_exact)
                         * (N_BUCKETS - max_exact)).astype(jnp.int32)
    large = jnp.minimum(large, N_BUCKETS - 1)
    return jnp.where(n < max_exact, n, large)


def _attn_kernel(bucket_ref, relb_ref, sink_ref, q_ref, kp_ref, kc_ref, vp_ref, vc_ref,
                 o_ref, bias_ref):
    blk = ATTN_BLOCK
    rows = ATTN_GROUP * blk
    n = pl.program_id(1)

    @pl.when((pl.program_id(0) == 0) & (n == 0))
    def _():
        bucket = bucket_ref[...]
        for h in range(ATTN_HEADS):
            acc = jnp.zeros((blk, blk), F32)
            for b in range(N_BUCKETS):
                acc = jnp.where(bucket == b, relb_ref[b, h], acc)
            hk, g = divmod(h, ATTN_GROUP)
            bias_ref[hk, g * blk:(g + 1) * blk, :] = acc

    qi = lax.broadcasted_iota(jnp.int32, (rows, blk), 0) & (blk - 1)
    kj = lax.broadcasted_iota(jnp.int32, (rows, blk), 1)
    own = kj <= qi
    for i in range(ATTN_BLOCKS_PER_STEP):
        cur = slice(i * blk, (i + 1) * blk)
        kprev, vprev = (kp_ref, vp_ref) if i == 0 else (kc_ref.at[(i - 1) * blk:i * blk],
                                                        vc_ref.at[(i - 1) * blk:i * blk])
        has_prev = (n > 0) if i == 0 else True
        for hk in range(ATTN_KV_HEADS):
            ksl = slice(hk * HEAD_DIM, (hk + 1) * HEAD_DIM)
            kband = jnp.concatenate([kprev[:, ksl], kc_ref[cur, ksl]], axis=0)
            vband = jnp.concatenate([vprev[:, ksl], vc_ref[cur, ksl]], axis=0)
            qg = jnp.concatenate(
                [q_ref[cur, (hk * ATTN_GROUP + g) * HEAD_DIM:(hk * ATTN_GROUP + g + 1) * HEAD_DIM]
                 for g in range(ATTN_GROUP)], axis=0)
            s2 = _dot_nt(qg, kband)
            s = jnp.where(own, s2[:, blk:], s2[:, :blk]) + bias_ref[hk]
            if i == 0:
                s = jnp.where(own | has_prev, s, -jnp.inf)
            sink = jnp.concatenate(
                [jnp.full((blk, blk), sink_ref[hk * ATTN_GROUP + g], F32)
                 for g in range(ATTN_GROUP)], axis=0)
            m = jnp.maximum(jnp.broadcast_to(jnp.max(s, axis=-1, keepdims=True), s.shape), sink)
            pb = jnp.exp(s - m).astype(BF16)
            zero = jnp.zeros_like(pb)
            pfull = jnp.concatenate([jnp.where(own, zero, pb), jnp.where(own, pb, zero)], axis=1)
            denom = _dot(pfull, jnp.ones((2 * blk, blk), BF16)) + jnp.exp(sink - m)
            o = _dot(pfull, vband) / denom[:, :HEAD_DIM]
            for g in range(ATTN_GROUP):
                h = hk * ATTN_GROUP + g
                o_ref[cur, h * HEAD_DIM:(h + 1) * HEAD_DIM] = o[g * blk:(g + 1) * blk].astype(BF16)


def _attention(q, k, v, bucket, rel_bias, sinks):
    b, s, _ = q.shape
    rows = ATTN_BLOCKS_PER_STEP * ATTN_BLOCK
    assert s % rows == 0
    cur = lambda i, j: (i, j, 0)
    prev = lambda i, j: (i, jnp.maximum(j * ATTN_BLOCKS_PER_STEP - 1, 0), 0)
    smem = pl.BlockSpec(memory_space=pltpu.SMEM)
    return pl.pallas_call(
        _attn_kernel,
        grid=(b, s // rows),
        in_specs=[
            pl.BlockSpec((ATTN_BLOCK, ATTN_BLOCK), lambda i, j: (0, 0)),
            smem, smem,
            pl.BlockSpec((None, rows, ATTN_WIDTH), cur),
            pl.BlockSpec((None, ATTN_BLOCK, KV_COLS), prev),
            pl.BlockSpec((None, rows, KV_COLS), cur),
            pl.BlockSpec((None, ATTN_BLOCK, KV_COLS), prev),
            pl.BlockSpec((None, rows, KV_COLS), cur),
        ],
        out_specs=pl.BlockSpec((None, rows, ATTN_WIDTH), cur),
        out_shape=jax.ShapeDtypeStruct((b, s, ATTN_WIDTH), BF16),
        scratch_shapes=[pltpu.VMEM((ATTN_KV_HEADS, ATTN_GROUP * ATTN_BLOCK, ATTN_BLOCK), F32)],
        compiler_params=_params("arbitrary", "arbitrary"),
        name="swa",
    )(bucket, rel_bias, sinks, q, k, k, v, v)


def _ssd_kernel(xbc_ref, z_ref, dt_ref, convw_ref, convb_ref, dtb_ref, alog_ref, dskip_ref,
                nw_ref, expand_ref, tri_ref, o_ref, xpad_ref, state_ref):
    L = SSD_CHUNK
    P = SSD_HEAD_DIM
    c = pl.program_id(1)

    @pl.when(c == 0)
    def _():
        xpad_ref[0:CONV_HALO, :] = jnp.zeros((CONV_HALO, XBC_COLS), F32)
        state_ref[...] = jnp.zeros_like(state_ref)

    @pl.when(c > 0)
    def _():
        xpad_ref[0:CONV_HALO, :] = xpad_ref[L:L + CONV_HALO, :]

    xpad_ref[CONV_HALO:CONV_HALO + L, :] = xbc_ref[...]
    conv = convb_ref[...]
    for j in range(SSD_CONV):
        off = CONV_HALO - (SSD_CONV - 1) + j
        conv = conv + convw_ref[j:j + 1, :] * xpad_ref[off:off + L, :]
    xbc = _silu(conv)
    xs = xbc[:, :SSD_INNER]
    bm = xbc[:, SSD_INNER:SSD_INNER + SSD_GROUPS * SSD_STATE].astype(BF16)
    cm = xbc[:, SSD_INNER + SSD_GROUPS * SSD_STATE:].astype(BF16)

    expand = expand_ref[...]
    dt = _softplus(dt_ref[...] + dtb_ref[...])
    a = dt * (-jnp.exp(alog_ref[...]))
    a_cum = sum(_dot(tri_ref[...], term) for term in _split3(a))
    a_last = a_cum[L - 1:L, :]
    stacked = jnp.concatenate([dt, jnp.exp(a_last - a_cum), jnp.exp(a_cum)], axis=0)
    wide = sum(_dot(term, expand) for term in _split3(stacked))
    x_dt = xs * wide[0:L]
    x_end = (x_dt * wide[L:2 * L]).astype(BF16)
    dec_start = wide[2 * L:3 * L]
    x_dt = x_dt.astype(BF16)
    a_cum_t = a_cum.T

    li = lax.broadcasted_iota(jnp.int32, (L, L), 0)
    si = lax.broadcasted_iota(jnp.int32, (L, L), 1)
    causal = si <= li
    ys = []
    for g in range(SSD_GROUPS):
        bg = bm[:, g * SSD_STATE:(g + 1) * SSD_STATE]
        cg = cm[:, g * SSD_STATE:(g + 1) * SSD_STATE]
        cb = _dot_nt(cg, bg)
        for r in range(SSD_HEADS_PER_GROUP):
            h = g * SSD_HEADS_PER_GROUP + r
            sl = slice(h * P, (h + 1) * P)
            seg = a_cum[:, h:h + 1] - a_cum_t[h:h + 1, :]
            lmat = jnp.where(causal, jnp.exp(jnp.where(causal, seg, 0.0)), 0.0)
            h_prev = state_ref[sl, :]
            y_diag = _dot((cb * lmat).astype(BF16), x_dt[:, sl])
            y_off = _dot_nt(cg, h_prev.astype(BF16)) * dec_start[:, sl]
            ys.append(y_diag + y_off)
            chunk_decay = jnp.exp(a_cum_t[h:h + 1, L - 1:L])
            state_ref[sl, :] = h_prev * chunk_decay + _dot_tn(x_end[:, sl], bg)
    y = jnp.concatenate(ys, axis=1) + dskip_ref[...] * xs
    y = y * _silu(z_ref[...])
    o_ref[...] = _rms(y, nw_ref[...]).astype(BF16)


def _ssd(xbc, z, dt, conv_w, conv_b, dt_bias_pad, a_log_pad, d_skip_wide, norm_w, expand, tri):
    b, s, _ = xbc.shape
    nc = s // SSD_CHUNK
    cur = lambda i, j: (i, j, 0)
    fixed = lambda i, j: (0, 0)
    full = lambda shape: pl.BlockSpec(shape, fixed)
    return pl.pallas_call(
        _ssd_kernel,
        grid=(b, nc),
        in_specs=[
            pl.BlockSpec((None, SSD_CHUNK, XBC_COLS), cur),
            pl.BlockSpec((None, SSD_CHUNK, SSD_INNER), cur),
            pl.BlockSpec((None, SSD_CHUNK, DT_PAD), cur),
            full((SSD_CONV, XBC_COLS)), full((1, XBC_COLS)), full((1, DT_PAD)), full((1, DT_PAD)),
            full((1, SSD_INNER)), full((1, SSD_INNER)), full((DT_PAD, SSD_INNER)),
            full((SSD_CHUNK, SSD_CHUNK)),
        ],
        out_specs=pl.BlockSpec((None, SSD_CHUNK, SSD_INNER), cur),
        out_shape=jax.ShapeDtypeStruct((b, s, SSD_INNER), BF16),
        scratch_shapes=[
            pltpu.VMEM((CONV_HALO + SSD_CHUNK, XBC_COLS), F32),
            pltpu.VMEM((SSD_INNER, SSD_STATE), F32),
        ],
        compiler_params=_params("arbitrary", "arbitrary"),
        name="ssd",
    )(xbc, z, dt, conv_w, conv_b, dt_bias_pad, a_log_pad, d_skip_wide, norm_w, expand, tri)


def _outproj_kernel(x_ref, attn_ref, ssd_ref, wo_ref, nw_ref, wq_ref, h1_ref, xnt_ref, q_ref):
    mix = (_dot(attn_ref[...], wo_ref[0:ATTN_WIDTH, :])
           + _dot(ssd_ref[...], wo_ref[ATTN_WIDTH:ATTN_WIDTH + SSD_INNER, :]))
    h1 = x_ref[...] + mix
    h1_ref[...] = h1
    xn = _rms(h1, nw_ref[...])
    xnt_ref[...] = pltpu.bitcast(xn.T.astype(BF16), jnp.uint32)
    qf = _dot(xn.astype(BF16), wq_ref[...])
    for j in range(2 * PEER_HEADS):
        q_ref[j] = qf[:, j * PEER_HALF:(j + 1) * PEER_HALF].astype(BF16)


def _outproj(x2, attn, ssd, w_out, norm_w, w_query):
    t = x2.shape[0]
    tm = min(OUTPROJ_TM, t)
    row = lambda i: (i, 0)
    fixed = lambda i: (0, 0)
    nq = 2 * PEER_HEADS
    return pl.pallas_call(
        _outproj_kernel,
        grid=(t // tm,),
        in_specs=[
            pl.BlockSpec((tm, D_MODEL), row),
            pl.BlockSpec((tm, ATTN_WIDTH), row),
            pl.BlockSpec((tm, SSD_INNER), row),
            pl.BlockSpec((ATTN_WIDTH + SSD_INNER, D_MODEL), fixed),
            pl.BlockSpec((1, D_MODEL), fixed),
            pl.BlockSpec((D_MODEL, nq * PEER_HALF), fixed),
        ],
        out_specs=[
            pl.BlockSpec((tm, D_MODEL), row),
            pl.BlockSpec((D_MODEL // 2, tm), lambda i: (0, i)),
            pl.BlockSpec((nq, tm, PEER_HALF), lambda i: (0, i, 0)),
        ],
        out_shape=[
            jax.ShapeDtypeStruct((t, D_MODEL), F32),
            jax.ShapeDtypeStruct((D_MODEL // 2, t), jnp.uint32),
            jax.ShapeDtypeStruct((nq, t, PEER_HALF), BF16),
        ],
        compiler_params=_params("parallel"),
        name="outproj",
    )(x2, attn, ssd, w_out, norm_w, w_query)


def _ce(x, hi, lo):
    a, b = x[hi], x[lo]
    if b is None:
        return
    if a is None:
        x[hi], x[lo] = b, None
        return
    x[hi], x[lo] = jnp.maximum(a, b), jnp.minimum(a, b)


def _bitonic_merge_desc(x):
    n = len(x)
    j = n // 2
    while j >= 1:
        for i in range(n):
            if i & j == 0:
                _ce(x, i, i | j)
        j //= 2


def _sort_desc(x):
    n = len(x)
    k = 2
    while k <= n:
        j = k // 2
        while j >= 1:
            for i in range(n):
                l = i ^ j
                if l > i:
                    if i & k == 0:
                        _ce(x, i, l)
                    else:
                        _ce(x, l, i)
            j //= 2
        k *= 2


def _merge_top(a, b, sort=True):
    n = len(a)
    out = []
    for i in range(n):
        u, v = a[i], b[n - 1 - i]
        out.append(v if u is None else u if v is None else jnp.maximum(u, v))
    if sort:
        _bitonic_merge_desc(out)
    return out


def _key_rows(k):
    return slice(k * SUBLANES, (k + 1) * SUBLANES)


def _top16_sorted(s_ref, lanes):
    groups = []
    for g in range(N_KEYS // PEER_TOPK):
        x = [s_ref[_key_rows(g * PEER_TOPK + i), lanes] for i in range(PEER_TOPK)]
        _sort_desc(x)
        groups.append(x)
    while len(groups) > 1:
        groups = [_merge_top(groups[i], groups[i + 1]) for i in range(0, len(groups), 2)]
    return groups[0]


def _route_select(h, lanes, s1_ref, s2_ref, r2_ref, e2_ref, c1_ref, e1_ref,
                  r2s_ref, e2s_ref, c1s_ref, e1s_ref):
    v1 = _top16_sorted(s1_ref, lanes)
    v2 = _top16_sorted(s2_ref, lanes)

    lists = []
    for a in range(PEER_TOPK):
        n_b = PEER_TOPK // (a + 1)
        lists.append([v1[a] + v2[b] if b < n_b else None for b in range(PEER_TOPK)])
    while len(lists) > 2:
        lists = [_merge_top(lists[i], lists[i + 1]) for i in range(0, len(lists), 2)]
    top = _merge_top(lists[0], lists[1], sort=False)
    tau = top[0]
    for x in top[1:]:
        tau = jnp.minimum(tau, x)

    e1_top = [jnp.exp(v - v1[0]) for v in v1]
    e2_top = [jnp.exp(v - v2[0]) for v in v2]
    zsum = jnp.zeros_like(tau)
    cnt = []
    for a in range(PEER_TOPK):
        n_a = jnp.zeros_like(tau)
        w_a = jnp.zeros_like(tau)
        for b in range(PEER_TOPK // (a + 1)):
            sel = (v1[a] + v2[b]) >= tau
            n_a = n_a + jnp.where(sel, 1.0, 0.0)
            w_a = w_a + jnp.where(sel, e2_top[b], 0.0)
        cnt.append(n_a)
        zsum = zsum + e1_top[a] * w_a
    half_inv_z = 0.5 / zsum

    for k in range(N_KEYS):
        rows = _key_rows(k)
        s1 = s1_ref[rows, lanes]
        s2 = s2_ref[rows, lanes]
        c1 = jnp.zeros_like(tau)
        for a in reversed(range(PEER_TOPK)):
            c1 = jnp.where(s1 >= v1[a], cnt[a], c1)
        r2 = jnp.zeros_like(tau)
        for b in range(PEER_TOPK):
            r2 = jnp.where(v2[b] > s2, float(b + 1), r2)
        c1s_ref[rows, :] = c1
        r2s_ref[rows, :] = r2
        e1s_ref[rows, :] = jnp.exp(s1 - v1[0]) * half_inv_z
        e2s_ref[rows, :] = jnp.exp(s2 - v2[0])

    for j in range(SUBLANES):
        cols = slice(j * LANES, (j + 1) * LANES)
        take = pl.ds(j, N_KEYS, stride=SUBLANES)
        r2_ref[h, :, cols] = r2s_ref[take, :].astype(BF16)
        e2_ref[h, :, cols] = e2s_ref[take, :].astype(BF16)
        c1_ref[h, :, cols] = c1s_ref[take, :]
        e1_ref[h, :, cols] = e1s_ref[take, :]


def _route_kernel(q_ref, k1_ref, k2_ref, r2_ref, e2_ref, c1_ref, e1_ref,
                  s1_ref, s2_ref, r2s_ref, e2s_ref, c1s_ref, e1s_ref):
    def head_pair(i, carry):
        for half, (k_ref, s_ref) in enumerate(((k1_ref, s1_ref), (k2_ref, s2_ref))):
            qcat = jnp.concatenate(
                [jnp.concatenate([q_ref[2 * (2 * i + hh) + half, j * LANES:(j + 1) * LANES, :]
                                  for j in range(SUBLANES)], axis=1)
                 for hh in range(2)], axis=0)
            s_ref[...] = _dot_nt(k_ref[...], qcat)
        for hh in range(2):
            _route_select(2 * i + hh, slice(hh * LANES, (hh + 1) * LANES), s1_ref, s2_ref,
                          r2_ref, e2_ref, c1_ref, e1_ref, r2s_ref, e2s_ref, c1s_ref, e1s_ref)
        return carry

    lax.fori_loop(0, PEER_HEADS // 2, head_pair, 0)


def _expand_keys(keys):
    eye = jnp.eye(SUBLANES, dtype=keys.dtype)
    return jnp.einsum("kd,jc->kjcd", keys, eye).reshape(N_KEYS * SUBLANES, SUBLANES * PEER_HALF)


def _route(q, keys1, keys2):
    nq, t, _ = q.shape
    tm = ROUTE_TM
    assert t % tm == 0
    fixed = lambda i: (0, 0)
    out_spec = pl.BlockSpec((PEER_HEADS, N_KEYS, tm), lambda i: (0, 0, i))
    shape = (PEER_HEADS, N_KEYS, t)
    key_spec = pl.BlockSpec((N_KEYS * SUBLANES, SUBLANES * PEER_HALF), fixed)
    stage = pltpu.VMEM((N_KEYS * SUBLANES, LANES), F32)
    scores = pltpu.VMEM((N_KEYS * SUBLANES, 2 * LANES), F32)
    return pl.pallas_call(
        _route_kernel,
        grid=(t // tm,),
        in_specs=[pl.BlockSpec((nq, tm, PEER_HALF), lambda i: (0, i, 0)), key_spec, key_spec],
        out_specs=[out_spec] * 4,
        out_shape=[jax.ShapeDtypeStruct(shape, BF16)] * 2 + [jax.ShapeDtypeStruct(shape, F32)] * 2,
        scratch_shapes=[scores] * 2 + [stage] * 4,
        compiler_params=_params("parallel"),
        name="route",
    )(q, _expand_keys(keys1), _expand_keys(keys2))


def _peer_activations(pre_ref, act_ref, r2_ref, e2_ref, c1_ref, e1_ref, sub):
    tm = pre_ref.shape[1]
    chunks = N_KEYS // PEER_ROWS
    shape = (PEER_ROWS, PEER_LANE_CHUNK)
    zero = jnp.zeros(shape, BF16)
    for a in range(PEER_SUB // N_KEYS):
        key1 = sub * (PEER_SUB // N_KEYS) + a
        for lc in range(tm // PEER_LANE_CHUNK):
            lanes = slice(lc * PEER_LANE_CHUNK, (lc + 1) * PEER_LANE_CHUNK)
            gates = [zero] * chunks
            for h in range(PEER_HEADS):
                cnt = jnp.broadcast_to(c1_ref[h, key1:key1 + 1, lanes], shape).astype(BF16)
                e1 = jnp.broadcast_to(e1_ref[h, key1:key1 + 1, lanes], shape).astype(BF16)
                for c in range(chunks):
                    rows = slice(c * PEER_ROWS, (c + 1) * PEER_ROWS)
                    sel = r2_ref[h, rows, lanes] < cnt
                    gates[c] = gates[c] + jnp.where(sel, e2_ref[h, rows, lanes], zero) * e1
            for c in range(chunks):
                rows = slice(a * N_KEYS + c * PEER_ROWS, a * N_KEYS + (c + 1) * PEER_ROWS)
                p = pre_ref[rows, lanes]
                gelu2 = p + p * lax.erf(p * (2.0 ** -0.5))
                act_ref[rows, lanes] = gelu2.astype(BF16) * gates[c]


def _peer_kernel(xnt_ref, d_ref, ut_ref, r2_ref, e2_ref, c1_ref, e1_ref, h1_ref, nw_ref,
                 o_ref, acc_ref, pre_ref, act_ref):
    j = pl.program_id(1)
    nsub = ut_ref.shape[1] // PEER_SUB

    @pl.when(j == 0)
    def _():
        acc_ref[...] = jnp.zeros_like(acc_ref)

    def down(sub):
        d_sub = d_ref[sub * (PEER_SUB // 2):(sub + 1) * (PEER_SUB // 2), :]
        pre_ref[sub % 2] = _dot(pltpu.bitcast(d_sub, BF16), pltpu.bitcast(xnt_ref[...], BF16))

    down(0)
    for sub in range(nsub):
        if sub + 1 < nsub:
            down(sub + 1)
        _peer_activations(pre_ref.at[sub % 2], act_ref.at[sub % 2], r2_ref, e2_ref, c1_ref,
                          e1_ref, sub)
        ut_sub = ut_ref[:, sub * PEER_SUB:(sub + 1) * PEER_SUB]
        acc_ref[...] += _dot(pltpu.bitcast(ut_sub, BF16), act_ref[sub % 2])

    @pl.when(j == pl.num_programs(1) - 1)
    def _():
        h2 = h1_ref[...] + acc_ref[...].T
        o_ref[...] = _rms(h2, nw_ref[...])


def _pack_kernel(w_ref, o_ref, *, transpose):
    w = w_ref[...].T if transpose else w_ref[...]
    o_ref[...] = pltpu.bitcast(w.astype(BF16), jnp.uint32)


def _pack_experts(w, transpose):
    tiles = N_EXPERTS // PEER_TE
    if transpose:
        out_spec = pl.BlockSpec((None, D_MODEL // 2, PEER_TE), lambda i: (i, 0, 0))
        out_shape = jax.ShapeDtypeStruct((tiles, D_MODEL // 2, PEER_TE), jnp.uint32)
    else:
        out_spec = pl.BlockSpec((PEER_TE // 2, D_MODEL), lambda i: (i, 0))
        out_shape = jax.ShapeDtypeStruct((N_EXPERTS // 2, D_MODEL), jnp.uint32)
    return pl.pallas_call(
        functools.partial(_pack_kernel, transpose=transpose),
        grid=(tiles,),
        in_specs=[pl.BlockSpec((PEER_TE, D_MODEL), lambda i: (i, 0))],
        out_specs=out_spec,
        out_shape=out_shape,
        compiler_params=_params("parallel"),
        name="pack_up" if transpose else "pack_down",
    )(w)


def _peer(xnt, down, up_t, r2, e2, c1, e1, h1, norm_w):
    t = h1.shape[0]
    tm = min(PEER_TM, t)
    te = PEER_TE
    key2_spec = pl.BlockSpec((PEER_HEADS, N_KEYS, tm), lambda i, j: (0, 0, i))
    key1_spec = pl.BlockSpec((PEER_HEADS, te // N_KEYS, tm), lambda i, j: (0, j, i))
    return pl.pallas_call(
        _peer_kernel,
        grid=(t // tm, N_EXPERTS // te),
        in_specs=[
            pl.BlockSpec((D_MODEL // 2, tm), lambda i, j: (0, i)),
            pl.BlockSpec((te // 2, D_MODEL), lambda i, j: (j, 0)),
            pl.BlockSpec((None, D_MODEL // 2, te), lambda i, j: (j, 0, 0)),
            key2_spec, key2_spec, key1_spec, key1_spec,
            pl.BlockSpec((tm, D_MODEL), lambda i, j: (i, 0)),
            pl.BlockSpec((1, D_MODEL), lambda i, j: (0, 0)),
        ],
        out_specs=pl.BlockSpec((tm, D_MODEL), lambda i, j: (i, 0)),
        out_shape=jax.ShapeDtypeStruct((t, D_MODEL), F32),
        scratch_shapes=[
            pltpu.VMEM((D_MODEL, tm), F32),
            pltpu.VMEM((2, PEER_SUB, tm), F32),
            pltpu.VMEM((2, PEER_SUB, tm), BF16),
        ],
        compiler_params=_params("parallel", "arbitrary"),
        name="peer",
    )(xnt, down, up_t, r2, e2, c1, e1, h1, norm_w)


def _pad_lanes(v, width):
    return jnp.pad(v.astype(F32), (0, width - v.shape[0])).reshape(1, width)


def _layer(x, norm_mix, w_in, conv_w, conv_b, dt_bias, a_log, d_skip, ssd_norm_w, attn_sinks,
           w_out, rel_bias, norm_ffn, w_query, sub_keys1, sub_keys2, expert_down, expert_up,
           out_norm):
    b, s, d = x.shape
    t = b * s
    x2 = x.reshape(t, d)

    w_in_pad = jnp.pad(w_in, ((0, 0), (0, IN_COLS_PAD - w_in.shape[1]))).astype(BF16)
    q, k, v, z, xbc, dt = _inproj(x2, norm_mix.reshape(1, d), w_in_pad)

    qi = jnp.arange(ATTN_BLOCK)[:, None]
    kj = jnp.arange(ATTN_BLOCK)[None, :]
    bucket = _t5_bucket(jnp.where(kj <= qi, qi - kj, qi + ATTN_BLOCK - kj)).astype(jnp.int32)
    attn = _attention(q.reshape(b, s, -1), k.reshape(b, s, -1), v.reshape(b, s, -1),
                      bucket, rel_bias.astype(F32), attn_sinks.astype(F32))

    lane = jnp.arange(SSD_INNER)[None, :] // SSD_HEAD_DIM
    expand = (jnp.arange(DT_PAD)[:, None] == lane).astype(BF16)
    tri = (jnp.arange(SSD_CHUNK)[None, :] <= jnp.arange(SSD_CHUNK)[:, None]).astype(BF16)
    ssd = _ssd(xbc.reshape(b, s, -1), z.reshape(b, s, -1), dt.reshape(b, s, -1),
               conv_w, conv_b.reshape(1, -1), _pad_lanes(dt_bias, DT_PAD),
               _pad_lanes(a_log, DT_PAD), jnp.repeat(d_skip.astype(F32), SSD_HEAD_DIM).reshape(1, -1),
               ssd_norm_w.reshape(1, -1), expand, tri)

    h1, xnt, pq = _outproj(x2, attn.reshape(t, -1), ssd.reshape(t, -1), w_out.astype(BF16),
                           norm_ffn.reshape(1, d), w_query.astype(BF16))
    r2, e2, c1, e1 = _route(pq, sub_keys1.astype(BF16), sub_keys2.astype(BF16))
    out = _peer(xnt, _pack_experts(expert_down, False), _pack_experts(expert_up, True),
                r2, e2, c1, e1, h1, out_norm.reshape(1, d))
    return out.reshape(b, s, d)


@jax.jit
def kernel(x, norm_mix, w_in, conv_w, conv_b, dt_bias, a_log, d_skip, ssd_norm_w, attn_sinks,
           w_out, rel_bias, norm_ffn, w_query, sub_keys1, sub_keys2, expert_down, expert_up,
           norm_final):
    assert norm_mix.shape[0] == 1, "single-layer block"
    return _layer(x, norm_mix[0], w_in[0], conv_w[0], conv_b[0], dt_bias[0], a_log[0], d_skip[0],
                  ssd_norm_w[0], attn_sinks[0], w_out[0], rel_bias, norm_ffn[0], w_query[0],
                  sub_keys1[0], sub_keys2[0], expert_down[0], expert_up[0], norm_final)
```

```python
import functools
import math

import jax
import jax.numpy as jnp
from jax import lax
from jax.experimental import pallas as pl
from jax.experimental.pallas import tpu as pltpu

F32 = jnp.float32
BF16 = jnp.bfloat16

D_MODEL = 1024
ATTN_HEADS = 8
ATTN_KV_HEADS = 2
ATTN_GROUP = ATTN_HEADS // ATTN_KV_HEADS
HEAD_DIM = 64
ATTN_WIDTH = ATTN_HEADS * HEAD_DIM
ATTN_BLOCK = 128
N_BUCKETS = 32
MAX_DISTANCE = 128
SSD_HEADS = 8
SSD_HEAD_DIM = 64
SSD_INNER = SSD_HEADS * SSD_HEAD_DIM
SSD_STATE = 128
SSD_GROUPS = 2
SSD_HEADS_PER_GROUP = SSD_HEADS // SSD_GROUPS
SSD_CONV = 4
SSD_CHUNK = 128
PEER_HEADS = 8
N_KEYS = 128
N_EXPERTS = N_KEYS * N_KEYS
PEER_HALF = 128
PEER_TOPK = 16
EPS = 1e-6

KV_COLS = ATTN_KV_HEADS * HEAD_DIM
XBC_COLS = SSD_INNER + 2 * SSD_GROUPS * SSD_STATE
OFF_K = ATTN_WIDTH
OFF_V = OFF_K + KV_COLS
OFF_Z = OFF_V + KV_COLS
OFF_XBC = OFF_Z + SSD_INNER
OFF_DT = OFF_XBC + XBC_COLS
LANES = 128
SUBLANES = 8
DT_PAD = LANES
IN_COLS_PAD = OFF_DT + DT_PAD
CONV_HALO = SUBLANES

VMEM_LIMIT = 56 * 1024 * 1024

ATTN_BLOCKS_PER_STEP = 4
SSD_SEQS_PER_STEP = 2
INPROJ_TM = 1024
OUTPROJ_TM = 512
ROUTE_TM = SUBLANES * LANES
PEER_TM = 512
PEER_TE = 2048
PEER_SUB = 512
PEER_LANE_CHUNK = 512
PEER_ROWS = 16


def _dot(a, b):
    return jnp.dot(a, b, preferred_element_type=F32)


def _dot_nt(a, b):
    return lax.dot_general(a, b, (((1,), (1,)), ((), ())), preferred_element_type=F32)


def _dot_tn(a, b):
    return lax.dot_general(a, b, (((0,), (0,)), ((), ())), preferred_element_type=F32)


def _rms(x, w):
    return x * lax.rsqrt(jnp.mean(x * x, axis=-1, keepdims=True) + EPS) * w


def _silu(x):
    return x / (1.0 + jnp.exp(-x))


def _softplus(x):
    return jnp.maximum(x, 0.0) + jnp.log1p(jnp.exp(-jnp.abs(x)))


def _split3(x):
    hi = x.astype(BF16)
    rest = x - hi.astype(F32)
    mid = rest.astype(BF16)
    return hi, mid, (rest - mid.astype(F32)).astype(BF16)


def _params(*sem):
    return pltpu.CompilerParams(dimension_semantics=sem, vmem_limit_bytes=VMEM_LIMIT)


def _inproj_kernel(x_ref, nw_ref, w_ref, q_ref, k_ref, v_ref, z_ref, xbc_ref, dt_ref):
    xb = _rms(x_ref[...], nw_ref[...]).astype(BF16)

    def proj(lo, hi):
        return _dot(xb, w_ref[:, lo:hi])

    q_ref[...] = (proj(0, OFF_K) * (HEAD_DIM ** -0.5)).astype(BF16)
    k_ref[...] = proj(OFF_K, OFF_V).astype(BF16)
    v_ref[...] = proj(OFF_V, OFF_Z).astype(BF16)
    z_ref[...] = proj(OFF_Z, OFF_XBC)
    xbc_ref[...] = proj(OFF_XBC, OFF_DT)
    dt_ref[...] = proj(OFF_DT, IN_COLS_PAD)


def _inproj(x2, norm_w, w_in_pad):
    t = x2.shape[0]
    tm = min(INPROJ_TM, t)
    row = lambda i: (i, 0)
    fixed = lambda i: (0, 0)
    return pl.pallas_call(
        _inproj_kernel,
        grid=(t // tm,),
        in_specs=[
            pl.BlockSpec((tm, D_MODEL), row),
            pl.BlockSpec((1, D_MODEL), fixed),
            pl.BlockSpec((D_MODEL, IN_COLS_PAD), fixed),
        ],
        out_specs=[
            pl.BlockSpec((tm, ATTN_WIDTH), row),
            pl.BlockSpec((tm, KV_COLS), row),
            pl.BlockSpec((tm, KV_COLS), row),
            pl.BlockSpec((tm, SSD_INNER), row),
            pl.BlockSpec((tm, XBC_COLS), row),
            pl.BlockSpec((tm, DT_PAD), row),
        ],
        out_shape=[
            jax.ShapeDtypeStruct((t, ATTN_WIDTH), BF16),
            jax.ShapeDtypeStruct((t, KV_COLS), BF16),
            jax.ShapeDtypeStruct((t, KV_COLS), BF16),
            jax.ShapeDtypeStruct((t, SSD_INNER), F32),
            jax.ShapeDtypeStruct((t, XBC_COLS), F32),
            jax.ShapeDtypeStruct((t, DT_PAD), F32),
        ],
        compiler_params=_params("parallel"),
        name="inproj",
    )(x2, norm_w, w_in_pad)


def _t5_bucket(dist):
    n = jnp.maximum(dist, 0)
    max_exact = N_BUCKETS // 2
    nf = jnp.maximum(n, 1).astype(F32)
    large = max_exact + (jnp.log(nf / max_exact) / math.log(MAX_DISTANCE / max_exact)
                         * (N_BUCKETS - max_exact)).astype(jnp.int32)
    large = jnp.minimum(large, N_BUCKETS - 1)
    return jnp.where(n < max_exact, n, large)


def _attn_kernel(bucket_ref, relb_ref, sink_ref, q_ref, kp_ref, kc_ref, vp_ref, vc_ref,
                 o_ref, bias_ref):
    blk = ATTN_BLOCK
    rows = ATTN_GROUP * blk
    n = pl.program_id(1)

    @pl.when((pl.program_id(0) == 0) & (n == 0))
    def _():
        bucket = bucket_ref[...]
        for h in range(ATTN_HEADS):
            acc = jnp.zeros((blk, blk), F32)
            for b in range(N_BUCKETS):
                acc = jnp.where(bucket == b, relb_ref[b, h], acc)
            hk, g = divmod(h, ATTN_GROUP)
            bias_ref[hk, g * blk:(g + 1) * blk, :] = acc

    qi = lax.broadcasted_iota(jnp.int32, (rows, blk), 0) & (blk - 1)
    kj = lax.broadcasted_iota(jnp.int32, (rows, blk), 1)
    own = kj <= qi
    for i in range(ATTN_BLOCKS_PER_STEP):
        cur = slice(i * blk, (i + 1) * blk)
        kprev, vprev = (kp_ref, vp_ref) if i == 0 else (kc_ref.at[(i - 1) * blk:i * blk],
                                                        vc_ref.at[(i - 1) * blk:i * blk])
        has_prev = (n > 0) if i == 0 else True
        for hk in range(ATTN_KV_HEADS):
            ksl = slice(hk * HEAD_DIM, (hk + 1) * HEAD_DIM)
            kband = jnp.concatenate([kprev[:, ksl], kc_ref[cur, ksl]], axis=0)
            vband = jnp.concatenate([vprev[:, ksl], vc_ref[cur, ksl]], axis=0)
            qg = jnp.concatenate(
                [q_ref[cur, (hk * ATTN_GROUP + g) * HEAD_DIM:(hk * ATTN_GROUP + g + 1) * HEAD_DIM]
                 for g in range(ATTN_GROUP)], axis=0)
            s2 = _dot_nt(qg, kband)
            s = jnp.where(own, s2[:, blk:], s2[:, :blk]) + bias_ref[hk]
            if i == 0:
                s = jnp.where(own | has_prev, s, -jnp.inf)
            sink = jnp.concatenate(
                [jnp.full((blk, blk), sink_ref[hk * ATTN_GROUP + g], F32)
                 for g in range(ATTN_GROUP)], axis=0)
            m = jnp.maximum(jnp.broadcast_to(jnp.max(s, axis=-1, keepdims=True), s.shape), sink)
            pb = jnp.exp(s - m).astype(BF16)
            zero = jnp.zeros_like(pb)
            pfull = jnp.concatenate([jnp.where(own, zero, pb), jnp.where(own, pb, zero)], axis=1)
            denom = _dot(pfull, jnp.ones((2 * blk, blk), BF16)) + jnp.exp(sink - m)
            o = _dot(pfull, vband) / denom[:, :HEAD_DIM]
            for g in range(ATTN_GROUP):
                h = hk * ATTN_GROUP + g
                o_ref[cur, h * HEAD_DIM:(h + 1) * HEAD_DIM] = o[g * blk:(g + 1) * blk].astype(BF16)


def _attention(q, k, v, bucket, rel_bias, sinks):
    b, s, _ = q.shape
    rows = ATTN_BLOCKS_PER_STEP * ATTN_BLOCK
    assert s % rows == 0
    cur = lambda i, j: (i, j, 0)
    prev = lambda i, j: (i, jnp.maximum(j * ATTN_BLOCKS_PER_STEP - 1, 0), 0)
    smem = pl.BlockSpec(memory_space=pltpu.SMEM)
    return pl.pallas_call(
        _attn_kernel,
        grid=(b, s // rows),
        in_specs=[
            pl.BlockSpec((ATTN_BLOCK, ATTN_BLOCK), lambda i, j: (0, 0)),
            smem, smem,
            pl.BlockSpec((None, rows, ATTN_WIDTH), cur),
            pl.BlockSpec((None, ATTN_BLOCK, KV_COLS), prev),
            pl.BlockSpec((None, rows, KV_COLS), cur),
            pl.BlockSpec((None, ATTN_BLOCK, KV_COLS), prev),
            pl.BlockSpec((None, rows, KV_COLS), cur),
        ],
        out_specs=pl.BlockSpec((None, rows, ATTN_WIDTH), cur),
        out_shape=jax.ShapeDtypeStruct((b, s, ATTN_WIDTH), BF16),
        scratch_shapes=[pltpu.VMEM((ATTN_KV_HEADS, ATTN_GROUP * ATTN_BLOCK, ATTN_BLOCK), F32)],
        compiler_params=_params("arbitrary", "arbitrary"),
        name="swa",
    )(bucket, rel_bias, sinks, q, k, k, v, v)


def _ssd_chunk(xbc_ref, z_ref, dt_ref, convw_ref, convb_ref, dtb_ref, alog_ref, dskip_ref,
               nw_ref, expand_ref, tri_ref, o_ref, xpad_ref, state_ref, c):
    L = SSD_CHUNK
    P = SSD_HEAD_DIM

    @pl.when(c == 0)
    def _():
        xpad_ref[0:CONV_HALO, :] = jnp.zeros((CONV_HALO, XBC_COLS), F32)
        state_ref[...] = jnp.zeros_like(state_ref)

    @pl.when(c > 0)
    def _():
        xpad_ref[0:CONV_HALO, :] = xpad_ref[L:L + CONV_HALO, :]

    xpad_ref[CONV_HALO:CONV_HALO + L, :] = xbc_ref[...]
    conv = convb_ref[...]
    for j in range(SSD_CONV):
        off = CONV_HALO - (SSD_CONV - 1) + j
        conv = conv + convw_ref[j:j + 1, :] * xpad_ref[off:off + L, :]
    xbc = _silu(conv)
    xs = xbc[:, :SSD_INNER]
    bm = xbc[:, SSD_INNER:SSD_INNER + SSD_GROUPS * SSD_STATE].astype(BF16)
    cm = xbc[:, SSD_INNER + SSD_GROUPS * SSD_STATE:].astype(BF16)

    expand = expand_ref[...]
    dt = _softplus(dt_ref[...] + dtb_ref[...])
    a = dt * (-jnp.exp(alog_ref[...]))
    a_cum = sum(_dot(tri_ref[...], term) for term in _split3(a))
    a_last = a_cum[L - 1:L, :]
    stacked = jnp.concatenate([dt, jnp.exp(a_last - a_cum), jnp.exp(a_cum)], axis=0)
    wide = sum(_dot(term, expand) for term in _split3(stacked))
    x_dt = xs * wide[0:L]
    x_end = (x_dt * wide[L:2 * L]).astype(BF16)
    dec_start = wide[2 * L:3 * L]
    x_dt = x_dt.astype(BF16)
    a_cum_t = a_cum.T

    li = lax.broadcasted_iota(jnp.int32, (L, L), 0)
    si = lax.broadcasted_iota(jnp.int32, (L, L), 1)
    causal = si <= li
    ys = []
    for g in range(SSD_GROUPS):
        bg = bm[:, g * SSD_STATE:(g + 1) * SSD_STATE]
        cg = cm[:, g * SSD_STATE:(g + 1) * SSD_STATE]
        cb = _dot_nt(cg, bg)
        for r in range(SSD_HEADS_PER_GROUP):
            h = g * SSD_HEADS_PER_GROUP + r
            sl = slice(h * P, (h + 1) * P)
            seg = a_cum[:, h:h + 1] - a_cum_t[h:h + 1, :]
            lmat = jnp.where(causal, jnp.exp(jnp.where(causal, seg, 0.0)), 0.0)
            h_prev = state_ref[sl, :]
            y_diag = _dot((cb * lmat).astype(BF16), x_dt[:, sl])
            y_off = _dot_nt(cg, h_prev.astype(BF16)) * dec_start[:, sl]
            ys.append(y_diag + y_off)
            chunk_decay = jnp.exp(a_cum_t[h:h + 1, L - 1:L])
            state_ref[sl, :] = h_prev * chunk_decay + _dot_tn(x_end[:, sl], bg)
    y = jnp.concatenate(ys, axis=1) + dskip_ref[...] * xs
    y = y * _silu(z_ref[...])
    o_ref[...] = _rms(y, nw_ref[...]).astype(BF16)


def _ssd_kernel(xbc_ref, z_ref, dt_ref, convw_ref, convb_ref, dtb_ref, alog_ref, dskip_ref,
                nw_ref, expand_ref, tri_ref, o_ref, xpad_ref, state_ref):
    c = pl.program_id(1)
    for b in range(SSD_SEQS_PER_STEP):
        _ssd_chunk(xbc_ref.at[b], z_ref.at[b], dt_ref.at[b], convw_ref, convb_ref, dtb_ref,
                   alog_ref, dskip_ref, nw_ref, expand_ref, tri_ref, o_ref.at[b],
                   xpad_ref.at[b], state_ref.at[b], c)


def _ssd(xbc, z, dt, conv_w, conv_b, dt_bias_pad, a_log_pad, d_skip_wide, norm_w, expand, tri):
    b, s, _ = xbc.shape
    nb = SSD_SEQS_PER_STEP
    assert b % nb == 0
    nc = s // SSD_CHUNK
    cur = lambda i, j: (i, j, 0)
    fixed = lambda i, j: (0, 0)
    full = lambda shape: pl.BlockSpec(shape, fixed)
    return pl.pallas_call(
        _ssd_kernel,
        grid=(b // nb, nc),
        in_specs=[
            pl.BlockSpec((nb, SSD_CHUNK, XBC_COLS), cur),
            pl.BlockSpec((nb, SSD_CHUNK, SSD_INNER), cur),
            pl.BlockSpec((nb, SSD_CHUNK, DT_PAD), cur),
            full((SSD_CONV, XBC_COLS)), full((1, XBC_COLS)), full((1, DT_PAD)), full((1, DT_PAD)),
            full((1, SSD_INNER)), full((1, SSD_INNER)), full((DT_PAD, SSD_INNER)),
            full((SSD_CHUNK, SSD_CHUNK)),
        ],
        out_specs=pl.BlockSpec((nb, SSD_CHUNK, SSD_INNER), cur),
        out_shape=jax.ShapeDtypeStruct((b, s, SSD_INNER), BF16),
        scratch_shapes=[
            pltpu.VMEM((nb, CONV_HALO + SSD_CHUNK, XBC_COLS), F32),
            pltpu.VMEM((nb, SSD_INNER, SSD_STATE), F32),
        ],
        compiler_params=_params("arbitrary", "arbitrary"),
        name="ssd",
    )(xbc, z, dt, conv_w, conv_b, dt_bias_pad, a_log_pad, d_skip_wide, norm_w, expand, tri)


def _outproj_kernel(x_ref, attn_ref, ssd_ref, wo_ref, nw_ref, wq_ref, h1_ref, xnt_ref, q_ref):
    mix = (_dot(attn_ref[...], wo_ref[0:ATTN_WIDTH, :])
           + _dot(ssd_ref[...], wo_ref[ATTN_WIDTH:ATTN_WIDTH + SSD_INNER, :]))
    h1 = x_ref[...] + mix
    h1_ref[...] = h1
    xn = _rms(h1, nw_ref[...])
    xnt_ref[...] = pltpu.bitcast(xn.T.astype(BF16), jnp.uint32)
    qf = _dot(xn.astype(BF16), wq_ref[...])
    for j in range(2 * PEER_HEADS):
        q_ref[j] = qf[:, j * PEER_HALF:(j + 1) * PEER_HALF].astype(BF16)


def _outproj(x2, attn, ssd, w_out, norm_w, w_query):
    t = x2.shape[0]
    tm = min(OUTPROJ_TM, t)
    row = lambda i: (i, 0)
    fixed = lambda i: (0, 0)
    nq = 2 * PEER_HEADS
    return pl.pallas_call(
        _outproj_kernel,
        grid=(t // tm,),
        in_specs=[
            pl.BlockSpec((tm, D_MODEL), row),
            pl.BlockSpec((tm, ATTN_WIDTH), row),
            pl.BlockSpec((tm, SSD_INNER), row),
            pl.BlockSpec((ATTN_WIDTH + SSD_INNER, D_MODEL), fixed),
            pl.BlockSpec((1, D_MODEL), fixed),
            pl.BlockSpec((D_MODEL, nq * PEER_HALF), fixed),
        ],
        out_specs=[
            pl.BlockSpec((tm, D_MODEL), row),
            pl.BlockSpec((D_MODEL // 2, tm), lambda i: (0, i)),
            pl.BlockSpec((nq, tm, PEER_HALF), lambda i: (0, i, 0)),
        ],
        out_shape=[
            jax.ShapeDtypeStruct((t, D_MODEL), F32),
            jax.ShapeDtypeStruct((D_MODEL // 2, t), jnp.uint32),
            jax.ShapeDtypeStruct((nq, t, PEER_HALF), BF16),
        ],
        compiler_params=_params("parallel"),
        name="outproj",
    )(x2, attn, ssd, w_out, norm_w, w_query)


def _ce(x, hi, lo):
    a, b = x[hi], x[lo]
    if b is None:
        return
    if a is None:
        x[hi], x[lo] = b, None
        return
    x[hi], x[lo] = jnp.maximum(a, b), jnp.minimum(a, b)


def _bitonic_merge_desc(x):
    n = len(x)
    j = n // 2
    while j >= 1:
        for i in range(n):
            if i & j == 0:
                _ce(x, i, i | j)
        j //= 2


def _sort_desc(x):
    n = len(x)
    k = 2
    while k <= n:
        j = k // 2
        while j >= 1:
            for i in range(n):
                l = i ^ j
                if l > i:
                    if i & k == 0:
                        _ce(x, i, l)
                    else:
                        _ce(x, l, i)
            j //= 2
        k *= 2


def _merge_top(a, b, sort=True):
    n = len(a)
    out = []
    for i in range(n):
        u, v = a[i], b[n - 1 - i]
        out.append(v if u is None else u if v is None else jnp.maximum(u, v))
    if sort:
        _bitonic_merge_desc(out)
    return out


def _key_rows(k):
    return slice(k * SUBLANES, (k + 1) * SUBLANES)


def _top16_sorted(s_ref, lanes):
    groups = []
    for g in range(N_KEYS // PEER_TOPK):
        x = [s_ref[_key_rows(g * PEER_TOPK + i), lanes] for i in range(PEER_TOPK)]
        _sort_desc(x)
        groups.append(x)
    while len(groups) > 1:
        groups = [_merge_top(groups[i], groups[i + 1]) for i in range(0, len(groups), 2)]
    return groups[0]


def _route_select(h, lanes, s1_ref, s2_ref, r2_ref, e2_ref, c1_ref, e1_ref,
                  r2s_ref, e2s_ref, c1s_ref, e1s_ref):
    v1 = _top16_sorted(s1_ref, lanes)
    v2 = _top16_sorted(s2_ref, lanes)

    lists = []
    for a in range(PEER_TOPK):
        n_b = PEER_TOPK // (a + 1)
        lists.append([v1[a] + v2[b] if b < n_b else None for b in range(PEER_TOPK)])
    while len(lists) > 2:
        lists = [_merge_top(lists[i], lists[i + 1]) for i in range(0, len(lists), 2)]
    top = _merge_top(lists[0], lists[1], sort=False)
    tau = top[0]
    for x in top[1:]:
        tau = jnp.minimum(tau, x)

    e1_top = [jnp.exp(v - v1[0]) for v in v1]
    e2_top = [jnp.exp(v - v2[0]) for v in v2]
    zsum = jnp.zeros_like(tau)
    cnt = []
    for a in range(PEER_TOPK):
        n_a = jnp.zeros_like(tau)
        w_a = jnp.zeros_like(tau)
        for b in range(PEER_TOPK // (a + 1)):
            sel = (v1[a] + v2[b]) >= tau
            n_a = n_a + jnp.where(sel, 1.0, 0.0)
            w_a = w_a + jnp.where(sel, e2_top[b], 0.0)
        cnt.append(n_a)
        zsum = zsum + e1_top[a] * w_a
    half_inv_z = 0.5 / zsum

    for k in range(N_KEYS):
        rows = _key_rows(k)
        s1 = s1_ref[rows, lanes]
        s2 = s2_ref[rows, lanes]
        c1 = jnp.zeros_like(tau)
        for a in reversed(range(PEER_TOPK)):
            c1 = jnp.where(s1 >= v1[a], cnt[a], c1)
        r2 = jnp.zeros_like(tau)
        for b in range(PEER_TOPK):
            r2 = jnp.where(v2[b] > s2, float(b + 1), r2)
        c1s_ref[rows, :] = c1
        r2s_ref[rows, :] = r2
        e1s_ref[rows, :] = jnp.exp(s1 - v1[0]) * half_inv_z
        e2s_ref[rows, :] = jnp.exp(s2 - v2[0])

    for j in range(SUBLANES):
        cols = slice(j * LANES, (j + 1) * LANES)
        take = pl.ds(j, N_KEYS, stride=SUBLANES)
        r2_ref[h, :, cols] = r2s_ref[take, :].astype(BF16)
        e2_ref[h, :, cols] = e2s_ref[take, :].astype(BF16)
        c1_ref[h, :, cols] = c1s_ref[take, :]
        e1_ref[h, :, cols] = e1s_ref[take, :]


def _route_kernel(q_ref, k1_ref, k2_ref, r2_ref, e2_ref, c1_ref, e1_ref,
                  s1_ref, s2_ref, r2s_ref, e2s_ref, c1s_ref, e1s_ref):
    def head_pair(i, carry):
        for half, (k_ref, s_ref) in enumerate(((k1_ref, s1_ref), (k2_ref, s2_ref))):
            qcat = jnp.concatenate(
                [jnp.concatenate([q_ref[2 * (2 * i + hh) + half, j * LANES:(j + 1) * LANES, :]
                                  for j in range(SUBLANES)], axis=1)
                 for hh in range(2)], axis=0)
            s_ref[...] = _dot_nt(k_ref[...], qcat)
        for hh in range(2):
            _route_select(2 * i + hh, slice(hh * LANES, (hh + 1) * LANES), s1_ref, s2_ref,
                          r2_ref, e2_ref, c1_ref, e1_ref, r2s_ref, e2s_ref, c1s_ref, e1s_ref)
        return carry

    lax.fori_loop(0, PEER_HEADS // 2, head_pair, 0)


def _expand_keys(keys):
    eye = jnp.eye(SUBLANES, dtype=keys.dtype)
    return jnp.einsum("kd,jc->kjcd", keys, eye).reshape(N_KEYS * SUBLANES, SUBLANES * PEER_HALF)


def _route(q, keys1, keys2):
    nq, t, _ = q.shape
    tm = ROUTE_TM
    assert t % tm == 0
    fixed = lambda i: (0, 0)
    out_spec = pl.BlockSpec((PEER_HEADS, N_KEYS, tm), lambda i: (0, 0, i))
    shape = (PEER_HEADS, N_KEYS, t)
    key_spec = pl.BlockSpec((N_KEYS * SUBLANES, SUBLANES * PEER_HALF), fixed)
    stage = pltpu.VMEM((N_KEYS * SUBLANES, LANES), F32)
    scores = pltpu.VMEM((N_KEYS * SUBLANES, 2 * LANES), F32)
    return pl.pallas_call(
        _route_kernel,
        grid=(t // tm,),
        in_specs=[pl.BlockSpec((nq, tm, PEER_HALF), lambda i: (0, i, 0)), key_spec, key_spec],
        out_specs=[out_spec] * 4,
        out_shape=[jax.ShapeDtypeStruct(shape, BF16)] * 2 + [jax.ShapeDtypeStruct(shape, F32)] * 2,
        scratch_shapes=[scores] * 2 + [stage] * 4,
        compiler_params=_params("parallel"),
        name="route",
    )(q, _expand_keys(keys1), _expand_keys(keys2))


def _peer_activations(pre_ref, act_ref, r2_ref, e2_ref, c1_ref, e1_ref, sub):
    tm = pre_ref.shape[1]
    chunks = N_KEYS // PEER_ROWS
    shape = (PEER_ROWS, PEER_LANE_CHUNK)
    zero = jnp.zeros(shape, BF16)
    for a in range(PEER_SUB // N_KEYS):
        key1 = sub * (PEER_SUB // N_KEYS) + a
        for lc in range(tm // PEER_LANE_CHUNK):
            lanes = slice(lc * PEER_LANE_CHUNK, (lc + 1) * PEER_LANE_CHUNK)
            gates = [zero] * chunks
            for h in range(PEER_HEADS):
                cnt = jnp.broadcast_to(c1_ref[h, key1:key1 + 1, lanes], shape).astype(BF16)
                e1 = jnp.broadcast_to(e1_ref[h, key1:key1 + 1, lanes], shape).astype(BF16)
                for c in range(chunks):
                    rows = slice(c * PEER_ROWS, (c + 1) * PEER_ROWS)
                    sel = r2_ref[h, rows, lanes] < cnt
                    gates[c] = gates[c] + jnp.where(sel, e2_ref[h, rows, lanes], zero) * e1
            for c in range(chunks):
                rows = slice(a * N_KEYS + c * PEER_ROWS, a * N_KEYS + (c + 1) * PEER_ROWS)
                p = pre_ref[rows, lanes]
                gelu2 = p + p * lax.erf(p * (2.0 ** -0.5))
                act_ref[rows, lanes] = gelu2.astype(BF16) * gates[c]


def _peer_kernel(xnt_ref, d_ref, ut_ref, r2_ref, e2_ref, c1_ref, e1_ref, h1_ref, nw_ref,
                 o_ref, acc_ref, pre_ref, act_ref):
    j = pl.program_id(1)
    nsub = ut_ref.shape[1] // PEER_SUB

    @pl.when(j == 0)
    def _():
        acc_ref[...] = jnp.zeros_like(acc_ref)

    def down(sub):
        d_sub = d_ref[sub * (PEER_SUB // 2):(sub + 1) * (PEER_SUB // 2), :]
        pre_ref[sub % 2] = _dot(pltpu.bitcast(d_sub, BF16), pltpu.bitcast(xnt_ref[...], BF16))

    down(0)
    for sub in range(nsub):
        if sub + 1 < nsub:
            down(sub + 1)
        _peer_activations(pre_ref.at[sub % 2], act_ref.at[sub % 2], r2_ref, e2_ref, c1_ref,
                          e1_ref, sub)
        ut_sub = ut_ref[:, sub * PEER_SUB:(sub + 1) * PEER_SUB]
        acc_ref[...] += _dot(pltpu.bitcast(ut_sub, BF16), act_ref[sub % 2])

    @pl.when(j == pl.num_programs(1) - 1)
    def _():
        h2 = h1_ref[...] + acc_ref[...].T
        o_ref[...] = _rms(h2, nw_ref[...])


def _pack_kernel(w_ref, o_ref, *, transpose):
    w = w_ref[...].T if transpose else w_ref[...]
    o_ref[...] = pltpu.bitcast(w.astype(BF16), jnp.uint32)


def _pack_experts(w, transpose):
    tiles = N_EXPERTS // PEER_TE
    if transpose:
        out_spec = pl.BlockSpec((None, D_MODEL // 2, PEER_TE), lambda i: (i, 0, 0))
        out_shape = jax.ShapeDtypeStruct((tiles, D_MODEL // 2, PEER_TE), jnp.uint32)
    else:
        out_spec = pl.BlockSpec((PEER_TE // 2, D_MODEL), lambda i: (i, 0))
        out_shape = jax.ShapeDtypeStruct((N_EXPERTS // 2, D_MODEL), jnp.uint32)
    return pl.pallas_call(
        functools.partial(_pack_kernel, transpose=transpose),
        grid=(tiles,),
        in_specs=[pl.BlockSpec((PEER_TE, D_MODEL), lambda i: (i, 0))],
        out_specs=out_spec,
        out_shape=out_shape,
        compiler_params=_params("parallel"),
        name="pack_up" if transpose else "pack_down",
    )(w)


def _peer(xnt, down, up_t, r2, e2, c1, e1, h1, norm_w):
    t = h1.shape[0]
    tm = min(PEER_TM, t)
    te = PEER_TE
    key2_spec = pl.BlockSpec((PEER_HEADS, N_KEYS, tm), lambda i, j: (0, 0, i))
    key1_spec = pl.BlockSpec((PEER_HEADS, te // N_KEYS, tm), lambda i, j: (0, j, i))
    return pl.pallas_call(
        _peer_kernel,
        grid=(t // tm, N_EXPERTS // te),
        in_specs=[
            pl.BlockSpec((D_MODEL // 2, tm), lambda i, j: (0, i)),
            pl.BlockSpec((te // 2, D_MODEL), lambda i, j: (j, 0)),
            pl.BlockSpec((None, D_MODEL // 2, te), lambda i, j: (j, 0, 0)),
            key2_spec, key2_spec, key1_spec, key1_spec,
            pl.BlockSpec((tm, D_MODEL), lambda i, j: (i, 0)),
            pl.BlockSpec((1, D_MODEL), lambda i, j: (0, 0)),
        ],
        out_specs=pl.BlockSpec((tm, D_MODEL), lambda i, j: (i, 0)),
        out_shape=jax.ShapeDtypeStruct((t, D_MODEL), F32),
        scratch_shapes=[
            pltpu.VMEM((D_MODEL, tm), F32),
            pltpu.VMEM((2, PEER_SUB, tm), F32),
            pltpu.VMEM((2, PEER_SUB, tm), BF16),
        ],
        compiler_params=_params("parallel", "arbitrary"),
        name="peer",
    )(xnt, down, up_t, r2, e2, c1, e1, h1, norm_w)


def _pad_lanes(v, width):
    return jnp.pad(v.astype(F32), (0, width - v.shape[0])).reshape(1, width)


def _layer(x, norm_mix, w_in, conv_w, conv_b, dt_bias, a_log, d_skip, ssd_norm_w, attn_sinks,
           w_out, rel_bias, norm_ffn, w_query, sub_keys1, sub_keys2, expert_down, expert_up,
           out_norm):
    b, s, d = x.shape
    t = b * s
    x2 = x.reshape(t, d)

    w_in_pad = jnp.pad(w_in, ((0, 0), (0, IN_COLS_PAD - w_in.shape[1]))).astype(BF16)
    q, k, v, z, xbc, dt = _inproj(x2, norm_mix.reshape(1, d), w_in_pad)

    qi = jnp.arange(ATTN_BLOCK)[:, None]
    kj = jnp.arange(ATTN_BLOCK)[None, :]
    bucket = _t5_bucket(jnp.where(kj <= qi, qi - kj, qi + ATTN_BLOCK - kj)).astype(jnp.int32)
    attn = _attention(q.reshape(b, s, -1), k.reshape(b, s, -1), v.reshape(b, s, -1),
                      bucket, rel_bias.astype(F32), attn_sinks.astype(F32))

    lane = jnp.arange(SSD_INNER)[None, :] // SSD_HEAD_DIM
    expand = (jnp.arange(DT_PAD)[:, None] == lane).astype(BF16)
    tri = (jnp.arange(SSD_CHUNK)[None, :] <= jnp.arange(SSD_CHUNK)[:, None]).astype(BF16)
    ssd = _ssd(xbc.reshape(b, s, -1), z.reshape(b, s, -1), dt.reshape(b, s, -1),
               conv_w, conv_b.reshape(1, -1), _pad_lanes(dt_bias, DT_PAD),
               _pad_lanes(a_log, DT_PAD), jnp.repeat(d_skip.astype(F32), SSD_HEAD_DIM).reshape(1, -1),
               ssd_norm_w.reshape(1, -1), expand, tri)

    h1, xnt, pq = _outproj(x2, attn.reshape(t, -1), ssd.reshape(t, -1), w_out.astype(BF16),
                           norm_ffn.reshape(1, d), w_query.astype(BF16))
    r2, e2, c1, e1 = _route(pq, sub_keys1.astype(BF16), sub_keys2.astype(BF16))
    out = _peer(xnt, _pack_experts(expert_down, False), _pack_experts(expert_up, True),
                r2, e2, c1, e1, h1, out_norm.reshape(1, d))
    return out.reshape(b, s, d)


@jax.jit
def kernel(x, norm_mix, w_in, conv_w, conv_b, dt_bias, a_log, d_skip, ssd_norm_w, attn_sinks,
           w_out, rel_bias, norm_ffn, w_query, sub_keys1, sub_keys2, expert_down, expert_up,
           norm_final):
    assert norm_mix.shape[0] == 1, "single-layer block"
    return _layer(x, norm_mix[0], w_in[0], conv_w[0], conv_b[0], dt_bias[0], a_log[0], d_skip[0],
                  ssd_norm_w[0], attn_sinks[0], w_out[0], rel_bias, norm_ffn[0], w_query[0],
                  sub_keys1[0], sub_keys2[0], expert_down[0], expert_up[0], norm_final)
```

```python
import functools
import math

import jax
import jax.numpy as jnp
from jax import lax
from jax.experimental import pallas as pl
from jax.experimental.pallas import tpu as pltpu

F32 = jnp.float32
BF16 = jnp.bfloat16

D_MODEL = 1024
ATTN_HEADS = 8
ATTN_KV_HEADS = 2
ATTN_GROUP = ATTN_HEADS // ATTN_KV_HEADS
HEAD_DIM = 64
ATTN_WIDTH = ATTN_HEADS * HEAD_DIM
ATTN_BLOCK = 128
N_BUCKETS = 32
MAX_DISTANCE = 128
SSD_HEADS = 8
SSD_HEAD_DIM = 64
SSD_INNER = SSD_HEADS * SSD_HEAD_DIM
SSD_STATE = 128
SSD_GROUPS = 2
SSD_HEADS_PER_GROUP = SSD_HEADS // SSD_GROUPS
SSD_CONV = 4
SSD_CHUNK = 128
PEER_HEADS = 8
N_KEYS = 128
N_EXPERTS = N_KEYS * N_KEYS
PEER_HALF = 128
PEER_TOPK = 16
EPS = 1e-6

KV_COLS = ATTN_KV_HEADS * HEAD_DIM
XBC_COLS = SSD_INNER + 2 * SSD_GROUPS * SSD_STATE
OFF_K = ATTN_WIDTH
OFF_V = OFF_K + KV_COLS
OFF_Z = OFF_V + KV_COLS
OFF_XBC = OFF_Z + SSD_INNER
OFF_DT = OFF_XBC + XBC_COLS
LANES = 128
SUBLANES = 8
DT_PAD = LANES
IN_COLS_PAD = OFF_DT + DT_PAD
CONV_HALO = SUBLANES

VMEM_LIMIT = 56 * 1024 * 1024

ATTN_BLOCKS_PER_STEP = 4
SSD_SEQS_PER_STEP = 2
INPROJ_TM = 1024
OUTPROJ_TM = 512
ROUTE_TM = SUBLANES * LANES
PEER_TM = 512
PEER_TE = 2048
PEER_SUB = 512
PEER_LANE_CHUNK = 512
PEER_ROWS = 16


def _dot(a, b):
    return jnp.dot(a, b, preferred_element_type=F32)


def _dot_nt(a, b):
    return lax.dot_general(a, b, (((1,), (1,)), ((), ())), preferred_element_type=F32)


def _dot_tn(a, b):
    return lax.dot_general(a, b, (((0,), (0,)), ((), ())), preferred_element_type=F32)


def _rms(x, w):
    return x * lax.rsqrt(jnp.mean(x * x, axis=-1, keepdims=True) + EPS) * w


def _silu(x):
    return x / (1.0 + jnp.exp(-x))


def _softplus(x):
    return jnp.maximum(x, 0.0) + jnp.log1p(jnp.exp(-jnp.abs(x)))


def _split3(x):
    hi = x.astype(BF16)
    rest = x - hi.astype(F32)
    mid = rest.astype(BF16)
    return hi, mid, (rest - mid.astype(F32)).astype(BF16)


def _params(*sem):
    return pltpu.CompilerParams(dimension_semantics=sem, vmem_limit_bytes=VMEM_LIMIT)


def _inproj_kernel(x_ref, nw_ref, w_ref, q_ref, k_ref, v_ref, z_ref, xbc_ref, dt_ref):
    xb = _rms(x_ref[...], nw_ref[...]).astype(BF16)

    def proj(lo, hi):
        return _dot(xb, w_ref[:, lo:hi])

    q_ref[...] = (proj(0, OFF_K) * (HEAD_DIM ** -0.5)).astype(BF16)
    k_ref[...] = proj(OFF_K, OFF_V).astype(BF16)
    v_ref[...] = proj(OFF_V, OFF_Z).astype(BF16)
    z_ref[...] = proj(OFF_Z, OFF_XBC)
    xbc_ref[...] = proj(OFF_XBC, OFF_DT)
    dt_ref[...] = proj(OFF_DT, IN_COLS_PAD)


def _inproj(x2, norm_w, w_in_pad):
    t = x2.shape[0]
    tm = min(INPROJ_TM, t)
    row = lambda i: (i, 0)
    fixed = lambda i: (0, 0)
    return pl.pallas_call(
        _inproj_kernel,
        grid=(t // tm,),
        in_specs=[
            pl.BlockSpec((tm, D_MODEL), row),
            pl.BlockSpec((1, D_MODEL), fixed),
            pl.BlockSpec((D_MODEL, IN_COLS_PAD), fixed),
        ],
        out_specs=[
            pl.BlockSpec((tm, ATTN_WIDTH), row),
            pl.BlockSpec((tm, KV_COLS), row),
            pl.BlockSpec((tm, KV_COLS), row),
            pl.BlockSpec((tm, SSD_INNER), row),
            pl.BlockSpec((tm, XBC_COLS), row),
            pl.BlockSpec((tm, DT_PAD), row),
        ],
        out_shape=[
            jax.ShapeDtypeStruct((t, ATTN_WIDTH), BF16),
            jax.ShapeDtypeStruct((t, KV_COLS), BF16),
            jax.ShapeDtypeStruct((t, KV_COLS), BF16),
            jax.ShapeDtypeStruct((t, SSD_INNER), F32),
            jax.ShapeDtypeStruct((t, XBC_COLS), F32),
            jax.ShapeDtypeStruct((t, DT_PAD), F32),
        ],
        compiler_params=_params("parallel"),
        name="inproj",
    )(x2, norm_w, w_in_pad)


def _t5_bucket(dist):
    n = jnp.maximum(dist, 0)
    max_exact = N_BUCKETS // 2
    nf = jnp.maximum(n, 1).astype(F32)
    large = max_exact + (jnp.log(nf / max_exact) / math.log(MAX_DISTANCE / max_exact)
                         * (N_BUCKETS - max_exact)).astype(jnp.int32)
    large = jnp.minimum(large, N_BUCKETS - 1)
    return jnp.where(n < max_exact, n, large)


def _attn_kernel(bucket_ref, relb_ref, sink_ref, q_ref, kp_ref, kc_ref, vp_ref, vc_ref,
                 o_ref, bias_ref):
    blk = ATTN_BLOCK
    rows = ATTN_GROUP * blk
    n = pl.program_id(1)

    @pl.when((pl.program_id(0) == 0) & (n == 0))
    def _():
        bucket = bucket_ref[...]
        for h in range(ATTN_HEADS):
            acc = jnp.zeros((blk, blk), F32)
            for b in range(N_BUCKETS):
                acc = jnp.where(bucket == b, relb_ref[b, h], acc)
            hk, g = divmod(h, ATTN_GROUP)
            bias_ref[hk, g * blk:(g + 1) * blk, :] = acc

    qi = lax.broadcasted_iota(jnp.int32, (rows, blk), 0) & (blk - 1)
    kj = lax.broadcasted_iota(jnp.int32, (rows, blk), 1)
    own = kj <= qi
    for i in range(ATTN_BLOCKS_PER_STEP):
        cur = slice(i * blk, (i + 1) * blk)
        kprev, vprev = (kp_ref, vp_ref) if i == 0 else (kc_ref.at[(i - 1) * blk:i * blk],
                                                        vc_ref.at[(i - 1) * blk:i * blk])
        has_prev = (n > 0) if i == 0 else True
        for hk in range(ATTN_KV_HEADS):
            ksl = slice(hk * HEAD_DIM, (hk + 1) * HEAD_DIM)
            kband = jnp.concatenate([kprev[:, ksl], kc_ref[cur, ksl]], axis=0)
            vband = jnp.concatenate([vprev[:, ksl], vc_ref[cur, ksl]], axis=0)
            qg = jnp.concatenate(
                [q_ref[cur, (hk * ATTN_GROUP + g) * HEAD_DIM:(hk * ATTN_GROUP + g + 1) * HEAD_DIM]
                 for g in range(ATTN_GROUP)], axis=0)
            s2 = _dot_nt(qg, kband)
            s = jnp.where(own, s2[:, blk:], s2[:, :blk]) + bias_ref[hk]
            if i == 0:
                s = jnp.where(own | has_prev, s, -jnp.inf)
            sink = jnp.concatenate(
                [jnp.full((blk, blk), sink_ref[hk * ATTN_GROUP + g], F32)
                 for g in range(ATTN_GROUP)], axis=0)
            m = jnp.maximum(jnp.broadcast_to(jnp.max(s, axis=-1, keepdims=True), s.shape), sink)
            pb = jnp.exp(s - m).astype(BF16)
            zero = jnp.zeros_like(pb)
            pfull = jnp.concatenate([jnp.where(own, zero, pb), jnp.where(own, pb, zero)], axis=1)
            denom = _dot(pfull, jnp.ones((2 * blk, blk), BF16)) + jnp.exp(sink - m)
            o = _dot(pfull, vband) / denom[:, :HEAD_DIM]
            for g in range(ATTN_GROUP):
                h = hk * ATTN_GROUP + g
                o_ref[cur, h * HEAD_DIM:(h + 1) * HEAD_DIM] = o[g * blk:(g + 1) * blk].astype(BF16)


def _attention(q, k, v, bucket, rel_bias, sinks):
    b, s, _ = q.shape
    rows = ATTN_BLOCKS_PER_STEP * ATTN_BLOCK
    assert s % rows == 0
    cur = lambda i, j: (i, j, 0)
    prev = lambda i, j: (i, jnp.maximum(j * ATTN_BLOCKS_PER_STEP - 1, 0), 0)
    smem = pl.BlockSpec(memory_space=pltpu.SMEM)
    return pl.pallas_call(
        _attn_kernel,
        grid=(b, s // rows),
        in_specs=[
            pl.BlockSpec((ATTN_BLOCK, ATTN_BLOCK), lambda i, j: (0, 0)),
            smem, smem,
            pl.BlockSpec((None, rows, ATTN_WIDTH), cur),
            pl.BlockSpec((None, ATTN_BLOCK, KV_COLS), prev),
            pl.BlockSpec((None, rows, KV_COLS), cur),
            pl.BlockSpec((None, ATTN_BLOCK, KV_COLS), prev),
            pl.BlockSpec((None, rows, KV_COLS), cur),
        ],
        out_specs=pl.BlockSpec((None, rows, ATTN_WIDTH), cur),
        out_shape=jax.ShapeDtypeStruct((b, s, ATTN_WIDTH), BF16),
        scratch_shapes=[pltpu.VMEM((ATTN_KV_HEADS, ATTN_GROUP * ATTN_BLOCK, ATTN_BLOCK), F32)],
        compiler_params=_params("arbitrary", "arbitrary"),
        name="swa",
    )(bucket, rel_bias, sinks, q, k, k, v, v)


def _ssd_chunk(xbc_ref, z_ref, dt_ref, convw_ref, convb_ref, dtb_ref, alog_ref, dskip_ref,
               nw_ref, expand_ref, tri_ref, o_ref, xpad_ref, state_ref):
    L = SSD_CHUNK
    P = SSD_HEAD_DIM
    xpad_ref[CONV_HALO:CONV_HALO + L, :] = xbc_ref[...]
    conv = convb_ref[...]
    for j in range(SSD_CONV):
        off = CONV_HALO - (SSD_CONV - 1) + j
        conv = conv + convw_ref[j:j + 1, :] * xpad_ref[off:off + L, :]
    xbc = _silu(conv)
    xs = xbc[:, :SSD_INNER]
    bm = xbc[:, SSD_INNER:SSD_INNER + SSD_GROUPS * SSD_STATE].astype(BF16)
    cm = xbc[:, SSD_INNER + SSD_GROUPS * SSD_STATE:].astype(BF16)

    expand = expand_ref[...]
    dt = _softplus(dt_ref[...] + dtb_ref[...])
    a = dt * (-jnp.exp(alog_ref[...]))
    a_cum = sum(_dot(tri_ref[...], term) for term in _split3(a))
    a_last = a_cum[L - 1:L, :]
    stacked = jnp.concatenate([dt, jnp.exp(a_last - a_cum), jnp.exp(a_cum)], axis=0)
    wide = sum(_dot(term, expand) for term in _split3(stacked))
    x_dt = xs * wide[0:L]
    x_end = (x_dt * wide[L:2 * L]).astype(BF16)
    dec_start = wide[2 * L:3 * L]
    x_dt = x_dt.astype(BF16)
    a_cum_t = a_cum.T

    li = lax.broadcasted_iota(jnp.int32, (L, L), 0)
    si = lax.broadcasted_iota(jnp.int32, (L, L), 1)
    causal = si <= li
    ys = []
    for g in range(SSD_GROUPS):
        bg = bm[:, g * SSD_STATE:(g + 1) * SSD_STATE]
        cg = cm[:, g * SSD_STATE:(g + 1) * SSD_STATE]
        cb = _dot_nt(cg, bg)
        for r in range(SSD_HEADS_PER_GROUP):
            h = g * SSD_HEADS_PER_GROUP + r
            sl = slice(h * P, (h + 1) * P)
            seg = a_cum[:, h:h + 1] - a_cum_t[h:h + 1, :]
            lmat = jnp.where(causal, jnp.exp(jnp.where(causal, seg, 0.0)), 0.0)
            h_prev = state_ref[sl, :]
            y_diag = _dot((cb * lmat).astype(BF16), x_dt[:, sl])
            y_off = _dot_nt(cg, h_prev.astype(BF16)) * dec_start[:, sl]
            ys.append(y_diag + y_off)
            chunk_decay = jnp.exp(a_cum_t[h:h + 1, L - 1:L])
            state_ref[sl, :] = h_prev * chunk_decay + _dot_tn(x_end[:, sl], bg)
    y = jnp.concatenate(ys, axis=1) + dskip_ref[...] * xs
    y = y * _silu(z_ref[...])
    o_ref[...] = _rms(y, nw_ref[...]).astype(BF16)


def _ssd_kernel(xbc_ref, z_ref, dt_ref, convw_ref, convb_ref, dtb_ref, alog_ref, dskip_ref,
                nw_ref, expand_ref, tri_ref, o_ref, xpad_ref, state_ref):
    c = pl.program_id(1)

    @pl.when(c == 0)
    def _():
        xpad_ref[:, 0:CONV_HALO, :] = jnp.zeros((SSD_SEQS_PER_STEP, CONV_HALO, XBC_COLS), F32)
        state_ref[...] = jnp.zeros_like(state_ref)

    @pl.when(c > 0)
    def _():
        xpad_ref[:, 0:CONV_HALO, :] = xpad_ref[:, SSD_CHUNK:SSD_CHUNK + CONV_HALO, :]

    for b in range(SSD_SEQS_PER_STEP):
        _ssd_chunk(xbc_ref.at[b], z_ref.at[b], dt_ref.at[b], convw_ref, convb_ref, dtb_ref,
                   alog_ref, dskip_ref, nw_ref, expand_ref, tri_ref, o_ref.at[b],
                   xpad_ref.at[b], state_ref.at[b])


def _ssd(xbc, z, dt, conv_w, conv_b, dt_bias_pad, a_log_pad, d_skip_wide, norm_w, expand, tri):
    b, s, _ = xbc.shape
    nb = SSD_SEQS_PER_STEP
    assert b % nb == 0
    nc = s // SSD_CHUNK
    cur = lambda i, j: (i, j, 0)
    fixed = lambda i, j: (0, 0)
    full = lambda shape: pl.BlockSpec(shape, fixed)
    return pl.pallas_call(
        _ssd_kernel,
        grid=(b // nb, nc),
        in_specs=[
            pl.BlockSpec((nb, SSD_CHUNK, XBC_COLS), cur),
            pl.BlockSpec((nb, SSD_CHUNK, SSD_INNER), cur),
            pl.BlockSpec((nb, SSD_CHUNK, DT_PAD), cur),
            full((SSD_CONV, XBC_COLS)), full((1, XBC_COLS)), full((1, DT_PAD)), full((1, DT_PAD)),
            full((1, SSD_INNER)), full((1, SSD_INNER)), full((DT_PAD, SSD_INNER)),
            full((SSD_CHUNK, SSD_CHUNK)),
        ],
        out_specs=pl.BlockSpec((nb, SSD_CHUNK, SSD_INNER), cur),
        out_shape=jax.ShapeDtypeStruct((b, s, SSD_INNER), BF16),
        scratch_shapes=[
            pltpu.VMEM((nb, CONV_HALO + SSD_CHUNK, XBC_COLS), F32),
            pltpu.VMEM((nb, SSD_INNER, SSD_STATE), F32),
        ],
        compiler_params=_params("arbitrary", "arbitrary"),
        name="ssd",
    )(xbc, z, dt, conv_w, conv_b, dt_bias_pad, a_log_pad, d_skip_wide, norm_w, expand, tri)


def _outproj_kernel(x_ref, attn_ref, ssd_ref, wo_ref, nw_ref, wq_ref, h1_ref, xnt_ref, q_ref):
    mix = (_dot(attn_ref[...], wo_ref[0:ATTN_WIDTH, :])
           + _dot(ssd_ref[...], wo_ref[ATTN_WIDTH:ATTN_WIDTH + SSD_INNER, :]))
    h1 = x_ref[...] + mix
    h1_ref[...] = h1
    xn = _rms(h1, nw_ref[...])
    xnt_ref[...] = pltpu.bitcast(xn.T.astype(BF16), jnp.uint32)
    qf = _dot(xn.astype(BF16), wq_ref[...])
    for j in range(2 * PEER_HEADS):
        q_ref[j] = qf[:, j * PEER_HALF:(j + 1) * PEER_HALF].astype(BF16)


def _outproj(x2, attn, ssd, w_out, norm_w, w_query):
    t = x2.shape[0]
    tm = min(OUTPROJ_TM, t)
    row = lambda i: (i, 0)
    fixed = lambda i: (0, 0)
    nq = 2 * PEER_HEADS
    return pl.pallas_call(
        _outproj_kernel,
        grid=(t // tm,),
        in_specs=[
            pl.BlockSpec((tm, D_MODEL), row),
            pl.BlockSpec((tm, ATTN_WIDTH), row),
            pl.BlockSpec((tm, SSD_INNER), row),
            pl.BlockSpec((ATTN_WIDTH + SSD_INNER, D_MODEL), fixed),
            pl.BlockSpec((1, D_MODEL), fixed),
            pl.BlockSpec((D_MODEL, nq * PEER_HALF), fixed),
        ],
        out_specs=[
            pl.BlockSpec((tm, D_MODEL), row),
            pl.BlockSpec((D_MODEL // 2, tm), lambda i: (0, i)),
            pl.BlockSpec((nq, tm, PEER_HALF), lambda i: (0, i, 0)),
        ],
        out_shape=[
            jax.ShapeDtypeStruct((t, D_MODEL), F32),
            jax.ShapeDtypeStruct((D_MODEL // 2, t), jnp.uint32),
            jax.ShapeDtypeStruct((nq, t, PEER_HALF), BF16),
        ],
        compiler_params=_params("parallel"),
        name="outproj",
    )(x2, attn, ssd, w_out, norm_w, w_query)


def _ce(x, hi, lo):
    a, b = x[hi], x[lo]
    if b is None:
        return
    if a is None:
        x[hi], x[lo] = b, None
        return
    x[hi], x[lo] = jnp.maximum(a, b), jnp.minimum(a, b)


def _bitonic_merge_desc(x):
    n = len(x)
    j = n // 2
    while j >= 1:
        for i in range(n):
            if i & j == 0:
                _ce(x, i, i | j)
        j //= 2


def _sort_desc(x):
    n = len(x)
    k = 2
    while k <= n:
        j = k // 2
        while j >= 1:
            for i in range(n):
                l = i ^ j
                if l > i:
                    if i & k == 0:
                        _ce(x, i, l)
                    else:
                        _ce(x, l, i)
            j //= 2
        k *= 2


def _merge_top(a, b, sort=True):
    n = len(a)
    out = []
    for i in range(n):
        u, v = a[i], b[n - 1 - i]
        out.append(v if u is None else u if v is None else jnp.maximum(u, v))
    if sort:
        _bitonic_merge_desc(out)
    return out


def _key_rows(k):
    return slice(k * SUBLANES, (k + 1) * SUBLANES)


def _top16_sorted(s_ref, lanes):
    groups = []
    for g in range(N_KEYS // PEER_TOPK):
        x = [s_ref[_key_rows(g * PEER_TOPK + i), lanes] for i in range(PEER_TOPK)]
        _sort_desc(x)
        groups.append(x)
    while len(groups) > 1:
        groups = [_merge_top(groups[i], groups[i + 1]) for i in range(0, len(groups), 2)]
    return groups[0]


def _route_select(h, lanes, s1_ref, s2_ref, r2_ref, e2_ref, c1_ref, e1_ref,
                  r2s_ref, e2s_ref, c1s_ref, e1s_ref):
    v1 = _top16_sorted(s1_ref, lanes)
    v2 = _top16_sorted(s2_ref, lanes)

    lists = []
    for a in range(PEER_TOPK):
        n_b = PEER_TOPK // (a + 1)
        lists.append([v1[a] + v2[b] if b < n_b else None for b in range(PEER_TOPK)])
    while len(lists) > 2:
        lists = [_merge_top(lists[i], lists[i + 1]) for i in range(0, len(lists), 2)]
    top = _merge_top(lists[0], lists[1], sort=False)
    tau = top[0]
    for x in top[1:]:
        tau = jnp.minimum(tau, x)

    e1_top = [jnp.exp(v - v1[0]) for v in v1]
    e2_top = [jnp.exp(v - v2[0]) for v in v2]
    zsum = jnp.zeros_like(tau)
    cnt = []
    for a in range(PEER_TOPK):
        n_a = jnp.zeros_like(tau)
        w_a = jnp.zeros_like(tau)
        for b in range(PEER_TOPK // (a + 1)):
            sel = (v1[a] + v2[b]) >= tau
            n_a = n_a + jnp.where(sel, 1.0, 0.0)
            w_a = w_a + jnp.where(sel, e2_top[b], 0.0)
        cnt.append(n_a)
        zsum = zsum + e1_top[a] * w_a
    half_inv_z = 0.5 / zsum

    for k in range(N_KEYS):
        rows = _key_rows(k)
        s1 = s1_ref[rows, lanes]
        s2 = s2_ref[rows, lanes]
        c1 = jnp.zeros_like(tau)
        for a in reversed(range(PEER_TOPK)):
            c1 = jnp.where(s1 >= v1[a], cnt[a], c1)
        r2 = jnp.zeros_like(tau)
        for b in range(PEER_TOPK):
            r2 = jnp.where(v2[b] > s2, float(b + 1), r2)
        c1s_ref[rows, :] = c1
        r2s_ref[rows, :] = r2
        e1s_ref[rows, :] = jnp.exp(s1 - v1[0]) * half_inv_z
        e2s_ref[rows, :] = jnp.exp(s2 - v2[0])

    for j in range(SUBLANES):
        cols = slice(j * LANES, (j + 1) * LANES)
        take = pl.ds(j, N_KEYS, stride=SUBLANES)
        r2_ref[h, :, cols] = r2s_ref[take, :].astype(BF16)
        e2_ref[h, :, cols] = e2s_ref[take, :].astype(BF16)
        c1_ref[h, :, cols] = c1s_ref[take, :]
        e1_ref[h, :, cols] = e1s_ref[take, :]


def _route_kernel(q_ref, k1_ref, k2_ref, r2_ref, e2_ref, c1_ref, e1_ref,
                  s1_ref, s2_ref, r2s_ref, e2s_ref, c1s_ref, e1s_ref):
    def head_pair(i, carry):
        for half, (k_ref, s_ref) in enumerate(((k1_ref, s1_ref), (k2_ref, s2_ref))):
            qcat = jnp.concatenate(
                [jnp.concatenate([q_ref[2 * (2 * i + hh) + half, j * LANES:(j + 1) * LANES, :]
                                  for j in range(SUBLANES)], axis=1)
                 for hh in range(2)], axis=0)
            s_ref[...] = _dot_nt(k_ref[...], qcat)
        for hh in range(2):
            _route_select(2 * i + hh, slice(hh * LANES, (hh + 1) * LANES), s1_ref, s2_ref,
                          r2_ref, e2_ref, c1_ref, e1_ref, r2s_ref, e2s_ref, c1s_ref, e1s_ref)
        return carry

    lax.fori_loop(0, PEER_HEADS // 2, head_pair, 0)


def _expand_keys(keys):
    eye = jnp.eye(SUBLANES, dtype=keys.dtype)
    return jnp.einsum("kd,jc->kjcd", keys, eye).reshape(N_KEYS * SUBLANES, SUBLANES * PEER_HALF)


def _route(q, keys1, keys2):
    nq, t, _ = q.shape
    tm = ROUTE_TM
    assert t % tm == 0
    fixed = lambda i: (0, 0)
    out_spec = pl.BlockSpec((PEER_HEADS, N_KEYS, tm), lambda i: (0, 0, i))
    shape = (PEER_HEADS, N_KEYS, t)
    key_spec = pl.BlockSpec((N_KEYS * SUBLANES, SUBLANES * PEER_HALF), fixed)
    stage = pltpu.VMEM((N_KEYS * SUBLANES, LANES), F32)
    scores = pltpu.VMEM((N_KEYS * SUBLANES, 2 * LANES), F32)
    return pl.pallas_call(
        _route_kernel,
        grid=(t // tm,),
        in_specs=[pl.BlockSpec((nq, tm, PEER_HALF), lambda i: (0, i, 0)), key_spec, key_spec],
        out_specs=[out_spec] * 4,
        out_shape=[jax.ShapeDtypeStruct(shape, BF16)] * 2 + [jax.ShapeDtypeStruct(shape, F32)] * 2,
        scratch_shapes=[scores] * 2 + [stage] * 4,
        compiler_params=_params("parallel"),
        name="route",
    )(q, _expand_keys(keys1), _expand_keys(keys2))


def _peer_activations(pre_ref, act_ref, r2_ref, e2_ref, c1_ref, e1_ref, sub):
    tm = pre_ref.shape[1]
    chunks = N_KEYS // PEER_ROWS
    shape = (PEER_ROWS, PEER_LANE_CHUNK)
    zero = jnp.zeros(shape, BF16)
    for a in range(PEER_SUB // N_KEYS):
        key1 = sub * (PEER_SUB // N_KEYS) + a
        for lc in range(tm // PEER_LANE_CHUNK):
            lanes = slice(lc * PEER_LANE_CHUNK, (lc + 1) * PEER_LANE_CHUNK)
            gates = [zero] * chunks
            for h in range(PEER_HEADS):
                cnt = jnp.broadcast_to(c1_ref[h, key1:key1 + 1, lanes], shape).astype(BF16)
                e1 = jnp.broadcast_to(e1_ref[h, key1:key1 + 1, lanes], shape).astype(BF16)
                for c in range(chunks):
                    rows = slice(c * PEER_ROWS, (c + 1) * PEER_ROWS)
                    sel = r2_ref[h, rows, lanes] < cnt
                    gates[c] = gates[c] + jnp.where(sel, e2_ref[h, rows, lanes], zero) * e1
            for c in range(chunks):
                rows = slice(a * N_KEYS + c * PEER_ROWS, a * N_KEYS + (c + 1) * PEER_ROWS)
                p = pre_ref[rows, lanes]
                gelu2 = p + p * lax.erf(p * (2.0 ** -0.5))
                act_ref[rows, lanes] = gelu2.astype(BF16) * gates[c]


def _peer_kernel(xnt_ref, d_ref, ut_ref, r2_ref, e2_ref, c1_ref, e1_ref, h1_ref, nw_ref,
                 o_ref, acc_ref, pre_ref, act_ref):
    j = pl.program_id(1)
    nsub = ut_ref.shape[1] // PEER_SUB

    @pl.when(j == 0)
    def _():
        acc_ref[...] = jnp.zeros_like(acc_ref)

    def down(sub):
        d_sub = d_ref[sub * (PEER_SUB // 2):(sub + 1) * (PEER_SUB // 2), :]
        pre_ref[sub % 2] = _dot(pltpu.bitcast(d_sub, BF16), pltpu.bitcast(xnt_ref[...], BF16))

    down(0)
    for sub in range(nsub):
        if sub + 1 < nsub:
            down(sub + 1)
        _peer_activations(pre_ref.at[sub % 2], act_ref.at[sub % 2], r2_ref, e2_ref, c1_ref,
                          e1_ref, sub)
        ut_sub = ut_ref[:, sub * PEER_SUB:(sub + 1) * PEER_SUB]
        acc_ref[...] += _dot(pltpu.bitcast(ut_sub, BF16), act_ref[sub % 2])

    @pl.when(j == pl.num_programs(1) - 1)
    def _():
        h2 = h1_ref[...] + acc_ref[...].T
        o_ref[...] = _rms(h2, nw_ref[...])


def _pack_kernel(w_ref, o_ref, *, transpose):
    w = w_ref[...].T if transpose else w_ref[...]
    o_ref[...] = pltpu.bitcast(w.astype(BF16), jnp.uint32)


def _pack_experts(w, transpose):
    tiles = N_EXPERTS // PEER_TE
    if transpose:
        out_spec = pl.BlockSpec((None, D_MODEL // 2, PEER_TE), lambda i: (i, 0, 0))
        out_shape = jax.ShapeDtypeStruct((tiles, D_MODEL // 2, PEER_TE), jnp.uint32)
    else:
        out_spec = pl.BlockSpec((PEER_TE // 2, D_MODEL), lambda i: (i, 0))
        out_shape = jax.ShapeDtypeStruct((N_EXPERTS // 2, D_MODEL), jnp.uint32)
    return pl.pallas_call(
        functools.partial(_pack_kernel, transpose=transpose),
        grid=(tiles,),
        in_specs=[pl.BlockSpec((PEER_TE, D_MODEL), lambda i: (i, 0))],
        out_specs=out_spec,
        out_shape=out_shape,
        compiler_params=_params("parallel"),
        name="pack_up" if transpose else "pack_down",
    )(w)


def _peer(xnt, down, up_t, r2, e2, c1, e1, h1, norm_w):
    t = h1.shape[0]
    tm = min(PEER_TM, t)
    te = PEER_TE
    key2_spec = pl.BlockSpec((PEER_HEADS, N_KEYS, tm), lambda i, j: (0, 0, i))
    key1_spec = pl.BlockSpec((PEER_HEADS, te // N_KEYS, tm), lambda i, j: (0, j, i))
    return pl.pallas_call(
        _peer_kernel,
        grid=(t // tm, N_EXPERTS // te),
        in_specs=[
            pl.BlockSpec((D_MODEL // 2, tm), lambda i, j: (0, i)),
            pl.BlockSpec((te // 2, D_MODEL), lambda i, j: (j, 0)),
            pl.BlockSpec((None, D_MODEL // 2, te), lambda i, j: (j, 0, 0)),
            key2_spec, key2_spec, key1_spec, key1_spec,
            pl.BlockSpec((tm, D_MODEL), lambda i, j: (i, 0)),
            pl.BlockSpec((1, D_MODEL), lambda i, j: (0, 0)),
        ],
        out_specs=pl.BlockSpec((tm, D_MODEL), lambda i, j: (i, 0)),
        out_shape=jax.ShapeDtypeStruct((t, D_MODEL), F32),
        scratch_shapes=[
            pltpu.VMEM((D_MODEL, tm), F32),
            pltpu.VMEM((2, PEER_SUB, tm), F32),
            pltpu.VMEM((2, PEER_SUB, tm), BF16),
        ],
        compiler_params=_params("parallel", "arbitrary"),
        name="peer",
    )(xnt, down, up_t, r2, e2, c1, e1, h1, norm_w)


def _pad_lanes(v, width):
    return jnp.pad(v.astype(F32), (0, width - v.shape[0])).reshape(1, width)


def _layer(x, norm_mix, w_in, conv_w, conv_b, dt_bias, a_log, d_skip, ssd_norm_w, attn_sinks,
           w_out, rel_bias, norm_ffn, w_query, sub_keys1, sub_keys2, expert_down, expert_up,
           out_norm):
    b, s, d = x.shape
    t = b * s
    x2 = x.reshape(t, d)

    w_in_pad = jnp.pad(w_in, ((0, 0), (0, IN_COLS_PAD - w_in.shape[1]))).astype(BF16)
    q, k, v, z, xbc, dt = _inproj(x2, norm_mix.reshape(1, d), w_in_pad)

    qi = jnp.arange(ATTN_BLOCK)[:, None]
    kj = jnp.arange(ATTN_BLOCK)[None, :]
    bucket = _t5_bucket(jnp.where(kj <= qi, qi - kj, qi + ATTN_BLOCK - kj)).astype(jnp.int32)
    attn = _attention(q.reshape(b, s, -1), k.reshape(b, s, -1), v.reshape(b, s, -1),
                      bucket, rel_bias.astype(F32), attn_sinks.astype(F32))

    lane = jnp.arange(SSD_INNER)[None, :] // SSD_HEAD_DIM
    expand = (jnp.arange(DT_PAD)[:, None] == lane).astype(BF16)
    tri = (jnp.arange(SSD_CHUNK)[None, :] <= jnp.arange(SSD_CHUNK)[:, None]).astype(BF16)
    ssd = _ssd(xbc.reshape(b, s, -1), z.reshape(b, s, -1), dt.reshape(b, s, -1),
               conv_w, conv_b.reshape(1, -1), _pad_lanes(dt_bias, DT_PAD),
               _pad_lanes(a_log, DT_PAD), jnp.repeat(d_skip.astype(F32), SSD_HEAD_DIM).reshape(1, -1),
               ssd_norm_w.reshape(1, -1), expand, tri)

    h1, xnt, pq = _outproj(x2, attn.reshape(t, -1), ssd.reshape(t, -1), w_out.astype(BF16),
                           norm_ffn.reshape(1, d), w_query.astype(BF16))
    r2, e2, c1, e1 = _route(pq, sub_keys1.astype(BF16), sub_keys2.astype(BF16))
    out = _peer(xnt, _pack_experts(expert_down, False), _pack_experts(expert_up, True),
                r2, e2, c1, e1, h1, out_norm.reshape(1, d))
    return out.reshape(b, s, d)


@jax.jit
def kernel(x, norm_mix, w_in, conv_w, conv_b, dt_bias, a_log, d_skip, ssd_norm_w, attn_sinks,
           w_out, rel_bias, norm_ffn, w_query, sub_keys1, sub_keys2, expert_down, expert_up,
           norm_final):
    assert norm_mix.shape[0] == 1, "single-layer block"
    return _layer(x, norm_mix[0], w_in[0], conv_w[0], conv_b[0], dt_bias[0], a_log[0], d_skip[0],
                  ssd_norm_w[0], attn_sinks[0], w_out[0], rel_bias, norm_ffn[0], w_query[0],
                  sub_keys1[0], sub_keys2[0], expert_down[0], expert_up[0], norm_final)
```

```python
import functools
import math

import jax
import jax.numpy as jnp
from jax import lax
from jax.experimental import pallas as pl
from jax.experimental.pallas import tpu as pltpu

F32 = jnp.float32
BF16 = jnp.bfloat16

D_MODEL = 1024
ATTN_HEADS = 8
ATTN_KV_HEADS = 2
ATTN_GROUP = ATTN_HEADS // ATTN_KV_HEADS
HEAD_DIM = 64
ATTN_WIDTH = ATTN_HEADS * HEAD_DIM
ATTN_BLOCK = 128
N_BUCKETS = 32
MAX_DISTANCE = 128
SSD_HEADS = 8
SSD_HEAD_DIM = 64
SSD_INNER = SSD_HEADS * SSD_HEAD_DIM
SSD_STATE = 128
SSD_GROUPS = 2
SSD_HEADS_PER_GROUP = SSD_HEADS // SSD_GROUPS
SSD_CONV = 4
SSD_CHUNK = 128
PEER_HEADS = 8
N_KEYS = 128
N_EXPERTS = N_KEYS * N_KEYS
PEER_HALF = 128
PEER_TOPK = 16
EPS = 1e-6

KV_COLS = ATTN_KV_HEADS * HEAD_DIM
XBC_COLS = SSD_INNER + 2 * SSD_GROUPS * SSD_STATE
OFF_K = ATTN_WIDTH
OFF_V = OFF_K + KV_COLS
OFF_Z = OFF_V + KV_COLS
OFF_XBC = OFF_Z + SSD_INNER
OFF_DT = OFF_XBC + XBC_COLS
LANES = 128
SUBLANES = 8
DT_PAD = LANES
IN_COLS_PAD = OFF_DT + DT_PAD
CONV_HALO = SUBLANES

VMEM_LIMIT = 56 * 1024 * 1024

ATTN_BLOCKS_PER_STEP = 4
SSD_SEQS_PER_STEP = 4
INPROJ_TM = 1024
OUTPROJ_TM = 512
ROUTE_TM = SUBLANES * LANES
PEER_TM = 512
PEER_TE = 2048
PEER_SUB = 512
PEER_LANE_CHUNK = 512
PEER_ROWS = 16


def _dot(a, b):
    return jnp.dot(a, b, preferred_element_type=F32)


def _dot_nt(a, b):
    return lax.dot_general(a, b, (((1,), (1,)), ((), ())), preferred_element_type=F32)


def _dot_tn(a, b):
    return lax.dot_general(a, b, (((0,), (0,)), ((), ())), preferred_element_type=F32)


def _rms(x, w):
    return x * lax.rsqrt(jnp.mean(x * x, axis=-1, keepdims=True) + EPS) * w


def _silu(x):
    return x / (1.0 + jnp.exp(-x))


def _softplus(x):
    return jnp.maximum(x, 0.0) + jnp.log1p(jnp.exp(-jnp.abs(x)))


def _split3(x):
    hi = x.astype(BF16)
    rest = x - hi.astype(F32)
    mid = rest.astype(BF16)
    return hi, mid, (rest - mid.astype(F32)).astype(BF16)


def _params(*sem):
    return pltpu.CompilerParams(dimension_semantics=sem, vmem_limit_bytes=VMEM_LIMIT)


def _inproj_kernel(x_ref, nw_ref, w_ref, q_ref, k_ref, v_ref, z_ref, xbc_ref, dt_ref):
    xb = _rms(x_ref[...], nw_ref[...]).astype(BF16)

    def proj(lo, hi):
        return _dot(xb, w_ref[:, lo:hi])

    q_ref[...] = (proj(0, OFF_K) * (HEAD_DIM ** -0.5)).astype(BF16)
    k_ref[...] = proj(OFF_K, OFF_V).astype(BF16)
    v_ref[...] = proj(OFF_V, OFF_Z).astype(BF16)
    z_ref[...] = proj(OFF_Z, OFF_XBC)
    xbc_ref[...] = proj(OFF_XBC, OFF_DT)
    dt_ref[...] = proj(OFF_DT, IN_COLS_PAD)


def _inproj(x2, norm_w, w_in_pad):
    t = x2.shape[0]
    tm = min(INPROJ_TM, t)
    row = lambda i: (i, 0)
    fixed = lambda i: (0, 0)
    return pl.pallas_call(
        _inproj_kernel,
        grid=(t // tm,),
        in_specs=[
            pl.BlockSpec((tm, D_MODEL), row),
            pl.BlockSpec((1, D_MODEL), fixed),
            pl.BlockSpec((D_MODEL, IN_COLS_PAD), fixed),
        ],
        out_specs=[
            pl.BlockSpec((tm, ATTN_WIDTH), row),
            pl.BlockSpec((tm, KV_COLS), row),
            pl.BlockSpec((tm, KV_COLS), row),
            pl.BlockSpec((tm, SSD_INNER), row),
            pl.BlockSpec((tm, XBC_COLS), row),
            pl.BlockSpec((tm, DT_PAD), row),
        ],
        out_shape=[
            jax.ShapeDtypeStruct((t, ATTN_WIDTH), BF16),
            jax.ShapeDtypeStruct((t, KV_COLS), BF16),
            jax.ShapeDtypeStruct((t, KV_COLS), BF16),
            jax.ShapeDtypeStruct((t, SSD_INNER), F32),
            jax.ShapeDtypeStruct((t, XBC_COLS), F32),
            jax.ShapeDtypeStruct((t, DT_PAD), F32),
        ],
        compiler_params=_params("parallel"),
        name="inproj",
    )(x2, norm_w, w_in_pad)


def _t5_bucket(dist):
    n = jnp.maximum(dist, 0)
    max_exact = N_BUCKETS // 2
    nf = jnp.maximum(n, 1).astype(F32)
    large = max_exact + (jnp.log(nf / max_exact) / math.log(MAX_DISTANCE / max_exact)
                         * (N_BUCKETS - max_exact)).astype(jnp.int32)
    large = jnp.minimum(large, N_BUCKETS - 1)
    return jnp.where(n < max_exact, n, large)


def _attn_kernel(bucket_ref, relb_ref, sink_ref, q_ref, kp_ref, kc_ref, vp_ref, vc_ref,
                 o_ref, bias_ref):
    blk = ATTN_BLOCK
    rows = ATTN_GROUP * blk
    n = pl.program_id(1)

    @pl.when((pl.program_id(0) == 0) & (n == 0))
    def _():
        bucket = bucket_ref[...]
        for h in range(ATTN_HEADS):
            acc = jnp.zeros((blk, blk), F32)
            for b in range(N_BUCKETS):
                acc = jnp.where(bucket == b, relb_ref[b, h], acc)
            hk, g = divmod(h, ATTN_GROUP)
            bias_ref[hk, g * blk:(g + 1) * blk, :] = acc

    qi = lax.broadcasted_iota(jnp.int32, (rows, blk), 0) & (blk - 1)
    kj = lax.broadcasted_iota(jnp.int32, (rows, blk), 1)
    own = kj <= qi
    for i in range(ATTN_BLOCKS_PER_STEP):
        cur = slice(i * blk, (i + 1) * blk)
        kprev, vprev = (kp_ref, vp_ref) if i == 0 else (kc_ref.at[(i - 1) * blk:i * blk],
                                                        vc_ref.at[(i - 1) * blk:i * blk])
        has_prev = (n > 0) if i == 0 else True
        for hk in range(ATTN_KV_HEADS):
            ksl = slice(hk * HEAD_DIM, (hk + 1) * HEAD_DIM)
            kband = jnp.concatenate([kprev[:, ksl], kc_ref[cur, ksl]], axis=0)
            vband = jnp.concatenate([vprev[:, ksl], vc_ref[cur, ksl]], axis=0)
            qg = jnp.concatenate(
                [q_ref[cur, (hk * ATTN_GROUP + g) * HEAD_DIM:(hk * ATTN_GROUP + g + 1) * HEAD_DIM]
                 for g in range(ATTN_GROUP)], axis=0)
            s2 = _dot_nt(qg, kband)
            s = jnp.where(own, s2[:, blk:], s2[:, :blk]) + bias_ref[hk]
            if i == 0:
                s = jnp.where(own | has_prev, s, -jnp.inf)
            sink = jnp.concatenate(
                [jnp.full((blk, blk), sink_ref[hk * ATTN_GROUP + g], F32)
                 for g in range(ATTN_GROUP)], axis=0)
            m = jnp.maximum(jnp.broadcast_to(jnp.max(s, axis=-1, keepdims=True), s.shape), sink)
            pb = jnp.exp(s - m).astype(BF16)
            zero = jnp.zeros_like(pb)
            pfull = jnp.concatenate([jnp.where(own, zero, pb), jnp.where(own, pb, zero)], axis=1)
            denom = _dot(pfull, jnp.ones((2 * blk, blk), BF16)) + jnp.exp(sink - m)
            o = _dot(pfull, vband) / denom[:, :HEAD_DIM]
            for g in range(ATTN_GROUP):
                h = hk * ATTN_GROUP + g
                o_ref[cur, h * HEAD_DIM:(h + 1) * HEAD_DIM] = o[g * blk:(g + 1) * blk].astype(BF16)


def _attention(q, k, v, bucket, rel_bias, sinks):
    b, s, _ = q.shape
    rows = ATTN_BLOCKS_PER_STEP * ATTN_BLOCK
    assert s % rows == 0
    cur = lambda i, j: (i, j, 0)
    prev = lambda i, j: (i, jnp.maximum(j * ATTN_BLOCKS_PER_STEP - 1, 0), 0)
    smem = pl.BlockSpec(memory_space=pltpu.SMEM)
    return pl.pallas_call(
        _attn_kernel,
        grid=(b, s // rows),
        in_specs=[
            pl.BlockSpec((ATTN_BLOCK, ATTN_BLOCK), lambda i, j: (0, 0)),
            smem, smem,
            pl.BlockSpec((None, rows, ATTN_WIDTH), cur),
            pl.BlockSpec((None, ATTN_BLOCK, KV_COLS), prev),
            pl.BlockSpec((None, rows, KV_COLS), cur),
            pl.BlockSpec((None, ATTN_BLOCK, KV_COLS), prev),
            pl.BlockSpec((None, rows, KV_COLS), cur),
        ],
        out_specs=pl.BlockSpec((None, rows, ATTN_WIDTH), cur),
        out_shape=jax.ShapeDtypeStruct((b, s, ATTN_WIDTH), BF16),
        scratch_shapes=[pltpu.VMEM((ATTN_KV_HEADS, ATTN_GROUP * ATTN_BLOCK, ATTN_BLOCK), F32)],
        compiler_params=_params("arbitrary", "arbitrary"),
        name="swa",
    )(bucket, rel_bias, sinks, q, k, k, v, v)


def _ssd_chunk(xbc_ref, z_ref, dt_ref, convw_ref, convb_ref, dtb_ref, alog_ref, dskip_ref,
               nw_ref, expand_ref, tri_ref, o_ref, xpad_ref, state_ref):
    L = SSD_CHUNK
    P = SSD_HEAD_DIM
    xpad_ref[CONV_HALO:CONV_HALO + L, :] = xbc_ref[...]
    conv = convb_ref[...]
    for j in range(SSD_CONV):
        off = CONV_HALO - (SSD_CONV - 1) + j
        conv = conv + convw_ref[j:j + 1, :] * xpad_ref[off:off + L, :]
    xbc = _silu(conv)
    xs = xbc[:, :SSD_INNER]
    bm = xbc[:, SSD_INNER:SSD_INNER + SSD_GROUPS * SSD_STATE].astype(BF16)
    cm = xbc[:, SSD_INNER + SSD_GROUPS * SSD_STATE:].astype(BF16)

    expand = expand_ref[...]
    dt = _softplus(dt_ref[...] + dtb_ref[...])
    a = dt * (-jnp.exp(alog_ref[...]))
    a_cum = sum(_dot(tri_ref[...], term) for term in _split3(a))
    a_last = a_cum[L - 1:L, :]
    stacked = jnp.concatenate([dt, jnp.exp(a_last - a_cum), jnp.exp(a_cum)], axis=0)
    wide = sum(_dot(term, expand) for term in _split3(stacked))
    x_dt = xs * wide[0:L]
    x_end = (x_dt * wide[L:2 * L]).astype(BF16)
    dec_start = wide[2 * L:3 * L]
    x_dt = x_dt.astype(BF16)
    a_cum_t = a_cum.T

    li = lax.broadcasted_iota(jnp.int32, (L, L), 0)
    si = lax.broadcasted_iota(jnp.int32, (L, L), 1)
    causal = si <= li
    ys = []
    for g in range(SSD_GROUPS):
        bg = bm[:, g * SSD_STATE:(g + 1) * SSD_STATE]
        cg = cm[:, g * SSD_STATE:(g + 1) * SSD_STATE]
        cb = _dot_nt(cg, bg)
        for r in range(SSD_HEADS_PER_GROUP):
            h = g * SSD_HEADS_PER_GROUP + r
            sl = slice(h * P, (h + 1) * P)
            seg = a_cum[:, h:h + 1] - a_cum_t[h:h + 1, :]
            lmat = jnp.where(causal, jnp.exp(jnp.where(causal, seg, 0.0)), 0.0)
            h_prev = state_ref[sl, :]
            y_diag = _dot((cb * lmat).astype(BF16), x_dt[:, sl])
            y_off = _dot_nt(cg, h_prev.astype(BF16)) * dec_start[:, sl]
            ys.append(y_diag + y_off)
            chunk_decay = jnp.exp(a_cum_t[h:h + 1, L - 1:L])
            state_ref[sl, :] = h_prev * chunk_decay + _dot_tn(x_end[:, sl], bg)
    y = jnp.concatenate(ys, axis=1) + dskip_ref[...] * xs
    y = y * _silu(z_ref[...])
    o_ref[...] = _rms(y, nw_ref[...]).astype(BF16)


def _ssd_kernel(xbc_ref, z_ref, dt_ref, convw_ref, convb_ref, dtb_ref, alog_ref, dskip_ref,
                nw_ref, expand_ref, tri_ref, o_ref, xpad_ref, state_ref):
    c = pl.program_id(1)

    @pl.when(c == 0)
    def _():
        xpad_ref[:, 0:CONV_HALO, :] = jnp.zeros((SSD_SEQS_PER_STEP, CONV_HALO, XBC_COLS), F32)
        state_ref[...] = jnp.zeros_like(state_ref)

    @pl.when(c > 0)
    def _():
        xpad_ref[:, 0:CONV_HALO, :] = xpad_ref[:, SSD_CHUNK:SSD_CHUNK + CONV_HALO, :]

    for b in range(SSD_SEQS_PER_STEP):
        _ssd_chunk(xbc_ref.at[b], z_ref.at[b], dt_ref.at[b], convw_ref, convb_ref, dtb_ref,
                   alog_ref, dskip_ref, nw_ref, expand_ref, tri_ref, o_ref.at[b],
                   xpad_ref.at[b], state_ref.at[b])


def _ssd(xbc, z, dt, conv_w, conv_b, dt_bias_pad, a_log_pad, d_skip_wide, norm_w, expand, tri):
    b, s, _ = xbc.shape
    nb = SSD_SEQS_PER_STEP
    assert b % nb == 0
    nc = s // SSD_CHUNK
    cur = lambda i, j: (i, j, 0)
    fixed = lambda i, j: (0, 0)
    full = lambda shape: pl.BlockSpec(shape, fixed)
    return pl.pallas_call(
        _ssd_kernel,
        grid=(b // nb, nc),
        in_specs=[
            pl.BlockSpec((nb, SSD_CHUNK, XBC_COLS), cur),
            pl.BlockSpec((nb, SSD_CHUNK, SSD_INNER), cur),
            pl.BlockSpec((nb, SSD_CHUNK, DT_PAD), cur),
            full((SSD_CONV, XBC_COLS)), full((1, XBC_COLS)), full((1, DT_PAD)), full((1, DT_PAD)),
            full((1, SSD_INNER)), full((1, SSD_INNER)), full((DT_PAD, SSD_INNER)),
            full((SSD_CHUNK, SSD_CHUNK)),
        ],
        out_specs=pl.BlockSpec((nb, SSD_CHUNK, SSD_INNER), cur),
        out_shape=jax.ShapeDtypeStruct((b, s, SSD_INNER), BF16),
        scratch_shapes=[
            pltpu.VMEM((nb, CONV_HALO + SSD_CHUNK, XBC_COLS), F32),
            pltpu.VMEM((nb, SSD_INNER, SSD_STATE), F32),
        ],
        compiler_params=_params("arbitrary", "arbitrary"),
        name="ssd",
    )(xbc, z, dt, conv_w, conv_b, dt_bias_pad, a_log_pad, d_skip_wide, norm_w, expand, tri)


def _outproj_kernel(x_ref, attn_ref, ssd_ref, wo_ref, nw_ref, wq_ref, h1_ref, xnt_ref, q_ref):
    mix = (_dot(attn_ref[...], wo_ref[0:ATTN_WIDTH, :])
           + _dot(ssd_ref[...], wo_ref[ATTN_WIDTH:ATTN_WIDTH + SSD_INNER, :]))
    h1 = x_ref[...] + mix
    h1_ref[...] = h1
    xn = _rms(h1, nw_ref[...])
    xnt_ref[...] = pltpu.bitcast(xn.T.astype(BF16), jnp.uint32)
    qf = _dot(xn.astype(BF16), wq_ref[...])
    for j in range(2 * PEER_HEADS):
        q_ref[j] = qf[:, j * PEER_HALF:(j + 1) * PEER_HALF].astype(BF16)


def _outproj(x2, attn, ssd, w_out, norm_w, w_query):
    t = x2.shape[0]
    tm = min(OUTPROJ_TM, t)
    row = lambda i: (i, 0)
    fixed = lambda i: (0, 0)
    nq = 2 * PEER_HEADS
    return pl.pallas_call(
        _outproj_kernel,
        grid=(t // tm,),
        in_specs=[
            pl.BlockSpec((tm, D_MODEL), row),
            pl.BlockSpec((tm, ATTN_WIDTH), row),
            pl.BlockSpec((tm, SSD_INNER), row),
            pl.BlockSpec((ATTN_WIDTH + SSD_INNER, D_MODEL), fixed),
            pl.BlockSpec((1, D_MODEL), fixed),
            pl.BlockSpec((D_MODEL, nq * PEER_HALF), fixed),
        ],
        out_specs=[
            pl.BlockSpec((tm, D_MODEL), row),
            pl.BlockSpec((D_MODEL // 2, tm), lambda i: (0, i)),
            pl.BlockSpec((nq, tm, PEER_HALF), lambda i: (0, i, 0)),
        ],
        out_shape=[
            jax.ShapeDtypeStruct((t, D_MODEL), F32),
            jax.ShapeDtypeStruct((D_MODEL // 2, t), jnp.uint32),
            jax.ShapeDtypeStruct((nq, t, PEER_HALF), BF16),
        ],
        compiler_params=_params("parallel"),
        name="outproj",
    )(x2, attn, ssd, w_out, norm_w, w_query)


def _ce(x, hi, lo):
    a, b = x[hi], x[lo]
    if b is None:
        return
    if a is None:
        x[hi], x[lo] = b, None
        return
    x[hi], x[lo] = jnp.maximum(a, b), jnp.minimum(a, b)


def _bitonic_merge_desc(x):
    n = len(x)
    j = n // 2
    while j >= 1:
        for i in range(n):
            if i & j == 0:
                _ce(x, i, i | j)
        j //= 2


def _sort_desc(x):
    n = len(x)
    k = 2
    while k <= n:
        j = k // 2
        while j >= 1:
            for i in range(n):
                l = i ^ j
                if l > i:
                    if i & k == 0:
                        _ce(x, i, l)
                    else:
                        _ce(x, l, i)
            j //= 2
        k *= 2


def _merge_top(a, b, sort=True):
    n = len(a)
    out = []
    for i in range(n):
        u, v = a[i], b[n - 1 - i]
        out.append(v if u is None else u if v is None else jnp.maximum(u, v))
    if sort:
        _bitonic_merge_desc(out)
    return out


def _key_rows(k):
    return slice(k * SUBLANES, (k + 1) * SUBLANES)


def _top16_sorted(s_ref, lanes):
    groups = []
    for g in range(N_KEYS // PEER_TOPK):
        x = [s_ref[_key_rows(g * PEER_TOPK + i), lanes] for i in range(PEER_TOPK)]
        _sort_desc(x)
        groups.append(x)
    while len(groups) > 1:
        groups = [_merge_top(groups[i], groups[i + 1]) for i in range(0, len(groups), 2)]
    return groups[0]


def _route_select(h, lanes, s1_ref, s2_ref, r2_ref, e2_ref, c1_ref, e1_ref,
                  r2s_ref, e2s_ref, c1s_ref, e1s_ref):
    v1 = _top16_sorted(s1_ref, lanes)
    v2 = _top16_sorted(s2_ref, lanes)

    lists = []
    for a in range(PEER_TOPK):
        n_b = PEER_TOPK // (a + 1)
        lists.append([v1[a] + v2[b] if b < n_b else None for b in range(PEER_TOPK)])
    while len(lists) > 2:
        lists = [_merge_top(lists[i], lists[i + 1]) for i in range(0, len(lists), 2)]
    top = _merge_top(lists[0], lists[1], sort=False)
    tau = top[0]
    for x in top[1:]:
        tau = jnp.minimum(tau, x)

    e1_top = [jnp.exp(v - v1[0]) for v in v1]
    e2_top = [jnp.exp(v - v2[0]) for v in v2]
    zsum = jnp.zeros_like(tau)
    cnt = []
    for a in range(PEER_TOPK):
        n_a = jnp.zeros_like(tau)
        w_a = jnp.zeros_like(tau)
        for b in range(PEER_TOPK // (a + 1)):
            sel = (v1[a] + v2[b]) >= tau
            n_a = n_a + jnp.where(sel, 1.0, 0.0)
            w_a = w_a + jnp.where(sel, e2_top[b], 0.0)
        cnt.append(n_a)
        zsum = zsum + e1_top[a] * w_a
    half_inv_z = 0.5 / zsum

    for k in range(N_KEYS):
        rows = _key_rows(k)
        s1 = s1_ref[rows, lanes]
        s2 = s2_ref[rows, lanes]
        c1 = jnp.zeros_like(tau)
        for a in reversed(range(PEER_TOPK)):
            c1 = jnp.where(s1 >= v1[a], cnt[a], c1)
        r2 = jnp.zeros_like(tau)
        for b in range(PEER_TOPK):
            r2 = jnp.where(v2[b] > s2, float(b + 1), r2)
        c1s_ref[rows, :] = c1
        r2s_ref[rows, :] = r2
        e1s_ref[rows, :] = jnp.exp(s1 - v1[0]) * half_inv_z
        e2s_ref[rows, :] = jnp.exp(s2 - v2[0])

    for j in range(SUBLANES):
        cols = slice(j * LANES, (j + 1) * LANES)
        take = pl.ds(j, N_KEYS, stride=SUBLANES)
        r2_ref[h, :, cols] = r2s_ref[take, :].astype(BF16)
        e2_ref[h, :, cols] = e2s_ref[take, :].astype(BF16)
        c1_ref[h, :, cols] = c1s_ref[take, :]
        e1_ref[h, :, cols] = e1s_ref[take, :]


def _route_kernel(q_ref, k1_ref, k2_ref, r2_ref, e2_ref, c1_ref, e1_ref,
                  s1_ref, s2_ref, r2s_ref, e2s_ref, c1s_ref, e1s_ref):
    def head_pair(i, carry):
        for half, (k_ref, s_ref) in enumerate(((k1_ref, s1_ref), (k2_ref, s2_ref))):
            qcat = jnp.concatenate(
                [jnp.concatenate([q_ref[2 * (2 * i + hh) + half, j * LANES:(j + 1) * LANES, :]
                                  for j in range(SUBLANES)], axis=1)
                 for hh in range(2)], axis=0)
            s_ref[...] = _dot_nt(k_ref[...], qcat)
        for hh in range(2):
            _route_select(2 * i + hh, slice(hh * LANES, (hh + 1) * LANES), s1_ref, s2_ref,
                          r2_ref, e2_ref, c1_ref, e1_ref, r2s_ref, e2s_ref, c1s_ref, e1s_ref)
        return carry

    lax.fori_loop(0, PEER_HEADS // 2, head_pair, 0)


def _expand_keys(keys):
    eye = jnp.eye(SUBLANES, dtype=keys.dtype)
    return jnp.einsum("kd,jc->kjcd", keys, eye).reshape(N_KEYS * SUBLANES, SUBLANES * PEER_HALF)


def _route(q, keys1, keys2):
    nq, t, _ = q.shape
    tm = ROUTE_TM
    assert t % tm == 0
    fixed = lambda i: (0, 0)
    out_spec = pl.BlockSpec((PEER_HEADS, N_KEYS, tm), lambda i: (0, 0, i))
    shape = (PEER_HEADS, N_KEYS, t)
    key_spec = pl.BlockSpec((N_KEYS * SUBLANES, SUBLANES * PEER_HALF), fixed)
    stage = pltpu.VMEM((N_KEYS * SUBLANES, LANES), F32)
    scores = pltpu.VMEM((N_KEYS * SUBLANES, 2 * LANES), F32)
    return pl.pallas_call(
        _route_kernel,
        grid=(t // tm,),
        in_specs=[pl.BlockSpec((nq, tm, PEER_HALF), lambda i: (0, i, 0)), key_spec, key_spec],
        out_specs=[out_spec] * 4,
        out_shape=[jax.ShapeDtypeStruct(shape, BF16)] * 2 + [jax.ShapeDtypeStruct(shape, F32)] * 2,
        scratch_shapes=[scores] * 2 + [stage] * 4,
        compiler_params=_params("parallel"),
        name="route",
    )(q, _expand_keys(keys1), _expand_keys(keys2))


def _peer_activations(pre_ref, act_ref, r2_ref, e2_ref, c1_ref, e1_ref, sub):
    tm = pre_ref.shape[1]
    chunks = N_KEYS // PEER_ROWS
    shape = (PEER_ROWS, PEER_LANE_CHUNK)
    zero = jnp.zeros(shape, BF16)
    for a in range(PEER_SUB // N_KEYS):
        key1 = sub * (PEER_SUB // N_KEYS) + a
        for lc in range(tm // PEER_LANE_CHUNK):
            lanes = slice(lc * PEER_LANE_CHUNK, (lc + 1) * PEER_LANE_CHUNK)
            gates = [zero] * chunks
            for h in range(PEER_HEADS):
                cnt = jnp.broadcast_to(c1_ref[h, key1:key1 + 1, lanes], shape).astype(BF16)
                e1 = jnp.broadcast_to(e1_ref[h, key1:key1 + 1, lanes], shape).astype(BF16)
                for c in range(chunks):
                    rows = slice(c * PEER_ROWS, (c + 1) * PEER_ROWS)
                    sel = r2_ref[h, rows, lanes] < cnt
                    gates[c] = gates[c] + jnp.where(sel, e2_ref[h, rows, lanes], zero) * e1
            for c in range(chunks):
                rows = slice(a * N_KEYS + c * PEER_ROWS, a * N_KEYS + (c + 1) * PEER_ROWS)
                p = pre_ref[rows, lanes]
                gelu2 = p + p * lax.erf(p * (2.0 ** -0.5))
                act_ref[rows, lanes] = gelu2.astype(BF16) * gates[c]


def _peer_kernel(xnt_ref, d_ref, ut_ref, r2_ref, e2_ref, c1_ref, e1_ref, h1_ref, nw_ref,
                 o_ref, acc_ref, pre_ref, act_ref):
    j = pl.program_id(1)
    nsub = ut_ref.shape[1] // PEER_SUB

    @pl.when(j == 0)
    def _():
        acc_ref[...] = jnp.zeros_like(acc_ref)

    def down(sub):
        d_sub = d_ref[sub * (PEER_SUB // 2):(sub + 1) * (PEER_SUB // 2), :]
        pre_ref[sub % 2] = _dot(pltpu.bitcast(d_sub, BF16), pltpu.bitcast(xnt_ref[...], BF16))

    down(0)
    for sub in range(nsub):
        if sub + 1 < nsub:
            down(sub + 1)
        _peer_activations(pre_ref.at[sub % 2], act_ref.at[sub % 2], r2_ref, e2_ref, c1_ref,
                          e1_ref, sub)
        ut_sub = ut_ref[:, sub * PEER_SUB:(sub + 1) * PEER_SUB]
        acc_ref[...] += _dot(pltpu.bitcast(ut_sub, BF16), act_ref[sub % 2])

    @pl.when(j == pl.num_programs(1) - 1)
    def _():
        h2 = h1_ref[...] + acc_ref[...].T
        o_ref[...] = _rms(h2, nw_ref[...])


def _pack_kernel(w_ref, o_ref, *, transpose):
    w = w_ref[...].T if transpose else w_ref[...]
    o_ref[...] = pltpu.bitcast(w.astype(BF16), jnp.uint32)


def _pack_experts(w, transpose):
    tiles = N_EXPERTS // PEER_TE
    if transpose:
        out_spec = pl.BlockSpec((None, D_MODEL // 2, PEER_TE), lambda i: (i, 0, 0))
        out_shape = jax.ShapeDtypeStruct((tiles, D_MODEL // 2, PEER_TE), jnp.uint32)
    else:
        out_spec = pl.BlockSpec((PEER_TE // 2, D_MODEL), lambda i: (i, 0))
        out_shape = jax.ShapeDtypeStruct((N_EXPERTS // 2, D_MODEL), jnp.uint32)
    return pl.pallas_call(
        functools.partial(_pack_kernel, transpose=transpose),
        grid=(tiles,),
        in_specs=[pl.BlockSpec((PEER_TE, D_MODEL), lambda i: (i, 0))],
        out_specs=out_spec,
        out_shape=out_shape,
        compiler_params=_params("parallel"),
        name="pack_up" if transpose else "pack_down",
    )(w)


def _peer(xnt, down, up_t, r2, e2, c1, e1, h1, norm_w):
    t = h1.shape[0]
    tm = min(PEER_TM, t)
    te = PEER_TE
    key2_spec = pl.BlockSpec((PEER_HEADS, N_KEYS, tm), lambda i, j: (0, 0, i))
    key1_spec = pl.BlockSpec((PEER_HEADS, te // N_KEYS, tm), lambda i, j: (0, j, i))
    return pl.pallas_call(
        _peer_kernel,
        grid=(t // tm, N_EXPERTS // te),
        in_specs=[
            pl.BlockSpec((D_MODEL // 2, tm), lambda i, j: (0, i)),
            pl.BlockSpec((te // 2, D_MODEL), lambda i, j: (j, 0)),
            pl.BlockSpec((None, D_MODEL // 2, te), lambda i, j: (j, 0, 0)),
            key2_spec, key2_spec, key1_spec, key1_spec,
            pl.BlockSpec((tm, D_MODEL), lambda i, j: (i, 0)),
            pl.BlockSpec((1, D_MODEL), lambda i, j: (0, 0)),
        ],
        out_specs=pl.BlockSpec((tm, D_MODEL), lambda i, j: (i, 0)),
        out_shape=jax.ShapeDtypeStruct((t, D_MODEL), F32),
        scratch_shapes=[
            pltpu.VMEM((D_MODEL, tm), F32),
            pltpu.VMEM((2, PEER_SUB, tm), F32),
            pltpu.VMEM((2, PEER_SUB, tm), BF16),
        ],
        compiler_params=_params("parallel", "arbitrary"),
        name="peer",
    )(xnt, down, up_t, r2, e2, c1, e1, h1, norm_w)


def _pad_lanes(v, width):
    return jnp.pad(v.astype(F32), (0, width - v.shape[0])).reshape(1, width)


def _layer(x, norm_mix, w_in, conv_w, conv_b, dt_bias, a_log, d_skip, ssd_norm_w, attn_sinks,
           w_out, rel_bias, norm_ffn, w_query, sub_keys1, sub_keys2, expert_down, expert_up,
           out_norm):
    b, s, d = x.shape
    t = b * s
    x2 = x.reshape(t, d)

    w_in_pad = jnp.pad(w_in, ((0, 0), (0, IN_COLS_PAD - w_in.shape[1]))).astype(BF16)
    q, k, v, z, xbc, dt = _inproj(x2, norm_mix.reshape(1, d), w_in_pad)

    qi = jnp.arange(ATTN_BLOCK)[:, None]
    kj = jnp.arange(ATTN_BLOCK)[None, :]
    bucket = _t5_bucket(jnp.where(kj <= qi, qi - kj, qi + ATTN_BLOCK - kj)).astype(jnp.int32)
    attn = _attention(q.reshape(b, s, -1), k.reshape(b, s, -1), v.reshape(b, s, -1),
                      bucket, rel_bias.astype(F32), attn_sinks.astype(F32))

    lane = jnp.arange(SSD_INNER)[None, :] // SSD_HEAD_DIM
    expand = (jnp.arange(DT_PAD)[:, None] == lane).astype(BF16)
    tri = (jnp.arange(SSD_CHUNK)[None, :] <= jnp.arange(SSD_CHUNK)[:, None]).astype(BF16)
    ssd = _ssd(xbc.reshape(b, s, -1), z.reshape(b, s, -1), dt.reshape(b, s, -1),
               conv_w, conv_b.reshape(1, -1), _pad_lanes(dt_bias, DT_PAD),
               _pad_lanes(a_log, DT_PAD), jnp.repeat(d_skip.astype(F32), SSD_HEAD_DIM).reshape(1, -1),
               ssd_norm_w.reshape(1, -1), expand, tri)

    h1, xnt, pq = _outproj(x2, attn.reshape(t, -1), ssd.reshape(t, -1), w_out.astype(BF16),
                           norm_ffn.reshape(1, d), w_query.astype(BF16))
    r2, e2, c1, e1 = _route(pq, sub_keys1.astype(BF16), sub_keys2.astype(BF16))
    out = _peer(xnt, _pack_experts(expert_down, False), _pack_experts(expert_up, True),
                r2, e2, c1, e1, h1, out_norm.reshape(1, d))
    return out.reshape(b, s, d)


@jax.jit
def kernel(x, norm_mix, w_in, conv_w, conv_b, dt_bias, a_log, d_skip, ssd_norm_w, attn_sinks,
           w_out, rel_bias, norm_ffn, w_query, sub_keys1, sub_keys2, expert_down, expert_up,
           norm_final):
    assert norm_mix.shape[0] == 1, "single-layer block"
    return _layer(x, norm_mix[0], w_in[0], conv_w[0], conv_b[0], dt_bias[0], a_log[0], d_skip[0],
                  ssd_norm_w[0], attn_sinks[0], w_out[0], rel_bias, norm_ffn[0], w_query[0],
                  sub_keys1[0], sub_keys2[0], expert_down[0], expert_up[0], norm_final)
```

```python
import functools
import math

import jax
import jax.numpy as jnp
from jax import lax
from jax.experimental import pallas as pl
from jax.experimental.pallas import tpu as pltpu

F32 = jnp.float32
BF16 = jnp.bfloat16

D_MODEL = 1024
ATTN_HEADS = 8
ATTN_KV_HEADS = 2
ATTN_GROUP = ATTN_HEADS // ATTN_KV_HEADS
HEAD_DIM = 64
ATTN_WIDTH = ATTN_HEADS * HEAD_DIM
ATTN_BLOCK = 128
N_BUCKETS = 32
MAX_DISTANCE = 128
SSD_HEADS = 8
SSD_HEAD_DIM = 64
SSD_INNER = SSD_HEADS * SSD_HEAD_DIM
SSD_STATE = 128
SSD_GROUPS = 2
SSD_HEADS_PER_GROUP = SSD_HEADS // SSD_GROUPS
SSD_CONV = 4
SSD_CHUNK = 128
PEER_HEADS = 8
N_KEYS = 128
N_EXPERTS = N_KEYS * N_KEYS
PEER_HALF = 128
PEER_TOPK = 16
EPS = 1e-6

KV_COLS = ATTN_KV_HEADS * HEAD_DIM
XBC_COLS = SSD_INNER + 2 * SSD_GROUPS * SSD_STATE
OFF_K = ATTN_WIDTH
OFF_V = OFF_K + KV_COLS
OFF_Z = OFF_V + KV_COLS
OFF_XBC = OFF_Z + SSD_INNER
OFF_DT = OFF_XBC + XBC_COLS
LANES = 128
SUBLANES = 8
DT_PAD = LANES
IN_COLS_PAD = OFF_DT + DT_PAD
CONV_HALO = SUBLANES

VMEM_LIMIT = 56 * 1024 * 1024

ATTN_BLOCKS_PER_STEP = 4
SSD_SEQS_PER_STEP = 2
INPROJ_TM = 1024
OUTPROJ_TM = 512
ROUTE_TM = SUBLANES * LANES
PEER_TM = 512
PEER_TE = 2048
PEER_SUB = 512
PEER_LANE_CHUNK = 256
PEER_ROWS = 16


def _dot(a, b):
    return jnp.dot(a, b, preferred_element_type=F32)


def _dot_nt(a, b):
    return lax.dot_general(a, b, (((1,), (1,)), ((), ())), preferred_element_type=F32)


def _dot_tn(a, b):
    return lax.dot_general(a, b, (((0,), (0,)), ((), ())), preferred_element_type=F32)


def _rms(x, w):
    return x * lax.rsqrt(jnp.mean(x * x, axis=-1, keepdims=True) + EPS) * w


def _silu(x):
    return x / (1.0 + jnp.exp(-x))


def _softplus(x):
    return jnp.maximum(x, 0.0) + jnp.log1p(jnp.exp(-jnp.abs(x)))


def _split3(x):
    hi = x.astype(BF16)
    rest = x - hi.astype(F32)
    mid = rest.astype(BF16)
    return hi, mid, (rest - mid.astype(F32)).astype(BF16)


def _params(*sem):
    return pltpu.CompilerParams(dimension_semantics=sem, vmem_limit_bytes=VMEM_LIMIT)


def _inproj_kernel(x_ref, nw_ref, w_ref, q_ref, k_ref, v_ref, z_ref, xbc_ref, dt_ref):
    xb = _rms(x_ref[...], nw_ref[...]).astype(BF16)

    def proj(lo, hi):
        return _dot(xb, w_ref[:, lo:hi])

    q_ref[...] = (proj(0, OFF_K) * (HEAD_DIM ** -0.5)).astype(BF16)
    k_ref[...] = proj(OFF_K, OFF_V).astype(BF16)
    v_ref[...] = proj(OFF_V, OFF_Z).astype(BF16)
    z_ref[...] = proj(OFF_Z, OFF_XBC)
    xbc_ref[...] = proj(OFF_XBC, OFF_DT)
    dt_ref[...] = proj(OFF_DT, IN_COLS_PAD)


def _inproj(x2, norm_w, w_in_pad):
    t = x2.shape[0]
    tm = min(INPROJ_TM, t)
    row = lambda i: (i, 0)
    fixed = lambda i: (0, 0)
    return pl.pallas_call(
        _inproj_kernel,
        grid=(t // tm,),
        in_specs=[
            pl.BlockSpec((tm, D_MODEL), row),
            pl.BlockSpec((1, D_MODEL), fixed),
            pl.BlockSpec((D_MODEL, IN_COLS_PAD), fixed),
        ],
        out_specs=[
            pl.BlockSpec((tm, ATTN_WIDTH), row),
            pl.BlockSpec((tm, KV_COLS), row),
            pl.BlockSpec((tm, KV_COLS), row),
            pl.BlockSpec((tm, SSD_INNER), row),
            pl.BlockSpec((tm, XBC_COLS), row),
            pl.BlockSpec((tm, DT_PAD), row),
        ],
        out_shape=[
            jax.ShapeDtypeStruct((t, ATTN_WIDTH), BF16),
            jax.ShapeDtypeStruct((t, KV_COLS), BF16),
            jax.ShapeDtypeStruct((t, KV_COLS), BF16),
            jax.ShapeDtypeStruct((t, SSD_INNER), F32),
            jax.ShapeDtypeStruct((t, XBC_COLS), F32),
            jax.ShapeDtypeStruct((t, DT_PAD), F32),
        ],
        compiler_params=_params("parallel"),
        name="inproj",
    )(x2, norm_w, w_in_pad)


def _t5_bucket(dist):
    n = jnp.maximum(dist, 0)
    max_exact = N_BUCKETS // 2
    nf = jnp.maximum(n, 1).astype(F32)
    large = max_exact + (jnp.log(nf / max_exact) / math.log(MAX_DISTANCE / max_exact)
                         * (N_BUCKETS - max_exact)).astype(jnp.int32)
    large = jnp.minimum(large, N_BUCKETS - 1)
    return jnp.where(n < max_exact, n, large)


def _attn_kernel(bucket_ref, relb_ref, sink_ref, q_ref, kp_ref, kc_ref, vp_ref, vc_ref,
                 o_ref, bias_ref):
    blk = ATTN_BLOCK
    rows = ATTN_GROUP * blk
    n = pl.program_id(1)

    @pl.when((pl.program_id(0) == 0) & (n == 0))
    def _():
        bucket = bucket_ref[...]
        for h in range(ATTN_HEADS):
            acc = jnp.zeros((blk, blk), F32)
            for b in range(N_BUCKETS):
                acc = jnp.where(bucket == b, relb_ref[b, h], acc)
            hk, g = divmod(h, ATTN_GROUP)
            bias_ref[hk, g * blk:(g + 1) * blk, :] = acc

    qi = lax.broadcasted_iota(jnp.int32, (rows, blk), 0) & (blk - 1)
    kj = lax.broadcasted_iota(jnp.int32, (rows, blk), 1)
    own = kj <= qi
    for i in range(ATTN_BLOCKS_PER_STEP):
        cur = slice(i * blk, (i + 1) * blk)
        kprev, vprev = (kp_ref, vp_ref) if i == 0 else (kc_ref.at[(i - 1) * blk:i * blk],
                                                        vc_ref.at[(i - 1) * blk:i * blk])
        has_prev = (n > 0) if i == 0 else True
        for hk in range(ATTN_KV_HEADS):
            ksl = slice(hk * HEAD_DIM, (hk + 1) * HEAD_DIM)
            kband = jnp.concatenate([kprev[:, ksl], kc_ref[cur, ksl]], axis=0)
            vband = jnp.concatenate([vprev[:, ksl], vc_ref[cur, ksl]], axis=0)
            qg = jnp.concatenate(
                [q_ref[cur, (hk * ATTN_GROUP + g) * HEAD_DIM:(hk * ATTN_GROUP + g + 1) * HEAD_DIM]
                 for g in range(ATTN_GROUP)], axis=0)
            s2 = _dot_nt(qg, kband)
            s = jnp.where(own, s2[:, blk:], s2[:, :blk]) + bias_ref[hk]
            if i == 0:
                s = jnp.where(own | has_prev, s, -jnp.inf)
            sink = jnp.concatenate(
                [jnp.full((blk, blk), sink_ref[hk * ATTN_GROUP + g], F32)
                 for g in range(ATTN_GROUP)], axis=0)
            m = jnp.maximum(jnp.broadcast_to(jnp.max(s, axis=-1, keepdims=True), s.shape), sink)
            pb = jnp.exp(s - m).astype(BF16)
            zero = jnp.zeros_like(pb)
            pfull = jnp.concatenate([jnp.where(own, zero, pb), jnp.where(own, pb, zero)], axis=1)
            denom = _dot(pfull, jnp.ones((2 * blk, blk), BF16)) + jnp.exp(sink - m)
            o = _dot(pfull, vband) / denom[:, :HEAD_DIM]
            for g in range(ATTN_GROUP):
                h = hk * ATTN_GROUP + g
                o_ref[cur, h * HEAD_DIM:(h + 1) * HEAD_DIM] = o[g * blk:(g + 1) * blk].astype(BF16)


def _attention(q, k, v, bucket, rel_bias, sinks):
    b, s, _ = q.shape
    rows = ATTN_BLOCKS_PER_STEP * ATTN_BLOCK
    assert s % rows == 0
    cur = lambda i, j: (i, j, 0)
    prev = lambda i, j: (i, jnp.maximum(j * ATTN_BLOCKS_PER_STEP - 1, 0), 0)
    smem = pl.BlockSpec(memory_space=pltpu.SMEM)
    return pl.pallas_call(
        _attn_kernel,
        grid=(b, s // rows),
        in_specs=[
            pl.BlockSpec((ATTN_BLOCK, ATTN_BLOCK), lambda i, j: (0, 0)),
            smem, smem,
            pl.BlockSpec((None, rows, ATTN_WIDTH), cur),
            pl.BlockSpec((None, ATTN_BLOCK, KV_COLS), prev),
            pl.BlockSpec((None, rows, KV_COLS), cur),
            pl.BlockSpec((None, ATTN_BLOCK, KV_COLS), prev),
            pl.BlockSpec((None, rows, KV_COLS), cur),
        ],
        out_specs=pl.BlockSpec((None, rows, ATTN_WIDTH), cur),
        out_shape=jax.ShapeDtypeStruct((b, s, ATTN_WIDTH), BF16),
        scratch_shapes=[pltpu.VMEM((ATTN_KV_HEADS, ATTN_GROUP * ATTN_BLOCK, ATTN_BLOCK), F32)],
        compiler_params=_params("arbitrary", "arbitrary"),
        name="swa",
    )(bucket, rel_bias, sinks, q, k, k, v, v)


def _ssd_chunk(xbc_ref, z_ref, dt_ref, convw_ref, convb_ref, dtb_ref, alog_ref, dskip_ref,
               nw_ref, expand_ref, tri_ref, o_ref, xpad_ref, state_ref):
    L = SSD_CHUNK
    P = SSD_HEAD_DIM
    xpad_ref[CONV_HALO:CONV_HALO + L, :] = xbc_ref[...]
    conv = convb_ref[...]
    for j in range(SSD_CONV):
        off = CONV_HALO - (SSD_CONV - 1) + j
        conv = conv + convw_ref[j:j + 1, :] * xpad_ref[off:off + L, :]
    xbc = _silu(conv)
    xs = xbc[:, :SSD_INNER]
    bm = xbc[:, SSD_INNER:SSD_INNER + SSD_GROUPS * SSD_STATE].astype(BF16)
    cm = xbc[:, SSD_INNER + SSD_GROUPS * SSD_STATE:].astype(BF16)

    expand = expand_ref[...]
    dt = _softplus(dt_ref[...] + dtb_ref[...])
    a = dt * (-jnp.exp(alog_ref[...]))
    a_cum = sum(_dot(tri_ref[...], term) for term in _split3(a))
    a_last = a_cum[L - 1:L, :]
    stacked = jnp.concatenate([dt, jnp.exp(a_last - a_cum), jnp.exp(a_cum)], axis=0)
    wide = sum(_dot(term, expand) for term in _split3(stacked))
    x_dt = xs * wide[0:L]
    x_end = (x_dt * wide[L:2 * L]).astype(BF16)
    dec_start = wide[2 * L:3 * L]
    x_dt = x_dt.astype(BF16)
    a_cum_t = a_cum.T

    li = lax.broadcasted_iota(jnp.int32, (L, L), 0)
    si = lax.broadcasted_iota(jnp.int32, (L, L), 1)
    causal = si <= li
    ys = []
    for g in range(SSD_GROUPS):
        bg = bm[:, g * SSD_STATE:(g + 1) * SSD_STATE]
        cg = cm[:, g * SSD_STATE:(g + 1) * SSD_STATE]
        cb = _dot_nt(cg, bg)
        for r in range(SSD_HEADS_PER_GROUP):
            h = g * SSD_HEADS_PER_GROUP + r
            sl = slice(h * P, (h + 1) * P)
            seg = a_cum[:, h:h + 1] - a_cum_t[h:h + 1, :]
            lmat = jnp.where(causal, jnp.exp(jnp.where(causal, seg, 0.0)), 0.0)
            h_prev = state_ref[sl, :]
            y_diag = _dot((cb * lmat).astype(BF16), x_dt[:, sl])
            y_off = _dot_nt(cg, h_prev.astype(BF16)) * dec_start[:, sl]
            ys.append(y_diag + y_off)
            chunk_decay = jnp.exp(a_cum_t[h:h + 1, L - 1:L])
            state_ref[sl, :] = h_prev * chunk_decay + _dot_tn(x_end[:, sl], bg)
    y = jnp.concatenate(ys, axis=1) + dskip_ref[...] * xs
    y = y * _silu(z_ref[...])
    o_ref[...] = _rms(y, nw_ref[...]).astype(BF16)


def _ssd_kernel(xbc_ref, z_ref, dt_ref, convw_ref, convb_ref, dtb_ref, alog_ref, dskip_ref,
                nw_ref, expand_ref, tri_ref, o_ref, xpad_ref, state_ref):
    c = pl.program_id(1)

    @pl.when(c == 0)
    def _():
        xpad_ref[:, 0:CONV_HALO, :] = jnp.zeros((SSD_SEQS_PER_STEP, CONV_HALO, XBC_COLS), F32)
        state_ref[...] = jnp.zeros_like(state_ref)

    @pl.when(c > 0)
    def _():
        xpad_ref[:, 0:CONV_HALO, :] = xpad_ref[:, SSD_CHUNK:SSD_CHUNK + CONV_HALO, :]

    for b in range(SSD_SEQS_PER_STEP):
        _ssd_chunk(xbc_ref.at[b], z_ref.at[b], dt_ref.at[b], convw_ref, convb_ref, dtb_ref,
                   alog_ref, dskip_ref, nw_ref, expand_ref, tri_ref, o_ref.at[b],
                   xpad_ref.at[b], state_ref.at[b])


def _ssd(xbc, z, dt, conv_w, conv_b, dt_bias_pad, a_log_pad, d_skip_wide, norm_w, expand, tri):
    b, s, _ = xbc.shape
    nb = SSD_SEQS_PER_STEP
    assert b % nb == 0
    nc = s // SSD_CHUNK
    cur = lambda i, j: (i, j, 0)
    fixed = lambda i, j: (0, 0)
    full = lambda shape: pl.BlockSpec(shape, fixed)
    return pl.pallas_call(
        _ssd_kernel,
        grid=(b // nb, nc),
        in_specs=[
            pl.BlockSpec((nb, SSD_CHUNK, XBC_COLS), cur),
            pl.BlockSpec((nb, SSD_CHUNK, SSD_INNER), cur),
            pl.BlockSpec((nb, SSD_CHUNK, DT_PAD), cur),
            full((SSD_CONV, XBC_COLS)), full((1, XBC_COLS)), full((1, DT_PAD)), full((1, DT_PAD)),
            full((1, SSD_INNER)), full((1, SSD_INNER)), full((DT_PAD, SSD_INNER)),
            full((SSD_CHUNK, SSD_CHUNK)),
        ],
        out_specs=pl.BlockSpec((nb, SSD_CHUNK, SSD_INNER), cur),
        out_shape=jax.ShapeDtypeStruct((b, s, SSD_INNER), BF16),
        scratch_shapes=[
            pltpu.VMEM((nb, CONV_HALO + SSD_CHUNK, XBC_COLS), F32),
            pltpu.VMEM((nb, SSD_INNER, SSD_STATE), F32),
        ],
        compiler_params=_params("arbitrary", "arbitrary"),
        name="ssd",
    )(xbc, z, dt, conv_w, conv_b, dt_bias_pad, a_log_pad, d_skip_wide, norm_w, expand, tri)


def _outproj_kernel(x_ref, attn_ref, ssd_ref, wo_ref, nw_ref, wq_ref, h1_ref, xnt_ref, q_ref):
    mix = (_dot(attn_ref[...], wo_ref[0:ATTN_WIDTH, :])
           + _dot(ssd_ref[...], wo_ref[ATTN_WIDTH:ATTN_WIDTH + SSD_INNER, :]))
    h1 = x_ref[...] + mix
    h1_ref[...] = h1
    xn = _rms(h1, nw_ref[...])
    xnt_ref[...] = pltpu.bitcast(xn.T.astype(BF16), jnp.uint32)
    qf = _dot(xn.astype(BF16), wq_ref[...])
    for j in range(2 * PEER_HEADS):
        q_ref[j] = qf[:, j * PEER_HALF:(j + 1) * PEER_HALF].astype(BF16)


def _outproj(x2, attn, ssd, w_out, norm_w, w_query):
    t = x2.shape[0]
    tm = min(OUTPROJ_TM, t)
    row = lambda i: (i, 0)
    fixed = lambda i: (0, 0)
    nq = 2 * PEER_HEADS
    return pl.pallas_call(
        _outproj_kernel,
        grid=(t // tm,),
        in_specs=[
            pl.BlockSpec((tm, D_MODEL), row),
            pl.BlockSpec((tm, ATTN_WIDTH), row),
            pl.BlockSpec((tm, SSD_INNER), row),
            pl.BlockSpec((ATTN_WIDTH + SSD_INNER, D_MODEL), fixed),
            pl.BlockSpec((1, D_MODEL), fixed),
            pl.BlockSpec((D_MODEL, nq * PEER_HALF), fixed),
        ],
        out_specs=[
            pl.BlockSpec((tm, D_MODEL), row),
            pl.BlockSpec((D_MODEL // 2, tm), lambda i: (0, i)),
            pl.BlockSpec((nq, tm, PEER_HALF), lambda i: (0, i, 0)),
        ],
        out_shape=[
            jax.ShapeDtypeStruct((t, D_MODEL), F32),
            jax.ShapeDtypeStruct((D_MODEL // 2, t), jnp.uint32),
            jax.ShapeDtypeStruct((nq, t, PEER_HALF), BF16),
        ],
        compiler_params=_params("parallel"),
        name="outproj",
    )(x2, attn, ssd, w_out, norm_w, w_query)


def _ce(x, hi, lo):
    a, b = x[hi], x[lo]
    if b is None:
        return
    if a is None:
        x[hi], x[lo] = b, None
        return
    x[hi], x[lo] = jnp.maximum(a, b), jnp.minimum(a, b)


def _bitonic_merge_desc(x):
    n = len(x)
    j = n // 2
    while j >= 1:
        for i in range(n):
            if i & j == 0:
                _ce(x, i, i | j)
        j //= 2


def _sort_desc(x):
    n = len(x)
    k = 2
    while k <= n:
        j = k // 2
        while j >= 1:
            for i in range(n):
                l = i ^ j
                if l > i:
                    if i & k == 0:
                        _ce(x, i, l)
                    else:
                        _ce(x, l, i)
            j //= 2
        k *= 2


def _merge_top(a, b, sort=True):
    n = len(a)
    out = []
    for i in range(n):
        u, v = a[i], b[n - 1 - i]
        out.append(v if u is None else u if v is None else jnp.maximum(u, v))
    if sort:
        _bitonic_merge_desc(out)
    return out


def _key_rows(k):
    return slice(k * SUBLANES, (k + 1) * SUBLANES)


def _top16_sorted(s_ref, lanes):
    groups = []
    for g in range(N_KEYS // PEER_TOPK):
        x = [s_ref[_key_rows(g * PEER_TOPK + i), lanes] for i in range(PEER_TOPK)]
        _sort_desc(x)
        groups.append(x)
    while len(groups) > 1:
        groups = [_merge_top(groups[i], groups[i + 1]) for i in range(0, len(groups), 2)]
    return groups[0]


def _route_select(h, lanes, s1_ref, s2_ref, r2_ref, e2_ref, c1_ref, e1_ref,
                  r2s_ref, e2s_ref, c1s_ref, e1s_ref):
    v1 = _top16_sorted(s1_ref, lanes)
    v2 = _top16_sorted(s2_ref, lanes)

    lists = []
    for a in range(PEER_TOPK):
        n_b = PEER_TOPK // (a + 1)
        lists.append([v1[a] + v2[b] if b < n_b else None for b in range(PEER_TOPK)])
    while len(lists) > 2:
        lists = [_merge_top(lists[i], lists[i + 1]) for i in range(0, len(lists), 2)]
    top = _merge_top(lists[0], lists[1], sort=False)
    tau = top[0]
    for x in top[1:]:
        tau = jnp.minimum(tau, x)

    e1_top = [jnp.exp(v - v1[0]) for v in v1]
    e2_top = [jnp.exp(v - v2[0]) for v in v2]
    zsum = jnp.zeros_like(tau)
    cnt = []
    for a in range(PEER_TOPK):
        n_a = jnp.zeros_like(tau)
        w_a = jnp.zeros_like(tau)
        for b in range(PEER_TOPK // (a + 1)):
            sel = (v1[a] + v2[b]) >= tau
            n_a = n_a + jnp.where(sel, 1.0, 0.0)
            w_a = w_a + jnp.where(sel, e2_top[b], 0.0)
        cnt.append(n_a)
        zsum = zsum + e1_top[a] * w_a
    half_inv_z = 0.5 / zsum

    for k in range(N_KEYS):
        rows = _key_rows(k)
        s1 = s1_ref[rows, lanes]
        s2 = s2_ref[rows, lanes]
        c1 = jnp.zeros_like(tau)
        for a in reversed(range(PEER_TOPK)):
            c1 = jnp.where(s1 >= v1[a], cnt[a], c1)
        r2 = jnp.zeros_like(tau)
        for b in range(PEER_TOPK):
            r2 = jnp.where(v2[b] > s2, float(b + 1), r2)
        c1s_ref[rows, :] = c1
        r2s_ref[rows, :] = r2
        e1s_ref[rows, :] = jnp.exp(s1 - v1[0]) * half_inv_z
        e2s_ref[rows, :] = jnp.exp(s2 - v2[0])

    for j in range(SUBLANES):
        cols = slice(j * LANES, (j + 1) * LANES)
        take = pl.ds(j, N_KEYS, stride=SUBLANES)
        r2_ref[h, :, cols] = r2s_ref[take, :].astype(BF16)
        e2_ref[h, :, cols] = e2s_ref[take, :].astype(BF16)
        c1_ref[h, :, cols] = c1s_ref[take, :]
        e1_ref[h, :, cols] = e1s_ref[take, :]


def _route_kernel(q_ref, k1_ref, k2_ref, r2_ref, e2_ref, c1_ref, e1_ref,
                  s1_ref, s2_ref, r2s_ref, e2s_ref, c1s_ref, e1s_ref):
    def head_pair(i, carry):
        for half, (k_ref, s_ref) in enumerate(((k1_ref, s1_ref), (k2_ref, s2_ref))):
            qcat = jnp.concatenate(
                [jnp.concatenate([q_ref[2 * (2 * i + hh) + half, j * LANES:(j + 1) * LANES, :]
                                  for j in range(SUBLANES)], axis=1)
                 for hh in range(2)], axis=0)
            s_ref[...] = _dot_nt(k_ref[...], qcat)
        for hh in range(2):
            _route_select(2 * i + hh, slice(hh * LANES, (hh + 1) * LANES), s1_ref, s2_ref,
                          r2_ref, e2_ref, c1_ref, e1_ref, r2s_ref, e2s_ref, c1s_ref, e1s_ref)
        return carry

    lax.fori_loop(0, PEER_HEADS // 2, head_pair, 0)


def _expand_keys(keys):
    eye = jnp.eye(SUBLANES, dtype=keys.dtype)
    return jnp.einsum("kd,jc->kjcd", keys, eye).reshape(N_KEYS * SUBLANES, SUBLANES * PEER_HALF)


def _route(q, keys1, keys2):
    nq, t, _ = q.shape
    tm = ROUTE_TM
    assert t % tm == 0
    fixed = lambda i: (0, 0)
    out_spec = pl.BlockSpec((PEER_HEADS, N_KEYS, tm), lambda i: (0, 0, i))
    shape = (PEER_HEADS, N_KEYS, t)
    key_spec = pl.BlockSpec((N_KEYS * SUBLANES, SUBLANES * PEER_HALF), fixed)
    stage = pltpu.VMEM((N_KEYS * SUBLANES, LANES), F32)
    scores = pltpu.VMEM((N_KEYS * SUBLANES, 2 * LANES), F32)
    return pl.pallas_call(
        _route_kernel,
        grid=(t // tm,),
        in_specs=[pl.BlockSpec((nq, tm, PEER_HALF), lambda i: (0, i, 0)), key_spec, key_spec],
        out_specs=[out_spec] * 4,
        out_shape=[jax.ShapeDtypeStruct(shape, BF16)] * 2 + [jax.ShapeDtypeStruct(shape, F32)] * 2,
        scratch_shapes=[scores] * 2 + [stage] * 4,
        compiler_params=_params("parallel"),
        name="route",
    )(q, _expand_keys(keys1), _expand_keys(keys2))


def _peer_activations(pre_ref, act_ref, r2_ref, e2_ref, c1_ref, e1_ref, sub):
    tm = pre_ref.shape[1]
    chunks = N_KEYS // PEER_ROWS
    shape = (PEER_ROWS, PEER_LANE_CHUNK)
    zero = jnp.zeros(shape, BF16)
    for a in range(PEER_SUB // N_KEYS):
        key1 = sub * (PEER_SUB // N_KEYS) + a
        for lc in range(tm // PEER_LANE_CHUNK):
            lanes = slice(lc * PEER_LANE_CHUNK, (lc + 1) * PEER_LANE_CHUNK)
            gates = [zero] * chunks
            for h in range(PEER_HEADS):
                cnt = jnp.broadcast_to(c1_ref[h, key1:key1 + 1, lanes], shape).astype(BF16)
                e1 = jnp.broadcast_to(e1_ref[h, key1:key1 + 1, lanes], shape).astype(BF16)
                for c in range(chunks):
                    rows = slice(c * PEER_ROWS, (c + 1) * PEER_ROWS)
                    sel = r2_ref[h, rows, lanes] < cnt
                    gates[c] = gates[c] + jnp.where(sel, e2_ref[h, rows, lanes], zero) * e1
            for c in range(chunks):
                rows = slice(a * N_KEYS + c * PEER_ROWS, a * N_KEYS + (c + 1) * PEER_ROWS)
                p = pre_ref[rows, lanes]
                gelu2 = p + p * lax.erf(p * (2.0 ** -0.5))
                act_ref[rows, lanes] = gelu2.astype(BF16) * gates[c]


def _peer_kernel(xnt_ref, d_ref, ut_ref, r2_ref, e2_ref, c1_ref, e1_ref, h1_ref, nw_ref,
                 o_ref, acc_ref, pre_ref, act_ref):
    j = pl.program_id(1)
    nsub = ut_ref.shape[1] // PEER_SUB

    @pl.when(j == 0)
    def _():
        acc_ref[...] = jnp.zeros_like(acc_ref)

    def down(sub):
        d_sub = d_ref[sub * (PEER_SUB // 2):(sub + 1) * (PEER_SUB // 2), :]
        pre_ref[sub % 2] = _dot(pltpu.bitcast(d_sub, BF16), pltpu.bitcast(xnt_ref[...], BF16))

    down(0)
    for sub in range(nsub):
        if sub + 1 < nsub:
            down(sub + 1)
        _peer_activations(pre_ref.at[sub % 2], act_ref.at[sub % 2], r2_ref, e2_ref, c1_ref,
                          e1_ref, sub)
        ut_sub = ut_ref[:, sub * PEER_SUB:(sub + 1) * PEER_SUB]
        acc_ref[...] += _dot(pltpu.bitcast(ut_sub, BF16), act_ref[sub % 2])

    @pl.when(j == pl.num_programs(1) - 1)
    def _():
        h2 = h1_ref[...] + acc_ref[...].T
        o_ref[...] = _rms(h2, nw_ref[...])


def _pack_kernel(w_ref, o_ref, *, transpose):
    w = w_ref[...].T if transpose else w_ref[...]
    o_ref[...] = pltpu.bitcast(w.astype(BF16), jnp.uint32)


def _pack_experts(w, transpose):
    tiles = N_EXPERTS // PEER_TE
    if transpose:
        out_spec = pl.BlockSpec((None, D_MODEL // 2, PEER_TE), lambda i: (i, 0, 0))
        out_shape = jax.ShapeDtypeStruct((tiles, D_MODEL // 2, PEER_TE), jnp.uint32)
    else:
        out_spec = pl.BlockSpec((PEER_TE // 2, D_MODEL), lambda i: (i, 0))
        out_shape = jax.ShapeDtypeStruct((N_EXPERTS // 2, D_MODEL), jnp.uint32)
    return pl.pallas_call(
        functools.partial(_pack_kernel, transpose=transpose),
        grid=(tiles,),
        in_specs=[pl.BlockSpec((PEER_TE, D_MODEL), lambda i: (i, 0))],
        out_specs=out_spec,
        out_shape=out_shape,
        compiler_params=_params("parallel"),
        name="pack_up" if transpose else "pack_down",
    )(w)


def _peer(xnt, down, up_t, r2, e2, c1, e1, h1, norm_w):
    t = h1.shape[0]
    tm = min(PEER_TM, t)
    te = PEER_TE
    key2_spec = pl.BlockSpec((PEER_HEADS, N_KEYS, tm), lambda i, j: (0, 0, i))
    key1_spec = pl.BlockSpec((PEER_HEADS, te // N_KEYS, tm), lambda i, j: (0, j, i))
    return pl.pallas_call(
        _peer_kernel,
        grid=(t // tm, N_EXPERTS // te),
        in_specs=[
            pl.BlockSpec((D_MODEL // 2, tm), lambda i, j: (0, i)),
            pl.BlockSpec((te // 2, D_MODEL), lambda i, j: (j, 0)),
            pl.BlockSpec((None, D_MODEL // 2, te), lambda i, j: (j, 0, 0)),
            key2_spec, key2_spec, key1_spec, key1_spec,
            pl.BlockSpec((tm, D_MODEL), lambda i, j: (i, 0)),
            pl.BlockSpec((1, D_MODEL), lambda i, j: (0, 0)),
        ],
        out_specs=pl.BlockSpec((tm, D_MODEL), lambda i, j: (i, 0)),
        out_shape=jax.ShapeDtypeStruct((t, D_MODEL), F32),
        scratch_shapes=[
            pltpu.VMEM((D_MODEL, tm), F32),
            pltpu.VMEM((2, PEER_SUB, tm), F32),
            pltpu.VMEM((2, PEER_SUB, tm), BF16),
        ],
        compiler_params=_params("parallel", "arbitrary"),
        name="peer",
    )(xnt, down, up_t, r2, e2, c1, e1, h1, norm_w)


def _pad_lanes(v, width):
    return jnp.pad(v.astype(F32), (0, width - v.shape[0])).reshape(1, width)


def _layer(x, norm_mix, w_in, conv_w, conv_b, dt_bias, a_log, d_skip, ssd_norm_w, attn_sinks,
           w_out, rel_bias, norm_ffn, w_query, sub_keys1, sub_keys2, expert_down, expert_up,
           out_norm):
    b, s, d = x.shape
    t = b * s
    x2 = x.reshape(t, d)

    w_in_pad = jnp.pad(w_in, ((0, 0), (0, IN_COLS_PAD - w_in.shape[1]))).astype(BF16)
    q, k, v, z, xbc, dt = _inproj(x2, norm_mix.reshape(1, d), w_in_pad)

    qi = jnp.arange(ATTN_BLOCK)[:, None]
    kj = jnp.arange(ATTN_BLOCK)[None, :]
    bucket = _t5_bucket(jnp.where(kj <= qi, qi - kj, qi + ATTN_BLOCK - kj)).astype(jnp.int32)
    attn = _attention(q.reshape(b, s, -1), k.reshape(b, s, -1), v.reshape(b, s, -1),
                      bucket, rel_bias.astype(F32), attn_sinks.astype(F32))

    lane = jnp.arange(SSD_INNER)[None, :] // SSD_HEAD_DIM
    expand = (jnp.arange(DT_PAD)[:, None] == lane).astype(BF16)
    tri = (jnp.arange(SSD_CHUNK)[None, :] <= jnp.arange(SSD_CHUNK)[:, None]).astype(BF16)
    ssd = _ssd(xbc.reshape(b, s, -1), z.reshape(b, s, -1), dt.reshape(b, s, -1),
               conv_w, conv_b.reshape(1, -1), _pad_lanes(dt_bias, DT_PAD),
               _pad_lanes(a_log, DT_PAD), jnp.repeat(d_skip.astype(F32), SSD_HEAD_DIM).reshape(1, -1),
               ssd_norm_w.reshape(1, -1), expand, tri)

    h1, xnt, pq = _outproj(x2, attn.reshape(t, -1), ssd.reshape(t, -1), w_out.astype(BF16),
                           norm_ffn.reshape(1, d), w_query.astype(BF16))
    r2, e2, c1, e1 = _route(pq, sub_keys1.astype(BF16), sub_keys2.astype(BF16))
    out = _peer(xnt, _pack_experts(expert_down, False), _pack_experts(expert_up, True),
                r2, e2, c1, e1, h1, out_norm.reshape(1, d))
    return out.reshape(b, s, d)


@jax.jit
def kernel(x, norm_mix, w_in, conv_w, conv_b, dt_bias, a_log, d_skip, ssd_norm_w, attn_sinks,
           w_out, rel_bias, norm_ffn, w_query, sub_keys1, sub_keys2, expert_down, expert_up,
           norm_final):
    assert norm_mix.shape[0] == 1, "single-layer block"
    return _layer(x, norm_mix[0], w_in[0], conv_w[0], conv_b[0], dt_bias[0], a_log[0], d_skip[0],
                  ssd_norm_w[0], attn_sinks[0], w_out[0], rel_bias, norm_ffn[0], w_query[0],
                  sub_keys1[0], sub_keys2[0], expert_down[0], expert_up[0], norm_final)
```

```python
import functools
import math

import jax
import jax.numpy as jnp
from jax import lax
from jax.experimental import pallas as pl
from jax.experimental.pallas import tpu as pltpu

F32 = jnp.float32
BF16 = jnp.bfloat16

D_MODEL = 1024
ATTN_HEADS = 8
ATTN_KV_HEADS = 2
ATTN_GROUP = ATTN_HEADS // ATTN_KV_HEADS
HEAD_DIM = 64
ATTN_WIDTH = ATTN_HEADS * HEAD_DIM
ATTN_BLOCK = 128
N_BUCKETS = 32
MAX_DISTANCE = 128
SSD_HEADS = 8
SSD_HEAD_DIM = 64
SSD_INNER = SSD_HEADS * SSD_HEAD_DIM
SSD_STATE = 128
SSD_GROUPS = 2
SSD_HEADS_PER_GROUP = SSD_HEADS // SSD_GROUPS
SSD_CONV = 4
SSD_CHUNK = 128
PEER_HEADS = 8
N_KEYS = 128
N_EXPERTS = N_KEYS * N_KEYS
PEER_HALF = 128
PEER_TOPK = 16
EPS = 1e-6

KV_COLS = ATTN_KV_HEADS * HEAD_DIM
XBC_COLS = SSD_INNER + 2 * SSD_GROUPS * SSD_STATE
OFF_K = ATTN_WIDTH
OFF_V = OFF_K + KV_COLS
OFF_Z = OFF_V + KV_COLS
OFF_XBC = OFF_Z + SSD_INNER
OFF_DT = OFF_XBC + XBC_COLS
LANES = 128
SUBLANES = 8
DT_PAD = LANES
IN_COLS_PAD = OFF_DT + DT_PAD
CONV_HALO = SUBLANES

VMEM_LIMIT = 56 * 1024 * 1024

ATTN_BLOCKS_PER_STEP = 4
SSD_SEQS_PER_STEP = 2
INPROJ_TM = 1024
OUTPROJ_TM = 512
ROUTE_TM = SUBLANES * LANES
PEER_TM = 512
PEER_TE = 2048
PEER_SUB = 256
PEER_LANE_CHUNK = 512
PEER_ROWS = 16


def _dot(a, b):
    return jnp.dot(a, b, preferred_element_type=F32)


def _dot_nt(a, b):
    return lax.dot_general(a, b, (((1,), (1,)), ((), ())), preferred_element_type=F32)


def _dot_tn(a, b):
    return lax.dot_general(a, b, (((0,), (0,)), ((), ())), preferred_element_type=F32)


def _rms(x, w):
    return x * lax.rsqrt(jnp.mean(x * x, axis=-1, keepdims=True) + EPS) * w


def _silu(x):
    return x / (1.0 + jnp.exp(-x))


def _softplus(x):
    return jnp.maximum(x, 0.0) + jnp.log1p(jnp.exp(-jnp.abs(x)))


def _split3(x):
    hi = x.astype(BF16)
    rest = x - hi.astype(F32)
    mid = rest.astype(BF16)
    return hi, mid, (rest - mid.astype(F32)).astype(BF16)


def _params(*sem):
    return pltpu.CompilerParams(dimension_semantics=sem, vmem_limit_bytes=VMEM_LIMIT)


def _inproj_kernel(x_ref, nw_ref, w_ref, q_ref, k_ref, v_ref, z_ref, xbc_ref, dt_ref):
    xb = _rms(x_ref[...], nw_ref[...]).astype(BF16)

    def proj(lo, hi):
        return _dot(xb, w_ref[:, lo:hi])

    q_ref[...] = (proj(0, OFF_K) * (HEAD_DIM ** -0.5)).astype(BF16)
    k_ref[...] = proj(OFF_K, OFF_V).astype(BF16)
    v_ref[...] = proj(OFF_V, OFF_Z).astype(BF16)
    z_ref[...] = proj(OFF_Z, OFF_XBC)
    xbc_ref[...] = proj(OFF_XBC, OFF_DT)
    dt_ref[...] = proj(OFF_DT, IN_COLS_PAD)


def _inproj(x2, norm_w, w_in_pad):
    t = x2.shape[0]
    tm = min(INPROJ_TM, t)
    row = lambda i: (i, 0)
    fixed = lambda i: (0, 0)
    return pl.pallas_call(
        _inproj_kernel,
        grid=(t // tm,),
        in_specs=[
            pl.BlockSpec((tm, D_MODEL), row),
            pl.BlockSpec((1, D_MODEL), fixed),
            pl.BlockSpec((D_MODEL, IN_COLS_PAD), fixed),
        ],
        out_specs=[
            pl.BlockSpec((tm, ATTN_WIDTH), row),
            pl.BlockSpec((tm, KV_COLS), row),
            pl.BlockSpec((tm, KV_COLS), row),
            pl.BlockSpec((tm, SSD_INNER), row),
            pl.BlockSpec((tm, XBC_COLS), row),
            pl.BlockSpec((tm, DT_PAD), row),
        ],
        out_shape=[
            jax.ShapeDtypeStruct((t, ATTN_WIDTH), BF16),
            jax.ShapeDtypeStruct((t, KV_COLS), BF16),
            jax.ShapeDtypeStruct((t, KV_COLS), BF16),
            jax.ShapeDtypeStruct((t, SSD_INNER), F32),
            jax.ShapeDtypeStruct((t, XBC_COLS), F32),
            jax.ShapeDtypeStruct((t, DT_PAD), F32),
        ],
        compiler_params=_params("parallel"),
        name="inproj",
    )(x2, norm_w, w_in_pad)


def _t5_bucket(dist):
    n = jnp.maximum(dist, 0)
    max_exact = N_BUCKETS // 2
    nf = jnp.maximum(n, 1).astype(F32)
    large = max_exact + (jnp.log(nf / max_exact) / math.log(MAX_DISTANCE / max_exact)
                         * (N_BUCKETS - max_exact)).astype(jnp.int32)
    large = jnp.minimum(large, N_BUCKETS - 1)
    return jnp.where(n < max_exact, n, large)


def _attn_kernel(bucket_ref, relb_ref, sink_ref, q_ref, kp_ref, kc_ref, vp_ref, vc_ref,
                 o_ref, bias_ref):
    blk = ATTN_BLOCK
    rows = ATTN_GROUP * blk
    n = pl.program_id(1)

    @pl.when((pl.program_id(0) == 0) & (n == 0))
    def _():
        bucket = bucket_ref[...]
        for h in range(ATTN_HEADS):
            acc = jnp.zeros((blk, blk), F32)
            for b in range(N_BUCKETS):
                acc = jnp.where(bucket == b, relb_ref[b, h], acc)
            hk, g = divmod(h, ATTN_GROUP)
            bias_ref[hk, g * blk:(g + 1) * blk, :] = acc

    qi = lax.broadcasted_iota(jnp.int32, (rows, blk), 0) & (blk - 1)
    kj = lax.broadcasted_iota(jnp.int32, (rows, blk), 1)
    own = kj <= qi
    for i in range(ATTN_BLOCKS_PER_STEP):
        cur = slice(i * blk, (i + 1) * blk)
        kprev, vprev = (kp_ref, vp_ref) if i == 0 else (kc_ref.at[(i - 1) * blk:i * blk],
                                                        vc_ref.at[(i - 1) * blk:i * blk])
        has_prev = (n > 0) if i == 0 else True
        for hk in range(ATTN_KV_HEADS):
            ksl = slice(hk * HEAD_DIM, (hk + 1) * HEAD_DIM)
            kband = jnp.concatenate([kprev[:, ksl], kc_ref[cur, ksl]], axis=0)
            vband = jnp.concatenate([vprev[:, ksl], vc_ref[cur, ksl]], axis=0)
            qg = jnp.concatenate(
                [q_ref[cur, (hk * ATTN_GROUP + g) * HEAD_DIM:(hk * ATTN_GROUP + g + 1) * HEAD_DIM]
                 for g in range(ATTN_GROUP)], axis=0)
            s2 = _dot_nt(qg, kband)
            s = jnp.where(own, s2[:, blk:], s2[:, :blk]) + bias_ref[hk]
            if i == 0:
                s = jnp.where(own | has_prev, s, -jnp.inf)
            sink = jnp.concatenate(
                [jnp.full((blk, blk), sink_ref[hk * ATTN_GROUP + g], F32)
                 for g in range(ATTN_GROUP)], axis=0)
            m = jnp.maximum(jnp.broadcast_to(jnp.max(s, axis=-1, keepdims=True), s.shape), sink)
            pb = jnp.exp(s - m).astype(BF16)
            zero = jnp.zeros_like(pb)
            pfull = jnp.concatenate([jnp.where(own, zero, pb), jnp.where(own, pb, zero)], axis=1)
            denom = _dot(pfull, jnp.ones((2 * blk, blk), BF16)) + jnp.exp(sink - m)
            o = _dot(pfull, vband) / denom[:, :HEAD_DIM]
            for g in range(ATTN_GROUP):
                h = hk * ATTN_GROUP + g
                o_ref[cur, h * HEAD_DIM:(h + 1) * HEAD_DIM] = o[g * blk:(g + 1) * blk].astype(BF16)


def _attention(q, k, v, bucket, rel_bias, sinks):
    b, s, _ = q.shape
    rows = ATTN_BLOCKS_PER_STEP * ATTN_BLOCK
    assert s % rows == 0
    cur = lambda i, j: (i, j, 0)
    prev = lambda i, j: (i, jnp.maximum(j * ATTN_BLOCKS_PER_STEP - 1, 0), 0)
    smem = pl.BlockSpec(memory_space=pltpu.SMEM)
    return pl.pallas_call(
        _attn_kernel,
        grid=(b, s // rows),
        in_specs=[
            pl.BlockSpec((ATTN_BLOCK, ATTN_BLOCK), lambda i, j: (0, 0)),
            smem, smem,
            pl.BlockSpec((None, rows, ATTN_WIDTH), cur),
            pl.BlockSpec((None, ATTN_BLOCK, KV_COLS), prev),
            pl.BlockSpec((None, rows, KV_COLS), cur),
            pl.BlockSpec((None, ATTN_BLOCK, KV_COLS), prev),
            pl.BlockSpec((None, rows, KV_COLS), cur),
        ],
        out_specs=pl.BlockSpec((None, rows, ATTN_WIDTH), cur),
        out_shape=jax.ShapeDtypeStruct((b, s, ATTN_WIDTH), BF16),
        scratch_shapes=[pltpu.VMEM((ATTN_KV_HEADS, ATTN_GROUP * ATTN_BLOCK, ATTN_BLOCK), F32)],
        compiler_params=_params("arbitrary", "arbitrary"),
        name="swa",
    )(bucket, rel_bias, sinks, q, k, k, v, v)


def _ssd_chunk(xbc_ref, z_ref, dt_ref, convw_ref, convb_ref, dtb_ref, alog_ref, dskip_ref,
               nw_ref, expand_ref, tri_ref, o_ref, xpad_ref, state_ref):
    L = SSD_CHUNK
    P = SSD_HEAD_DIM
    xpad_ref[CONV_HALO:CONV_HALO + L, :] = xbc_ref[...]
    conv = convb_ref[...]
    for j in range(SSD_CONV):
        off = CONV_HALO - (SSD_CONV - 1) + j
        conv = conv + convw_ref[j:j + 1, :] * xpad_ref[off:off + L, :]
    xbc = _silu(conv)
    xs = xbc[:, :SSD_INNER]
    bm = xbc[:, SSD_INNER:SSD_INNER + SSD_GROUPS * SSD_STATE].astype(BF16)
    cm = xbc[:, SSD_INNER + SSD_GROUPS * SSD_STATE:].astype(BF16)

    expand = expand_ref[...]
    dt = _softplus(dt_ref[...] + dtb_ref[...])
    a = dt * (-jnp.exp(alog_ref[...]))
    a_cum = sum(_dot(tri_ref[...], term) for term in _split3(a))
    a_last = a_cum[L - 1:L, :]
    stacked = jnp.concatenate([dt, jnp.exp(a_last - a_cum), jnp.exp(a_cum)], axis=0)
    wide = sum(_dot(term, expand) for term in _split3(stacked))
    x_dt = xs * wide[0:L]
    x_end = (x_dt * wide[L:2 * L]).astype(BF16)
    dec_start = wide[2 * L:3 * L]
    x_dt = x_dt.astype(BF16)
    a_cum_t = a_cum.T

    li = lax.broadcasted_iota(jnp.int32, (L, L), 0)
    si = lax.broadcasted_iota(jnp.int32, (L, L), 1)
    causal = si <= li
    ys = []
    for g in range(SSD_GROUPS):
        bg = bm[:, g * SSD_STATE:(g + 1) * SSD_STATE]
        cg = cm[:, g * SSD_STATE:(g + 1) * SSD_STATE]
        cb = _dot_nt(cg, bg)
        for r in range(SSD_HEADS_PER_GROUP):
            h = g * SSD_HEADS_PER_GROUP + r
            sl = slice(h * P, (h + 1) * P)
            seg = a_cum[:, h:h + 1] - a_cum_t[h:h + 1, :]
            lmat = jnp.where(causal, jnp.exp(jnp.where(causal, seg, 0.0)), 0.0)
            h_prev = state_ref[sl, :]
            y_diag = _dot((cb * lmat).astype(BF16), x_dt[:, sl])
            y_off = _dot_nt(cg, h_prev.astype(BF16)) * dec_start[:, sl]
            ys.append(y_diag + y_off)
            chunk_decay = jnp.exp(a_cum_t[h:h + 1, L - 1:L])
            state_ref[sl, :] = h_prev * chunk_decay + _dot_tn(x_end[:, sl], bg)
    y = jnp.concatenate(ys, axis=1) + dskip_ref[...] * xs
    y = y * _silu(z_ref[...])
    o_ref[...] = _rms(y, nw_ref[...]).astype(BF16)


def _ssd_kernel(xbc_ref, z_ref, dt_ref, convw_ref, convb_ref, dtb_ref, alog_ref, dskip_ref,
                nw_ref, expand_ref, tri_ref, o_ref, xpad_ref, state_ref):
    c = pl.program_id(1)

    @pl.when(c == 0)
    def _():
        xpad_ref[:, 0:CONV_HALO, :] = jnp.zeros((SSD_SEQS_PER_STEP, CONV_HALO, XBC_COLS), F32)
        state_ref[...] = jnp.zeros_like(state_ref)

    @pl.when(c > 0)
    def _():
        xpad_ref[:, 0:CONV_HALO, :] = xpad_ref[:, SSD_CHUNK:SSD_CHUNK + CONV_HALO, :]

    for b in range(SSD_SEQS_PER_STEP):
        _ssd_chunk(xbc_ref.at[b], z_ref.at[b], dt_ref.at[b], convw_ref, convb_ref, dtb_ref,
                   alog_ref, dskip_ref, nw_ref, expand_ref, tri_ref, o_ref.at[b],
                   xpad_ref.at[b], state_ref.at[b])


def _ssd(xbc, z, dt, conv_w, conv_b, dt_bias_pad, a_log_pad, d_skip_wide, norm_w, expand, tri):
    b, s, _ = xbc.shape
    nb = SSD_SEQS_PER_STEP
    assert b % nb == 0
    nc = s // SSD_CHUNK
    cur = lambda i, j: (i, j, 0)
    fixed = lambda i, j: (0, 0)
    full = lambda shape: pl.BlockSpec(shape, fixed)
    return pl.pallas_call(
        _ssd_kernel,
        grid=(b // nb, nc),
        in_specs=[
            pl.BlockSpec((nb, SSD_CHUNK, XBC_COLS), cur),
            pl.BlockSpec((nb, SSD_CHUNK, SSD_INNER), cur),
            pl.BlockSpec((nb, SSD_CHUNK, DT_PAD), cur),
            full((SSD_CONV, XBC_COLS)), full((1, XBC_COLS)), full((1, DT_PAD)), full((1, DT_PAD)),
            full((1, SSD_INNER)), full((1, SSD_INNER)), full((DT_PAD, SSD_INNER)),
            full((SSD_CHUNK, SSD_CHUNK)),
        ],
        out_specs=pl.BlockSpec((nb, SSD_CHUNK, SSD_INNER), cur),
        out_shape=jax.ShapeDtypeStruct((b, s, SSD_INNER), BF16),
        scratch_shapes=[
            pltpu.VMEM((nb, CONV_HALO + SSD_CHUNK, XBC_COLS), F32),
            pltpu.VMEM((nb, SSD_INNER, SSD_STATE), F32),
        ],
        compiler_params=_params("arbitrary", "arbitrary"),
        name="ssd",
    )(xbc, z, dt, conv_w, conv_b, dt_bias_pad, a_log_pad, d_skip_wide, norm_w, expand, tri)


def _outproj_kernel(x_ref, attn_ref, ssd_ref, wo_ref, nw_ref, wq_ref, h1_ref, xnt_ref, q_ref):
    mix = (_dot(attn_ref[...], wo_ref[0:ATTN_WIDTH, :])
           + _dot(ssd_ref[...], wo_ref[ATTN_WIDTH:ATTN_WIDTH + SSD_INNER, :]))
    h1 = x_ref[...] + mix
    h1_ref[...] = h1
    xn = _rms(h1, nw_ref[...])
    xnt_ref[...] = pltpu.bitcast(xn.T.astype(BF16), jnp.uint32)
    qf = _dot(xn.astype(BF16), wq_ref[...])
    for j in range(2 * PEER_HEADS):
        q_ref[j] = qf[:, j * PEER_HALF:(j + 1) * PEER_HALF].astype(BF16)


def _outproj(x2, attn, ssd, w_out, norm_w, w_query):
    t = x2.shape[0]
    tm = min(OUTPROJ_TM, t)
    row = lambda i: (i, 0)
    fixed = lambda i: (0, 0)
    nq = 2 * PEER_HEADS
    return pl.pallas_call(
        _outproj_kernel,
        grid=(t // tm,),
        in_specs=[
            pl.BlockSpec((tm, D_MODEL), row),
            pl.BlockSpec((tm, ATTN_WIDTH), row),
            pl.BlockSpec((tm, SSD_INNER), row),
            pl.BlockSpec((ATTN_WIDTH + SSD_INNER, D_MODEL), fixed),
            pl.BlockSpec((1, D_MODEL), fixed),
            pl.BlockSpec((D_MODEL, nq * PEER_HALF), fixed),
        ],
        out_specs=[
            pl.BlockSpec((tm, D_MODEL), row),
            pl.BlockSpec((D_MODEL // 2, tm), lambda i: (0, i)),
            pl.BlockSpec((nq, tm, PEER_HALF), lambda i: (0, i, 0)),
        ],
        out_shape=[
            jax.ShapeDtypeStruct((t, D_MODEL), F32),
            jax.ShapeDtypeStruct((D_MODEL // 2, t), jnp.uint32),
            jax.ShapeDtypeStruct((nq, t, PEER_HALF), BF16),
        ],
        compiler_params=_params("parallel"),
        name="outproj",
    )(x2, attn, ssd, w_out, norm_w, w_query)


def _ce(x, hi, lo):
    a, b = x[hi], x[lo]
    if b is None:
        return
    if a is None:
        x[hi], x[lo] = b, None
        return
    x[hi], x[lo] = jnp.maximum(a, b), jnp.minimum(a, b)


def _bitonic_merge_desc(x):
    n = len(x)
    j = n // 2
    while j >= 1:
        for i in range(n):
            if i & j == 0:
                _ce(x, i, i | j)
        j //= 2


def _sort_desc(x):
    n = len(x)
    k = 2
    while k <= n:
        j = k // 2
        while j >= 1:
            for i in range(n):
                l = i ^ j
                if l > i:
                    if i & k == 0:
                        _ce(x, i, l)
                    else:
                        _ce(x, l, i)
            j //= 2
        k *= 2


def _merge_top(a, b, sort=True):
    n = len(a)
    out = []
    for i in range(n):
        u, v = a[i], b[n - 1 - i]
        out.append(v if u is None else u if v is None else jnp.maximum(u, v))
    if sort:
        _bitonic_merge_desc(out)
    return out


def _key_rows(k):
    return slice(k * SUBLANES, (k + 1) * SUBLANES)


def _top16_sorted(s_ref, lanes):
    groups = []
    for g in range(N_KEYS // PEER_TOPK):
        x = [s_ref[_key_rows(g * PEER_TOPK + i), lanes] for i in range(PEER_TOPK)]
        _sort_desc(x)
        groups.append(x)
    while len(groups) > 1:
        groups = [_merge_top(groups[i], groups[i + 1]) for i in range(0, len(groups), 2)]
    return groups[0]


def _route_select(h, lanes, s1_ref, s2_ref, r2_ref, e2_ref, c1_ref, e1_ref,
                  r2s_ref, e2s_ref, c1s_ref, e1s_ref):
    v1 = _top16_sorted(s1_ref, lanes)
    v2 = _top16_sorted(s2_ref, lanes)

    lists = []
    for a in range(PEER_TOPK):
        n_b = PEER_TOPK // (a + 1)
        lists.append([v1[a] + v2[b] if b < n_b else None for b in range(PEER_TOPK)])
    while len(lists) > 2:
        lists = [_merge_top(lists[i], lists[i + 1]) for i in range(0, len(lists), 2)]
    top = _merge_top(lists[0], lists[1], sort=False)
    tau = top[0]
    for x in top[1:]:
        tau = jnp.minimum(tau, x)

    e1_top = [jnp.exp(v - v1[0]) for v in v1]
    e2_top = [jnp.exp(v - v2[0]) for v in v2]
    zsum = jnp.zeros_like(tau)
    cnt = []
    for a in range(PEER_TOPK):
        n_a = jnp.zeros_like(tau)
        w_a = jnp.zeros_like(tau)
        for b in range(PEER_TOPK // (a + 1)):
            sel = (v1[a] + v2[b]) >= tau
            n_a = n_a + jnp.where(sel, 1.0, 0.0)
            w_a = w_a + jnp.where(sel, e2_top[b], 0.0)
        cnt.append(n_a)
        zsum = zsum + e1_top[a] * w_a
    half_inv_z = 0.5 / zsum

    for k in range(N_KEYS):
        rows = _key_rows(k)
        s1 = s1_ref[rows, lanes]
        s2 = s2_ref[rows, lanes]
        c1 = jnp.zeros_like(tau)
        for a in reversed(range(PEER_TOPK)):
            c1 = jnp.where(s1 >= v1[a], cnt[a], c1)
        r2 = jnp.zeros_like(tau)
        for b in range(PEER_TOPK):
            r2 = jnp.where(v2[b] > s2, float(b + 1), r2)
        c1s_ref[rows, :] = c1
        r2s_ref[rows, :] = r2
        e1s_ref[rows, :] = jnp.exp(s1 - v1[0]) * half_inv_z
        e2s_ref[rows, :] = jnp.exp(s2 - v2[0])

    for j in range(SUBLANES):
        cols = slice(j * LANES, (j + 1) * LANES)
        take = pl.ds(j, N_KEYS, stride=SUBLANES)
        r2_ref[h, :, cols] = r2s_ref[take, :].astype(BF16)
        e2_ref[h, :, cols] = e2s_ref[take, :].astype(BF16)
        c1_ref[h, :, cols] = c1s_ref[take, :]
        e1_ref[h, :, cols] = e1s_ref[take, :]


def _route_kernel(q_ref, k1_ref, k2_ref, r2_ref, e2_ref, c1_ref, e1_ref,
                  s1_ref, s2_ref, r2s_ref, e2s_ref, c1s_ref, e1s_ref):
    def head_pair(i, carry):
        for half, (k_ref, s_ref) in enumerate(((k1_ref, s1_ref), (k2_ref, s2_ref))):
            qcat = jnp.concatenate(
                [jnp.concatenate([q_ref[2 * (2 * i + hh) + half, j * LANES:(j + 1) * LANES, :]
                                  for j in range(SUBLANES)], axis=1)
                 for hh in range(2)], axis=0)
            s_ref[...] = _dot_nt(k_ref[...], qcat)
        for hh in range(2):
            _route_select(2 * i + hh, slice(hh * LANES, (hh + 1) * LANES), s1_ref, s2_ref,
                          r2_ref, e2_ref, c1_ref, e1_ref, r2s_ref, e2s_ref, c1s_ref, e1s_ref)
        return carry

    lax.fori_loop(0, PEER_HEADS // 2, head_pair, 0)


def _expand_keys(keys):
    eye = jnp.eye(SUBLANES, dtype=keys.dtype)
    return jnp.einsum("kd,jc->kjcd", keys, eye).reshape(N_KEYS * SUBLANES, SUBLANES * PEER_HALF)


def _route(q, keys1, keys2):
    nq, t, _ = q.shape
    tm = ROUTE_TM
    assert t % tm == 0
    fixed = lambda i: (0, 0)
    out_spec = pl.BlockSpec((PEER_HEADS, N_KEYS, tm), lambda i: (0, 0, i))
    shape = (PEER_HEADS, N_KEYS, t)
    key_spec = pl.BlockSpec((N_KEYS * SUBLANES, SUBLANES * PEER_HALF), fixed)
    stage = pltpu.VMEM((N_KEYS * SUBLANES, LANES), F32)
    scores = pltpu.VMEM((N_KEYS * SUBLANES, 2 * LANES), F32)
    return pl.pallas_call(
        _route_kernel,
        grid=(t // tm,),
        in_specs=[pl.BlockSpec((nq, tm, PEER_HALF), lambda i: (0, i, 0)), key_spec, key_spec],
        out_specs=[out_spec] * 4,
        out_shape=[jax.ShapeDtypeStruct(shape, BF16)] * 2 + [jax.ShapeDtypeStruct(shape, F32)] * 2,
        scratch_shapes=[scores] * 2 + [stage] * 4,
        compiler_params=_params("parallel"),
        name="route",
    )(q, _expand_keys(keys1), _expand_keys(keys2))


def _peer_activations(pre_ref, act_ref, r2_ref, e2_ref, c1_ref, e1_ref, sub):
    tm = pre_ref.shape[1]
    chunks = N_KEYS // PEER_ROWS
    shape = (PEER_ROWS, PEER_LANE_CHUNK)
    zero = jnp.zeros(shape, BF16)
    for a in range(PEER_SUB // N_KEYS):
        key1 = sub * (PEER_SUB // N_KEYS) + a
        for lc in range(tm // PEER_LANE_CHUNK):
            lanes = slice(lc * PEER_LANE_CHUNK, (lc + 1) * PEER_LANE_CHUNK)
            gates = [zero] * chunks
            for h in range(PEER_HEADS):
                cnt = jnp.broadcast_to(c1_ref[h, key1:key1 + 1, lanes], shape).astype(BF16)
                e1 = jnp.broadcast_to(e1_ref[h, key1:key1 + 1, lanes], shape).astype(BF16)
                for c in range(chunks):
                    rows = slice(c * PEER_ROWS, (c + 1) * PEER_ROWS)
                    sel = r2_ref[h, rows, lanes] < cnt
                    gates[c] = gates[c] + jnp.where(sel, e2_ref[h, rows, lanes], zero) * e1
            for c in range(chunks):
                rows = slice(a * N_KEYS + c * PEER_ROWS, a * N_KEYS + (c + 1) * PEER_ROWS)
                p = pre_ref[rows, lanes]
                gelu2 = p + p * lax.erf(p * (2.0 ** -0.5))
                act_ref[rows, lanes] = gelu2.astype(BF16) * gates[c]


def _peer_kernel(xnt_ref, d_ref, ut_ref, r2_ref, e2_ref, c1_ref, e1_ref, h1_ref, nw_ref,
                 o_ref, acc_ref, pre_ref, act_ref):
    j = pl.program_id(1)
    nsub = ut_ref.shape[1] // PEER_SUB

    @pl.when(j == 0)
    def _():
        acc_ref[...] = jnp.zeros_like(acc_ref)

    def down(sub):
        d_sub = d_ref[sub * (PEER_SUB // 2):(sub + 1) * (PEER_SUB // 2), :]
        pre_ref[sub % 2] = _dot(pltpu.bitcast(d_sub, BF16), pltpu.bitcast(xnt_ref[...], BF16))

    down(0)
    for sub in range(nsub):
        if sub + 1 < nsub:
            down(sub + 1)
        _peer_activations(pre_ref.at[sub % 2], act_ref.at[sub % 2], r2_ref, e2_ref, c1_ref,
                          e1_ref, sub)
        ut_sub = ut_ref[:, sub * PEER_SUB:(sub + 1) * PEER_SUB]
        acc_ref[...] += _dot(pltpu.bitcast(ut_sub, BF16), act_ref[sub % 2])

    @pl.when(j == pl.num_programs(1) - 1)
    def _():
        h2 = h1_ref[...] + acc_ref[...].T
        o_ref[...] = _rms(h2, nw_ref[...])


def _pack_kernel(w_ref, o_ref, *, transpose):
    w = w_ref[...].T if transpose else w_ref[...]
    o_ref[...] = pltpu.bitcast(w.astype(BF16), jnp.uint32)


def _pack_experts(w, transpose):
    tiles = N_EXPERTS // PEER_TE
    if transpose:
        out_spec = pl.BlockSpec((None, D_MODEL // 2, PEER_TE), lambda i: (i, 0, 0))
        out_shape = jax.ShapeDtypeStruct((tiles, D_MODEL // 2, PEER_TE), jnp.uint32)
    else:
        out_spec = pl.BlockSpec((PEER_TE // 2, D_MODEL), lambda i: (i, 0))
        out_shape = jax.ShapeDtypeStruct((N_EXPERTS // 2, D_MODEL), jnp.uint32)
    return pl.pallas_call(
        functools.partial(_pack_kernel, transpose=transpose),
        grid=(tiles,),
        in_specs=[pl.BlockSpec((PEER_TE, D_MODEL), lambda i: (i, 0))],
        out_specs=out_spec,
        out_shape=out_shape,
        compiler_params=_params("parallel"),
        name="pack_up" if transpose else "pack_down",
    )(w)


def _peer(xnt, down, up_t, r2, e2, c1, e1, h1, norm_w):
    t = h1.shape[0]
    tm = min(PEER_TM, t)
    te = PEER_TE
    key2_spec = pl.BlockSpec((PEER_HEADS, N_KEYS, tm), lambda i, j: (0, 0, i))
    key1_spec = pl.BlockSpec((PEER_HEADS, te // N_KEYS, tm), lambda i, j: (0, j, i))
    return pl.pallas_call(
        _peer_kernel,
        grid=(t // tm, N_EXPERTS // te),
        in_specs=[
            pl.BlockSpec((D_MODEL // 2, tm), lambda i, j: (0, i)),
            pl.BlockSpec((te // 2, D_MODEL), lambda i, j: (j, 0)),
            pl.BlockSpec((None, D_MODEL // 2, te), lambda i, j: (j, 0, 0)),
            key2_spec, key2_spec, key1_spec, key1_spec,
            pl.BlockSpec((tm, D_MODEL), lambda i, j: (i, 0)),
            pl.BlockSpec((1, D_MODEL), lambda i, j: (0, 0)),
        ],
        out_specs=pl.BlockSpec((tm, D_MODEL), lambda i, j: (i, 0)),
        out_shape=jax.ShapeDtypeStruct((t, D_MODEL), F32),
        scratch_shapes=[
            pltpu.VMEM((D_MODEL, tm), F32),
            pltpu.VMEM((2, PEER_SUB, tm), F32),
            pltpu.VMEM((2, PEER_SUB, tm), BF16),
        ],
        compiler_params=_params("parallel", "arbitrary"),
        name="peer",
    )(xnt, down, up_t, r2, e2, c1, e1, h1, norm_w)


def _pad_lanes(v, width):
    return jnp.pad(v.astype(F32), (0, width - v.shape[0])).reshape(1, width)


def _layer(x, norm_mix, w_in, conv_w, conv_b, dt_bias, a_log, d_skip, ssd_norm_w, attn_sinks,
           w_out, rel_bias, norm_ffn, w_query, sub_keys1, sub_keys2, expert_down, expert_up,
           out_norm):
    b, s, d = x.shape
    t = b * s
    x2 = x.reshape(t, d)

    w_in_pad = jnp.pad(w_in, ((0, 0), (0, IN_COLS_PAD - w_in.shape[1]))).astype(BF16)
    q, k, v, z, xbc, dt = _inproj(x2, norm_mix.reshape(1, d), w_in_pad)

    qi = jnp.arange(ATTN_BLOCK)[:, None]
    kj = jnp.arange(ATTN_BLOCK)[None, :]
    bucket = _t5_bucket(jnp.where(kj <= qi, qi - kj, qi + ATTN_BLOCK - kj)).astype(jnp.int32)
    attn = _attention(q.reshape(b, s, -1), k.reshape(b, s, -1), v.reshape(b, s, -1),
                      bucket, rel_bias.astype(F32), attn_sinks.astype(F32))

    lane = jnp.arange(SSD_INNER)[None, :] // SSD_HEAD_DIM
    expand = (jnp.arange(DT_PAD)[:, None] == lane).astype(BF16)
    tri = (jnp.arange(SSD_CHUNK)[None, :] <= jnp.arange(SSD_CHUNK)[:, None]).astype(BF16)
    ssd = _ssd(xbc.reshape(b, s, -1), z.reshape(b, s, -1), dt.reshape(b, s, -1),
               conv_w, conv_b.reshape(1, -1), _pad_lanes(dt_bias, DT_PAD),
               _pad_lanes(a_log, DT_PAD), jnp.repeat(d_skip.astype(F32), SSD_HEAD_DIM).reshape(1, -1),
               ssd_norm_w.reshape(1, -1), expand, tri)

    h1, xnt, pq = _outproj(x2, attn.reshape(t, -1), ssd.reshape(t, -1), w_out.astype(BF16),
                           norm_ffn.reshape(1, d), w_query.astype(BF16))
    r2, e2, c1, e1 = _route(pq, sub_keys1.astype(BF16), sub_keys2.astype(BF16))
    out = _peer(xnt, _pack_experts(expert_down, False), _pack_experts(expert_up, True),
                r2, e2, c1, e1, h1, out_norm.reshape(1, d))
    return out.reshape(b, s, d)


@jax.jit
def kernel(x, norm_mix, w_in, conv_w, conv_b, dt_bias, a_log, d_skip, ssd_norm_w, attn_sinks,
           w_out, rel_bias, norm_ffn, w_query, sub_keys1, sub_keys2, expert_down, expert_up,
           norm_final):
    assert norm_mix.shape[0] == 1, "single-layer block"
    return _layer(x, norm_mix[0], w_in[0], conv_w[0], conv_b[0], dt_bias[0], a_log[0], d_skip[0],
                  ssd_norm_w[0], attn_sinks[0], w_out[0], rel_bias, norm_ffn[0], w_query[0],
                  sub_keys1[0], sub_keys2[0], expert_down[0], expert_up[0], norm_final)
```

```python
import functools
import math

import jax
import jax.numpy as jnp
from jax import lax
from jax.experimental import pallas as pl
from jax.experimental.pallas import tpu as pltpu

F32 = jnp.float32
BF16 = jnp.bfloat16

D_MODEL = 1024
ATTN_HEADS = 8
ATTN_KV_HEADS = 2
ATTN_GROUP = ATTN_HEADS // ATTN_KV_HEADS
HEAD_DIM = 64
ATTN_WIDTH = ATTN_HEADS * HEAD_DIM
ATTN_BLOCK = 128
N_BUCKETS = 32
MAX_DISTANCE = 128
SSD_HEADS = 8
SSD_HEAD_DIM = 64
SSD_INNER = SSD_HEADS * SSD_HEAD_DIM
SSD_STATE = 128
SSD_GROUPS = 2
SSD_HEADS_PER_GROUP = SSD_HEADS // SSD_GROUPS
SSD_CONV = 4
SSD_CHUNK = 128
PEER_HEADS = 8
N_KEYS = 128
N_EXPERTS = N_KEYS * N_KEYS
PEER_HALF = 128
PEER_TOPK = 16
EPS = 1e-6

KV_COLS = ATTN_KV_HEADS * HEAD_DIM
XBC_COLS = SSD_INNER + 2 * SSD_GROUPS * SSD_STATE
OFF_K = ATTN_WIDTH
OFF_V = OFF_K + KV_COLS
OFF_Z = OFF_V + KV_COLS
OFF_XBC = OFF_Z + SSD_INNER
OFF_DT = OFF_XBC + XBC_COLS
LANES = 128
SUBLANES = 8
DT_PAD = LANES
IN_COLS_PAD = OFF_DT + DT_PAD
CONV_HALO = SUBLANES

VMEM_LIMIT = 56 * 1024 * 1024

ATTN_BLOCKS_PER_STEP = 4
SSD_SEQS_PER_STEP = 2
INPROJ_TM = 1024
OUTPROJ_TM = 512
ROUTE_TM = SUBLANES * LANES
PEER_TM = 512
PEER_TE = 2048
PEER_SUB = 1024
PEER_LANE_CHUNK = 512
PEER_ROWS = 16


def _dot(a, b):
    return jnp.dot(a, b, preferred_element_type=F32)


def _dot_nt(a, b):
    return lax.dot_general(a, b, (((1,), (1,)), ((), ())), preferred_element_type=F32)


def _dot_tn(a, b):
    return lax.dot_general(a, b, (((0,), (0,)), ((), ())), preferred_element_type=F32)


def _rms(x, w):
    return x * lax.rsqrt(jnp.mean(x * x, axis=-1, keepdims=True) + EPS) * w


def _silu(x):
    return x / (1.0 + jnp.exp(-x))


def _softplus(x):
    return jnp.maximum(x, 0.0) + jnp.log1p(jnp.exp(-jnp.abs(x)))


def _split3(x):
    hi = x.astype(BF16)
    rest = x - hi.astype(F32)
    mid = rest.astype(BF16)
    return hi, mid, (rest - mid.astype(F32)).astype(BF16)


def _params(*sem):
    return pltpu.CompilerParams(dimension_semantics=sem, vmem_limit_bytes=VMEM_LIMIT)


def _inproj_kernel(x_ref, nw_ref, w_ref, q_ref, k_ref, v_ref, z_ref, xbc_ref, dt_ref):
    xb = _rms(x_ref[...], nw_ref[...]).astype(BF16)

    def proj(lo, hi):
        return _dot(xb, w_ref[:, lo:hi])

    q_ref[...] = (proj(0, OFF_K) * (HEAD_DIM ** -0.5)).astype(BF16)
    k_ref[...] = proj(OFF_K, OFF_V).astype(BF16)
    v_ref[...] = proj(OFF_V, OFF_Z).astype(BF16)
    z_ref[...] = proj(OFF_Z, OFF_XBC)
    xbc_ref[...] = proj(OFF_XBC, OFF_DT)
    dt_ref[...] = proj(OFF_DT, IN_COLS_PAD)


def _inproj(x2, norm_w, w_in_pad):
    t = x2.shape[0]
    tm = min(INPROJ_TM, t)
    row = lambda i: (i, 0)
    fixed = lambda i: (0, 0)
    return pl.pallas_call(
        _inproj_kernel,
        grid=(t // tm,),
        in_specs=[
            pl.BlockSpec((tm, D_MODEL), row),
            pl.BlockSpec((1, D_MODEL), fixed),
            pl.BlockSpec((D_MODEL, IN_COLS_PAD), fixed),
        ],
        out_specs=[
            pl.BlockSpec((tm, ATTN_WIDTH), row),
            pl.BlockSpec((tm, KV_COLS), row),
            pl.BlockSpec((tm, KV_COLS), row),
            pl.BlockSpec((tm, SSD_INNER), row),
            pl.BlockSpec((tm, XBC_COLS), row),
            pl.BlockSpec((tm, DT_PAD), row),
        ],
        out_shape=[
            jax.ShapeDtypeStruct((t, ATTN_WIDTH), BF16),
            jax.ShapeDtypeStruct((t, KV_COLS), BF16),
            jax.ShapeDtypeStruct((t, KV_COLS), BF16),
            jax.ShapeDtypeStruct((t, SSD_INNER), F32),
            jax.ShapeDtypeStruct((t, XBC_COLS), F32),
            jax.ShapeDtypeStruct((t, DT_PAD), F32),
        ],
        compiler_params=_params("parallel"),
        name="inproj",
    )(x2, norm_w, w_in_pad)


def _t5_bucket(dist):
    n = jnp.maximum(dist, 0)
    max_exact = N_BUCKETS // 2
    nf = jnp.maximum(n, 1).astype(F32)
    large = max_exact + (jnp.log(nf / max_exact) / math.log(MAX_DISTANCE / max_exact)
                         * (N_BUCKETS - max_exact)).astype(jnp.int32)
    large = jnp.minimum(large, N_BUCKETS - 1)
    return jnp.where(n < max_exact, n, large)


def _attn_kernel(bucket_ref, relb_ref, sink_ref, q_ref, kp_ref, kc_ref, vp_ref, vc_ref,
                 o_ref, bias_ref):
    blk = ATTN_BLOCK
    rows = ATTN_GROUP * blk
    n = pl.program_id(1)

    @pl.when((pl.program_id(0) == 0) & (n == 0))
    def _():
        bucket = bucket_ref[...]
        for h in range(ATTN_HEADS):
            acc = jnp.zeros((blk, blk), F32)
            for b in range(N_BUCKETS):
                acc = jnp.where(bucket == b, relb_ref[b, h], acc)
            hk, g = divmod(h, ATTN_GROUP)
            bias_ref[hk, g * blk:(g + 1) * blk, :] = acc

    qi = lax.broadcasted_iota(jnp.int32, (rows, blk), 0) & (blk - 1)
    kj = lax.broadcasted_iota(jnp.int32, (rows, blk), 1)
    own = kj <= qi
    for i in range(ATTN_BLOCKS_PER_STEP):
        cur = slice(i * blk, (i + 1) * blk)
        kprev, vprev = (kp_ref, vp_ref) if i == 0 else (kc_ref.at[(i - 1) * blk:i * blk],
                                                        vc_ref.at[(i - 1) * blk:i * blk])
        has_prev = (n > 0) if i == 0 else True
        for hk in range(ATTN_KV_HEADS):
            ksl = slice(hk * HEAD_DIM, (hk + 1) * HEAD_DIM)
            kband = jnp.concatenate([kprev[:, ksl], kc_ref[cur, ksl]], axis=0)
            vband = jnp.concatenate([vprev[:, ksl], vc_ref[cur, ksl]], axis=0)
            qg = jnp.concatenate(
                [q_ref[cur, (hk * ATTN_GROUP + g) * HEAD_DIM:(hk * ATTN_GROUP + g + 1) * HEAD_DIM]
                 for g in range(ATTN_GROUP)], axis=0)
            s2 = _dot_nt(qg, kband)
            s = jnp.where(own, s2[:, blk:], s2[:, :blk]) + bias_ref[hk]
            if i == 0:
                s = jnp.where(own | has_prev, s, -jnp.inf)
            sink = jnp.concatenate(
                [jnp.full((blk, blk), sink_ref[hk * ATTN_GROUP + g], F32)
                 for g in range(ATTN_GROUP)], axis=0)
            m = jnp.maximum(jnp.broadcast_to(jnp.max(s, axis=-1, keepdims=True), s.shape), sink)
            pb = jnp.exp(s - m).astype(BF16)
            zero = jnp.zeros_like(pb)
            pfull = jnp.concatenate([jnp.where(own, zero, pb), jnp.where(own, pb, zero)], axis=1)
            denom = _dot(pfull, jnp.ones((2 * blk, blk), BF16)) + jnp.exp(sink - m)
            o = _dot(pfull, vband) / denom[:, :HEAD_DIM]
            for g in range(ATTN_GROUP):
                h = hk * ATTN_GROUP + g
                o_ref[cur, h * HEAD_DIM:(h + 1) * HEAD_DIM] = o[g * blk:(g + 1) * blk].astype(BF16)


def _attention(q, k, v, bucket, rel_bias, sinks):
    b, s, _ = q.shape
    rows = ATTN_BLOCKS_PER_STEP * ATTN_BLOCK
    assert s % rows == 0
    cur = lambda i, j: (i, j, 0)
    prev = lambda i, j: (i, jnp.maximum(j * ATTN_BLOCKS_PER_STEP - 1, 0), 0)
    smem = pl.BlockSpec(memory_space=pltpu.SMEM)
    return pl.pallas_call(
        _attn_kernel,
        grid=(b, s // rows),
        in_specs=[
            pl.BlockSpec((ATTN_BLOCK, ATTN_BLOCK), lambda i, j: (0, 0)),
            smem, smem,
            pl.BlockSpec((None, rows, ATTN_WIDTH), cur),
            pl.BlockSpec((None, ATTN_BLOCK, KV_COLS), prev),
            pl.BlockSpec((None, rows, KV_COLS), cur),
            pl.BlockSpec((None, ATTN_BLOCK, KV_COLS), prev),
            pl.BlockSpec((None, rows, KV_COLS), cur),
        ],
        out_specs=pl.BlockSpec((None, rows, ATTN_WIDTH), cur),
        out_shape=jax.ShapeDtypeStruct((b, s, ATTN_WIDTH), BF16),
        scratch_shapes=[pltpu.VMEM((ATTN_KV_HEADS, ATTN_GROUP * ATTN_BLOCK, ATTN_BLOCK), F32)],
        compiler_params=_params("arbitrary", "arbitrary"),
        name="swa",
    )(bucket, rel_bias, sinks, q, k, k, v, v)


def _ssd_chunk(xbc_ref, z_ref, dt_ref, convw_ref, convb_ref, dtb_ref, alog_ref, dskip_ref,
               nw_ref, expand_ref, tri_ref, o_ref, xpad_ref, state_ref):
    L = SSD_CHUNK
    P = SSD_HEAD_DIM
    xpad_ref[CONV_HALO:CONV_HALO + L, :] = xbc_ref[...]
    conv = convb_ref[...]
    for j in range(SSD_CONV):
        off = CONV_HALO - (SSD_CONV - 1) + j
        conv = conv + convw_ref[j:j + 1, :] * xpad_ref[off:off + L, :]
    xbc = _silu(conv)
    xs = xbc[:, :SSD_INNER]
    bm = xbc[:, SSD_INNER:SSD_INNER + SSD_GROUPS * SSD_STATE].astype(BF16)
    cm = xbc[:, SSD_INNER + SSD_GROUPS * SSD_STATE:].astype(BF16)

    expand = expand_ref[...]
    dt = _softplus(dt_ref[...] + dtb_ref[...])
    a = dt * (-jnp.exp(alog_ref[...]))
    a_cum = sum(_dot(tri_ref[...], term) for term in _split3(a))
    a_last = a_cum[L - 1:L, :]
    stacked = jnp.concatenate([dt, jnp.exp(a_last - a_cum), jnp.exp(a_cum)], axis=0)
    wide = sum(_dot(term, expand) for term in _split3(stacked))
    x_dt = xs * wide[0:L]
    x_end = (x_dt * wide[L:2 * L]).astype(BF16)
    dec_start = wide[2 * L:3 * L]
    x_dt = x_dt.astype(BF16)
    a_cum_t = a_cum.T

    li = lax.broadcasted_iota(jnp.int32, (L, L), 0)
    si = lax.broadcasted_iota(jnp.int32, (L, L), 1)
    causal = si <= li
    ys = []
    for g in range(SSD_GROUPS):
        bg = bm[:, g * SSD_STATE:(g + 1) * SSD_STATE]
        cg = cm[:, g * SSD_STATE:(g + 1) * SSD_STATE]
        cb = _dot_nt(cg, bg)
        for r in range(SSD_HEADS_PER_GROUP):
            h = g * SSD_HEADS_PER_GROUP + r
            sl = slice(h * P, (h + 1) * P)
            seg = a_cum[:, h:h + 1] - a_cum_t[h:h + 1, :]
            lmat = jnp.where(causal, jnp.exp(jnp.where(causal, seg, 0.0)), 0.0)
            h_prev = state_ref[sl, :]
            y_diag = _dot((cb * lmat).astype(BF16), x_dt[:, sl])
            y_off = _dot_nt(cg, h_prev.astype(BF16)) * dec_start[:, sl]
            ys.append(y_diag + y_off)
            chunk_decay = jnp.exp(a_cum_t[h:h + 1, L - 1:L])
            state_ref[sl, :] = h_prev * chunk_decay + _dot_tn(x_end[:, sl], bg)
    y = jnp.concatenate(ys, axis=1) + dskip_ref[...] * xs
    y = y * _silu(z_ref[...])
    o_ref[...] = _rms(y, nw_ref[...]).astype(BF16)


def _ssd_kernel(xbc_ref, z_ref, dt_ref, convw_ref, convb_ref, dtb_ref, alog_ref, dskip_ref,
                nw_ref, expand_ref, tri_ref, o_ref, xpad_ref, state_ref):
    c = pl.program_id(1)

    @pl.when(c == 0)
    def _():
        xpad_ref[:, 0:CONV_HALO, :] = jnp.zeros((SSD_SEQS_PER_STEP, CONV_HALO, XBC_COLS), F32)
        state_ref[...] = jnp.zeros_like(state_ref)

    @pl.when(c > 0)
    def _():
        xpad_ref[:, 0:CONV_HALO, :] = xpad_ref[:, SSD_CHUNK:SSD_CHUNK + CONV_HALO, :]

    for b in range(SSD_SEQS_PER_STEP):
        _ssd_chunk(xbc_ref.at[b], z_ref.at[b], dt_ref.at[b], convw_ref, convb_ref, dtb_ref,
                   alog_ref, dskip_ref, nw_ref, expand_ref, tri_ref, o_ref.at[b],
                   xpad_ref.at[b], state_ref.at[b])


def _ssd(xbc, z, dt, conv_w, conv_b, dt_bias_pad, a_log_pad, d_skip_wide, norm_w, expand, tri):
    b, s, _ = xbc.shape
    nb = SSD_SEQS_PER_STEP
    assert b % nb == 0
    nc = s // SSD_CHUNK
    cur = lambda i, j: (i, j, 0)
    fixed = lambda i, j: (0, 0)
    full = lambda shape: pl.BlockSpec(shape, fixed)
    return pl.pallas_call(
        _ssd_kernel,
        grid=(b // nb, nc),
        in_specs=[
            pl.BlockSpec((nb, SSD_CHUNK, XBC_COLS), cur),
            pl.BlockSpec((nb, SSD_CHUNK, SSD_INNER), cur),
            pl.BlockSpec((nb, SSD_CHUNK, DT_PAD), cur),
            full((SSD_CONV, XBC_COLS)), full((1, XBC_COLS)), full((1, DT_PAD)), full((1, DT_PAD)),
            full((1, SSD_INNER)), full((1, SSD_INNER)), full((DT_PAD, SSD_INNER)),
            full((SSD_CHUNK, SSD_CHUNK)),
        ],
        out_specs=pl.BlockSpec((nb, SSD_CHUNK, SSD_INNER), cur),
        out_shape=jax.ShapeDtypeStruct((b, s, SSD_INNER), BF16),
        scratch_shapes=[
            pltpu.VMEM((nb, CONV_HALO + SSD_CHUNK, XBC_COLS), F32),
            pltpu.VMEM((nb, SSD_INNER, SSD_STATE), F32),
        ],
        compiler_params=_params("arbitrary", "arbitrary"),
        name="ssd",
    )(xbc, z, dt, conv_w, conv_b, dt_bias_pad, a_log_pad, d_skip_wide, norm_w, expand, tri)


def _outproj_kernel(x_ref, attn_ref, ssd_ref, wo_ref, nw_ref, wq_ref, h1_ref, xnt_ref, q_ref):
    mix = (_dot(attn_ref[...], wo_ref[0:ATTN_WIDTH, :])
           + _dot(ssd_ref[...], wo_ref[ATTN_WIDTH:ATTN_WIDTH + SSD_INNER, :]))
    h1 = x_ref[...] + mix
    h1_ref[...] = h1
    xn = _rms(h1, nw_ref[...])
    xnt_ref[...] = pltpu.bitcast(xn.T.astype(BF16), jnp.uint32)
    qf = _dot(xn.astype(BF16), wq_ref[...])
    for j in range(2 * PEER_HEADS):
        q_ref[j] = qf[:, j * PEER_HALF:(j + 1) * PEER_HALF].astype(BF16)


def _outproj(x2, attn, ssd, w_out, norm_w, w_query):
    t = x2.shape[0]
    tm = min(OUTPROJ_TM, t)
    row = lambda i: (i, 0)
    fixed = lambda i: (0, 0)
    nq = 2 * PEER_HEADS
    return pl.pallas_call(
        _outproj_kernel,
        grid=(t // tm,),
        in_specs=[
            pl.BlockSpec((tm, D_MODEL), row),
            pl.BlockSpec((tm, ATTN_WIDTH), row),
            pl.BlockSpec((tm, SSD_INNER), row),
            pl.BlockSpec((ATTN_WIDTH + SSD_INNER, D_MODEL), fixed),
            pl.BlockSpec((1, D_MODEL), fixed),
            pl.BlockSpec((D_MODEL, nq * PEER_HALF), fixed),
        ],
        out_specs=[
            pl.BlockSpec((tm, D_MODEL), row),
            pl.BlockSpec((D_MODEL // 2, tm), lambda i: (0, i)),
            pl.BlockSpec((nq, tm, PEER_HALF), lambda i: (0, i, 0)),
        ],
        out_shape=[
            jax.ShapeDtypeStruct((t, D_MODEL), F32),
            jax.ShapeDtypeStruct((D_MODEL // 2, t), jnp.uint32),
            jax.ShapeDtypeStruct((nq, t, PEER_HALF), BF16),
        ],
        compiler_params=_params("parallel"),
        name="outproj",
    )(x2, attn, ssd, w_out, norm_w, w_query)


def _ce(x, hi, lo):
    a, b = x[hi], x[lo]
    if b is None:
        return
    if a is None:
        x[hi], x[lo] = b, None
        return
    x[hi], x[lo] = jnp.maximum(a, b), jnp.minimum(a, b)


def _bitonic_merge_desc(x):
    n = len(x)
    j = n // 2
    while j >= 1:
        for i in range(n):
            if i & j == 0:
                _ce(x, i, i | j)
        j //= 2


def _sort_desc(x):
    n = len(x)
    k = 2
    while k <= n:
        j = k // 2
        while j >= 1:
            for i in range(n):
                l = i ^ j
                if l > i:
                    if i & k == 0:
                        _ce(x, i, l)
                    else:
                        _ce(x, l, i)
            j //= 2
        k *= 2


def _merge_top(a, b, sort=True):
    n = len(a)
    out = []
    for i in range(n):
        u, v = a[i], b[n - 1 - i]
        out.append(v if u is None else u if v is None else jnp.maximum(u, v))
    if sort:
        _bitonic_merge_desc(out)
    return out


def _key_rows(k):
    return slice(k * SUBLANES, (k + 1) * SUBLANES)


def _top16_sorted(s_ref, lanes):
    groups = []
    for g in range(N_KEYS // PEER_TOPK):
        x = [s_ref[_key_rows(g * PEER_TOPK + i), lanes] for i in range(PEER_TOPK)]
        _sort_desc(x)
        groups.append(x)
    while len(groups) > 1:
        groups = [_merge_top(groups[i], groups[i + 1]) for i in range(0, len(groups), 2)]
    return groups[0]


def _route_select(h, lanes, s1_ref, s2_ref, r2_ref, e2_ref, c1_ref, e1_ref,
                  r2s_ref, e2s_ref, c1s_ref, e1s_ref):
    v1 = _top16_sorted(s1_ref, lanes)
    v2 = _top16_sorted(s2_ref, lanes)

    lists = []
    for a in range(PEER_TOPK):
        n_b = PEER_TOPK // (a + 1)
        lists.append([v1[a] + v2[b] if b < n_b else None for b in range(PEER_TOPK)])
    while len(lists) > 2:
        lists = [_merge_top(lists[i], lists[i + 1]) for i in range(0, len(lists), 2)]
    top = _merge_top(lists[0], lists[1], sort=False)
    tau = top[0]
    for x in top[1:]:
        tau = jnp.minimum(tau, x)

    e1_top = [jnp.exp(v - v1[0]) for v in v1]
    e2_top = [jnp.exp(v - v2[0]) for v in v2]
    zsum = jnp.zeros_like(tau)
    cnt = []
    for a in range(PEER_TOPK):
        n_a = jnp.zeros_like(tau)
        w_a = jnp.zeros_like(tau)
        for b in range(PEER_TOPK // (a + 1)):
            sel = (v1[a] + v2[b]) >= tau
            n_a = n_a + jnp.where(sel, 1.0, 0.0)
            w_a = w_a + jnp.where(sel, e2_top[b], 0.0)
        cnt.append(n_a)
        zsum = zsum + e1_top[a] * w_a
    half_inv_z = 0.5 / zsum

    for k in range(N_KEYS):
        rows = _key_rows(k)
        s1 = s1_ref[rows, lanes]
        s2 = s2_ref[rows, lanes]
        c1 = jnp.zeros_like(tau)
        for a in reversed(range(PEER_TOPK)):
            c1 = jnp.where(s1 >= v1[a], cnt[a], c1)
        r2 = jnp.zeros_like(tau)
        for b in range(PEER_TOPK):
            r2 = jnp.where(v2[b] > s2, float(b + 1), r2)
        c1s_ref[rows, :] = c1
        r2s_ref[rows, :] = r2
        e1s_ref[rows, :] = jnp.exp(s1 - v1[0]) * half_inv_z
        e2s_ref[rows, :] = jnp.exp(s2 - v2[0])

    for j in range(SUBLANES):
        cols = slice(j * LANES, (j + 1) * LANES)
        take = pl.ds(j, N_KEYS, stride=SUBLANES)
        r2_ref[h, :, cols] = r2s_ref[take, :].astype(BF16)
        e2_ref[h, :, cols] = e2s_ref[take, :].astype(BF16)
        c1_ref[h, :, cols] = c1s_ref[take, :]
        e1_ref[h, :, cols] = e1s_ref[take, :]


def _route_kernel(q_ref, k1_ref, k2_ref, r2_ref, e2_ref, c1_ref, e1_ref,
                  s1_ref, s2_ref, r2s_ref, e2s_ref, c1s_ref, e1s_ref):
    def head_pair(i, carry):
        for half, (k_ref, s_ref) in enumerate(((k1_ref, s1_ref), (k2_ref, s2_ref))):
            qcat = jnp.concatenate(
                [jnp.concatenate([q_ref[2 * (2 * i + hh) + half, j * LANES:(j + 1) * LANES, :]
                                  for j in range(SUBLANES)], axis=1)
                 for hh in range(2)], axis=0)
            s_ref[...] = _dot_nt(k_ref[...], qcat)
        for hh in range(2):
            _route_select(2 * i + hh, slice(hh * LANES, (hh + 1) * LANES), s1_ref, s2_ref,
                          r2_ref, e2_ref, c1_ref, e1_ref, r2s_ref, e2s_ref, c1s_ref, e1s_ref)
        return carry

    lax.fori_loop(0, PEER_HEADS // 2, head_pair, 0)


def _expand_keys(keys):
    eye = jnp.eye(SUBLANES, dtype=keys.dtype)
    return jnp.einsum("kd,jc->kjcd", keys, eye).reshape(N_KEYS * SUBLANES, SUBLANES * PEER_HALF)


def _route(q, keys1, keys2):
    nq, t, _ = q.shape
    tm = ROUTE_TM
    assert t % tm == 0
    fixed = lambda i: (0, 0)
    out_spec = pl.BlockSpec((PEER_HEADS, N_KEYS, tm), lambda i: (0, 0, i))
    shape = (PEER_HEADS, N_KEYS, t)
    key_spec = pl.BlockSpec((N_KEYS * SUBLANES, SUBLANES * PEER_HALF), fixed)
    stage = pltpu.VMEM((N_KEYS * SUBLANES, LANES), F32)
    scores = pltpu.VMEM((N_KEYS * SUBLANES, 2 * LANES), F32)
    return pl.pallas_call(
        _route_kernel,
        grid=(t // tm,),
        in_specs=[pl.BlockSpec((nq, tm, PEER_HALF), lambda i: (0, i, 0)), key_spec, key_spec],
        out_specs=[out_spec] * 4,
        out_shape=[jax.ShapeDtypeStruct(shape, BF16)] * 2 + [jax.ShapeDtypeStruct(shape, F32)] * 2,
        scratch_shapes=[scores] * 2 + [stage] * 4,
        compiler_params=_params("parallel"),
        name="route",
    )(q, _expand_keys(keys1), _expand_keys(keys2))


def _peer_activations(pre_ref, act_ref, r2_ref, e2_ref, c1_ref, e1_ref, sub):
    tm = pre_ref.shape[1]
    chunks = N_KEYS // PEER_ROWS
    shape = (PEER_ROWS, PEER_LANE_CHUNK)
    zero = jnp.zeros(shape, BF16)
    for a in range(PEER_SUB // N_KEYS):
        key1 = sub * (PEER_SUB // N_KEYS) + a
        for lc in range(tm // PEER_LANE_CHUNK):
            lanes = slice(lc * PEER_LANE_CHUNK, (lc + 1) * PEER_LANE_CHUNK)
            gates = [zero] * chunks
            for h in range(PEER_HEADS):
                cnt = jnp.broadcast_to(c1_ref[h, key1:key1 + 1, lanes], shape).astype(BF16)
                e1 = jnp.broadcast_to(e1_ref[h, key1:key1 + 1, lanes], shape).astype(BF16)
                for c in range(chunks):
                    rows = slice(c * PEER_ROWS, (c + 1) * PEER_ROWS)
                    sel = r2_ref[h, rows, lanes] < cnt
                    gates[c] = gates[c] + jnp.where(sel, e2_ref[h, rows, lanes], zero) * e1
            for c in range(chunks):
                rows = slice(a * N_KEYS + c * PEER_ROWS, a * N_KEYS + (c + 1) * PEER_ROWS)
                p = pre_ref[rows, lanes]
                gelu2 = p + p * lax.erf(p * (2.0 ** -0.5))
                act_ref[rows, lanes] = gelu2.astype(BF16) * gates[c]


def _peer_kernel(xnt_ref, d_ref, ut_ref, r2_ref, e2_ref, c1_ref, e1_ref, h1_ref, nw_ref,
                 o_ref, acc_ref, pre_ref, act_ref):
    j = pl.program_id(1)
    nsub = ut_ref.shape[1] // PEER_SUB

    @pl.when(j == 0)
    def _():
        acc_ref[...] = jnp.zeros_like(acc_ref)

    def down(sub):
        d_sub = d_ref[sub * (PEER_SUB // 2):(sub + 1) * (PEER_SUB // 2), :]
        pre_ref[sub % 2] = _dot(pltpu.bitcast(d_sub, BF16), pltpu.bitcast(xnt_ref[...], BF16))

    down(0)
    for sub in range(nsub):
        if sub + 1 < nsub:
            down(sub + 1)
        _peer_activations(pre_ref.at[sub % 2], act_ref.at[sub % 2], r2_ref, e2_ref, c1_ref,
                          e1_ref, sub)
        ut_sub = ut_ref[:, sub * PEER_SUB:(sub + 1) * PEER_SUB]
        acc_ref[...] += _dot(pltpu.bitcast(ut_sub, BF16), act_ref[sub % 2])

    @pl.when(j == pl.num_programs(1) - 1)
    def _():
        h2 = h1_ref[...] + acc_ref[...].T
        o_ref[...] = _rms(h2, nw_ref[...])


def _pack_kernel(w_ref, o_ref, *, transpose):
    w = w_ref[...].T if transpose else w_ref[...]
    o_ref[...] = pltpu.bitcast(w.astype(BF16), jnp.uint32)


def _pack_experts(w, transpose):
    tiles = N_EXPERTS // PEER_TE
    if transpose:
        out_spec = pl.BlockSpec((None, D_MODEL // 2, PEER_TE), lambda i: (i, 0, 0))
        out_shape = jax.ShapeDtypeStruct((tiles, D_MODEL // 2, PEER_TE), jnp.uint32)
    else:
        out_spec = pl.BlockSpec((PEER_TE // 2, D_MODEL), lambda i: (i, 0))
        out_shape = jax.ShapeDtypeStruct((N_EXPERTS // 2, D_MODEL), jnp.uint32)
    return pl.pallas_call(
        functools.partial(_pack_kernel, transpose=transpose),
        grid=(tiles,),
        in_specs=[pl.BlockSpec((PEER_TE, D_MODEL), lambda i: (i, 0))],
        out_specs=out_spec,
        out_shape=out_shape,
        compiler_params=_params("parallel"),
        name="pack_up" if transpose else "pack_down",
    )(w)


def _peer(xnt, down, up_t, r2, e2, c1, e1, h1, norm_w):
    t = h1.shape[0]
    tm = min(PEER_TM, t)
    te = PEER_TE
    key2_spec = pl.BlockSpec((PEER_HEADS, N_KEYS, tm), lambda i, j: (0, 0, i))
    key1_spec = pl.BlockSpec((PEER_HEADS, te // N_KEYS, tm), lambda i, j: (0, j, i))
    return pl.pallas_call(
        _peer_kernel,
        grid=(t // tm, N_EXPERTS // te),
        in_specs=[
            pl.BlockSpec((D_MODEL // 2, tm), lambda i, j: (0, i)),
            pl.BlockSpec((te // 2, D_MODEL), lambda i, j: (j, 0)),
            pl.BlockSpec((None, D_MODEL // 2, te), lambda i, j: (j, 0, 0)),
            key2_spec, key2_spec, key1_spec, key1_spec,
            pl.BlockSpec((tm, D_MODEL), lambda i, j: (i, 0)),
            pl.BlockSpec((1, D_MODEL), lambda i, j: (0, 0)),
        ],
        out_specs=pl.BlockSpec((tm, D_MODEL), lambda i, j: (i, 0)),
        out_shape=jax.ShapeDtypeStruct((t, D_MODEL), F32),
        scratch_shapes=[
            pltpu.VMEM((D_MODEL, tm), F32),
            pltpu.VMEM((2, PEER_SUB, tm), F32),
            pltpu.VMEM((2, PEER_SUB, tm), BF16),
        ],
        compiler_params=_params("parallel", "arbitrary"),
        name="peer",
    )(xnt, down, up_t, r2, e2, c1, e1, h1, norm_w)


def _pad_lanes(v, width):
    return jnp.pad(v.astype(F32), (0, width - v.shape[0])).reshape(1, width)


def _layer(x, norm_mix, w_in, conv_w, conv_b, dt_bias, a_log, d_skip, ssd_norm_w, attn_sinks,
           w_out, rel_bias, norm_ffn, w_query, sub_keys1, sub_keys2, expert_down, expert_up,
           out_norm):
    b, s, d = x.shape
    t = b * s
    x2 = x.reshape(t, d)

    w_in_pad = jnp.pad(w_in, ((0, 0), (0, IN_COLS_PAD - w_in.shape[1]))).astype(BF16)
    q, k, v, z, xbc, dt = _inproj(x2, norm_mix.reshape(1, d), w_in_pad)

    qi = jnp.arange(ATTN_BLOCK)[:, None]
    kj = jnp.arange(ATTN_BLOCK)[None, :]
    bucket = _t5_bucket(jnp.where(kj <= qi, qi - kj, qi + ATTN_BLOCK - kj)).astype(jnp.int32)
    attn = _attention(q.reshape(b, s, -1), k.reshape(b, s, -1), v.reshape(b, s, -1),
                      bucket, rel_bias.astype(F32), attn_sinks.astype(F32))

    lane = jnp.arange(SSD_INNER)[None, :] // SSD_HEAD_DIM
    expand = (jnp.arange(DT_PAD)[:, None] == lane).astype(BF16)
    tri = (jnp.arange(SSD_CHUNK)[None, :] <= jnp.arange(SSD_CHUNK)[:, None]).astype(BF16)
    ssd = _ssd(xbc.reshape(b, s, -1), z.reshape(b, s, -1), dt.reshape(b, s, -1),
               conv_w, conv_b.reshape(1, -1), _pad_lanes(dt_bias, DT_PAD),
               _pad_lanes(a_log, DT_PAD), jnp.repeat(d_skip.astype(F32), SSD_HEAD_DIM).reshape(1, -1),
               ssd_norm_w.reshape(1, -1), expand, tri)

    h1, xnt, pq = _outproj(x2, attn.reshape(t, -1), ssd.reshape(t, -1), w_out.astype(BF16),
                           norm_ffn.reshape(1, d), w_query.astype(BF16))
    r2, e2, c1, e1 = _route(pq, sub_keys1.astype(BF16), sub_keys2.astype(BF16))
    out = _peer(xnt, _pack_experts(expert_down, False), _pack_experts(expert_up, True),
                r2, e2, c1, e1, h1, out_norm.reshape(1, d))
    return out.reshape(b, s, d)


@jax.jit
def kernel(x, norm_mix, w_in, conv_w, conv_b, dt_bias, a_log, d_skip, ssd_norm_w, attn_sinks,
           w_out, rel_bias, norm_ffn, w_query, sub_keys1, sub_keys2, expert_down, expert_up,
           norm_final):
    assert norm_mix.shape[0] == 1, "single-layer block"
    return _layer(x, norm_mix[0], w_in[0], conv_w[0], conv_b[0], dt_bias[0], a_log[0], d_skip[0],
                  ssd_norm_w[0], attn_sinks[0], w_out[0], rel_bias, norm_ffn[0], w_query[0],
                  sub_keys1[0], sub_keys2[0], expert_down[0], expert_up[0], norm_final)
```

```python
import functools
import math

import jax
import jax.numpy as jnp
from jax import lax
from jax.experimental import pallas as pl
from jax.experimental.pallas import tpu as pltpu

F32 = jnp.float32
BF16 = jnp.bfloat16

D_MODEL = 1024
ATTN_HEADS = 8
ATTN_KV_HEADS = 2
ATTN_GROUP = ATTN_HEADS // ATTN_KV_HEADS
HEAD_DIM = 64
ATTN_WIDTH = ATTN_HEADS * HEAD_DIM
ATTN_BLOCK = 128
N_BUCKETS = 32
MAX_DISTANCE = 128
SSD_HEADS = 8
SSD_HEAD_DIM = 64
SSD_INNER = SSD_HEADS * SSD_HEAD_DIM
SSD_STATE = 128
SSD_GROUPS = 2
SSD_HEADS_PER_GROUP = SSD_HEADS // SSD_GROUPS
SSD_CONV = 4
SSD_CHUNK = 128
PEER_HEADS = 8
N_KEYS = 128
N_EXPERTS = N_KEYS * N_KEYS
PEER_HALF = 128
PEER_TOPK = 16
EPS = 1e-6

KV_COLS = ATTN_KV_HEADS * HEAD_DIM
XBC_COLS = SSD_INNER + 2 * SSD_GROUPS * SSD_STATE
OFF_K = ATTN_WIDTH
OFF_V = OFF_K + KV_COLS
OFF_Z = OFF_V + KV_COLS
OFF_XBC = OFF_Z + SSD_INNER
OFF_DT = OFF_XBC + XBC_COLS
LANES = 128
SUBLANES = 8
DT_PAD = LANES
IN_COLS_PAD = OFF_DT + DT_PAD
CONV_HALO = SUBLANES

VMEM_LIMIT = 56 * 1024 * 1024

ATTN_BLOCKS_PER_STEP = 4
SSD_SEQS_PER_STEP = 2
INPROJ_TM = 1024
OUTPROJ_TM = 512
ROUTE_TM = SUBLANES * LANES
PEER_TM = 512
PEER_TE = 2048
PEER_SUB = 2048
PEER_LANE_CHUNK = 512
PEER_ROWS = 16


def _dot(a, b):
    return jnp.dot(a, b, preferred_element_type=F32)


def _dot_nt(a, b):
    return lax.dot_general(a, b, (((1,), (1,)), ((), ())), preferred_element_type=F32)


def _dot_tn(a, b):
    return lax.dot_general(a, b, (((0,), (0,)), ((), ())), preferred_element_type=F32)


def _rms(x, w):
    return x * lax.rsqrt(jnp.mean(x * x, axis=-1, keepdims=True) + EPS) * w


def _silu(x):
    return x / (1.0 + jnp.exp(-x))


def _softplus(x):
    return jnp.maximum(x, 0.0) + jnp.log1p(jnp.exp(-jnp.abs(x)))


def _split3(x):
    hi = x.astype(BF16)
    rest = x - hi.astype(F32)
    mid = rest.astype(BF16)
    return hi, mid, (rest - mid.astype(F32)).astype(BF16)


def _params(*sem):
    return pltpu.CompilerParams(dimension_semantics=sem, vmem_limit_bytes=VMEM_LIMIT)


def _inproj_kernel(x_ref, nw_ref, w_ref, q_ref, k_ref, v_ref, z_ref, xbc_ref, dt_ref):
    xb = _rms(x_ref[...], nw_ref[...]).astype(BF16)

    def proj(lo, hi):
        return _dot(xb, w_ref[:, lo:hi])

    q_ref[...] = (proj(0, OFF_K) * (HEAD_DIM ** -0.5)).astype(BF16)
    k_ref[...] = proj(OFF_K, OFF_V).astype(BF16)
    v_ref[...] = proj(OFF_V, OFF_Z).astype(BF16)
    z_ref[...] = proj(OFF_Z, OFF_XBC)
    xbc_ref[...] = proj(OFF_XBC, OFF_DT)
    dt_ref[...] = proj(OFF_DT, IN_COLS_PAD)


def _inproj(x2, norm_w, w_in_pad):
    t = x2.shape[0]
    tm = min(INPROJ_TM, t)
    row = lambda i: (i, 0)
    fixed = lambda i: (0, 0)
    return pl.pallas_call(
        _inproj_kernel,
        grid=(t // tm,),
        in_specs=[
            pl.BlockSpec((tm, D_MODEL), row),
            pl.BlockSpec((1, D_MODEL), fixed),
            pl.BlockSpec((D_MODEL, IN_COLS_PAD), fixed),
        ],
        out_specs=[
            pl.BlockSpec((tm, ATTN_WIDTH), row),
            pl.BlockSpec((tm, KV_COLS), row),
            pl.BlockSpec((tm, KV_COLS), row),
            pl.BlockSpec((tm, SSD_INNER), row),
            pl.BlockSpec((tm, XBC_COLS), row),
            pl.BlockSpec((tm, DT_PAD), row),
        ],
        out_shape=[
            jax.ShapeDtypeStruct((t, ATTN_WIDTH), BF16),
            jax.ShapeDtypeStruct((t, KV_COLS), BF16),
            jax.ShapeDtypeStruct((t, KV_COLS), BF16),
            jax.ShapeDtypeStruct((t, SSD_INNER), F32),
            jax.ShapeDtypeStruct((t, XBC_COLS), F32),
            jax.ShapeDtypeStruct((t, DT_PAD), F32),
        ],
        compiler_params=_params("parallel"),
        name="inproj",
    )(x2, norm_w, w_in_pad)


def _t5_bucket(dist):
    n = jnp.maximum(dist, 0)
    max_exact = N_BUCKETS // 2
    nf = jnp.maximum(n, 1).astype(F32)
    large = max_exact + (jnp.log(nf / max_exact) / math.log(MAX_DISTANCE / max_exact)
                         * (N_BUCKETS - max_exact)).astype(jnp.int32)
    large = jnp.minimum(large, N_BUCKETS - 1)
    return jnp.where(n < max_exact, n, large)


def _attn_kernel(bucket_ref, relb_ref, sink_ref, q_ref, kp_ref, kc_ref, vp_ref, vc_ref,
                 o_ref, bias_ref):
    blk = ATTN_BLOCK
    rows = ATTN_GROUP * blk
    n = pl.program_id(1)

    @pl.when((pl.program_id(0) == 0) & (n == 0))
    def _():
        bucket = bucket_ref[...]
        for h in range(ATTN_HEADS):
            acc = jnp.zeros((blk, blk), F32)
            for b in range(N_BUCKETS):
                acc = jnp.where(bucket == b, relb_ref[b, h], acc)
            hk, g = divmod(h, ATTN_GROUP)
            bias_ref[hk, g * blk:(g + 1) * blk, :] = acc

    qi = lax.broadcasted_iota(jnp.int32, (rows, blk), 0) & (blk - 1)
    kj = lax.broadcasted_iota(jnp.int32, (rows, blk), 1)
    own = kj <= qi
    for i in range(ATTN_BLOCKS_PER_STEP):
        cur = slice(i * blk, (i + 1) * blk)
        kprev, vprev = (kp_ref, vp_ref) if i == 0 else (kc_ref.at[(i - 1) * blk:i * blk],
                                                        vc_ref.at[(i - 1) * blk:i * blk])
        has_prev = (n > 0) if i == 0 else True
        for hk in range(ATTN_KV_HEADS):
            ksl = slice(hk * HEAD_DIM, (hk + 1) * HEAD_DIM)
            kband = jnp.concatenate([kprev[:, ksl], kc_ref[cur, ksl]], axis=0)
            vband = jnp.concatenate([vprev[:, ksl], vc_ref[cur, ksl]], axis=0)
            qg = jnp.concatenate(
                [q_ref[cur, (hk * ATTN_GROUP + g) * HEAD_DIM:(hk * ATTN_GROUP + g + 1) * HEAD_DIM]
                 for g in range(ATTN_GROUP)], axis=0)
            s2 = _dot_nt(qg, kband)
            s = jnp.where(own, s2[:, blk:], s2[:, :blk]) + bias_ref[hk]
            if i == 0:
                s = jnp.where(own | has_prev, s, -jnp.inf)
            sink = jnp.concatenate(
                [jnp.full((blk, blk), sink_ref[hk * ATTN_GROUP + g], F32)
                 for g in range(ATTN_GROUP)], axis=0)
            m = jnp.maximum(jnp.broadcast_to(jnp.max(s, axis=-1, keepdims=True), s.shape), sink)
            pb = jnp.exp(s - m).astype(BF16)
            zero = jnp.zeros_like(pb)
            pfull = jnp.concatenate([jnp.where(own, zero, pb), jnp.where(own, pb, zero)], axis=1)
            denom = _dot(pfull, jnp.ones((2 * blk, blk), BF16)) + jnp.exp(sink - m)
            o = _dot(pfull, vband) / denom[:, :HEAD_DIM]
            for g in range(ATTN_GROUP):
                h = hk * ATTN_GROUP + g
                o_ref[cur, h * HEAD_DIM:(h + 1) * HEAD_DIM] = o[g * blk:(g + 1) * blk].astype(BF16)


def _attention(q, k, v, bucket, rel_bias, sinks):
    b, s, _ = q.shape
    rows = ATTN_BLOCKS_PER_STEP * ATTN_BLOCK
    assert s % rows == 0
    cur = lambda i, j: (i, j, 0)
    prev = lambda i, j: (i, jnp.maximum(j * ATTN_BLOCKS_PER_STEP - 1, 0), 0)
    smem = pl.BlockSpec(memory_space=pltpu.SMEM)
    return pl.pallas_call(
        _attn_kernel,
        grid=(b, s // rows),
        in_specs=[
            pl.BlockSpec((ATTN_BLOCK, ATTN_BLOCK), lambda i, j: (0, 0)),
            smem, smem,
            pl.BlockSpec((None, rows, ATTN_WIDTH), cur),
            pl.BlockSpec((None, ATTN_BLOCK, KV_COLS), prev),
            pl.BlockSpec((None, rows, KV_COLS), cur),
            pl.BlockSpec((None, ATTN_BLOCK, KV_COLS), prev),
            pl.BlockSpec((None, rows, KV_COLS), cur),
        ],
        out_specs=pl.BlockSpec((None, rows, ATTN_WIDTH), cur),
        out_shape=jax.ShapeDtypeStruct((b, s, ATTN_WIDTH), BF16),
        scratch_shapes=[pltpu.VMEM((ATTN_KV_HEADS, ATTN_GROUP * ATTN_BLOCK, ATTN_BLOCK), F32)],
        compiler_params=_params("arbitrary", "arbitrary"),
        name="swa",
    )(bucket, rel_bias, sinks, q, k, k, v, v)


def _ssd_chunk(xbc_ref, z_ref, dt_ref, convw_ref, convb_ref, dtb_ref, alog_ref, dskip_ref,
               nw_ref, expand_ref, tri_ref, o_ref, xpad_ref, state_ref):
    L = SSD_CHUNK
    P = SSD_HEAD_DIM
    xpad_ref[CONV_HALO:CONV_HALO + L, :] = xbc_ref[...]
    conv = convb_ref[...]
    for j in range(SSD_CONV):
        off = CONV_HALO - (SSD_CONV - 1) + j
        conv = conv + convw_ref[j:j + 1, :] * xpad_ref[off:off + L, :]
    xbc = _silu(conv)
    xs = xbc[:, :SSD_INNER]
    bm = xbc[:, SSD_INNER:SSD_INNER + SSD_GROUPS * SSD_STATE].astype(BF16)
    cm = xbc[:, SSD_INNER + SSD_GROUPS * SSD_STATE:].astype(BF16)

    expand = expand_ref[...]
    dt = _softplus(dt_ref[...] + dtb_ref[...])
    a = dt * (-jnp.exp(alog_ref[...]))
    a_cum = sum(_dot(tri_ref[...], term) for term in _split3(a))
    a_last = a_cum[L - 1:L, :]
    stacked = jnp.concatenate([dt, jnp.exp(a_last - a_cum), jnp.exp(a_cum)], axis=0)
    wide = sum(_dot(term, expand) for term in _split3(stacked))
    x_dt = xs * wide[0:L]
    x_end = (x_dt * wide[L:2 * L]).astype(BF16)
    dec_start = wide[2 * L:3 * L]
    x_dt = x_dt.astype(BF16)
    a_cum_t = a_cum.T

    li = lax.broadcasted_iota(jnp.int32, (L, L), 0)
    si = lax.broadcasted_iota(jnp.int32, (L, L), 1)
    causal = si <= li
    ys = []
    for g in range(SSD_GROUPS):
        bg = bm[:, g * SSD_STATE:(g + 1) * SSD_STATE]
        cg = cm[:, g * SSD_STATE:(g + 1) * SSD_STATE]
        cb = _dot_nt(cg, bg)
        for r in range(SSD_HEADS_PER_GROUP):
            h = g * SSD_HEADS_PER_GROUP + r
            sl = slice(h * P, (h + 1) * P)
            seg = a_cum[:, h:h + 1] - a_cum_t[h:h + 1, :]
            lmat = jnp.where(causal, jnp.exp(jnp.where(causal, seg, 0.0)), 0.0)
            h_prev = state_ref[sl, :]
            y_diag = _dot((cb * lmat).astype(BF16), x_dt[:, sl])
            y_off = _dot_nt(cg, h_prev.astype(BF16)) * dec_start[:, sl]
            ys.append(y_diag + y_off)
            chunk_decay = jnp.exp(a_cum_t[h:h + 1, L - 1:L])
            state_ref[sl, :] = h_prev * chunk_decay + _dot_tn(x_end[:, sl], bg)
    y = jnp.concatenate(ys, axis=1) + dskip_ref[...] * xs
    y = y * _silu(z_ref[...])
    o_ref[...] = _rms(y, nw_ref[...]).astype(BF16)


def _ssd_kernel(xbc_ref, z_ref, dt_ref, convw_ref, convb_ref, dtb_ref, alog_ref, dskip_ref,
                nw_ref, expand_ref, tri_ref, o_ref, xpad_ref, state_ref):
    c = pl.program_id(1)

    @pl.when(c == 0)
    def _():
        xpad_ref[:, 0:CONV_HALO, :] = jnp.zeros((SSD_SEQS_PER_STEP, CONV_HALO, XBC_COLS), F32)
        state_ref[...] = jnp.zeros_like(state_ref)

    @pl.when(c > 0)
    def _():
        xpad_ref[:, 0:CONV_HALO, :] = xpad_ref[:, SSD_CHUNK:SSD_CHUNK + CONV_HALO, :]

    for b in range(SSD_SEQS_PER_STEP):
        _ssd_chunk(xbc_ref.at[b], z_ref.at[b], dt_ref.at[b], convw_ref, convb_ref, dtb_ref,
                   alog_ref, dskip_ref, nw_ref, expand_ref, tri_ref, o_ref.at[b],
                   xpad_ref.at[b], state_ref.at[b])


def _ssd(xbc, z, dt, conv_w, conv_b, dt_bias_pad, a_log_pad, d_skip_wide, norm_w, expand, tri):
    b, s, _ = xbc.shape
    nb = SSD_SEQS_PER_STEP
    assert b % nb == 0
    nc = s // SSD_CHUNK
    cur = lambda i, j: (i, j, 0)
    fixed = lambda i, j: (0, 0)
    full = lambda shape: pl.BlockSpec(shape, fixed)
    return pl.pallas_call(
        _ssd_kernel,
        grid=(b // nb, nc),
        in_specs=[
            pl.BlockSpec((nb, SSD_CHUNK, XBC_COLS), cur),
            pl.BlockSpec((nb, SSD_CHUNK, SSD_INNER), cur),
            pl.BlockSpec((nb, SSD_CHUNK, DT_PAD), cur),
            full((SSD_CONV, XBC_COLS)), full((1, XBC_COLS)), full((1, DT_PAD)), full((1, DT_PAD)),
            full((1, SSD_INNER)), full((1, SSD_INNER)), full((DT_PAD, SSD_INNER)),
            full((SSD_CHUNK, SSD_CHUNK)),
        ],
        out_specs=pl.BlockSpec((nb, SSD_CHUNK, SSD_INNER), cur),
        out_shape=jax.ShapeDtypeStruct((b, s, SSD_INNER), BF16),
        scratch_shapes=[
            pltpu.VMEM((nb, CONV_HALO + SSD_CHUNK, XBC_COLS), F32),
            pltpu.VMEM((nb, SSD_INNER, SSD_STATE), F32),
        ],
        compiler_params=_params("arbitrary", "arbitrary"),
        name="ssd",
    )(xbc, z, dt, conv_w, conv_b, dt_bias_pad, a_log_pad, d_skip_wide, norm_w, expand, tri)


def _outproj_kernel(x_ref, attn_ref, ssd_ref, wo_ref, nw_ref, wq_ref, h1_ref, xnt_ref, q_ref):
    mix = (_dot(attn_ref[...], wo_ref[0:ATTN_WIDTH, :])
           + _dot(ssd_ref[...], wo_ref[ATTN_WIDTH:ATTN_WIDTH + SSD_INNER, :]))
    h1 = x_ref[...] + mix
    h1_ref[...] = h1
    xn = _rms(h1, nw_ref[...])
    xnt_ref[...] = pltpu.bitcast(xn.T.astype(BF16), jnp.uint32)
    qf = _dot(xn.astype(BF16), wq_ref[...])
    for j in range(2 * PEER_HEADS):
        q_ref[j] = qf[:, j * PEER_HALF:(j + 1) * PEER_HALF].astype(BF16)


def _outproj(x2, attn, ssd, w_out, norm_w, w_query):
    t = x2.shape[0]
    tm = min(OUTPROJ_TM, t)
    row = lambda i: (i, 0)
    fixed = lambda i: (0, 0)
    nq = 2 * PEER_HEADS
    return pl.pallas_call(
        _outproj_kernel,
        grid=(t // tm,),
        in_specs=[
            pl.BlockSpec((tm, D_MODEL), row),
            pl.BlockSpec((tm, ATTN_WIDTH), row),
            pl.BlockSpec((tm, SSD_INNER), row),
            pl.BlockSpec((ATTN_WIDTH + SSD_INNER, D_MODEL), fixed),
            pl.BlockSpec((1, D_MODEL), fixed),
            pl.BlockSpec((D_MODEL, nq * PEER_HALF), fixed),
        ],
        out_specs=[
            pl.BlockSpec((tm, D_MODEL), row),
            pl.BlockSpec((D_MODEL // 2, tm), lambda i: (0, i)),
            pl.BlockSpec((nq, tm, PEER_HALF), lambda i: (0, i, 0)),
        ],
        out_shape=[
            jax.ShapeDtypeStruct((t, D_MODEL), F32),
            jax.ShapeDtypeStruct((D_MODEL // 2, t), jnp.uint32),
            jax.ShapeDtypeStruct((nq, t, PEER_HALF), BF16),
        ],
        compiler_params=_params("parallel"),
        name="outproj",
    )(x2, attn, ssd, w_out, norm_w, w_query)


def _ce(x, hi, lo):
    a, b = x[hi], x[lo]
    if b is None:
        return
    if a is None:
        x[hi], x[lo] = b, None
        return
    x[hi], x[lo] = jnp.maximum(a, b), jnp.minimum(a, b)


def _bitonic_merge_desc(x):
    n = len(x)
    j = n // 2
    while j >= 1:
        for i in range(n):
            if i & j == 0:
                _ce(x, i, i | j)
        j //= 2


def _sort_desc(x):
    n = len(x)
    k = 2
    while k <= n:
        j = k // 2
        while j >= 1:
            for i in range(n):
                l = i ^ j
                if l > i:
                    if i & k == 0:
                        _ce(x, i, l)
                    else:
                        _ce(x, l, i)
            j //= 2
        k *= 2


def _merge_top(a, b, sort=True):
    n = len(a)
    out = []
    for i in range(n):
        u, v = a[i], b[n - 1 - i]
        out.append(v if u is None else u if v is None else jnp.maximum(u, v))
    if sort:
        _bitonic_merge_desc(out)
    return out


def _key_rows(k):
    return slice(k * SUBLANES, (k + 1) * SUBLANES)


def _top16_sorted(s_ref, lanes):
    groups = []
    for g in range(N_KEYS // PEER_TOPK):
        x = [s_ref[_key_rows(g * PEER_TOPK + i), lanes] for i in range(PEER_TOPK)]
        _sort_desc(x)
        groups.append(x)
    while len(groups) > 1:
        groups = [_merge_top(groups[i], groups[i + 1]) for i in range(0, len(groups), 2)]
    return groups[0]


def _route_select(h, lanes, s1_ref, s2_ref, r2_ref, e2_ref, c1_ref, e1_ref,
                  r2s_ref, e2s_ref, c1s_ref, e1s_ref):
    v1 = _top16_sorted(s1_ref, lanes)
    v2 = _top16_sorted(s2_ref, lanes)

    lists = []
    for a in range(PEER_TOPK):
        n_b = PEER_TOPK // (a + 1)
        lists.append([v1[a] + v2[b] if b < n_b else None for b in range(PEER_TOPK)])
    while len(lists) > 2:
        lists = [_merge_top(lists[i], lists[i + 1]) for i in range(0, len(lists), 2)]
    top = _merge_top(lists[0], lists[1], sort=False)
    tau = top[0]
    for x in top[1:]:
        tau = jnp.minimum(tau, x)

    e1_top = [jnp.exp(v - v1[0]) for v in v1]
    e2_top = [jnp.exp(v - v2[0]) for v in v2]
    zsum = jnp.zeros_like(tau)
    cnt = []
    for a in range(PEER_TOPK):
        n_a = jnp.zeros_like(tau)
        w_a = jnp.zeros_like(tau)
        for b in range(PEER_TOPK // (a + 1)):
            sel = (v1[a] + v2[b]) >= tau
            n_a = n_a + jnp.where(sel, 1.0, 0.0)
            w_a = w_a + jnp.where(sel, e2_top[b], 0.0)
        cnt.append(n_a)
        zsum = zsum + e1_top[a] * w_a
    half_inv_z = 0.5 / zsum

    for k in range(N_KEYS):
        rows = _key_rows(k)
        s1 = s1_ref[rows, lanes]
        s2 = s2_ref[rows, lanes]
        c1 = jnp.zeros_like(tau)
        for a in reversed(range(PEER_TOPK)):
            c1 = jnp.where(s1 >= v1[a], cnt[a], c1)
        r2 = jnp.zeros_like(tau)
        for b in range(PEER_TOPK):
            r2 = jnp.where(v2[b] > s2, float(b + 1), r2)
        c1s_ref[rows, :] = c1
        r2s_ref[rows, :] = r2
        e1s_ref[rows, :] = jnp.exp(s1 - v1[0]) * half_inv_z
        e2s_ref[rows, :] = jnp.exp(s2 - v2[0])

    for j in range(SUBLANES):
        cols = slice(j * LANES, (j + 1) * LANES)
        take = pl.ds(j, N_KEYS, stride=SUBLANES)
        r2_ref[h, :, cols] = r2s_ref[take, :].astype(BF16)
        e2_ref[h, :, cols] = e2s_ref[take, :].astype(BF16)
        c1_ref[h, :, cols] = c1s_ref[take, :]
        e1_ref[h, :, cols] = e1s_ref[take, :]


def _route_kernel(q_ref, k1_ref, k2_ref, r2_ref, e2_ref, c1_ref, e1_ref,
                  s1_ref, s2_ref, r2s_ref, e2s_ref, c1s_ref, e1s_ref):
    def head_pair(i, carry):
        for half, (k_ref, s_ref) in enumerate(((k1_ref, s1_ref), (k2_ref, s2_ref))):
            qcat = jnp.concatenate(
                [jnp.concatenate([q_ref[2 * (2 * i + hh) + half, j * LANES:(j + 1) * LANES, :]
                                  for j in range(SUBLANES)], axis=1)
                 for hh in range(2)], axis=0)
            s_ref[...] = _dot_nt(k_ref[...], qcat)
        for hh in range(2):
            _route_select(2 * i + hh, slice(hh * LANES, (hh + 1) * LANES), s1_ref, s2_ref,
                          r2_ref, e2_ref, c1_ref, e1_ref, r2s_ref, e2s_ref, c1s_ref, e1s_ref)
        return carry

    lax.fori_loop(0, PEER_HEADS // 2, head_pair, 0)


def _expand_keys(keys):
    eye = jnp.eye(SUBLANES, dtype=keys.dtype)
    return jnp.einsum("kd,jc->kjcd", keys, eye).reshape(N_KEYS * SUBLANES, SUBLANES * PEER_HALF)


def _route(q, keys1, keys2):
    nq, t, _ = q.shape
    tm = ROUTE_TM
    assert t % tm == 0
    fixed = lambda i: (0, 0)
    out_spec = pl.BlockSpec((PEER_HEADS, N_KEYS, tm), lambda i: (0, 0, i))
    shape = (PEER_HEADS, N_KEYS, t)
    key_spec = pl.BlockSpec((N_KEYS * SUBLANES, SUBLANES * PEER_HALF), fixed)
    stage = pltpu.VMEM((N_KEYS * SUBLANES, LANES), F32)
    scores = pltpu.VMEM((N_KEYS * SUBLANES, 2 * LANES), F32)
    return pl.pallas_call(
        _route_kernel,
        grid=(t // tm,),
        in_specs=[pl.BlockSpec((nq, tm, PEER_HALF), lambda i: (0, i, 0)), key_spec, key_spec],
        out_specs=[out_spec] * 4,
        out_shape=[jax.ShapeDtypeStruct(shape, BF16)] * 2 + [jax.ShapeDtypeStruct(shape, F32)] * 2,
        scratch_shapes=[scores] * 2 + [stage] * 4,
        compiler_params=_params("parallel"),
        name="route",
    )(q, _expand_keys(keys1), _expand_keys(keys2))


def _peer_activations(pre_ref, act_ref, r2_ref, e2_ref, c1_ref, e1_ref, sub):
    tm = pre_ref.shape[1]
    chunks = N_KEYS // PEER_ROWS
    shape = (PEER_ROWS, PEER_LANE_CHUNK)
    zero = jnp.zeros(shape, BF16)
    for a in range(PEER_SUB // N_KEYS):
        key1 = sub * (PEER_SUB // N_KEYS) + a
        for lc in range(tm // PEER_LANE_CHUNK):
            lanes = slice(lc * PEER_LANE_CHUNK, (lc + 1) * PEER_LANE_CHUNK)
            gates = [zero] * chunks
            for h in range(PEER_HEADS):
                cnt = jnp.broadcast_to(c1_ref[h, key1:key1 + 1, lanes], shape).astype(BF16)
                e1 = jnp.broadcast_to(e1_ref[h, key1:key1 + 1, lanes], shape).astype(BF16)
                for c in range(chunks):
                    rows = slice(c * PEER_ROWS, (c + 1) * PEER_ROWS)
                    sel = r2_ref[h, rows, lanes] < cnt
                    gates[c] = gates[c] + jnp.where(sel, e2_ref[h, rows, lanes], zero) * e1
            for c in range(chunks):
                rows = slice(a * N_KEYS + c * PEER_ROWS, a * N_KEYS + (c + 1) * PEER_ROWS)
                p = pre_ref[rows, lanes]
                gelu2 = p + p * lax.erf(p * (2.0 ** -0.5))
                act_ref[rows, lanes] = gelu2.astype(BF16) * gates[c]


def _peer_kernel(xnt_ref, d_ref, ut_ref, r2_ref, e2_ref, c1_ref, e1_ref, h1_ref, nw_ref,
                 o_ref, acc_ref, pre_ref, act_ref):
    j = pl.program_id(1)
    nsub = ut_ref.shape[1] // PEER_SUB

    @pl.when(j == 0)
    def _():
        acc_ref[...] = jnp.zeros_like(acc_ref)

    def down(sub):
        d_sub = d_ref[sub * (PEER_SUB // 2):(sub + 1) * (PEER_SUB // 2), :]
        pre_ref[sub % 2] = _dot(pltpu.bitcast(d_sub, BF16), pltpu.bitcast(xnt_ref[...], BF16))

    down(0)
    for sub in range(nsub):
        if sub + 1 < nsub:
            down(sub + 1)
        _peer_activations(pre_ref.at[sub % 2], act_ref.at[sub % 2], r2_ref, e2_ref, c1_ref,
                          e1_ref, sub)
        ut_sub = ut_ref[:, sub * PEER_SUB:(sub + 1) * PEER_SUB]
        acc_ref[...] += _dot(pltpu.bitcast(ut_sub, BF16), act_ref[sub % 2])

    @pl.when(j == pl.num_programs(1) - 1)
    def _():
        h2 = h1_ref[...] + acc_ref[...].T
        o_ref[...] = _rms(h2, nw_ref[...])


def _pack_kernel(w_ref, o_ref, *, transpose):
    w = w_ref[...].T if transpose else w_ref[...]
    o_ref[...] = pltpu.bitcast(w.astype(BF16), jnp.uint32)


def _pack_experts(w, transpose):
    tiles = N_EXPERTS // PEER_TE
    if transpose:
        out_spec = pl.BlockSpec((None, D_MODEL // 2, PEER_TE), lambda i: (i, 0, 0))
        out_shape = jax.ShapeDtypeStruct((tiles, D_MODEL // 2, PEER_TE), jnp.uint32)
    else:
        out_spec = pl.BlockSpec((PEER_TE // 2, D_MODEL), lambda i: (i, 0))
        out_shape = jax.ShapeDtypeStruct((N_EXPERTS // 2, D_MODEL), jnp.uint32)
    return pl.pallas_call(
        functools.partial(_pack_kernel, transpose=transpose),
        grid=(tiles,),
        in_specs=[pl.BlockSpec((PEER_TE, D_MODEL), lambda i: (i, 0))],
        out_specs=out_spec,
        out_shape=out_shape,
        compiler_params=_params("parallel"),
        name="pack_up" if transpose else "pack_down",
    )(w)


def _peer(xnt, down, up_t, r2, e2, c1, e1, h1, norm_w):
    t = h1.shape[0]
    tm = min(PEER_TM, t)
    te = PEER_TE
    key2_spec = pl.BlockSpec((PEER_HEADS, N_KEYS, tm), lambda i, j: (0, 0, i))
    key1_spec = pl.BlockSpec((PEER_HEADS, te // N_KEYS, tm), lambda i, j: (0, j, i))
    return pl.pallas_call(
        _peer_kernel,
        grid=(t // tm, N_EXPERTS // te),
        in_specs=[
            pl.BlockSpec((D_MODEL // 2, tm), lambda i, j: (0, i)),
            pl.BlockSpec((te // 2, D_MODEL), lambda i, j: (j, 0)),
            pl.BlockSpec((None, D_MODEL // 2, te), lambda i, j: (j, 0, 0)),
            key2_spec, key2_spec, key1_spec, key1_spec,
            pl.BlockSpec((tm, D_MODEL), lambda i, j: (i, 0)),
            pl.BlockSpec((1, D_MODEL), lambda i, j: (0, 0)),
        ],
        out_specs=pl.BlockSpec((tm, D_MODEL), lambda i, j: (i, 0)),
        out_shape=jax.ShapeDtypeStruct((t, D_MODEL), F32),
        scratch_shapes=[
            pltpu.VMEM((D_MODEL, tm), F32),
            pltpu.VMEM((2, PEER_SUB, tm), F32),
            pltpu.VMEM((2, PEER_SUB, tm), BF16),
        ],
        compiler_params=_params("parallel", "arbitrary"),
        name="peer",
    )(xnt, down, up_t, r2, e2, c1, e1, h1, norm_w)


def _pad_lanes(v, width):
    return jnp.pad(v.astype(F32), (0, width - v.shape[0])).reshape(1, width)


def _layer(x, norm_mix, w_in, conv_w, conv_b, dt_bias, a_log, d_skip, ssd_norm_w, attn_sinks,
           w_out, rel_bias, norm_ffn, w_query, sub_keys1, sub_keys2, expert_down, expert_up,
           out_norm):
    b, s, d = x.shape
    t = b * s
    x2 = x.reshape(t, d)

    w_in_pad = jnp.pad(w_in, ((0, 0), (0, IN_COLS_PAD - w_in.shape[1]))).astype(BF16)
    q, k, v, z, xbc, dt = _inproj(x2, norm_mix.reshape(1, d), w_in_pad)

    qi = jnp.arange(ATTN_BLOCK)[:, None]
    kj = jnp.arange(ATTN_BLOCK)[None, :]
    bucket = _t5_bucket(jnp.where(kj <= qi, qi - kj, qi + ATTN_BLOCK - kj)).astype(jnp.int32)
    attn = _attention(q.reshape(b, s, -1), k.reshape(b, s, -1), v.reshape(b, s, -1),
                      bucket, rel_bias.astype(F32), attn_sinks.astype(F32))

    lane = jnp.arange(SSD_INNER)[None, :] // SSD_HEAD_DIM
    expand = (jnp.arange(DT_PAD)[:, None] == lane).astype(BF16)
    tri = (jnp.arange(SSD_CHUNK)[None, :] <= jnp.arange(SSD_CHUNK)[:, None]).astype(BF16)
    ssd = _ssd(xbc.reshape(b, s, -1), z.reshape(b, s, -1), dt.reshape(b, s, -1),
               conv_w, conv_b.reshape(1, -1), _pad_lanes(dt_bias, DT_PAD),
               _pad_lanes(a_log, DT_PAD), jnp.repeat(d_skip.astype(F32), SSD_HEAD_DIM).reshape(1, -1),
               ssd_norm_w.reshape(1, -1), expand, tri)

    h1, xnt, pq = _outproj(x2, attn.reshape(t, -1), ssd.reshape(t, -1), w_out.astype(BF16),
                           norm_ffn.reshape(1, d), w_query.astype(BF16))
    r2, e2, c1, e1 = _route(pq, sub_keys1.astype(BF16), sub_keys2.astype(BF16))
    out = _peer(xnt, _pack_experts(expert_down, False), _pack_experts(expert_up, True),
                r2, e2, c1, e1, h1, out_norm.reshape(1, d))
    return out.reshape(b, s, d)


@jax.jit
def kernel(x, norm_mix, w_in, conv_w, conv_b, dt_bias, a_log, d_skip, ssd_norm_w, attn_sinks,
           w_out, rel_bias, norm_ffn, w_query, sub_keys1, sub_keys2, expert_down, expert_up,
           norm_final):
    assert norm_mix.shape[0] == 1, "single-layer block"
    return _layer(x, norm_mix[0], w_in[0], conv_w[0], conv_b[0], dt_bias[0], a_log[0], d_skip[0],
                  ssd_norm_w[0], attn_sinks[0], w_out[0], rel_bias, norm_ffn[0], w_query[0],
                  sub_keys1[0], sub_keys2[0], expert_down[0], expert_up[0], norm_final)
```

```python
import functools
import math

import jax
import jax.numpy as jnp
from jax import lax
from jax.experimental import pallas as pl
from jax.experimental.pallas import tpu as pltpu

F32 = jnp.float32
BF16 = jnp.bfloat16

D_MODEL = 1024
ATTN_HEADS = 8
ATTN_KV_HEADS = 2
ATTN_GROUP = ATTN_HEADS // ATTN_KV_HEADS
HEAD_DIM = 64
ATTN_WIDTH = ATTN_HEADS * HEAD_DIM
ATTN_BLOCK = 128
N_BUCKETS = 32
MAX_DISTANCE = 128
SSD_HEADS = 8
SSD_HEAD_DIM = 64
SSD_INNER = SSD_HEADS * SSD_HEAD_DIM
SSD_STATE = 128
SSD_GROUPS = 2
SSD_HEADS_PER_GROUP = SSD_HEADS // SSD_GROUPS
SSD_CONV = 4
SSD_CHUNK = 128
PEER_HEADS = 8
N_KEYS = 128
N_EXPERTS = N_KEYS * N_KEYS
PEER_HALF = 128
PEER_TOPK = 16
EPS = 1e-6

KV_COLS = ATTN_KV_HEADS * HEAD_DIM
XBC_COLS = SSD_INNER + 2 * SSD_GROUPS * SSD_STATE
OFF_K = ATTN_WIDTH
OFF_V = OFF_K + KV_COLS
OFF_Z = OFF_V + KV_COLS
OFF_XBC = OFF_Z + SSD_INNER
OFF_DT = OFF_XBC + XBC_COLS
LANES = 128
SUBLANES = 8
DT_PAD = LANES
IN_COLS_PAD = OFF_DT + DT_PAD
CONV_HALO = SUBLANES

VMEM_LIMIT = 56 * 1024 * 1024

ATTN_BLOCKS_PER_STEP = 4
SSD_SEQS_PER_STEP = 2
INPROJ_TM = 1024
OUTPROJ_TM = 512
ROUTE_TM = SUBLANES * LANES
PEER_TM = 512
PEER_TE = 2048
PEER_SUBS = (512, 1024, 512)
PEER_SUB = max(PEER_SUBS)
PEER_LANE_CHUNK = 512
PEER_ROWS = 16


def _dot(a, b):
    return jnp.dot(a, b, preferred_element_type=F32)


def _dot_nt(a, b):
    return lax.dot_general(a, b, (((1,), (1,)), ((), ())), preferred_element_type=F32)


def _dot_tn(a, b):
    return lax.dot_general(a, b, (((0,), (0,)), ((), ())), preferred_element_type=F32)


def _rms(x, w):
    return x * lax.rsqrt(jnp.mean(x * x, axis=-1, keepdims=True) + EPS) * w


def _silu(x):
    return x / (1.0 + jnp.exp(-x))


def _softplus(x):
    return jnp.maximum(x, 0.0) + jnp.log1p(jnp.exp(-jnp.abs(x)))


def _split3(x):
    hi = x.astype(BF16)
    rest = x - hi.astype(F32)
    mid = rest.astype(BF16)
    return hi, mid, (rest - mid.astype(F32)).astype(BF16)


def _params(*sem):
    return pltpu.CompilerParams(dimension_semantics=sem, vmem_limit_bytes=VMEM_LIMIT)


def _inproj_kernel(x_ref, nw_ref, w_ref, q_ref, k_ref, v_ref, z_ref, xbc_ref, dt_ref):
    xb = _rms(x_ref[...], nw_ref[...]).astype(BF16)

    def proj(lo, hi):
        return _dot(xb, w_ref[:, lo:hi])

    q_ref[...] = (proj(0, OFF_K) * (HEAD_DIM ** -0.5)).astype(BF16)
    k_ref[...] = proj(OFF_K, OFF_V).astype(BF16)
    v_ref[...] = proj(OFF_V, OFF_Z).astype(BF16)
    z_ref[...] = proj(OFF_Z, OFF_XBC)
    xbc_ref[...] = proj(OFF_XBC, OFF_DT)
    dt_ref[...] = proj(OFF_DT, IN_COLS_PAD)


def _inproj(x2, norm_w, w_in_pad):
    t = x2.shape[0]
    tm = min(INPROJ_TM, t)
    row = lambda i: (i, 0)
    fixed = lambda i: (0, 0)
    return pl.pallas_call(
        _inproj_kernel,
        grid=(t // tm,),
        in_specs=[
            pl.BlockSpec((tm, D_MODEL), row),
            pl.BlockSpec((1, D_MODEL), fixed),
            pl.BlockSpec((D_MODEL, IN_COLS_PAD), fixed),
        ],
        out_specs=[
            pl.BlockSpec((tm, ATTN_WIDTH), row),
            pl.BlockSpec((tm, KV_COLS), row),
            pl.BlockSpec((tm, KV_COLS), row),
            pl.BlockSpec((tm, SSD_INNER), row),
            pl.BlockSpec((tm, XBC_COLS), row),
            pl.BlockSpec((tm, DT_PAD), row),
        ],
        out_shape=[
            jax.ShapeDtypeStruct((t, ATTN_WIDTH), BF16),
            jax.ShapeDtypeStruct((t, KV_COLS), BF16),
            jax.ShapeDtypeStruct((t, KV_COLS), BF16),
            jax.ShapeDtypeStruct((t, SSD_INNER), F32),
            jax.ShapeDtypeStruct((t, XBC_COLS), F32),
            jax.ShapeDtypeStruct((t, DT_PAD), F32),
        ],
        compiler_params=_params("parallel"),
        name="inproj",
    )(x2, norm_w, w_in_pad)


def _t5_bucket(dist):
    n = jnp.maximum(dist, 0)
    max_exact = N_BUCKETS // 2
    nf = jnp.maximum(n, 1).astype(F32)
    large = max_exact + (jnp.log(nf / max_exact) / math.log(MAX_DISTANCE / max_exact)
                         * (N_BUCKETS - max_exact)).astype(jnp.int32)
    large = jnp.minimum(large, N_BUCKETS - 1)
    return jnp.where(n < max_exact, n, large)


def _attn_kernel(bucket_ref, relb_ref, sink_ref, q_ref, kp_ref, kc_ref, vp_ref, vc_ref,
                 o_ref, bias_ref):
    blk = ATTN_BLOCK
    rows = ATTN_GROUP * blk
    n = pl.program_id(1)

    @pl.when((pl.program_id(0) == 0) & (n == 0))
    def _():
        bucket = bucket_ref[...]
        for h in range(ATTN_HEADS):
            acc = jnp.zeros((blk, blk), F32)
            for b in range(N_BUCKETS):
                acc = jnp.where(bucket == b, relb_ref[b, h], acc)
            hk, g = divmod(h, ATTN_GROUP)
            bias_ref[hk, g * blk:(g + 1) * blk, :] = acc

    qi = lax.broadcasted_iota(jnp.int32, (rows, blk), 0) & (blk - 1)
    kj = lax.broadcasted_iota(jnp.int32, (rows, blk), 1)
    own = kj <= qi
    for i in range(ATTN_BLOCKS_PER_STEP):
        cur = slice(i * blk, (i + 1) * blk)
        kprev, vprev = (kp_ref, vp_ref) if i == 0 else (kc_ref.at[(i - 1) * blk:i * blk],
                                                        vc_ref.at[(i - 1) * blk:i * blk])
        has_prev = (n > 0) if i == 0 else True
        for hk in range(ATTN_KV_HEADS):
            ksl = slice(hk * HEAD_DIM, (hk + 1) * HEAD_DIM)
            kband = jnp.concatenate([kprev[:, ksl], kc_ref[cur, ksl]], axis=0)
            vband = jnp.concatenate([vprev[:, ksl], vc_ref[cur, ksl]], axis=0)
            qg = jnp.concatenate(
                [q_ref[cur, (hk * ATTN_GROUP + g) * HEAD_DIM:(hk * ATTN_GROUP + g + 1) * HEAD_DIM]
                 for g in range(ATTN_GROUP)], axis=0)
            s2 = _dot_nt(qg, kband)
            s = jnp.where(own, s2[:, blk:], s2[:, :blk]) + bias_ref[hk]
            if i == 0:
                s = jnp.where(own | has_prev, s, -jnp.inf)
            sink = jnp.concatenate(
                [jnp.full((blk, blk), sink_ref[hk * ATTN_GROUP + g], F32)
                 for g in range(ATTN_GROUP)], axis=0)
            m = jnp.maximum(jnp.broadcast_to(jnp.max(s, axis=-1, keepdims=True), s.shape), sink)
            pb = jnp.exp(s - m).astype(BF16)
            zero = jnp.zeros_like(pb)
            pfull = jnp.concatenate([jnp.where(own, zero, pb), jnp.where(own, pb, zero)], axis=1)
            denom = _dot(pfull, jnp.ones((2 * blk, blk), BF16)) + jnp.exp(sink - m)
            o = _dot(pfull, vband) / denom[:, :HEAD_DIM]
            for g in range(ATTN_GROUP):
                h = hk * ATTN_GROUP + g
                o_ref[cur, h * HEAD_DIM:(h + 1) * HEAD_DIM] = o[g * blk:(g + 1) * blk].astype(BF16)


def _attention(q, k, v, bucket, rel_bias, sinks):
    b, s, _ = q.shape
    rows = ATTN_BLOCKS_PER_STEP * ATTN_BLOCK
    assert s % rows == 0
    cur = lambda i, j: (i, j, 0)
    prev = lambda i, j: (i, jnp.maximum(j * ATTN_BLOCKS_PER_STEP - 1, 0), 0)
    smem = pl.BlockSpec(memory_space=pltpu.SMEM)
    return pl.pallas_call(
        _attn_kernel,
        grid=(b, s // rows),
        in_specs=[
            pl.BlockSpec((ATTN_BLOCK, ATTN_BLOCK), lambda i, j: (0, 0)),
            smem, smem,
            pl.BlockSpec((None, rows, ATTN_WIDTH), cur),
            pl.BlockSpec((None, ATTN_BLOCK, KV_COLS), prev),
            pl.BlockSpec((None, rows, KV_COLS), cur),
            pl.BlockSpec((None, ATTN_BLOCK, KV_COLS), prev),
            pl.BlockSpec((None, rows, KV_COLS), cur),
        ],
        out_specs=pl.BlockSpec((None, rows, ATTN_WIDTH), cur),
        out_shape=jax.ShapeDtypeStruct((b, s, ATTN_WIDTH), BF16),
        scratch_shapes=[pltpu.VMEM((ATTN_KV_HEADS, ATTN_GROUP * ATTN_BLOCK, ATTN_BLOCK), F32)],
        compiler_params=_params("arbitrary", "arbitrary"),
        name="swa",
    )(bucket, rel_bias, sinks, q, k, k, v, v)


def _ssd_chunk(xbc_ref, z_ref, dt_ref, convw_ref, convb_ref, dtb_ref, alog_ref, dskip_ref,
               nw_ref, expand_ref, tri_ref, o_ref, xpad_ref, state_ref):
    L = SSD_CHUNK
    P = SSD_HEAD_DIM
    xpad_ref[CONV_HALO:CONV_HALO + L, :] = xbc_ref[...]
    conv = convb_ref[...]
    for j in range(SSD_CONV):
        off = CONV_HALO - (SSD_CONV - 1) + j
        conv = conv + convw_ref[j:j + 1, :] * xpad_ref[off:off + L, :]
    xbc = _silu(conv)
    xs = xbc[:, :SSD_INNER]
    bm = xbc[:, SSD_INNER:SSD_INNER + SSD_GROUPS * SSD_STATE].astype(BF16)
    cm = xbc[:, SSD_INNER + SSD_GROUPS * SSD_STATE:].astype(BF16)

    expand = expand_ref[...]
    dt = _softplus(dt_ref[...] + dtb_ref[...])
    a = dt * (-jnp.exp(alog_ref[...]))
    a_cum = sum(_dot(tri_ref[...], term) for term in _split3(a))
    a_last = a_cum[L - 1:L, :]
    stacked = jnp.concatenate([dt, jnp.exp(a_last - a_cum), jnp.exp(a_cum)], axis=0)
    wide = sum(_dot(term, expand) for term in _split3(stacked))
    x_dt = xs * wide[0:L]
    x_end = (x_dt * wide[L:2 * L]).astype(BF16)
    dec_start = wide[2 * L:3 * L]
    x_dt = x_dt.astype(BF16)
    a_cum_t = a_cum.T

    li = lax.broadcasted_iota(jnp.int32, (L, L), 0)
    si = lax.broadcasted_iota(jnp.int32, (L, L), 1)
    causal = si <= li
    ys = []
    for g in range(SSD_GROUPS):
        bg = bm[:, g * SSD_STATE:(g + 1) * SSD_STATE]
        cg = cm[:, g * SSD_STATE:(g + 1) * SSD_STATE]
        cb = _dot_nt(cg, bg)
        for r in range(SSD_HEADS_PER_GROUP):
            h = g * SSD_HEADS_PER_GROUP + r
            sl = slice(h * P, (h + 1) * P)
            seg = a_cum[:, h:h + 1] - a_cum_t[h:h + 1, :]
            lmat = jnp.where(causal, jnp.exp(jnp.where(causal, seg, 0.0)), 0.0)
            h_prev = state_ref[sl, :]
            y_diag = _dot((cb * lmat).astype(BF16), x_dt[:, sl])
            y_off = _dot_nt(cg, h_prev.astype(BF16)) * dec_start[:, sl]
            ys.append(y_diag + y_off)
            chunk_decay = jnp.exp(a_cum_t[h:h + 1, L - 1:L])
            state_ref[sl, :] = h_prev * chunk_decay + _dot_tn(x_end[:, sl], bg)
    y = jnp.concatenate(ys, axis=1) + dskip_ref[...] * xs
    y = y * _silu(z_ref[...])
    o_ref[...] = _rms(y, nw_ref[...]).astype(BF16)


def _ssd_kernel(xbc_ref, z_ref, dt_ref, convw_ref, convb_ref, dtb_ref, alog_ref, dskip_ref,
                nw_ref, expand_ref, tri_ref, o_ref, xpad_ref, state_ref):
    c = pl.program_id(1)

    @pl.when(c == 0)
    def _():
        xpad_ref[:, 0:CONV_HALO, :] = jnp.zeros((SSD_SEQS_PER_STEP, CONV_HALO, XBC_COLS), F32)
        state_ref[...] = jnp.zeros_like(state_ref)

    @pl.when(c > 0)
    def _():
        xpad_ref[:, 0:CONV_HALO, :] = xpad_ref[:, SSD_CHUNK:SSD_CHUNK + CONV_HALO, :]

    for b in range(SSD_SEQS_PER_STEP):
        _ssd_chunk(xbc_ref.at[b], z_ref.at[b], dt_ref.at[b], convw_ref, convb_ref, dtb_ref,
                   alog_ref, dskip_ref, nw_ref, expand_ref, tri_ref, o_ref.at[b],
                   xpad_ref.at[b], state_ref.at[b])


def _ssd(xbc, z, dt, conv_w, conv_b, dt_bias_pad, a_log_pad, d_skip_wide, norm_w, expand, tri):
    b, s, _ = xbc.shape
    nb = SSD_SEQS_PER_STEP
    assert b % nb == 0
    nc = s // SSD_CHUNK
    cur = lambda i, j: (i, j, 0)
    fixed = lambda i, j: (0, 0)
    full = lambda shape: pl.BlockSpec(shape, fixed)
    return pl.pallas_call(
        _ssd_kernel,
        grid=(b // nb, nc),
        in_specs=[
            pl.BlockSpec((nb, SSD_CHUNK, XBC_COLS), cur),
            pl.BlockSpec((nb, SSD_CHUNK, SSD_INNER), cur),
            pl.BlockSpec((nb, SSD_CHUNK, DT_PAD), cur),
            full((SSD_CONV, XBC_COLS)), full((1, XBC_COLS)), full((1, DT_PAD)), full((1, DT_PAD)),
            full((1, SSD_INNER)), full((1, SSD_INNER)), full((DT_PAD, SSD_INNER)),
            full((SSD_CHUNK, SSD_CHUNK)),
        ],
        out_specs=pl.BlockSpec((nb, SSD_CHUNK, SSD_INNER), cur),
        out_shape=jax.ShapeDtypeStruct((b, s, SSD_INNER), BF16),
        scratch_shapes=[
            pltpu.VMEM((nb, CONV_HALO + SSD_CHUNK, XBC_COLS), F32),
            pltpu.VMEM((nb, SSD_INNER, SSD_STATE), F32),
        ],
        compiler_params=_params("arbitrary", "arbitrary"),
        name="ssd",
    )(xbc, z, dt, conv_w, conv_b, dt_bias_pad, a_log_pad, d_skip_wide, norm_w, expand, tri)


def _outproj_kernel(x_ref, attn_ref, ssd_ref, wo_ref, nw_ref, wq_ref, h1_ref, xnt_ref, q_ref):
    mix = (_dot(attn_ref[...], wo_ref[0:ATTN_WIDTH, :])
           + _dot(ssd_ref[...], wo_ref[ATTN_WIDTH:ATTN_WIDTH + SSD_INNER, :]))
    h1 = x_ref[...] + mix
    h1_ref[...] = h1
    xn = _rms(h1, nw_ref[...])
    xnt_ref[...] = pltpu.bitcast(xn.T.astype(BF16), jnp.uint32)
    qf = _dot(xn.astype(BF16), wq_ref[...])
    for j in range(2 * PEER_HEADS):
        q_ref[j] = qf[:, j * PEER_HALF:(j + 1) * PEER_HALF].astype(BF16)


def _outproj(x2, attn, ssd, w_out, norm_w, w_query):
    t = x2.shape[0]
    tm = min(OUTPROJ_TM, t)
    row = lambda i: (i, 0)
    fixed = lambda i: (0, 0)
    nq = 2 * PEER_HEADS
    return pl.pallas_call(
        _outproj_kernel,
        grid=(t // tm,),
        in_specs=[
            pl.BlockSpec((tm, D_MODEL), row),
            pl.BlockSpec((tm, ATTN_WIDTH), row),
            pl.BlockSpec((tm, SSD_INNER), row),
            pl.BlockSpec((ATTN_WIDTH + SSD_INNER, D_MODEL), fixed),
            pl.BlockSpec((1, D_MODEL), fixed),
            pl.BlockSpec((D_MODEL, nq * PEER_HALF), fixed),
        ],
        out_specs=[
            pl.BlockSpec((tm, D_MODEL), row),
            pl.BlockSpec((D_MODEL // 2, tm), lambda i: (0, i)),
            pl.BlockSpec((nq, tm, PEER_HALF), lambda i: (0, i, 0)),
        ],
        out_shape=[
            jax.ShapeDtypeStruct((t, D_MODEL), F32),
            jax.ShapeDtypeStruct((D_MODEL // 2, t), jnp.uint32),
            jax.ShapeDtypeStruct((nq, t, PEER_HALF), BF16),
        ],
        compiler_params=_params("parallel"),
        name="outproj",
    )(x2, attn, ssd, w_out, norm_w, w_query)


def _ce(x, hi, lo):
    a, b = x[hi], x[lo]
    if b is None:
        return
    if a is None:
        x[hi], x[lo] = b, None
        return
    x[hi], x[lo] = jnp.maximum(a, b), jnp.minimum(a, b)


def _bitonic_merge_desc(x):
    n = len(x)
    j = n // 2
    while j >= 1:
        for i in range(n):
            if i & j == 0:
                _ce(x, i, i | j)
        j //= 2


def _sort_desc(x):
    n = len(x)
    k = 2
    while k <= n:
        j = k // 2
        while j >= 1:
            for i in range(n):
                l = i ^ j
                if l > i:
                    if i & k == 0:
                        _ce(x, i, l)
                    else:
                        _ce(x, l, i)
            j //= 2
        k *= 2


def _merge_top(a, b, sort=True):
    n = len(a)
    out = []
    for i in range(n):
        u, v = a[i], b[n - 1 - i]
        out.append(v if u is None else u if v is None else jnp.maximum(u, v))
    if sort:
        _bitonic_merge_desc(out)
    return out


def _key_rows(k):
    return slice(k * SUBLANES, (k + 1) * SUBLANES)


def _top16_sorted(s_ref, lanes):
    groups = []
    for g in range(N_KEYS // PEER_TOPK):
        x = [s_ref[_key_rows(g * PEER_TOPK + i), lanes] for i in range(PEER_TOPK)]
        _sort_desc(x)
        groups.append(x)
    while len(groups) > 1:
        groups = [_merge_top(groups[i], groups[i + 1]) for i in range(0, len(groups), 2)]
    return groups[0]


def _route_select(h, lanes, s1_ref, s2_ref, r2_ref, e2_ref, c1_ref, e1_ref,
                  r2s_ref, e2s_ref, c1s_ref, e1s_ref):
    v1 = _top16_sorted(s1_ref, lanes)
    v2 = _top16_sorted(s2_ref, lanes)

    lists = []
    for a in range(PEER_TOPK):
        n_b = PEER_TOPK // (a + 1)
        lists.append([v1[a] + v2[b] if b < n_b else None for b in range(PEER_TOPK)])
    while len(lists) > 2:
        lists = [_merge_top(lists[i], lists[i + 1]) for i in range(0, len(lists), 2)]
    top = _merge_top(lists[0], lists[1], sort=False)
    tau = top[0]
    for x in top[1:]:
        tau = jnp.minimum(tau, x)

    e1_top = [jnp.exp(v - v1[0]) for v in v1]
    e2_top = [jnp.exp(v - v2[0]) for v in v2]
    zsum = jnp.zeros_like(tau)
    cnt = []
    for a in range(PEER_TOPK):
        n_a = jnp.zeros_like(tau)
        w_a = jnp.zeros_like(tau)
        for b in range(PEER_TOPK // (a + 1)):
            sel = (v1[a] + v2[b]) >= tau
            n_a = n_a + jnp.where(sel, 1.0, 0.0)
            w_a = w_a + jnp.where(sel, e2_top[b], 0.0)
        cnt.append(n_a)
        zsum = zsum + e1_top[a] * w_a
    half_inv_z = 0.5 / zsum

    for k in range(N_KEYS):
        rows = _key_rows(k)
        s1 = s1_ref[rows, lanes]
        s2 = s2_ref[rows, lanes]
        c1 = jnp.zeros_like(tau)
        for a in reversed(range(PEER_TOPK)):
            c1 = jnp.where(s1 >= v1[a], cnt[a], c1)
        r2 = jnp.zeros_like(tau)
        for b in range(PEER_TOPK):
            r2 = jnp.where(v2[b] > s2, float(b + 1), r2)
        c1s_ref[rows, :] = c1
        r2s_ref[rows, :] = r2
        e1s_ref[rows, :] = jnp.exp(s1 - v1[0]) * half_inv_z
        e2s_ref[rows, :] = jnp.exp(s2 - v2[0])

    for j in range(SUBLANES):
        cols = slice(j * LANES, (j + 1) * LANES)
        take = pl.ds(j, N_KEYS, stride=SUBLANES)
        r2_ref[h, :, cols] = r2s_ref[take, :].astype(BF16)
        e2_ref[h, :, cols] = e2s_ref[take, :].astype(BF16)
        c1_ref[h, :, cols] = c1s_ref[take, :]
        e1_ref[h, :, cols] = e1s_ref[take, :]


def _route_kernel(q_ref, k1_ref, k2_ref, r2_ref, e2_ref, c1_ref, e1_ref,
                  s1_ref, s2_ref, r2s_ref, e2s_ref, c1s_ref, e1s_ref):
    def head_pair(i, carry):
        for half, (k_ref, s_ref) in enumerate(((k1_ref, s1_ref), (k2_ref, s2_ref))):
            qcat = jnp.concatenate(
                [jnp.concatenate([q_ref[2 * (2 * i + hh) + half, j * LANES:(j + 1) * LANES, :]
                                  for j in range(SUBLANES)], axis=1)
                 for hh in range(2)], axis=0)
            s_ref[...] = _dot_nt(k_ref[...], qcat)
        for hh in range(2):
            _route_select(2 * i + hh, slice(hh * LANES, (hh + 1) * LANES), s1_ref, s2_ref,
                          r2_ref, e2_ref, c1_ref, e1_ref, r2s_ref, e2s_ref, c1s_ref, e1s_ref)
        return carry

    lax.fori_loop(0, PEER_HEADS // 2, head_pair, 0)


def _expand_keys(keys):
    eye = jnp.eye(SUBLANES, dtype=keys.dtype)
    return jnp.einsum("kd,jc->kjcd", keys, eye).reshape(N_KEYS * SUBLANES, SUBLANES * PEER_HALF)


def _route(q, keys1, keys2):
    nq, t, _ = q.shape
    tm = ROUTE_TM
    assert t % tm == 0
    fixed = lambda i: (0, 0)
    out_spec = pl.BlockSpec((PEER_HEADS, N_KEYS, tm), lambda i: (0, 0, i))
    shape = (PEER_HEADS, N_KEYS, t)
    key_spec = pl.BlockSpec((N_KEYS * SUBLANES, SUBLANES * PEER_HALF), fixed)
    stage = pltpu.VMEM((N_KEYS * SUBLANES, LANES), F32)
    scores = pltpu.VMEM((N_KEYS * SUBLANES, 2 * LANES), F32)
    return pl.pallas_call(
        _route_kernel,
        grid=(t // tm,),
        in_specs=[pl.BlockSpec((nq, tm, PEER_HALF), lambda i: (0, i, 0)), key_spec, key_spec],
        out_specs=[out_spec] * 4,
        out_shape=[jax.ShapeDtypeStruct(shape, BF16)] * 2 + [jax.ShapeDtypeStruct(shape, F32)] * 2,
        scratch_shapes=[scores] * 2 + [stage] * 4,
        compiler_params=_params("parallel"),
        name="route",
    )(q, _expand_keys(keys1), _expand_keys(keys2))


def _peer_activations(pre_ref, act_ref, r2_ref, e2_ref, c1_ref, e1_ref, first_key, n_keys):
    tm = pre_ref.shape[1]
    chunks = N_KEYS // PEER_ROWS
    shape = (PEER_ROWS, PEER_LANE_CHUNK)
    zero = jnp.zeros(shape, BF16)
    for a in range(n_keys):
        key1 = first_key + a
        for lc in range(tm // PEER_LANE_CHUNK):
            lanes = slice(lc * PEER_LANE_CHUNK, (lc + 1) * PEER_LANE_CHUNK)
            gates = [zero] * chunks
            for h in range(PEER_HEADS):
                cnt = jnp.broadcast_to(c1_ref[h, key1:key1 + 1, lanes], shape).astype(BF16)
                e1 = jnp.broadcast_to(e1_ref[h, key1:key1 + 1, lanes], shape).astype(BF16)
                for c in range(chunks):
                    rows = slice(c * PEER_ROWS, (c + 1) * PEER_ROWS)
                    sel = r2_ref[h, rows, lanes] < cnt
                    gates[c] = gates[c] + jnp.where(sel, e2_ref[h, rows, lanes], zero) * e1
            for c in range(chunks):
                rows = slice(a * N_KEYS + c * PEER_ROWS, a * N_KEYS + (c + 1) * PEER_ROWS)
                p = pre_ref[rows, lanes]
                gelu2 = p + p * lax.erf(p * (2.0 ** -0.5))
                act_ref[rows, lanes] = gelu2.astype(BF16) * gates[c]


def _peer_kernel(xnt_ref, d_ref, ut_ref, r2_ref, e2_ref, c1_ref, e1_ref, h1_ref, nw_ref,
                 o_ref, acc_ref, pre_ref, act_ref):
    j = pl.program_id(1)
    assert sum(PEER_SUBS) == ut_ref.shape[1] and max(PEER_SUBS) == PEER_SUB
    starts = [sum(PEER_SUBS[:i]) for i in range(len(PEER_SUBS))]

    @pl.when(j == 0)
    def _():
        acc_ref[...] = jnp.zeros_like(acc_ref)

    def down(i):
        lo, n = starts[i], PEER_SUBS[i]
        d_sub = d_ref[lo // 2:(lo + n) // 2, :]
        pre_ref[i % 2, 0:n, :] = _dot(pltpu.bitcast(d_sub, BF16),
                                      pltpu.bitcast(xnt_ref[...], BF16))

    down(0)
    for i, (lo, n) in enumerate(zip(starts, PEER_SUBS)):
        if i + 1 < len(PEER_SUBS):
            down(i + 1)
        _peer_activations(pre_ref.at[i % 2], act_ref.at[i % 2], r2_ref, e2_ref, c1_ref, e1_ref,
                          lo // N_KEYS, n // N_KEYS)
        acc_ref[...] += _dot(pltpu.bitcast(ut_ref[:, lo:lo + n], BF16), act_ref[i % 2, 0:n, :])

    @pl.when(j == pl.num_programs(1) - 1)
    def _():
        h2 = h1_ref[...] + acc_ref[...].T
        o_ref[...] = _rms(h2, nw_ref[...])


def _pack_kernel(w_ref, o_ref, *, transpose):
    w = w_ref[...].T if transpose else w_ref[...]
    o_ref[...] = pltpu.bitcast(w.astype(BF16), jnp.uint32)


def _pack_experts(w, transpose):
    tiles = N_EXPERTS // PEER_TE
    if transpose:
        out_spec = pl.BlockSpec((None, D_MODEL // 2, PEER_TE), lambda i: (i, 0, 0))
        out_shape = jax.ShapeDtypeStruct((tiles, D_MODEL // 2, PEER_TE), jnp.uint32)
    else:
        out_spec = pl.BlockSpec((PEER_TE // 2, D_MODEL), lambda i: (i, 0))
        out_shape = jax.ShapeDtypeStruct((N_EXPERTS // 2, D_MODEL), jnp.uint32)
    return pl.pallas_call(
        functools.partial(_pack_kernel, transpose=transpose),
        grid=(tiles,),
        in_specs=[pl.BlockSpec((PEER_TE, D_MODEL), lambda i: (i, 0))],
        out_specs=out_spec,
        out_shape=out_shape,
        compiler_params=_params("parallel"),
        name="pack_up" if transpose else "pack_down",
    )(w)


def _peer(xnt, down, up_t, r2, e2, c1, e1, h1, norm_w):
    t = h1.shape[0]
    tm = min(PEER_TM, t)
    te = PEER_TE
    key2_spec = pl.BlockSpec((PEER_HEADS, N_KEYS, tm), lambda i, j: (0, 0, i))
    key1_spec = pl.BlockSpec((PEER_HEADS, te // N_KEYS, tm), lambda i, j: (0, j, i))
    return pl.pallas_call(
        _peer_kernel,
        grid=(t // tm, N_EXPERTS // te),
        in_specs=[
            pl.BlockSpec((D_MODEL // 2, tm), lambda i, j: (0, i)),
            pl.BlockSpec((te // 2, D_MODEL), lambda i, j: (j, 0)),
            pl.BlockSpec((None, D_MODEL // 2, te), lambda i, j: (j, 0, 0)),
            key2_spec, key2_spec, key1_spec, key1_spec,
            pl.BlockSpec((tm, D_MODEL), lambda i, j: (i, 0)),
            pl.BlockSpec((1, D_MODEL), lambda i, j: (0, 0)),
        ],
        out_specs=pl.BlockSpec((tm, D_MODEL), lambda i, j: (i, 0)),
        out_shape=jax.ShapeDtypeStruct((t, D_MODEL), F32),
        scratch_shapes=[
            pltpu.VMEM((D_MODEL, tm), F32),
            pltpu.VMEM((2, PEER_SUB, tm), F32),
            pltpu.VMEM((2, PEER_SUB, tm), BF16),
        ],
        compiler_params=_params("parallel", "arbitrary"),
        name="peer",
    )(xnt, down, up_t, r2, e2, c1, e1, h1, norm_w)


def _pad_lanes(v, width):
    return jnp.pad(v.astype(F32), (0, width - v.shape[0])).reshape(1, width)


def _layer(x, norm_mix, w_in, conv_w, conv_b, dt_bias, a_log, d_skip, ssd_norm_w, attn_sinks,
           w_out, rel_bias, norm_ffn, w_query, sub_keys1, sub_keys2, expert_down, expert_up,
           out_norm):
    b, s, d = x.shape
    t = b * s
    x2 = x.reshape(t, d)

    w_in_pad = jnp.pad(w_in, ((0, 0), (0, IN_COLS_PAD - w_in.shape[1]))).astype(BF16)
    q, k, v, z, xbc, dt = _inproj(x2, norm_mix.reshape(1, d), w_in_pad)

    qi = jnp.arange(ATTN_BLOCK)[:, None]
    kj = jnp.arange(ATTN_BLOCK)[None, :]
    bucket = _t5_bucket(jnp.where(kj <= qi, qi - kj, qi + ATTN_BLOCK - kj)).astype(jnp.int32)
    attn = _attention(q.reshape(b, s, -1), k.reshape(b, s, -1), v.reshape(b, s, -1),
                      bucket, rel_bias.astype(F32), attn_sinks.astype(F32))

    lane = jnp.arange(SSD_INNER)[None, :] // SSD_HEAD_DIM
    expand = (jnp.arange(DT_PAD)[:, None] == lane).astype(BF16)
    tri = (jnp.arange(SSD_CHUNK)[None, :] <= jnp.arange(SSD_CHUNK)[:, None]).astype(BF16)
    ssd = _ssd(xbc.reshape(b, s, -1), z.reshape(b, s, -1), dt.reshape(b, s, -1),
               conv_w, conv_b.reshape(1, -1), _pad_lanes(dt_bias, DT_PAD),
               _pad_lanes(a_log, DT_PAD), jnp.repeat(d_skip.astype(F32), SSD_HEAD_DIM).reshape(1, -1),
               ssd_norm_w.reshape(1, -1), expand, tri)

    h1, xnt, pq = _outproj(x2, attn.reshape(t, -1), ssd.reshape(t, -1), w_out.astype(BF16),
                           norm_ffn.reshape(1, d), w_query.astype(BF16))
    r2, e2, c1, e1 = _route(pq, sub_keys1.astype(BF16), sub_keys2.astype(BF16))
    out = _peer(xnt, _pack_experts(expert_down, False), _pack_experts(expert_up, True),
                r2, e2, c1, e1, h1, out_norm.reshape(1, d))
    return out.reshape(b, s, d)


@jax.jit
def kernel(x, norm_mix, w_in, conv_w, conv_b, dt_bias, a_log, d_skip, ssd_norm_w, attn_sinks,
           w_out, rel_bias, norm_ffn, w_query, sub_keys1, sub_keys2, expert_down, expert_up,
           norm_final):
    assert norm_mix.shape[0] == 1, "single-layer block"
    return _layer(x, norm_mix[0], w_in[0], conv_w[0], conv_b[0], dt_bias[0], a_log[0], d_skip[0],
                  ssd_norm_w[0], attn_sinks[0], w_out[0], rel_bias, norm_ffn[0], w_query[0],
                  sub_keys1[0], sub_keys2[0], expert_down[0], expert_up[0], norm_final)
```
